```python
import math
import jax, jax.numpy as jnp
from jax import lax
import numpy as np

D_MODEL = 1024
BATCH = 32
SEQ = 256
DEPTH = 2
DEC_BATCH = 2
DEC_SEQ = 2048
PAST_LEN = 512

GRID_W = 64
D_MIX = D_MODEL
D_A = D_MIX // 4
H_A = 4
BA = D_A // H_A
CONV_A = 4
LRU_C = 8.0
D_B = D_MIX // 4
CONV_B = 3
HY_ORDER = 2
HY_BANDS = 16
HY_EMB = 1 + 2 * HY_BANDS
HY_FH = 64
HY_DECAY_TARGET = 1e-2
D_C = D_MIX // 2
H_C = 4
DK = D_C // H_C
DV = DK
CHUNK = 64
D_IN = 2 * D_A + 3 * D_B + 4 * D_C + 4 * H_C
N_GROUPS = 4
E_PER_GROUP = 4
N_EXP = N_GROUPS * E_PER_GROUP
TOP_K = 2
D_E = 512
ALPHA = (2 * DEPTH) ** 0.25
BETA = (8 * DEPTH) ** -0.25
EPS = 1e-6

kernel_name = "hybrid_diffusion_lru_hyena_mlstm_hmoe_step"


def _layer_norm(x, g=None, b=None):
    xf = x.astype(jnp.float32)
    mu = xf.mean(-1, keepdims=True)
    var = jnp.square(xf - mu).mean(-1, keepdims=True)
    y = (xf - mu) * lax.rsqrt(var + EPS)
    if g is not None:
        y = y * g.astype(jnp.float32) + b.astype(jnp.float32)
    return y.astype(x.dtype)


def _rms(x):
    xf = x.astype(jnp.float32)
    return (xf * lax.rsqrt(jnp.mean(xf * xf, -1, keepdims=True) + EPS)).astype(x.dtype)


def _sincos_2d(rows, cols):
    quarter = D_MODEL // 4
    omega = 1.0 / (10000.0 ** (jnp.arange(quarter, dtype=jnp.float32) / quarter))

    def emb(n):
        ang = jnp.arange(n, dtype=jnp.float32)[:, None] * omega[None]
        return jnp.concatenate([jnp.sin(ang), jnp.cos(ang)], -1)

    er, ec = emb(rows), emb(cols)
    half = D_MODEL // 2
    pos = jnp.concatenate([jnp.broadcast_to(er[:, None], (rows, cols, half)),
                           jnp.broadcast_to(ec[None], (rows, cols, half))], -1)
    return pos.reshape(rows * cols, D_MODEL)


def _dwconv(x, w, b, pad_l, pad_r):
    y = lax.conv_general_dilated(x, w[:, None, :].astype(x.dtype), window_strides=(1,),
                                 padding=[(pad_l, pad_r)],
                                 dimension_numbers=("NWC", "WIO", "NWC"),
                                 feature_group_count=x.shape[-1])
    return y + b.astype(x.dtype)


def _linear_scan(a, bx, h0, reverse):
    if reverse:
        bx = bx.at[:, -1].add(a[:, -1] * h0)
    else:
        bx = bx.at[:, 0].add(a[:, 0] * h0)

    def comb(e1, e2):
        a1, b1 = e1
        a2, b2 = e2
        return a1 * a2, a2 * b1 + b2

    _, h = lax.associative_scan(comb, (a, bx), reverse=reverse, axis=1)
    return h


def _rglru(x, wa, ba, wx, bx, lam, h0, reverse):
    B, L, _ = x.shape
    xf = x.astype(jnp.float32)
    xh = xf.reshape(B, L, H_A, BA)
    r = jax.nn.sigmoid(jnp.einsum("blhi,hij->blhj", xh, wa.astype(jnp.float32)).reshape(B, L, D_A) + ba)
    i = jax.nn.sigmoid(jnp.einsum("blhi,hij->blhj", xh, wx.astype(jnp.float32)).reshape(B, L, D_A) + bx)
    log_a = -LRU_C * r * jax.nn.softplus(-lam.astype(jnp.float32))
    a = jnp.exp(log_a)
    gated = jnp.sqrt(-jnp.expm1(2.0 * log_a)) * (i * xf)
    h = _linear_scan(a, gated, h0.astype(jnp.float32), reverse)
    last = h[:, 0] if reverse else h[:, -1]
    return h, last


def _hyena_spectra(L, lp):
    f32 = jnp.float32
    t = jnp.arange(L, dtype=f32) / max(L - 1, 1)
    w = 2.0 * math.pi * jnp.arange(L, dtype=f32) / L
    bands = jnp.linspace(1e-4, HY_BANDS - 1, HY_BANDS, dtype=f32)
    z = jnp.concatenate([t[:, None], jnp.cos(w[:, None] * bands), -jnp.sin(w[:, None] * bands)], -1)
    freq = lp["hy_freq"].astype(f32)
    hdn = jnp.sin(freq * (z @ lp["hy_w1"].astype(f32) + lp["hy_b1"]))
    hdn = jnp.sin(freq * (hdn @ lp["hy_w2"].astype(f32) + lp["hy_b2"]))
    taps = (hdn @ lp["hy_w3"].astype(f32)).reshape(L, HY_ORDER, 2, D_B)
    min_decay = math.log(HY_DECAY_TARGET) / 1.5
    max_decay = math.log(HY_DECAY_TARGET) / 0.3
    deltas = jnp.abs(jnp.linspace(min_decay, max_decay, D_B, dtype=f32))
    taps = taps * jnp.exp(-t[:, None] * deltas)[:, None, None, :]
    fwd, bwd = taps[:, :, 0], taps[:, :, 1]
    k_circ = jnp.concatenate([fwd, jnp.zeros((1, HY_ORDER, D_B), f32), bwd[1:][::-1]], axis=0)
    return jnp.fft.rfft(k_circ, axis=0)


def _fftconv(u, spec):
    L = u.shape[1]
    U = jnp.fft.rfft(u, n=2 * L, axis=1)
    return jnp.fft.irfft(U * spec[None], n=2 * L, axis=1)[:, :L]


def _mlstm_dir(q, k, v, ig, lf, C0, n0, m0):
    B, L, H, _ = q.shape
    nc = L // CHUNK

    def chunks(t):
        t = t.reshape((B, nc, CHUNK) + t.shape[2:])
        return jnp.moveaxis(jnp.moveaxis(t, 1, 0), 3, 2)

    causal = jnp.tril(jnp.ones((CHUNK, CHUNK), bool))

    def step(carry, inp):
        C, n, m = carry
        qc, kc, vc, igc, lfc = inp
        b = jnp.cumsum(lfc, axis=-1)
        dmat = jnp.where(causal, b[..., :, None] - b[..., None, :] + igc[..., None, :], -jnp.inf)
        inter = b + m[..., None]
        mj = jnp.maximum(inter, dmat.max(-1))
        wd = jnp.exp(dmat - mj[..., None])
        wi = jnp.exp(inter - mj)
        s = jnp.einsum("bhjd,bhsd->bhjs", qc, kc) * wd
        num = jnp.einsum("bhjs,bhsv->bhjv", s, vc) + wi[..., None] * jnp.einsum("bhjd,bhdv->bhjv", qc, C)
        den = s.sum(-1) + wi * jnp.einsum("bhjd,bhd->bhj", qc, n)
        hc = num / jnp.maximum(jnp.abs(den), jnp.exp(-mj))[..., None]
        bT = b[..., -1]
        ds = bT[..., None] - b + igc
        m_new = jnp.maximum(bT + m, ds.max(-1))
        ws = jnp.exp(ds - m_new[..., None])
        wc = jnp.exp(bT + m - m_new)
        C_new = wc[..., None, None] * C + jnp.einsum("bhs,bhsd,bhsv->bhdv", ws, kc, vc)
        n_new = wc[..., None] * n + jnp.einsum("bhs,bhsd->bhd", ws, kc)
        return (C_new, n_new, m_new), hc

    carry0 = (C0.astype(jnp.float32), n0.astype(jnp.float32), m0.astype(jnp.float32))
    (C, n, m), hs = lax.scan(step, carry0, (chunks(q), chunks(k), chunks(v), chunks(ig), chunks(lf)))
    h = jnp.moveaxis(jnp.moveaxis(hs, 0, 1), 2, 3).reshape(B, L, H, DV)
    return h, (C, n, m)


def _mixer(h, lp, init, with_state):
    B, L, _ = h.shape
    f32 = jnp.float32
    z = h @ lp["w_in"] + lp["b_in"]
    cuts = np.cumsum([D_A, D_A, 3 * D_B, D_C, D_C, D_C, D_C]).tolist()
    xa, ya, hyb, q, k, v, o, gates = jnp.split(z, cuts, axis=-1)
    h0, C0, n0, m0 = init

    xa = _dwconv(xa, lp["conv_a_w"], lp["conv_a_b"], 2, 1)
    ha_f, last_f = _rglru(xa, lp["lru_wa"][0], lp["lru_ba"][0], lp["lru_wx"][0], lp["lru_bx"][0],
                          lp["lru_lam"][0], h0[:, 0], False)
    ha_b, last_b = _rglru(xa, lp["lru_wa"][1], lp["lru_ba"][1], lp["lru_wx"][1], lp["lru_bx"][1],
                          lp["lru_lam"][1], h0[:, 1], True)
    out_a = _rms(jax.nn.gelu(ya) * (ha_f + ha_b).astype(h.dtype))

    hyb = _dwconv(hyb, lp["conv_b_w"], lp["conv_b_b"], 1, 1)
    hv, hx1, hx2 = jnp.split(hyb, 3, axis=-1)
    spec = _hyena_spectra(L, lp)
    y = hv.astype(f32)
    for o_idx, gate in enumerate((hx1, hx2)):
        y = gate.astype(f32) * (_fftconv(y, spec[:, o_idx]) + y * lp["hy_bias"][o_idx].astype(f32))
    out_b = _rms(y.astype(h.dtype))

    qh = q.reshape(B, L, H_C, DK).astype(f32) * DK ** -0.5
    kh = k.reshape(B, L, H_C, DK).astype(f32)
    vh = v.reshape(B, L, H_C, DV).astype(f32)
    g = gates.astype(f32).reshape(B, L, 4, H_C)
    i_f, lf_f = g[:, :, 0], jax.nn.log_sigmoid(g[:, :, 1])
    i_b, lf_b = g[:, :, 2], jax.nn.log_sigmoid(g[:, :, 3])
    hc_f, (Cf, nf, mf) = _mlstm_dir(qh, kh, vh, i_f, lf_f, C0[:, 0], n0[:, 0], m0[:, 0])
    flip = lambda t: jnp.flip(t, axis=1)
    hc_b, (Cb, nb, mb) = _mlstm_dir(flip(qh), flip(kh), flip(vh), flip(i_b), flip(lf_b),
                                    C0[:, 1], n0[:, 1], m0[:, 1])
    hc = (hc_f + flip(hc_b)).astype(h.dtype)
    out_c = jax.nn.sigmoid(o) * _rms(hc).reshape(B, L, D_C)

    mix = jnp.concatenate([out_a, out_b, out_c], -1) * lp["mix_g"]
    out = mix @ lp["w_out"]
    state = None
    if with_state:
        dt = h.dtype
        state = (jnp.stack([last_f, last_b], 1).astype(dt), jnp.stack([Cf, Cb], 1).astype(dt),
                 jnp.stack([nf, nb], 1).astype(dt), jnp.stack([mf, mb], 1).astype(dt))
    return out, state


def _hier_moe(h, lp):
    B, L, D = h.shape
    t = h.reshape(B * L, D)
    lg = (t @ lp["rt_wg"] + lp["rt_bg"]).astype(jnp.float32)
    pg = jax.nn.softmax(lg, -1)
    g_idx = jnp.argmax(lg, -1)
    pg_top = jnp.take_along_axis(pg, g_idx[:, None], 1)[:, 0]
    le = (t @ lp["rt_we"] + lp["rt_be"]).astype(jnp.float32).reshape(-1, N_GROUPS, E_PER_GROUP)
    le_g = jnp.take_along_axis(le, g_idx[:, None, None], 1)[:, 0]
    top_v, top_i = lax.top_k(le_g, TOP_K)
    w_sel = jax.nn.softmax(top_v, -1) * pg_top[:, None]
    expert_id = g_idx[:, None] * E_PER_GROUP + top_i
    gate = jnp.sum(jax.nn.one_hot(expert_id, N_EXP, dtype=jnp.float32) * w_sel[..., None], 1).astype(t.dtype)
    a = jnp.einsum("nd,edf->nef", t, lp["moe_w1"])
    bgt = jnp.einsum("nd,edf->nef", t, lp["moe_w3"])
    y = jnp.einsum("nef,efd,ne->nd", jax.nn.silu(a) * bgt, lp["moe_w2"], gate)
    return y.reshape(B, L, D)


def _modulation(cond, lp):
    return (jax.nn.silu(cond) @ lp["w_ada"] + lp["b_ada"]).reshape(cond.shape[0], 6, D_MODEL)


def _layer(x, mod, lp, init, with_state):
    sh1, sc1, g1, sh2, sc2, g2 = [mod[:, j][:, None, :] for j in range(6)]
    mix, state = _mixer(x * (1.0 + sc1) + sh1, lp, init, with_state)
    x = _layer_norm(ALPHA * x + g1 * mix, lp["ln1_g"], lp["ln1_b"])
    ffn = _hier_moe(x * (1.0 + sc2) + sh2, lp)
    x = _layer_norm(ALPHA * x + g2 * ffn, lp["ln2_g"], lp["ln2_b"])
    return x, state


def setup_inputs(seed: int = 0) -> dict:
    key = jax.random.key(seed)
    ks = iter(jax.random.split(key, 64))
    f32 = jnp.float32
    nrm = lambda shape, scale: jax.random.normal(next(ks), shape, f32) * scale
    D = D_MODEL

    x_prompt = nrm((BATCH, SEQ, D), 1.0)
    x_sample = nrm((DEC_BATCH, DEC_SEQ, D), 1.0)
    c = nrm((DEC_BATCH, D), 1.0)
    state_lru = nrm((DEC_BATCH, DEPTH, 2, D_A), 0.5)
    state_mlstm_C = nrm((DEC_BATCH, DEPTH, 2, H_C, DK, DV), 0.1)
    state_mlstm_n = nrm((DEC_BATCH, DEPTH, 2, H_C, DK), 0.1)
    state_mlstm_m = nrm((DEC_BATCH, DEPTH, 2, H_C), 1.0)
    c_ctx = nrm((D,), 1.0)

    w_ada = nrm((DEPTH, D, 6 * D), 0.2 * D ** -0.5)
    b_ada = nrm((DEPTH, 6, D), 0.02).at[:, 2].add(1.0).at[:, 5].add(1.0).reshape(DEPTH, 6 * D)
    w_in = nrm((DEPTH, D, D_IN), D ** -0.5)
    g0 = D_IN - 4 * H_C
    fbias = jnp.linspace(3.0, 6.0, H_C, dtype=f32)
    b_in = nrm((DEPTH, D_IN), 0.02).at[:, g0 + H_C:g0 + 2 * H_C].add(fbias).at[:, g0 + 3 * H_C:].add(fbias)
    conv_a_w = nrm((DEPTH, CONV_A, D_A), CONV_A ** -0.5)
    conv_a_b = nrm((DEPTH, D_A), 0.02)
    lru_wa = nrm((DEPTH, 2, H_A, BA, BA), BA ** -0.5)
    lru_ba = nrm((DEPTH, 2, D_A), 0.02)
    lru_wx = nrm((DEPTH, 2, H_A, BA, BA), BA ** -0.5)
    lru_bx = nrm((DEPTH, 2, D_A), 0.02)
    a_init = jax.random.uniform(next(ks), (DEPTH, 2, D_A), f32, 0.9, 0.999) ** (1.0 / LRU_C)
    lru_lam = jnp.log(a_init) - jnp.log1p(-a_init)
    conv_b_w = nrm((DEPTH, CONV_B, 3 * D_B), CONV_B ** -0.5)
    conv_b_b = nrm((DEPTH, 3 * D_B), 0.02)
    hy_w1 = nrm((DEPTH, HY_EMB, HY_FH), HY_EMB ** -0.5)
    hy_b1 = nrm((DEPTH, HY_FH), 0.02)
    hy_w2 = nrm((DEPTH, HY_FH, HY_FH), HY_FH ** -0.5)
    hy_b2 = nrm((DEPTH, HY_FH), 0.02)
    hy_freq = 1.0 + nrm((DEPTH, HY_FH), 0.05)
    hy_w3 = nrm((DEPTH, HY_FH, HY_ORDER * 2 * D_B), HY_FH ** -0.5)
    hy_bias = nrm((DEPTH, HY_ORDER, D_B), 0.1)
    mix_g = 1.0 + nrm((DEPTH, D_MIX), 0.02)
    w_out = nrm((DEPTH, D_MIX, D), BETA * D_MIX ** -0.5)
    ln1_g = 1.0 + nrm((DEPTH, D), 0.02)
    ln1_b = nrm((DEPTH, D), 0.02)
    rt_wg = nrm((DEPTH, D, N_GROUPS), D ** -0.5)
    rt_bg = nrm((DEPTH, N_GROUPS), 0.01)
    rt_we = nrm((DEPTH, D, N_EXP), D ** -0.5)
    rt_be = nrm((DEPTH, N_EXP), 0.01)
    moe_w1 = nrm((DEPTH, N_EXP, D, D_E), D ** -0.5)
    moe_w3 = nrm((DEPTH, N_EXP, D, D_E), D ** -0.5)
    moe_w2 = nrm((DEPTH, N_EXP, D_E, D), BETA * D_E ** -0.5)
    ln2_g = 1.0 + nrm((DEPTH, D), 0.02)
    ln2_b = nrm((DEPTH, D), 0.02)
    return {"x_prompt": x_prompt, "x_sample": x_sample, "c": c,
            "state_lru": state_lru, "state_mlstm_C": state_mlstm_C,
            "state_mlstm_n": state_mlstm_n, "state_mlstm_m": state_mlstm_m,
            "c_ctx": c_ctx, "w_ada": w_ada, "b_ada": b_ada, "w_in": w_in, "b_in": b_in,
            "conv_a_w": conv_a_w, "conv_a_b": conv_a_b, "lru_wa": lru_wa, "lru_ba": lru_ba,
            "lru_wx": lru_wx, "lru_bx": lru_bx, "lru_lam": lru_lam,
            "conv_b_w": conv_b_w, "conv_b_b": conv_b_b, "hy_w1": hy_w1, "hy_b1": hy_b1,
            "hy_w2": hy_w2, "hy_b2": hy_b2, "hy_freq": hy_freq, "hy_w3": hy_w3, "hy_bias": hy_bias,
            "mix_g": mix_g, "w_out": w_out, "ln1_g": ln1_g, "ln1_b": ln1_b,
            "rt_wg": rt_wg, "rt_bg": rt_bg, "rt_we": rt_we, "rt_be": rt_be,
            "moe_w1": moe_w1, "moe_w3": moe_w3, "moe_w2": moe_w2, "ln2_g": ln2_g, "ln2_b": ln2_b}


def reference(x_prompt, x_sample, c, state_lru, state_mlstm_C, state_mlstm_n, state_mlstm_m,
              c_ctx, w_ada, b_ada, w_in, b_in, conv_a_w, conv_a_b, lru_wa, lru_ba, lru_wx, lru_bx,
              lru_lam, conv_b_w, conv_b_b, hy_w1, hy_b1, hy_w2, hy_b2, hy_freq, hy_w3, hy_bias,
              mix_g, w_out, ln1_g, ln1_b, rt_wg, rt_bg, rt_we, rt_be, moe_w1, moe_w3, moe_w2,
              ln2_g, ln2_b):
    B = x_prompt.shape[0]
    L_lat = x_sample.shape[1]
    rows = L_lat // GRID_W
    dt = x_prompt.dtype
    xc = _layer_norm(x_prompt)
    xl = _layer_norm(x_sample + _sincos_2d(rows, GRID_W).astype(x_sample.dtype)[None])
    zero_init = (jnp.zeros((B, 2, D_A), dt), jnp.zeros((B, 2, H_C, DK, DV), dt),
                 jnp.zeros((B, 2, H_C, DK), dt), jnp.zeros((B, 2, H_C), dt))
    s_lru, s_C, s_n, s_m = [], [], [], []
    for l in range(DEPTH):
        lp = {"w_ada": w_ada[l], "b_ada": b_ada[l], "w_in": w_in[l], "b_in": b_in[l],
              "conv_a_w": conv_a_w[l], "conv_a_b": conv_a_b[l], "lru_wa": lru_wa[l], "lru_ba": lru_ba[l],
              "lru_wx": lru_wx[l], "lru_bx": lru_bx[l], "lru_lam": lru_lam[l],
              "conv_b_w": conv_b_w[l], "conv_b_b": conv_b_b[l], "hy_w1": hy_w1[l], "hy_b1": hy_b1[l],
              "hy_w2": hy_w2[l], "hy_b2": hy_b2[l], "hy_freq": hy_freq[l], "hy_w3": hy_w3[l],
              "hy_bias": hy_bias[l], "mix_g": mix_g[l], "w_out": w_out[l],
              "ln1_g": ln1_g[l], "ln1_b": ln1_b[l], "rt_wg": rt_wg[l], "rt_bg": rt_bg[l],
              "rt_we": rt_we[l], "rt_be": rt_be[l], "moe_w1": moe_w1[l], "moe_w3": moe_w3[l],
              "moe_w2": moe_w2[l], "ln2_g": ln2_g[l], "ln2_b": ln2_b[l]}
        xc, st = _layer(xc, _modulation(c_ctx[None], lp), lp, zero_init, True)
        s_lru.append(st[0]); s_C.append(st[1]); s_n.append(st[2]); s_m.append(st[3])
        lat_init = (state_lru[:, l], state_mlstm_C[:, l], state_mlstm_n[:, l], state_mlstm_m[:, l])
        xl, _ = _layer(xl, _modulation(c, lp), lp, lat_init, False)
    new_state_lru = jnp.stack(s_lru, axis=1)
    new_state_mlstm_C = jnp.stack(s_C, axis=1)
    new_state_mlstm_n = jnp.stack(s_n, axis=1)
    new_state_mlstm_m = jnp.stack(s_m, axis=1)
    return (xc, xl, new_state_lru, new_state_mlstm_C, new_state_mlstm_n, new_state_mlstm_m)
```

```python
import functools
import math

import numpy as np
import jax
import jax.numpy as jnp
from jax import lax
from jax.experimental import pallas as pl
from jax.experimental.pallas import tpu as pltpu

F32 = jnp.float32
BF16 = jnp.bfloat16

D_MODEL = 1024
DEPTH = 2
GRID_W = 64
D_A = 256
H_A = 4
BA = D_A // H_A
LRU_C = 8.0
D_B = 256
HY_ORDER = 2
HY_BANDS = 16
HY_EMB = 1 + 2 * HY_BANDS
HY_FH = 64
HY_DECAY_TARGET = 1e-2
D_C = 512
H_C = 4
DK = D_C // H_C
N_GROUPS = 4
E_PER_GROUP = 4
N_EXP = N_GROUPS * E_PER_GROUP
N_PAIRS = 6
N_CLASS = N_GROUPS * N_PAIRS
D_E = 512
ALPHA = (2 * DEPTH) ** 0.25
EPS = 1e-6
D_MAIN = 2 * D_A + 3 * D_B + 4 * D_C
N_GATE = 4 * H_C

LANE = 128
SUB = 8
VMEM_LIMIT = 56 * 1024 * 1024

CH = 256
SEG = 2048
TM = 256
D_EXT = D_MODEL + LANE
FS = 64

I_CLS, I_ELO, I_EHI, I_WLO, I_WHI, I_RANK = range(6)


def _cp(sem, vmem=VMEM_LIMIT):
    return pltpu.CompilerParams(dimension_semantics=sem, vmem_limit_bytes=vmem)


def _dot(a, b):
    return jnp.dot(a, b, preferred_element_type=F32)


def _split2(x):
    hi = x.astype(BF16)
    lo = (x - hi.astype(F32)).astype(BF16)
    return hi, lo


def _dot3(a, b):
    ah, al = _split2(a)
    bh, bl = _split2(b)
    return _dot(ah, bh) + (_dot(ah, bl) + _dot(al, bh))


def _split3(x):
    hi = x.astype(BF16)
    r1 = x - hi.astype(F32)
    mid = r1.astype(BF16)
    lo = (r1 - mid.astype(F32)).astype(BF16)
    return hi, mid, lo


def _sigmoid(x):
    return 1.0 / (1.0 + jnp.exp(-x))


def _log_sigmoid(x):
    return jnp.minimum(x, 0.0) - jnp.log1p(jnp.exp(-jnp.abs(x)))


def _gelu_tanh(x):
    return 0.5 * x * (1.0 + jnp.tanh(math.sqrt(2.0 / math.pi) * (x + 0.044715 * (x * x * x))))


def _ln_plain(x):
    mu = jnp.mean(x, -1, keepdims=True)
    xc = x - mu
    var = jnp.mean(xc * xc, -1, keepdims=True)
    return xc * lax.rsqrt(var + EPS)


def _rms(x):
    return x * lax.rsqrt(jnp.mean(x * x, -1, keepdims=True) + EPS)


def _halo_rows(ref, start, rows):
    total = ref.shape[0]
    prev = ref[pl.ds(pl.multiple_of(jnp.maximum(start - SUB, 0), SUB), SUB), :]
    main = ref[pl.ds(start, rows), :]
    nxt = ref[pl.ds(pl.multiple_of(jnp.minimum(start + rows, total - SUB), SUB), SUB), :]
    return jnp.concatenate([prev, main, nxt], axis=0), main


def _mod_kernel(c_ref, w_ref, b_ref, o_ref):
    c = c_ref[...]
    o_ref[0] = _dot3(c * _sigmoid(c), w_ref[0]) + b_ref[0]


def _mod_call(cond, w_ada, b_ada):
    tn = 1536
    n6 = w_ada.shape[-1]
    return pl.pallas_call(
        _mod_kernel,
        grid=(DEPTH, n6 // tn),
        in_specs=[pl.BlockSpec((SUB, D_MODEL), lambda l, j: (0, 0)),
                  pl.BlockSpec((1, D_MODEL, tn), lambda l, j: (l, 0, j)),
                  pl.BlockSpec((1, 1, tn), lambda l, j: (l, 0, j))],
        out_specs=pl.BlockSpec((1, SUB, tn), lambda l, j: (l, 0, j)),
        out_shape=jax.ShapeDtypeStruct((DEPTH, SUB, n6), F32),
        compiler_params=_cp(("parallel", "parallel")),
        name="adaln_mod",
    )(cond, w_ada, b_ada.reshape(DEPTH, 1, n6))


def _entry_kernel(xc_ref, xl_ref, pos_ref, o_ref, *, n_ctx_tiles):
    i = pl.program_id(0)

    @pl.when(i < n_ctx_tiles)
    def _():
        o_ref[...] = _ln_plain(xc_ref[...])

    @pl.when(i >= n_ctx_tiles)
    def _():
        o_ref[...] = _ln_plain(xl_ref[...] + pos_ref[...])


def _entry_call(xc, xl, pos):
    tm = 512
    n_ctx, n_lat, l_lat = xc.shape[0], xl.shape[0], pos.shape[0]
    nct = n_ctx // tm
    per_seq = l_lat // tm
    return pl.pallas_call(
        functools.partial(_entry_kernel, n_ctx_tiles=nct),
        grid=((n_ctx + n_lat) // tm,),
        in_specs=[pl.BlockSpec((tm, D_MODEL), lambda i: (jnp.minimum(i, nct - 1), 0)),
                  pl.BlockSpec((tm, D_MODEL), lambda i: (jnp.maximum(i - nct, 0), 0)),
                  pl.BlockSpec((tm, D_MODEL), lambda i: (jnp.maximum(i - nct, 0) % per_seq, 0))],
        out_specs=pl.BlockSpec((tm, D_MODEL), lambda i: (i, 0)),
        out_shape=jax.ShapeDtypeStruct((n_ctx + n_lat, D_MODEL), F32),
        compiler_params=_cp(("parallel",)),
        name="entry_ln",
    )(xc, xl, pos)


_Z_CUTS = (0, D_A, 2 * D_A, 2 * D_A + 3 * D_B, 2 * D_A + 3 * D_B + D_C, 2 * D_A + 3 * D_B + 2 * D_C,
           2 * D_A + 3 * D_B + 3 * D_C, D_MAIN)


def _in_kernel(x_ref, mod_ref, w_ref, b_ref, wg_ref, bg_ref, *out_refs):
    m = mod_ref[0]
    h = x_ref[...] * (1.0 + m[1:2]) + m[0:1]
    hb = h.astype(BF16)
    for ref, a, b in zip(out_refs[:-1], _Z_CUTS[:-1], _Z_CUTS[1:]):
        ref[...] = _dot(hb, w_ref[:, a:b]) + b_ref[:, a:b]
    out_refs[-1][...] = _dot3(h, wg_ref[...]) + bg_ref[...]


def _in_call(x, mod_l, w_main, b_main, w_gate, b_gate, group_of_tile):
    n = x.shape[0]
    widths = [b - a for a, b in zip(_Z_CUTS[:-1], _Z_CUTS[1:])] + [LANE]
    row = lambda i: (i, 0)
    fixed = lambda i: (0, 0)
    return pl.pallas_call(
        _in_kernel,
        grid=(n // TM,),
        in_specs=[pl.BlockSpec((TM, D_MODEL), row),
                  pl.BlockSpec((1, 6, D_MODEL), lambda i: (group_of_tile(i), 0, 0)),
                  pl.BlockSpec((D_MODEL, D_MAIN), fixed),
                  pl.BlockSpec((1, D_MAIN), fixed),
                  pl.BlockSpec((D_MODEL, LANE), fixed),
                  pl.BlockSpec((1, LANE), fixed)],
        out_specs=[pl.BlockSpec((TM, w), row) for w in widths],
        out_shape=[jax.ShapeDtypeStruct((n, w), F32) for w in widths],
        compiler_params=_cp(("parallel",)),
        name="in_proj",
    )(x, mod_l, w_main, b_main, w_gate, b_gate)


def _lru_variant(L, xa_ref, ya_ref, cw_ref, cb_ref, wg_ref, bg_ref, lam_ref, h0_ref, o_ref, st_ref,
                 s_af, s_bf, s_ab, s_bb):
    nch, nseq, ntile = SEG // CH, SEG // L, L // SUB
    lam = lam_ref[...]
    sp = jnp.maximum(-lam, 0.0) + jnp.log1p(jnp.exp(-jnp.abs(lam)))
    cw = cw_ref[...]
    cb = cb_ref[...]
    row = lax.broadcasted_iota(jnp.int32, (CH, 1), 0)
    sub = row & (SUB - 1)

    def gates_and_tile_scan(c, carry):
        start = pl.multiple_of(c * CH, CH)
        xcat, main = _halo_rows(xa_ref, start, CH)
        tpos = (start + row) & (L - 1)
        xm2 = jnp.where(tpos >= 2, xcat[SUB - 2:SUB - 2 + CH], 0.0)
        xm1 = jnp.where(tpos >= 1, xcat[SUB - 1:SUB - 1 + CH], 0.0)
        xp1 = jnp.where(tpos <= L - 2, xcat[SUB + 1:SUB + 1 + CH], 0.0)
        xc = cw[0:1] * xm2 + cw[1:2] * xm1 + cw[2:3] * main + cw[3:4] * xp1 + cb
        g = _dot(xc.astype(BF16), wg_ref[...]) + bg_ref[...]
        for d, (sa, sb) in enumerate(((s_af, s_bf), (s_ab, s_bb))):
            r = _sigmoid(g[:, 2 * d * D_A:(2 * d + 1) * D_A])
            ig = _sigmoid(g[:, (2 * d + 1) * D_A:(2 * d + 2) * D_A])
            a = jnp.exp(-LRU_C * r * sp[d:d + 1])
            b = jnp.sqrt(1.0 - a * a) * (ig * xc)
            for s in (1, 2, 4):
                if d == 0:
                    keep = sub >= s
                    a_sh, b_sh = pltpu.roll(a, s, 0), pltpu.roll(b, s, 0)
                else:
                    keep = sub < SUB - s
                    a_sh, b_sh = pltpu.roll(a, CH - s, 0), pltpu.roll(b, CH - s, 0)
                b = a * jnp.where(keep, b_sh, 0.0) + b
                a = a * jnp.where(keep, a_sh, 1.0)
            sa[pl.ds(start, CH), :] = a
            sb[pl.ds(start, CH), :] = b
        return carry

    lax.fori_loop(0, nch, gates_and_tile_scan, 0)

    def carry_tiles(k, carry):
        cf, cbk = carry
        nf, nb = [], []
        for s in range(nseq):
            rf = pl.multiple_of(s * L + k * SUB, SUB)
            hf = s_af[pl.ds(rf, SUB), :] * cf[s] + s_bf[pl.ds(rf, SUB), :]
            s_bf[pl.ds(rf, SUB), :] = hf
            nf.append(hf[SUB - 1:SUB, :])
            rb = pl.multiple_of(s * L + (ntile - 1 - k) * SUB, SUB)
            hb = s_ab[pl.ds(rb, SUB), :] * cbk[s] + s_bb[pl.ds(rb, SUB), :]
            s_bb[pl.ds(rb, SUB), :] = hb
            nb.append(hb[0:1, :])
        return tuple(nf), tuple(nb)

    cf0 = tuple(h0_ref[0, s, 0:1, :] for s in range(nseq))
    cb0 = tuple(h0_ref[0, s, 1:2, :] for s in range(nseq))
    cf, cbk = lax.fori_loop(0, ntile, carry_tiles, (cf0, cb0))

    st_ref[...] = jnp.zeros(st_ref.shape, F32)
    for s in range(nseq):
        st_ref[0, s] = jnp.concatenate([cf[s], cbk[s]], axis=0)

    def finish(c, carry):
        start = pl.multiple_of(c * CH, CH)
        h = s_bf[pl.ds(start, CH), :] + s_bb[pl.ds(start, CH), :]
        o_ref[pl.ds(start, CH), :] = _rms(_gelu_tanh(ya_ref[pl.ds(start, CH), :]) * h)
        return carry

    lax.fori_loop(0, nch, finish, 0)


def _lru_kernel(*refs, l_ctx, l_lat, n_ctx_blk):
    i = pl.program_id(0)

    @pl.when(i < n_ctx_blk)
    def _():
        _lru_variant(l_ctx, *refs)

    @pl.when(i >= n_ctx_blk)
    def _():
        _lru_variant(l_lat, *refs)


def _lru_call(xa, ya, conv_w, conv_b, w_gate, b_gate, lam, h0_all, l_ctx, l_lat, n_ctx_blk):
    n = xa.shape[0]
    nblk = n // SEG
    row = lambda i: (i, 0)
    fixed = lambda i: (0, 0)
    slots = SEG // l_ctx
    return pl.pallas_call(
        functools.partial(_lru_kernel, l_ctx=l_ctx, l_lat=l_lat, n_ctx_blk=n_ctx_blk),
        grid=(nblk,),
        in_specs=[pl.BlockSpec((SEG, D_A), row), pl.BlockSpec((SEG, D_A), row),
                  pl.BlockSpec((4, D_A), fixed), pl.BlockSpec((1, D_A), fixed),
                  pl.BlockSpec((D_A, 4 * D_A), fixed), pl.BlockSpec((1, 4 * D_A), fixed),
                  pl.BlockSpec((2, D_A), fixed),
                  pl.BlockSpec((1, slots, 2, D_A), lambda i: (i, 0, 0, 0))],
        out_specs=[pl.BlockSpec((SEG, D_A), row),
                   pl.BlockSpec((1, slots, 2, D_A), lambda i: (i, 0, 0, 0))],
        out_shape=[jax.ShapeDtypeStruct((n, D_A), F32),
                   jax.ShapeDtypeStruct((nblk, slots, 2, D_A), F32)],
        scratch_shapes=[pltpu.VMEM((SEG, D_A), F32) for _ in range(4)],
        compiler_params=_cp(("parallel",)),
        name="rglru",
    )(xa, ya, conv_w, conv_b, w_gate, b_gate, lam, h0_all)


def _filt_kernel(z_ref, dec_ref, w1_ref, b1_ref, w2_ref, b2_ref, fr_ref, w3_ref, fwd_ref,
                 oab_ref, od0_ref, s_k, s_kf, *, L):
    nblk = 2 * L // CH
    d_idx = pl.program_id(1)
    row = lax.broadcasted_iota(jnp.int32, (CH, 1), 0)

    @pl.when(d_idx == 0)
    def _():
        fr = fr_ref[0]

        def taps(c, carry):
            start = pl.multiple_of(c * CH, CH)
            h1 = jnp.sin(fr * (_dot3(z_ref[pl.ds(start, CH), :], w1_ref[0]) + b1_ref[0]))
            h2 = jnp.sin(fr * (_dot3(h1, w2_ref[0]) + b2_ref[0]))
            t = _dot3(h2, w3_ref[0])
            dec = dec_ref[pl.ds(start, CH), :]
            rg = start + row
            for o in range(HY_ORDER):
                fwd_t = t[:, (2 * o) * D_B:(2 * o + 1) * D_B]
                bwd_t = t[:, (2 * o + 1) * D_B:(2 * o + 2) * D_B]
                ko = jnp.where(rg < L, bwd_t, fwd_t) * dec
                s_k[pl.ds(start, CH), o * D_B:(o + 1) * D_B] = jnp.where(rg == 0, 0.0, ko)
            return carry

        lax.fori_loop(0, nblk, taps, 0)
        fwd = fwd_ref[...]

        def spectra(e, carry):
            start = pl.multiple_of(e * CH, CH)
            s_kf[e] = _dot3(fwd, s_k[pl.ds(start, CH), :])
            return carry

        lax.fori_loop(0, nblk, spectra, 0)

    kd = s_kf[d_idx + 1]
    km = s_kf[d_idx]
    k0 = s_k[pl.ds(pl.multiple_of(d_idx * CH, CH), 1), :]
    sgn = jnp.where((row & 1) == 0, 1.0, -1.0)
    a = kd[:CH] + sgn * (km[:CH] - k0)
    b = jnp.where(row == 0, 0.0, kd[CH:] + sgn * km[CH:])
    hn = kd[CH:CH + 1] + km[CH:CH + 1] - k0
    for o in range(HY_ORDER):
        oab_ref[0, o, 0, 0] = a[:, o * D_B:(o + 1) * D_B]
        oab_ref[0, o, 0, 1] = b[:, o * D_B:(o + 1) * D_B]
        od0_ref[0, o, 0] = jnp.broadcast_to(hn[:, o * D_B:(o + 1) * D_B], (SUB, D_B))


def _filt_call(L, z, dec, w1, b1, w2, b2, fr, w3, fwd32):
    nd = 2 * (L // CH) - 1
    fixed = lambda l, d: (0, 0)
    lay3 = lambda l, d: (l, 0, 0)
    return pl.pallas_call(
        functools.partial(_filt_kernel, L=L),
        grid=(DEPTH, nd),
        in_specs=[pl.BlockSpec((2 * L, LANE), fixed), pl.BlockSpec((2 * L, D_B), fixed),
                  pl.BlockSpec((1, LANE, LANE), lay3), pl.BlockSpec((1, 1, LANE), lay3),
                  pl.BlockSpec((1, LANE, LANE), lay3), pl.BlockSpec((1, 1, LANE), lay3),
                  pl.BlockSpec((1, 1, LANE), lay3),
                  pl.BlockSpec((1, LANE, HY_ORDER * 2 * D_B), lay3),
                  pl.BlockSpec((2 * CH, CH), fixed)],
        out_specs=[pl.BlockSpec((1, HY_ORDER, 1, 2, CH, D_B), lambda l, d: (l, 0, d, 0, 0, 0)),
                   pl.BlockSpec((1, HY_ORDER, 1, SUB, D_B), lambda l, d: (l, 0, d, 0, 0))],
        out_shape=[jax.ShapeDtypeStruct((DEPTH, HY_ORDER, nd, 2, CH, D_B), F32),
                   jax.ShapeDtypeStruct((DEPTH, HY_ORDER, nd, SUB, D_B), F32)],
        scratch_shapes=[pltpu.VMEM((2 * L, HY_ORDER * D_B), F32),
                        pltpu.VMEM((2 * L // CH, 2 * CH, HY_ORDER * D_B), F32)],
        compiler_params=_cp(("parallel", "arbitrary")),
        name=f"hyena_filter_{L}",
    )(z, dec, w1, b1, w2, b2, fr, w3, fwd32)


def _hy_variant(L, o_idx, hy_ref, cw_ref, cb_ref, fwd_ref, inv_ref, hab_ref, hd0_ref, bias_ref, o_ref,
                s_y, s_x, s_u, s_v):
    nch, nseq, P = SEG // CH, SEG // L, L // CH
    row = lax.broadcasted_iota(jnp.int32, (CH, 1), 0)
    frow = lax.broadcasted_iota(jnp.int32, (FS, 1), 0)

    @pl.when(o_idx == 0)
    def _():
        cw = cw_ref[...]
        cb = cb_ref[...]

        def short_conv(c, carry):
            start = pl.multiple_of(c * CH, CH)
            xcat, main = _halo_rows(hy_ref, start, CH)
            tpos = (start + row) & (L - 1)
            xm1 = jnp.where(tpos >= 1, xcat[SUB - 1:SUB - 1 + CH], 0.0)
            xp1 = jnp.where(tpos <= L - 2, xcat[SUB + 1:SUB + 1 + CH], 0.0)
            hc = cw[0:1] * xm1 + cw[1:2] * main + cw[2:3] * xp1 + cb
            s_y[pl.ds(start, CH), :] = hc[:, :D_B]
            s_x[0, pl.ds(start, CH), :] = hc[:, D_B:2 * D_B]
            s_x[1, pl.ds(start, CH), :] = hc[:, 2 * D_B:]
            return carry

        lax.fori_loop(0, nch, short_conv, 0)

    bias = bias_ref[0]

    def one_sequence(s, carry):
        base = s * L

        def forward_dft(j, cc):
            r = pl.multiple_of(base + j * CH, CH)
            s_u[j] = _dot(fwd_ref[...], s_y[pl.ds(r, CH), :].astype(BF16))
            return cc

        lax.fori_loop(0, P, forward_dft, 0)

        def output_block(i, cc):
            for fs in range(CH // FS):
                lo = fs * FS

                def accumulate(j, acc):
                    yre, yim = acc
                    d = i - j + (P - 1)
                    ure = s_u[j, lo:lo + FS, :]
                    uim = s_u[j, CH + lo:CH + lo + FS, :]
                    a = hab_ref[0, d, 0, lo:lo + FS, :]
                    b = hab_ref[0, d, 1, lo:lo + FS, :]
                    dd = jnp.where(frow == 0, hd0_ref[0, d, 0:1, :], a) if fs == 0 else a
                    return yre + ure * a - uim * b, yim + ure * b + uim * dd

                zero = jnp.zeros((FS, D_B), F32)
                yre, yim = lax.fori_loop(0, P, accumulate, (zero, zero))
                s_v[lo:lo + FS, :] = yre.astype(BF16)
                s_v[CH + lo:CH + lo + FS, :] = yim.astype(BF16)
            yc = _dot(inv_ref[...], s_v[...])
            r = pl.multiple_of(base + i * CH, CH)
            s_y[pl.ds(r, CH), :] = s_x[o_idx, pl.ds(r, CH), :] * (yc + s_y[pl.ds(r, CH), :] * bias)
            return cc

        lax.fori_loop(0, P, output_block, 0)
        return carry

    lax.fori_loop(0, nseq, one_sequence, 0)

    @pl.when(o_idx == HY_ORDER - 1)
    def _():
        def finish(c, carry):
            start = pl.multiple_of(c * CH, CH)
            o_ref[pl.ds(start, CH), :] = _rms(s_y[pl.ds(start, CH), :])
            return carry

        lax.fori_loop(0, nch, finish, 0)


def _hy_kernel(hy_ref, cw_ref, cb_ref, fwd_ref, inv_ref, habc_ref, hd0c_ref, habl_ref, hd0l_ref, bias_ref,
               o_ref, s_y, s_x, s_u, s_v, *, l_ctx, l_lat, n_ctx_blk):
    i = pl.program_id(0)
    o_idx = pl.program_id(1)
    scratch = (s_y, s_x, s_u, s_v)

    @pl.when(i < n_ctx_blk)
    def _():
        _hy_variant(l_ctx, o_idx, hy_ref, cw_ref, cb_ref, fwd_ref, inv_ref, habc_ref, hd0c_ref, bias_ref, o_ref,
                    *scratch)

    @pl.when(i >= n_ctx_blk)
    def _():
        _hy_variant(l_lat, o_idx, hy_ref, cw_ref, cb_ref, fwd_ref, inv_ref, habl_ref, hd0l_ref, bias_ref, o_ref,
                    *scratch)


def _hy_call(hyb, conv_w, conv_b, fwd, inv, habc, hd0c, habl, hd0l, bias, l_ctx, l_lat, n_ctx_blk):
    n = hyb.shape[0]
    ndc, ndl = habc.shape[1], habl.shape[1]
    pmax = max(l_ctx, l_lat) // CH
    row = lambda i, o: (i, 0)
    fixed = lambda i, o: (0, 0)
    lat_o = lambda i, o: jnp.where(i >= n_ctx_blk, o, 0)
    ctx_o = lambda i, o: jnp.where(i < n_ctx_blk, o, 0)
    return pl.pallas_call(
        functools.partial(_hy_kernel, l_ctx=l_ctx, l_lat=l_lat, n_ctx_blk=n_ctx_blk),
        grid=(n // SEG, HY_ORDER),
        in_specs=[pl.BlockSpec((SEG, 3 * D_B), row),
                  pl.BlockSpec((3, 3 * D_B), fixed), pl.BlockSpec((1, 3 * D_B), fixed),
                  pl.BlockSpec((2 * CH, CH), fixed), pl.BlockSpec((CH, 2 * CH), fixed),
                  pl.BlockSpec((1, ndc, 2, CH, D_B), lambda i, o: (ctx_o(i, o), 0, 0, 0, 0)),
                  pl.BlockSpec((1, ndc, SUB, D_B), lambda i, o: (ctx_o(i, o), 0, 0, 0)),
                  pl.BlockSpec((1, ndl, 2, CH, D_B), lambda i, o: (lat_o(i, o), 0, 0, 0, 0)),
                  pl.BlockSpec((1, ndl, SUB, D_B), lambda i, o: (lat_o(i, o), 0, 0, 0)),
                  pl.BlockSpec((1, 1, D_B), lambda i, o: (o, 0, 0))],
        out_specs=pl.BlockSpec((SEG, D_B), row),
        out_shape=jax.ShapeDtypeStruct((n, D_B), F32),
        scratch_shapes=[pltpu.VMEM((SEG, D_B), F32), pltpu.VMEM((HY_ORDER, SEG, D_B), F32),
                        pltpu.VMEM((pmax, 2 * CH, D_B), F32), pltpu.VMEM((2 * CH, D_B), BF16)],
        compiler_params=_cp(("parallel", "arbitrary")),
        name="hyena",
    )(hyb, conv_w, conv_b, fwd, inv, habc, hd0c, habl, hd0l, bias)


def _mlstm_dir(d, q_ref, k_ref, v_ref, g_ref, h_ref, s_c, s_n, s_m):
    T = CH
    g = g_ref[...]
    gt = g.T
    ii = lax.broadcasted_iota(jnp.int32, (T, T), 0)
    jj = lax.broadcasted_iota(jnp.int32, (T, T), 1)
    tri = (jj <= ii) if d == 0 else (jj >= ii)
    tri_t = (ii <= jj) if d == 0 else (ii >= jj)
    m_col = jnp.where(tri, 1.0, 0.0).astype(BF16)
    m_row = jnp.where(tri_t, 1.0, 0.0).astype(BF16)
    b_cols = sum(_dot(m_col, p) for p in _split3(_log_sigmoid(g)))
    b_rows = sum(_dot(p, m_row) for p in _split3(_log_sigmoid(gt)))
    last = T - 1 if d == 0 else 0
    for h in range(H_C):
        ic, fc = 2 * d * H_C + h, (2 * d + 1) * H_C + h
        idx = d * H_C + h
        q = q_ref[:, h * DK:(h + 1) * DK] * (DK ** -0.5)
        k = k_ref[:, h * DK:(h + 1) * DK]
        v = v_ref[:, h * DK:(h + 1) * DK]
        qb, vb = q.astype(BF16), v.astype(BF16)
        b_col, b_row = b_cols[:, fc:fc + 1], b_rows[fc:fc + 1, :]
        ig_col, ig_row = g[:, ic:ic + 1], gt[ic:ic + 1, :]
        m_old = s_m[idx:idx + 1, 0:1]
        c_old = s_c[idx]
        n_old = s_n[idx:idx + 1, :]
        dm = jnp.where(tri, b_col - b_row + ig_row, -jnp.inf)
        inter = b_col + m_old
        mj = jnp.maximum(inter, jnp.max(dm, -1, keepdims=True))
        wd = jnp.exp(dm - mj)
        wi = jnp.exp(inter - mj)
        s = lax.dot_general(qb, k.astype(BF16), (((1,), (1,)), ((), ())), preferred_element_type=F32) * wd
        num = _dot(s.astype(BF16), vb) + wi * _dot(qb, c_old.astype(BF16))
        den = jnp.sum(s, -1, keepdims=True) + wi * jnp.sum(q * n_old, -1, keepdims=True)
        h_ref[:, h * DK:(h + 1) * DK] = num / jnp.maximum(jnp.abs(den), jnp.exp(-mj))
        b_t = b_col[last:last + 1, :]
        ds_row = b_t - b_row + ig_row
        m_new = jnp.maximum(b_t + m_old, jnp.max(ds_row, -1, keepdims=True))
        wc = jnp.exp(b_t + m_old - m_new)
        kw = k * jnp.exp(b_t - b_col + ig_col - m_new)
        s_c[idx] = wc * c_old + _dot(kw.T.astype(BF16), vb)
        s_n[idx:idx + 1, :] = wc * n_old + jnp.sum(kw, 0, keepdims=True)
        s_m[idx:idx + 1, :] = jnp.broadcast_to(m_new, (1, LANE))


def _mlstm_kernel(qf, kf, vf, gf, qb, kb, vb, gb, c0_ref, n0_ref, m0_ref, hf_ref, hb_ref, co_ref, no_ref, mo_ref,
                  s_c, s_n, s_m, *, n_ctx_steps, nc_lat):
    t = pl.program_id(0)
    is_ctx = t < n_ctx_steps

    @pl.when(is_ctx)
    def _():
        s_c[...] = jnp.zeros(s_c.shape, F32)
        s_n[...] = jnp.zeros(s_n.shape, F32)
        s_m[...] = jnp.zeros(s_m.shape, F32)

    @pl.when(jnp.logical_and(jnp.logical_not(is_ctx), (t - n_ctx_steps) % nc_lat == 0))
    def _():
        s_c[...] = c0_ref[0]
        s_n[...] = n0_ref[0]
        s_m[...] = m0_ref[0]

    _mlstm_dir(0, qf, kf, vf, gf, hf_ref, s_c, s_n, s_m)
    _mlstm_dir(1, qb, kb, vb, gb, hb_ref, s_c, s_n, s_m)

    @pl.when(is_ctx)
    def _():
        co_ref[0] = s_c[...]
        no_ref[0] = s_n[...]
        mo_ref[0] = s_m[...]


def _mlstm_call(q, k, v, gates, c0, n0, m0, n_ctx_steps, nc_lat):
    n = q.shape[0]
    steps = n // CH
    nst = 2 * H_C

    def bwd_blk(t):
        r = jnp.maximum(t - n_ctx_steps, 0)
        lat = n_ctx_steps + (r // nc_lat) * nc_lat + (nc_lat - 1 - r % nc_lat)
        return jnp.where(t < n_ctx_steps, t, lat)

    lat_b = lambda t: jnp.maximum(t - n_ctx_steps, 0) // nc_lat
    ctx_b = lambda t: jnp.minimum(t, n_ctx_steps - 1)
    fwd_spec = lambda w: pl.BlockSpec((CH, w), lambda t: (t, 0))
    bwd_spec = lambda w: pl.BlockSpec((CH, w), lambda t: (bwd_blk(t), 0))
    return pl.pallas_call(
        functools.partial(_mlstm_kernel, n_ctx_steps=n_ctx_steps, nc_lat=nc_lat),
        grid=(steps,),
        in_specs=[fwd_spec(D_C), fwd_spec(D_C), fwd_spec(D_C), fwd_spec(LANE),
                  bwd_spec(D_C), bwd_spec(D_C), bwd_spec(D_C), bwd_spec(LANE),
                  pl.BlockSpec((1, nst, DK, DK), lambda t: (lat_b(t), 0, 0, 0)),
                  pl.BlockSpec((1, nst, DK), lambda t: (lat_b(t), 0, 0)),
                  pl.BlockSpec((1, nst, LANE), lambda t: (lat_b(t), 0, 0))],
        out_specs=[fwd_spec(D_C), bwd_spec(D_C),
                   pl.BlockSpec((1, nst, DK, DK), lambda t: (ctx_b(t), 0, 0, 0)),
                   pl.BlockSpec((1, nst, DK), lambda t: (ctx_b(t), 0, 0)),
                   pl.BlockSpec((1, nst, LANE), lambda t: (ctx_b(t), 0, 0))],
        out_shape=[jax.ShapeDtypeStruct((n, D_C), F32), jax.ShapeDtypeStruct((n, D_C), F32),
                   jax.ShapeDtypeStruct((n_ctx_steps, nst, DK, DK), F32),
                   jax.ShapeDtypeStruct((n_ctx_steps, nst, DK), F32),
                   jax.ShapeDtypeStruct((n_ctx_steps, nst, LANE), F32)],
        scratch_shapes=[pltpu.VMEM((nst, DK, DK), F32), pltpu.VMEM((nst, DK), F32),
                        pltpu.VMEM((nst, LANE), F32)],
        compiler_params=_cp(("arbitrary",)),
        name="mlstm",
    )(q, k, v, gates, q, k, v, gates, c0, n0, m0)


def _out_kernel(x_ref, oa_ref, ob_ref, hf_ref, hb_ref, og_ref, mod_ref, mg_ref, w_ref, g_ref, b_ref,
                rw_ref, rb_ref, x1_ref, he_ref, cnt_ref, s_cnt):
    i = pl.program_id(0)

    @pl.when(i == 0)
    def _():
        s_cnt[...] = jnp.zeros(s_cnt.shape, F32)

    m = mod_ref[0]
    mg = mg_ref[...]
    acc = _dot((oa_ref[...] * mg[:, :D_A]).astype(BF16), w_ref[0:D_A, :])
    acc += _dot((ob_ref[...] * mg[:, D_A:D_A + D_B]).astype(BF16), w_ref[D_A:D_A + D_B, :])
    hc = hf_ref[...] + hb_ref[...]
    og = og_ref[...]
    off = D_A + D_B
    for h in range(H_C):
        sl = slice(h * DK, (h + 1) * DK)
        oc = _sigmoid(og[:, sl]) * _rms(hc[:, sl]) * mg[:, off + h * DK:off + (h + 1) * DK]
        acc += _dot(oc.astype(BF16), w_ref[off + h * DK:off + (h + 1) * DK, :])
    x1 = _ln_plain(ALPHA * x_ref[...] + m[2:3] * acc) * g_ref[...] + b_ref[...]
    x1_ref[...] = x1
    h2 = x1 * (1.0 + m[4:5]) + m[3:4]
    he_ref[:, :D_MODEL] = h2

    lg = _dot3(h2, rw_ref[...]) + rb_ref[...]
    col = lax.broadcasted_iota(jnp.int32, lg.shape, 1)
    ninf = -jnp.inf
    lgm = jnp.where(col < N_GROUPS, lg, ninf)
    mx = jnp.max(lgm, -1, keepdims=True)
    gi = jnp.min(jnp.where(lgm == mx, col, LANE), -1, keepdims=True)
    pg_top = 1.0 / jnp.sum(jnp.where(col < N_GROUPS, jnp.exp(lg - mx), 0.0), -1, keepdims=True)
    lo4 = N_GROUPS + E_PER_GROUP * gi
    lem = jnp.where(jnp.logical_and(col >= lo4, col < lo4 + E_PER_GROUP), lg, ninf)
    v1 = jnp.max(lem, -1, keepdims=True)
    i1 = jnp.min(jnp.where(lem == v1, col, LANE), -1, keepdims=True)
    lem2 = jnp.where(col == i1, ninf, lem)
    v2 = jnp.max(lem2, -1, keepdims=True)
    i2 = jnp.min(jnp.where(lem2 == v2, col, LANE), -1, keepdims=True)
    e21 = jnp.exp(v2 - v1)
    w1 = pg_top / (1.0 + e21)
    w2 = pg_top * e21 / (1.0 + e21)
    e1, e2 = i1 - N_GROUPS, i2 - N_GROUPS
    first_lo = e1 < e2
    elo, ehi = jnp.minimum(e1, e2), jnp.maximum(e1, e2)
    wlo, whi = jnp.where(first_lo, w1, w2), jnp.where(first_lo, w2, w1)
    llo, lhi = elo - E_PER_GROUP * gi, ehi - E_PER_GROUP * gi
    cls = gi * N_PAIRS + ((llo * (7 - llo)) >> 1) + lhi - llo - 1

    oh = jnp.where(col == cls, 1.0, 0.0)
    ii = lax.broadcasted_iota(jnp.int32, (TM, TM), 0)
    jj = lax.broadcasted_iota(jnp.int32, (TM, TM), 1)
    before = jnp.where(jj < ii, 1.0, 0.0).astype(BF16)
    cnt = s_cnt[0:1, :]
    rank = jnp.sum(oh * (_dot(before, oh.astype(BF16)) + cnt), -1, keepdims=True)
    cnt = cnt + jnp.sum(oh, 0, keepdims=True)
    s_cnt[...] = jnp.broadcast_to(cnt, s_cnt.shape)
    cnt_ref[...] = jnp.broadcast_to(cnt, cnt_ref.shape)

    info = jnp.zeros(lg.shape, F32)
    for c, val in ((I_CLS, cls.astype(F32)), (I_ELO, elo.astype(F32)), (I_EHI, ehi.astype(F32)),
                   (I_WLO, wlo), (I_WHI, whi), (I_RANK, rank)):
        info = jnp.where(col == c, val, info)
    he_ref[:, D_MODEL:] = info


def _out_call(x, out_a, out_b, hcf, hcb, ogate, mod_l, mix_g, w_out, ln_g, ln_b, rt_w, rt_b, group_of_tile):
    n = x.shape[0]
    row = lambda i: (i, 0)
    fixed = lambda i: (0, 0)
    return pl.pallas_call(
        _out_kernel,
        grid=(n // TM,),
        in_specs=[pl.BlockSpec((TM, D_MODEL), row), pl.BlockSpec((TM, D_A), row), pl.BlockSpec((TM, D_B), row),
                  pl.BlockSpec((TM, D_C), row), pl.BlockSpec((TM, D_C), row), pl.BlockSpec((TM, D_C), row),
                  pl.BlockSpec((1, 6, D_MODEL), lambda i: (group_of_tile(i), 0, 0)),
                  pl.BlockSpec((1, D_MODEL), fixed), pl.BlockSpec((D_MODEL, D_MODEL), fixed),
                  pl.BlockSpec((1, D_MODEL), fixed), pl.BlockSpec((1, D_MODEL), fixed),
                  pl.BlockSpec((D_MODEL, LANE), fixed), pl.BlockSpec((1, LANE), fixed)],
        out_specs=[pl.BlockSpec((TM, D_MODEL), row), pl.BlockSpec((TM, D_EXT), row),
                   pl.BlockSpec((SUB, LANE), fixed)],
        out_shape=[jax.ShapeDtypeStruct((n, D_MODEL), F32), jax.ShapeDtypeStruct((n, D_EXT), F32),
                   jax.ShapeDtypeStruct((SUB, LANE), F32)],
        scratch_shapes=[pltpu.VMEM((SUB, LANE), F32)],
        compiler_params=_cp(("arbitrary",)),
        name="out_proj_router",
    )(x, out_a, out_b, hcf, hcb, ogate, mod_l, mix_g, w_out, ln_g, ln_b, rt_w, rt_b)


def _row_copy(src_ref, src_row, dst_ref, dst_row, sem):
    return pltpu.make_async_copy(src_ref.at[pl.ds(src_row, 1), :], dst_ref.at[pl.ds(dst_row, 1), :], sem)


def _scatter_kernel(dest_ref, x_ref, init_ref, o_ref, sem):
    del init_ref
    base = pl.program_id(0) * TM

    def start(r, carry):
        _row_copy(x_ref, r, o_ref, dest_ref[base + r], sem).start()
        return carry

    lax.fori_loop(0, TM, start, 0)

    def wait(r, carry):
        _row_copy(x_ref, r, o_ref, dest_ref[base + r], sem).wait()
        return carry

    lax.fori_loop(0, TM, wait, 0)


def _scatter_call(dest, h_ext, zeros_sorted):
    n = h_ext.shape[0]
    return pl.pallas_call(
        _scatter_kernel,
        grid_spec=pltpu.PrefetchScalarGridSpec(
            num_scalar_prefetch=1, grid=(n // TM,),
            in_specs=[pl.BlockSpec((TM, D_EXT), lambda i, d: (i, 0)), pl.BlockSpec(memory_space=pl.ANY)],
            out_specs=pl.BlockSpec(memory_space=pl.ANY),
            scratch_shapes=[pltpu.SemaphoreType.DMA(())]),
        out_shape=jax.ShapeDtypeStruct(zeros_sorted.shape, F32),
        input_output_aliases={2: 0},
        compiler_params=_cp(("arbitrary",)),
        name="moe_scatter",
    )(dest, h_ext, zeros_sorted)


def _moe_kernel(ta_ref, tb_ref, na_ref, x_ref, w1a, w3a, w2a, w1b, w3b, w2b, o_ref):
    del ta_ref, tb_ref
    active = pl.program_id(0) < na_ref[0]

    @pl.when(jnp.logical_not(active))
    def _():
        o_ref[...] = jnp.zeros(o_ref.shape, F32)

    @pl.when(active)
    def _():
        xe = x_ref[...]
        x = xe[:, :D_MODEL].astype(BF16)

        def expert(w1, w3, w2, gate):
            a = _dot(x, w1[0])
            hm = a * _sigmoid(a) * _dot(x, w3[0]) * gate
            return _dot(hm.astype(BF16), w2[0])

        o_ref[...] = (expert(w1a, w3a, w2a, xe[:, D_MODEL + I_WLO:D_MODEL + I_WLO + 1])
                      + expert(w1b, w3b, w2b, xe[:, D_MODEL + I_WHI:D_MODEL + I_WHI + 1]))


def _moe_call(tile_a, tile_b, n_act, x_sorted, w1, w3, w2):
    r = x_sorted.shape[0]
    act = lambda t, ta, tb, na: (jnp.minimum(t, na[0] - 1), 0)
    ea = lambda t, ta, tb, na: (ta[t], 0, 0)
    eb = lambda t, ta, tb, na: (tb[t], 0, 0)
    up = lambda m: pl.BlockSpec((1, D_MODEL, D_E), m)
    down = lambda m: pl.BlockSpec((1, D_E, D_MODEL), m)
    return pl.pallas_call(
        _moe_kernel,
        grid_spec=pltpu.PrefetchScalarGridSpec(
            num_scalar_prefetch=3, grid=(r // TM,),
            in_specs=[pl.BlockSpec((TM, D_EXT), act), up(ea), up(ea), down(ea), up(eb), up(eb), down(eb)],
            out_specs=pl.BlockSpec((TM, D_MODEL), lambda t, ta, tb, na: (t, 0))),
        out_shape=jax.ShapeDtypeStruct((r, D_MODEL), F32),
        compiler_params=_cp(("arbitrary",)),
        name="moe_experts",
    )(tile_a, tile_b, n_act, x_sorted, w1, w3, w2, w1, w3, w2)


def _ln2_kernel(dest_ref, x1_ref, mod_ref, g_ref, b_ref, y_ref, o_ref, buf, sem):
    base = pl.program_id(0) * TM

    def start(r, carry):
        _row_copy(y_ref, dest_ref[base + r], buf, r, sem).start()
        return carry

    lax.fori_loop(0, TM, start, 0)

    def wait(r, carry):
        _row_copy(y_ref, dest_ref[base + r], buf, r, sem).wait()
        return carry

    lax.fori_loop(0, TM, wait, 0)
    m = mod_ref[0]
    o_ref[...] = _ln_plain(ALPHA * x1_ref[...] + m[5:6] * buf[...]) * g_ref[...] + b_ref[...]


def _ln2_call(dest, x1, mod_l, ln_g, ln_b, y_sorted, group_of_tile):
    n = x1.shape[0]
    return pl.pallas_call(
        _ln2_kernel,
        grid_spec=pltpu.PrefetchScalarGridSpec(
            num_scalar_prefetch=1, grid=(n // TM,),
            in_specs=[pl.BlockSpec((TM, D_MODEL), lambda i, d: (i, 0)),
                      pl.BlockSpec((1, 6, D_MODEL), lambda i, d: (group_of_tile(i), 0, 0)),
                      pl.BlockSpec((1, D_MODEL), lambda i, d: (0, 0)),
                      pl.BlockSpec((1, D_MODEL), lambda i, d: (0, 0)),
                      pl.BlockSpec(memory_space=pl.ANY)],
            out_specs=pl.BlockSpec((TM, D_MODEL), lambda i, d: (i, 0)),
            scratch_shapes=[pltpu.VMEM((TM, D_MODEL), F32), pltpu.SemaphoreType.DMA(())]),
        out_shape=jax.ShapeDtypeStruct((n, D_MODEL), F32),
        compiler_params=_cp(("arbitrary",)),
        name="moe_gather_ln2",
    )(dest, x1, mod_l, ln_g, ln_b, y_sorted)


def _dft_matrices():
    n2 = 2 * CH
    f = np.arange(CH, dtype=np.float64)[:, None]
    t = np.arange(CH, dtype=np.float64)[None, :]
    ang = 2.0 * np.pi * f * t / n2
    re, im = np.cos(ang), -np.sin(ang)
    im[0, :] = np.cos(np.pi * t[0])
    fwd = np.concatenate([re, im], axis=0)
    scale = np.full((CH, 1), 2.0 / n2)
    scale[0, 0] = 1.0 / n2
    inv = np.concatenate([(re * scale).T, (im * scale).T], axis=1)
    return fwd.astype(np.float32), inv.astype(np.float32)


def _filter_features(L):
    lag = np.arange(-L, L)
    m = np.minimum(np.abs(lag), L - 1)
    t = (np.arange(L, dtype=np.float32) / np.float32(max(L - 1, 1)))[m]
    w = (np.float32(2.0 * math.pi) * np.arange(L, dtype=np.float32) / np.float32(L))[m]
    bands = np.linspace(1e-4, HY_BANDS - 1, HY_BANDS, dtype=np.float32)
    z = np.zeros((2 * L, LANE), np.float32)
    z[:, 0] = t
    z[:, 1:1 + HY_BANDS] = np.cos(w[:, None] * bands)
    z[:, 1 + HY_BANDS:HY_EMB] = -np.sin(w[:, None] * bands)
    lo, hi = math.log(HY_DECAY_TARGET) / 1.5, math.log(HY_DECAY_TARGET) / 0.3
    deltas = np.abs(np.linspace(lo, hi, D_B, dtype=np.float32))
    dec = np.exp(-t[:, None] * deltas)
    return z, dec.astype(np.float32)


def _sincos_2d(rows, cols):
    quarter = D_MODEL // 4
    omega = 1.0 / (10000.0 ** (jnp.arange(quarter, dtype=F32) / quarter))

    def emb(n):
        ang = jnp.arange(n, dtype=F32)[:, None] * omega[None]
        return jnp.concatenate([jnp.sin(ang), jnp.cos(ang)], -1)

    er, ec = emb(rows), emb(cols)
    half = D_MODEL // 2
    pos = jnp.concatenate([jnp.broadcast_to(er[:, None], (rows, cols, half)),
                           jnp.broadcast_to(ec[None], (rows, cols, half))], -1)
    return pos.reshape(rows * cols, D_MODEL)


def _pad_to(x, shape):
    return jnp.pad(x, [(0, s - d) for d, s in zip(x.shape, shape)])


def _block_diag(w):
    eye = jnp.eye(H_A, dtype=w.dtype)
    return jnp.einsum("hij,hg->higj", w, eye).reshape(D_A, D_A)


_PAIR_LO = np.array([0, 0, 0, 1, 1, 2], np.int32)
_PAIR_HI = np.array([1, 2, 3, 2, 3, 3], np.int32)


def _routing_plan(info, counts, n_tiles_max):
    cls = info[:, I_CLS].astype(jnp.int32)
    rank = info[:, I_RANK].astype(jnp.int32)
    cnt = counts[0, :N_CLASS].astype(jnp.int32)
    tiles = (cnt + TM - 1) // TM
    tile_end = jnp.cumsum(tiles)
    row_start = (tile_end - tiles) * TM
    dest = row_start[cls] + rank
    n_act = tile_end[-1]
    t = jnp.minimum(jnp.arange(n_tiles_max, dtype=jnp.int32), n_act - 1)
    tcls = jnp.minimum(jnp.searchsorted(tile_end, t, side="right"), N_CLASS - 1).astype(jnp.int32)
    grp, pair = tcls // N_PAIRS, tcls % N_PAIRS
    tile_a = grp * E_PER_GROUP + jnp.asarray(_PAIR_LO)[pair]
    tile_b = grp * E_PER_GROUP + jnp.asarray(_PAIR_HI)[pair]
    return dest, tile_a.astype(jnp.int32), tile_b.astype(jnp.int32), n_act.reshape(1).astype(jnp.int32)


def kernel(x_prompt, x_sample, c, state_lru, state_mlstm_C, state_mlstm_n, state_mlstm_m, c_ctx, w_ada, b_ada, w_in, b_in, conv_a_w, conv_a_b, lru_wa, lru_ba, lru_wx, lru_bx, lru_lam, conv_b_w, conv_b_b, hy_w1, hy_b1, hy_w2, hy_b2, hy_freq, hy_w3, hy_bias, mix_g, w_out, ln1_g, ln1_b, rt_wg, rt_bg, rt_we, rt_be, moe_w1, moe_w3, moe_w2, ln2_g, ln2_b):
    B, l_ctx, D = x_prompt.shape
    b_lat, l_lat, _ = x_sample.shape
    n_ctx, n_lat = B * l_ctx, b_lat * l_lat
    n = n_ctx + n_lat
    assert D == D_MODEL and w_in.shape[-1] == D_MAIN + N_GATE
    assert SEG % l_ctx == 0 and l_lat == SEG and l_ctx % CH == 0 and n_ctx % SEG == 0
    assert l_ctx == CH, "the mLSTM step schedule assumes one chunk per context sequence"
    assert 1 + b_lat <= SUB
    n_ctx_blk = n_ctx // SEG
    n_ctx_tiles = n_ctx // TM
    tiles_per_lat = l_lat // TM
    nc_lat = l_lat // CH

    def group_of_tile(i):
        return jnp.where(i < n_ctx_tiles, 0, 1 + (i - n_ctx_tiles) // tiles_per_lat)

    cond = jnp.concatenate([c_ctx[None], c, jnp.zeros((SUB - 1 - b_lat, D), F32)], 0)
    mod = _mod_call(cond, w_ada, b_ada).reshape(DEPTH, SUB, 6, D)
    pos = _sincos_2d(l_lat // GRID_W, GRID_W)
    x = _entry_call(x_prompt.reshape(n_ctx, D), x_sample.reshape(n_lat, D), pos)

    fwd_np, inv_np = _dft_matrices()
    fwd32 = jnp.asarray(fwd_np)
    fwd16, inv16 = fwd32.astype(BF16), jnp.asarray(inv_np).astype(BF16)
    fw1 = _pad_to(hy_w1, (DEPTH, LANE, LANE))
    fb1 = _pad_to(hy_b1[:, None, :], (DEPTH, 1, LANE))
    fw2 = _pad_to(hy_w2, (DEPTH, LANE, LANE))
    fb2 = _pad_to(hy_b2[:, None, :], (DEPTH, 1, LANE))
    ffr = _pad_to(hy_freq[:, None, :], (DEPTH, 1, LANE))
    fw3 = _pad_to(hy_w3, (DEPTH, LANE, HY_ORDER * 2 * D_B))
    spectra = {}
    for L in (l_ctx, l_lat):
        z_np, dec_np = _filter_features(L)
        spectra[L] = _filt_call(L, jnp.asarray(z_np), jnp.asarray(dec_np), fw1, fb1, fw2, fb2, ffr, fw3, fwd32)

    lat_slots = SEG // l_ctx
    st_lru, st_c, st_n, st_m = [], [], [], []
    for l in range(DEPTH):
        w_main = w_in[l, :, :D_MAIN].astype(BF16)
        b_main = b_in[l, None, :D_MAIN]
        w_gate = _pad_to(w_in[l, :, D_MAIN:], (D, LANE))
        b_gate = _pad_to(b_in[l, None, D_MAIN:], (1, LANE))
        xa, ya, hyb, q, k, v, og, gates = _in_call(x, mod[l], w_main, b_main, w_gate, b_gate, group_of_tile)

        lru_w = jnp.concatenate([_block_diag(lru_wa[l, 0]), _block_diag(lru_wx[l, 0]),
                                 _block_diag(lru_wa[l, 1]), _block_diag(lru_wx[l, 1])], 1).astype(BF16)
        lru_b = jnp.concatenate([lru_ba[l, 0], lru_bx[l, 0], lru_ba[l, 1], lru_bx[l, 1]])[None]
        h0_lat = _pad_to(state_lru[:, l][:, None], (b_lat, lat_slots, 2, D_A))
        h0_all = jnp.concatenate([jnp.zeros((n_ctx_blk, lat_slots, 2, D_A), F32), h0_lat], 0)
        out_a, lru_last = _lru_call(xa, ya, conv_a_w[l], conv_a_b[l, None], lru_w, lru_b, lru_lam[l], h0_all,
                                    l_ctx, l_lat, n_ctx_blk)

        habc, hd0c = spectra[l_ctx]
        habl, hd0l = spectra[l_lat]
        out_b = _hy_call(hyb, conv_b_w[l], conv_b_b[l, None], fwd16, inv16, habc[l], hd0c[l], habl[l], hd0l[l],
                         hy_bias[l][:, None, :], l_ctx, l_lat, n_ctx_blk)

        c0 = state_mlstm_C[:, l].reshape(b_lat, 2 * H_C, DK, DK)
        n0 = state_mlstm_n[:, l].reshape(b_lat, 2 * H_C, DK)
        m0 = jnp.broadcast_to(state_mlstm_m[:, l].reshape(b_lat, 2 * H_C, 1), (b_lat, 2 * H_C, LANE))
        hcf, hcb, c_fin, n_fin, m_fin = _mlstm_call(q, k, v, gates, c0, n0, m0, n_ctx // CH, nc_lat)

        rt_w = _pad_to(jnp.concatenate([rt_wg[l], rt_we[l]], 1), (D, LANE))
        rt_b = _pad_to(jnp.concatenate([rt_bg[l], rt_be[l]])[None], (1, LANE))
        x1, h_ext, counts = _out_call(x, out_a, out_b, hcf, hcb, og, mod[l], mix_g[l, None],
                                      w_out[l].astype(BF16), ln1_g[l, None], ln1_b[l, None], rt_w, rt_b,
                                      group_of_tile)

        n_tiles_max = n // TM + N_CLASS
        dest, tile_a, tile_b, n_act = _routing_plan(h_ext[:, D_MODEL:D_MODEL + SUB], counts, n_tiles_max)
        x_sorted = _scatter_call(dest, h_ext, jnp.zeros((n_tiles_max * TM, D_EXT), F32))
        y_sorted = _moe_call(tile_a, tile_b, n_act, x_sorted, moe_w1[l].astype(BF16), moe_w3[l].astype(BF16),
                             moe_w2[l].astype(BF16))
        x = _ln2_call(dest, x1, mod[l], ln2_g[l, None], ln2_b[l, None], y_sorted, group_of_tile)

        st_lru.append(lru_last[:n_ctx_blk].reshape(B, 2, D_A))
        st_c.append(c_fin.reshape(B, 2, H_C, DK, DK))
        st_n.append(n_fin.reshape(B, 2, H_C, DK))
        st_m.append(m_fin[:, :, 0].reshape(B, 2, H_C))

    return (x[:n_ctx].reshape(B, l_ctx, D), x[n_ctx:].reshape(b_lat, l_lat, D),
            jnp.stack(st_lru, 1), jnp.stack(st_c, 1), jnp.stack(st_n, 1), jnp.stack(st_m, 1))
```

```python
import functools
import math

import numpy as np
import jax
import jax.numpy as jnp
from jax import lax
from jax.experimental import pallas as pl
from jax.experimental.pallas import tpu as pltpu

F32 = jnp.float32
BF16 = jnp.bfloat16

D_MODEL = 1024
DEPTH = 2
GRID_W = 64
D_A = 256
H_A = 4
BA = D_A // H_A
LRU_C = 8.0
D_B = 256
HY_ORDER = 2
HY_BANDS = 16
HY_EMB = 1 + 2 * HY_BANDS
HY_FH = 64
HY_DECAY_TARGET = 1e-2
D_C = 512
H_C = 4
DK = D_C // H_C
N_GROUPS = 4
E_PER_GROUP = 4
N_EXP = N_GROUPS * E_PER_GROUP
N_PAIRS = 6
N_CLASS = N_GROUPS * N_PAIRS
D_E = 512
ALPHA = (2 * DEPTH) ** 0.25
EPS = 1e-6
D_MAIN = 2 * D_A + 3 * D_B + 4 * D_C
N_GATE = 4 * H_C

LANE = 128
SUB = 8
VMEM_LIMIT = 56 * 1024 * 1024

CH = 256
SEG = 2048
TM = 256
D_EXT = D_MODEL + LANE
FS = 64
DMA_UNROLL = 8

I_CLS, I_ELO, I_EHI, I_WLO, I_WHI, I_RANK = range(6)


def _cp(sem, vmem=VMEM_LIMIT):
    return pltpu.CompilerParams(dimension_semantics=sem, vmem_limit_bytes=vmem)


def _dot(a, b):
    return jnp.dot(a, b, preferred_element_type=F32)


def _split2(x):
    hi = x.astype(BF16)
    lo = (x - hi.astype(F32)).astype(BF16)
    return hi, lo


def _dot3(a, b):
    ah, al = _split2(a)
    bh, bl = _split2(b)
    return _dot(ah, bh) + (_dot(ah, bl) + _dot(al, bh))


def _split3(x):
    hi = x.astype(BF16)
    r1 = x - hi.astype(F32)
    mid = r1.astype(BF16)
    lo = (r1 - mid.astype(F32)).astype(BF16)
    return hi, mid, lo


def _sigmoid(x):
    return 1.0 / (1.0 + jnp.exp(-x))


def _log_sigmoid(x):
    return jnp.minimum(x, 0.0) - jnp.log1p(jnp.exp(-jnp.abs(x)))


def _gelu_tanh(x):
    return 0.5 * x * (1.0 + jnp.tanh(math.sqrt(2.0 / math.pi) * (x + 0.044715 * (x * x * x))))


def _ln_plain(x):
    mu = jnp.mean(x, -1, keepdims=True)
    xc = x - mu
    var = jnp.mean(xc * xc, -1, keepdims=True)
    return xc * lax.rsqrt(var + EPS)


def _rms(x):
    return x * lax.rsqrt(jnp.mean(x * x, -1, keepdims=True) + EPS)


def _halo_rows(ref, start, rows):
    total = ref.shape[0]
    prev = ref[pl.ds(pl.multiple_of(jnp.maximum(start - SUB, 0), SUB), SUB), :]
    main = ref[pl.ds(start, rows), :]
    nxt = ref[pl.ds(pl.multiple_of(jnp.minimum(start + rows, total - SUB), SUB), SUB), :]
    return jnp.concatenate([prev, main, nxt], axis=0), main


def _mod_kernel(c_ref, w_ref, b_ref, o_ref):
    c = c_ref[...]
    o_ref[0] = _dot3(c * _sigmoid(c), w_ref[0]) + b_ref[0]


def _mod_call(cond, w_ada, b_ada):
    tn = 1536
    n6 = w_ada.shape[-1]
    return pl.pallas_call(
        _mod_kernel,
        grid=(DEPTH, n6 // tn),
        in_specs=[pl.BlockSpec((SUB, D_MODEL), lambda l, j: (0, 0)),
                  pl.BlockSpec((1, D_MODEL, tn), lambda l, j: (l, 0, j)),
                  pl.BlockSpec((1, 1, tn), lambda l, j: (l, 0, j))],
        out_specs=pl.BlockSpec((1, SUB, tn), lambda l, j: (l, 0, j)),
        out_shape=jax.ShapeDtypeStruct((DEPTH, SUB, n6), F32),
        compiler_params=_cp(("parallel", "parallel")),
        name="adaln_mod",
    )(cond, w_ada, b_ada.reshape(DEPTH, 1, n6))


def _entry_kernel(xc_ref, xl_ref, pos_ref, o_ref, *, n_ctx_tiles):
    i = pl.program_id(0)

    @pl.when(i < n_ctx_tiles)
    def _():
        o_ref[...] = _ln_plain(xc_ref[...])

    @pl.when(i >= n_ctx_tiles)
    def _():
        o_ref[...] = _ln_plain(xl_ref[...] + pos_ref[...])


def _entry_call(xc, xl, pos):
    tm = 512
    n_ctx, n_lat, l_lat = xc.shape[0], xl.shape[0], pos.shape[0]
    nct = n_ctx // tm
    per_seq = l_lat // tm
    return pl.pallas_call(
        functools.partial(_entry_kernel, n_ctx_tiles=nct),
        grid=((n_ctx + n_lat) // tm,),
        in_specs=[pl.BlockSpec((tm, D_MODEL), lambda i: (jnp.minimum(i, nct - 1), 0)),
                  pl.BlockSpec((tm, D_MODEL), lambda i: (jnp.maximum(i - nct, 0), 0)),
                  pl.BlockSpec((tm, D_MODEL), lambda i: (jnp.maximum(i - nct, 0) % per_seq, 0))],
        out_specs=pl.BlockSpec((tm, D_MODEL), lambda i: (i, 0)),
        out_shape=jax.ShapeDtypeStruct((n_ctx + n_lat, D_MODEL), F32),
        compiler_params=_cp(("parallel",)),
        name="entry_ln",
    )(xc, xl, pos)


_Z_CUTS = (0, D_A, 2 * D_A, 2 * D_A + 3 * D_B, 2 * D_A + 3 * D_B + D_C, 2 * D_A + 3 * D_B + 2 * D_C,
           2 * D_A + 3 * D_B + 3 * D_C, D_MAIN)


_K_CUT = 4
_ROW_CUTS = tuple(c for i, c in enumerate(zip(_Z_CUTS[:-1], _Z_CUTS[1:])) if i != _K_CUT)
_NT = (((1,), (1,)), ((), ()))


def _in_kernel(x_ref, mod_ref, w_ref, b_ref, wkt_ref, bkt_ref, wg_ref, bg_ref, *out_refs):
    m = mod_ref[0]
    h = x_ref[...] * (1.0 + m[1:2]) + m[0:1]
    hb = h.astype(BF16)
    for ref, (a, b) in zip(out_refs[:-2], _ROW_CUTS):
        ref[...] = _dot(hb, w_ref[:, a:b]) + b_ref[:, a:b]
    g_ref, kt_ref = out_refs[-2:]
    g_ref[...] = _dot3(h, wg_ref[...]) + bg_ref[...]
    kt_ref[...] = lax.dot_general(wkt_ref[...], hb, _NT, preferred_element_type=F32) + bkt_ref[...]


def _in_call(x, mod_l, w_main, b_main, w_kt, b_kt, w_gate, b_gate, group_of_tile):
    n = x.shape[0]
    widths = [b - a for a, b in _ROW_CUTS] + [LANE]
    row = lambda i: (i, 0)
    fixed = lambda i: (0, 0)
    return pl.pallas_call(
        _in_kernel,
        grid=(n // TM,),
        in_specs=[pl.BlockSpec((TM, D_MODEL), row),
                  pl.BlockSpec((1, 6, D_MODEL), lambda i: (group_of_tile(i), 0, 0)),
                  pl.BlockSpec((D_MODEL, D_MAIN), fixed), pl.BlockSpec((1, D_MAIN), fixed),
                  pl.BlockSpec((D_C, D_MODEL), fixed), pl.BlockSpec((D_C, TM), fixed),
                  pl.BlockSpec((D_MODEL, LANE), fixed), pl.BlockSpec((1, LANE), fixed)],
        out_specs=[pl.BlockSpec((TM, w), row) for w in widths] + [pl.BlockSpec((D_C, TM), lambda i: (0, i))],
        out_shape=[jax.ShapeDtypeStruct((n, w), F32) for w in widths] + [jax.ShapeDtypeStruct((D_C, n), F32)],
        compiler_params=_cp(("parallel",)),
        name="in_proj",
    )(x, mod_l, w_main, b_main, w_kt, b_kt, w_gate, b_gate)


def _lru_variant(L, xa_ref, ya_ref, cw_ref, cb_ref, wg_ref, bg_ref, lam_ref, h0_ref, o_ref, st_ref,
                 s_af, s_bf, s_ab, s_bb):
    nch, nseq, ntile = SEG // CH, SEG // L, L // SUB
    lam = lam_ref[...]
    sp = jnp.maximum(-lam, 0.0) + jnp.log1p(jnp.exp(-jnp.abs(lam)))
    cw = cw_ref[...]
    cb = cb_ref[...]
    row = lax.broadcasted_iota(jnp.int32, (CH, 1), 0)
    sub = row & (SUB - 1)

    def gates_and_tile_scan(c, carry):
        start = pl.multiple_of(c * CH, CH)
        xcat, main = _halo_rows(xa_ref, start, CH)
        tpos = (start + row) & (L - 1)
        xm2 = jnp.where(tpos >= 2, xcat[SUB - 2:SUB - 2 + CH], 0.0)
        xm1 = jnp.where(tpos >= 1, xcat[SUB - 1:SUB - 1 + CH], 0.0)
        xp1 = jnp.where(tpos <= L - 2, xcat[SUB + 1:SUB + 1 + CH], 0.0)
        xc = cw[0:1] * xm2 + cw[1:2] * xm1 + cw[2:3] * main + cw[3:4] * xp1 + cb
        g = _dot(xc.astype(BF16), wg_ref[...]) + bg_ref[...]
        for d, (sa, sb) in enumerate(((s_af, s_bf), (s_ab, s_bb))):
            r = _sigmoid(g[:, 2 * d * D_A:(2 * d + 1) * D_A])
            ig = _sigmoid(g[:, (2 * d + 1) * D_A:(2 * d + 2) * D_A])
            a = jnp.exp(-LRU_C * r * sp[d:d + 1])
            b = jnp.sqrt(1.0 - a * a) * (ig * xc)
            for s in (1, 2, 4):
                if d == 0:
                    keep = sub >= s
                    a_sh, b_sh = pltpu.roll(a, s, 0), pltpu.roll(b, s, 0)
                else:
                    keep = sub < SUB - s
                    a_sh, b_sh = pltpu.roll(a, CH - s, 0), pltpu.roll(b, CH - s, 0)
                b = a * jnp.where(keep, b_sh, 0.0) + b
                a = a * jnp.where(keep, a_sh, 1.0)
            sa[pl.ds(start, CH), :] = a
            sb[pl.ds(start, CH), :] = b
        return carry

    lax.fori_loop(0, nch, gates_and_tile_scan, 0)

    def carry_tiles(k, carry):
        cf, cbk = carry
        nf, nb = [], []
        for s in range(nseq):
            rf = pl.multiple_of(s * L + k * SUB, SUB)
            hf = s_af[pl.ds(rf, SUB), :] * cf[s] + s_bf[pl.ds(rf, SUB), :]
            s_bf[pl.ds(rf, SUB), :] = hf
            nf.append(hf[SUB - 1:SUB, :])
            rb = pl.multiple_of(s * L + (ntile - 1 - k) * SUB, SUB)
            hb = s_ab[pl.ds(rb, SUB), :] * cbk[s] + s_bb[pl.ds(rb, SUB), :]
            s_bb[pl.ds(rb, SUB), :] = hb
            nb.append(hb[0:1, :])
        return tuple(nf), tuple(nb)

    cf0 = tuple(h0_ref[0, s, 0:1, :] for s in range(nseq))
    cb0 = tuple(h0_ref[0, s, 1:2, :] for s in range(nseq))
    cf, cbk = lax.fori_loop(0, ntile, carry_tiles, (cf0, cb0))

    st_ref[...] = jnp.zeros(st_ref.shape, F32)
    for s in range(nseq):
        st_ref[0, s] = jnp.concatenate([cf[s], cbk[s]], axis=0)

    def finish(c, carry):
        start = pl.multiple_of(c * CH, CH)
        h = s_bf[pl.ds(start, CH), :] + s_bb[pl.ds(start, CH), :]
        o_ref[pl.ds(start, CH), :] = _rms(_gelu_tanh(ya_ref[pl.ds(start, CH), :]) * h)
        return carry

    lax.fori_loop(0, nch, finish, 0)


def _lru_kernel(*refs, l_ctx, l_lat, n_ctx_blk):
    i = pl.program_id(0)

    @pl.when(i < n_ctx_blk)
    def _():
        _lru_variant(l_ctx, *refs)

    @pl.when(i >= n_ctx_blk)
    def _():
        _lru_variant(l_lat, *refs)


def _lru_call(xa, ya, conv_w, conv_b, w_gate, b_gate, lam, h0_all, l_ctx, l_lat, n_ctx_blk):
    n = xa.shape[0]
    nblk = n // SEG
    row = lambda i: (i, 0)
    fixed = lambda i: (0, 0)
    slots = SEG // l_ctx
    return pl.pallas_call(
        functools.partial(_lru_kernel, l_ctx=l_ctx, l_lat=l_lat, n_ctx_blk=n_ctx_blk),
        grid=(nblk,),
        in_specs=[pl.BlockSpec((SEG, D_A), row), pl.BlockSpec((SEG, D_A), row),
                  pl.BlockSpec((4, D_A), fixed), pl.BlockSpec((1, D_A), fixed),
                  pl.BlockSpec((D_A, 4 * D_A), fixed), pl.BlockSpec((1, 4 * D_A), fixed),
                  pl.BlockSpec((2, D_A), fixed),
                  pl.BlockSpec((1, slots, 2, D_A), lambda i: (i, 0, 0, 0))],
        out_specs=[pl.BlockSpec((SEG, D_A), row),
                   pl.BlockSpec((1, slots, 2, D_A), lambda i: (i, 0, 0, 0))],
        out_shape=[jax.ShapeDtypeStruct((n, D_A), F32),
                   jax.ShapeDtypeStruct((nblk, slots, 2, D_A), F32)],
        scratch_shapes=[pltpu.VMEM((SEG, D_A), F32) for _ in range(4)],
        compiler_params=_cp(("parallel",)),
        name="rglru",
    )(xa, ya, conv_w, conv_b, w_gate, b_gate, lam, h0_all)


def _filt_kernel(z_ref, dec_ref, w1_ref, b1_ref, w2_ref, b2_ref, fr_ref, w3_ref, fwd_ref,
                 oab_ref, od0_ref, s_k, s_kf, *, L):
    nblk = 2 * L // CH
    d_idx = pl.program_id(1)
    row = lax.broadcasted_iota(jnp.int32, (CH, 1), 0)

    @pl.when(d_idx == 0)
    def _():
        fr = fr_ref[0]

        def taps(c, carry):
            start = pl.multiple_of(c * CH, CH)
            h1 = jnp.sin(fr * (_dot3(z_ref[pl.ds(start, CH), :], w1_ref[0]) + b1_ref[0]))
            h2 = jnp.sin(fr * (_dot3(h1, w2_ref[0]) + b2_ref[0]))
            t = _dot3(h2, w3_ref[0])
            dec = dec_ref[pl.ds(start, CH), :]
            rg = start + row
            for o in range(HY_ORDER):
                fwd_t = t[:, (2 * o) * D_B:(2 * o + 1) * D_B]
                bwd_t = t[:, (2 * o + 1) * D_B:(2 * o + 2) * D_B]
                ko = jnp.where(rg < L, bwd_t, fwd_t) * dec
                s_k[pl.ds(start, CH), o * D_B:(o + 1) * D_B] = jnp.where(rg == 0, 0.0, ko)
            return carry

        lax.fori_loop(0, nblk, taps, 0)
        fwd = fwd_ref[...]

        def spectra(e, carry):
            start = pl.multiple_of(e * CH, CH)
            s_kf[e] = _dot3(fwd, s_k[pl.ds(start, CH), :])
            return carry

        lax.fori_loop(0, nblk, spectra, 0)

    kd = s_kf[d_idx + 1]
    km = s_kf[d_idx]
    k0 = s_k[pl.ds(pl.multiple_of(d_idx * CH, CH), 1), :]
    sgn = jnp.where((row & 1) == 0, 1.0, -1.0)
    a = kd[:CH] + sgn * (km[:CH] - k0)
    b = jnp.where(row == 0, 0.0, kd[CH:] + sgn * km[CH:])
    hn = kd[CH:CH + 1] + km[CH:CH + 1] - k0
    for o in range(HY_ORDER):
        oab_ref[0, o, 0, 0] = a[:, o * D_B:(o + 1) * D_B]
        oab_ref[0, o, 0, 1] = b[:, o * D_B:(o + 1) * D_B]
        od0_ref[0, o, 0] = jnp.broadcast_to(hn[:, o * D_B:(o + 1) * D_B], (SUB, D_B))


def _filt_call(L, z, dec, w1, b1, w2, b2, fr, w3, fwd32):
    nd = 2 * (L // CH) - 1
    fixed = lambda l, d: (0, 0)
    lay3 = lambda l, d: (l, 0, 0)
    return pl.pallas_call(
        functools.partial(_filt_kernel, L=L),
        grid=(DEPTH, nd),
        in_specs=[pl.BlockSpec((2 * L, LANE), fixed), pl.BlockSpec((2 * L, D_B), fixed),
                  pl.BlockSpec((1, LANE, LANE), lay3), pl.BlockSpec((1, 1, LANE), lay3),
                  pl.BlockSpec((1, LANE, LANE), lay3), pl.BlockSpec((1, 1, LANE), lay3),
                  pl.BlockSpec((1, 1, LANE), lay3),
                  pl.BlockSpec((1, LANE, HY_ORDER * 2 * D_B), lay3),
                  pl.BlockSpec((2 * CH, CH), fixed)],
        out_specs=[pl.BlockSpec((1, HY_ORDER, 1, 2, CH, D_B), lambda l, d: (l, 0, d, 0, 0, 0)),
                   pl.BlockSpec((1, HY_ORDER, 1, SUB, D_B), lambda l, d: (l, 0, d, 0, 0))],
        out_shape=[jax.ShapeDtypeStruct((DEPTH, HY_ORDER, nd, 2, CH, D_B), F32),
                   jax.ShapeDtypeStruct((DEPTH, HY_ORDER, nd, SUB, D_B), F32)],
        scratch_shapes=[pltpu.VMEM((2 * L, HY_ORDER * D_B), F32),
                        pltpu.VMEM((2 * L // CH, 2 * CH, HY_ORDER * D_B), F32)],
        compiler_params=_cp(("parallel", "arbitrary")),
        name=f"hyena_filter_{L}",
    )(z, dec, w1, b1, w2, b2, fr, w3, fwd32)


def _hy_variant(L, o_idx, hy_ref, cw_ref, cb_ref, fwd_ref, inv_ref, hab_ref, hd0_ref, bias_ref, o_ref,
                s_y, s_x, s_u, s_v):
    nch, nseq, P = SEG // CH, SEG // L, L // CH
    row = lax.broadcasted_iota(jnp.int32, (CH, 1), 0)
    frow = lax.broadcasted_iota(jnp.int32, (FS, 1), 0)

    @pl.when(o_idx == 0)
    def _():
        cw = cw_ref[...]
        cb = cb_ref[...]

        def short_conv(c, carry):
            start = pl.multiple_of(c * CH, CH)
            xcat, main = _halo_rows(hy_ref, start, CH)
            tpos = (start + row) & (L - 1)
            xm1 = jnp.where(tpos >= 1, xcat[SUB - 1:SUB - 1 + CH], 0.0)
            xp1 = jnp.where(tpos <= L - 2, xcat[SUB + 1:SUB + 1 + CH], 0.0)
            hc = cw[0:1] * xm1 + cw[1:2] * main + cw[2:3] * xp1 + cb
            s_y[pl.ds(start, CH), :] = hc[:, :D_B]
            s_x[0, pl.ds(start, CH), :] = hc[:, D_B:2 * D_B]
            s_x[1, pl.ds(start, CH), :] = hc[:, 2 * D_B:]
            return carry

        lax.fori_loop(0, nch, short_conv, 0)

    bias = bias_ref[0]

    def one_sequence(s, carry):
        base = s * L

        def forward_dft(j, cc):
            r = pl.multiple_of(base + j * CH, CH)
            s_u[j] = _dot(fwd_ref[...], s_y[pl.ds(r, CH), :].astype(BF16))
            return cc

        lax.fori_loop(0, P, forward_dft, 0)

        def output_block(i, cc):
            for fs in range(CH // FS):
                lo = fs * FS

                def accumulate(j, acc):
                    yre, yim = acc
                    d = i - j + (P - 1)
                    ure = s_u[j, lo:lo + FS, :]
                    uim = s_u[j, CH + lo:CH + lo + FS, :]
                    a = hab_ref[0, 0, d, 0, lo:lo + FS, :]
                    b = hab_ref[0, 0, d, 1, lo:lo + FS, :]
                    dd = jnp.where(frow == 0, hd0_ref[0, 0, d, 0:1, :], a) if fs == 0 else a
                    return yre + ure * a - uim * b, yim + ure * b + uim * dd

                zero = jnp.zeros((FS, D_B), F32)
                yre, yim = lax.fori_loop(0, P, accumulate, (zero, zero))
                s_v[lo:lo + FS, :] = yre.astype(BF16)
                s_v[CH + lo:CH + lo + FS, :] = yim.astype(BF16)
            yc = _dot(inv_ref[...], s_v[...])
            r = pl.multiple_of(base + i * CH, CH)
            s_y[pl.ds(r, CH), :] = s_x[o_idx, pl.ds(r, CH), :] * (yc + s_y[pl.ds(r, CH), :] * bias)
            return cc

        lax.fori_loop(0, P, output_block, 0)
        return carry

    lax.fori_loop(0, nseq, one_sequence, 0)

    @pl.when(o_idx == HY_ORDER - 1)
    def _():
        def finish(c, carry):
            start = pl.multiple_of(c * CH, CH)
            o_ref[pl.ds(start, CH), :] = _rms(s_y[pl.ds(start, CH), :])
            return carry

        lax.fori_loop(0, nch, finish, 0)


def _hy_kernel(hy_ref, cw_ref, cb_ref, fwd_ref, inv_ref, habc_ref, hd0c_ref, habl_ref, hd0l_ref, bias_ref,
               o_ref, s_y, s_x, s_u, s_v, *, l_ctx, l_lat, n_ctx_blk):
    i = pl.program_id(0)
    o_idx = pl.program_id(1)
    scratch = (s_y, s_x, s_u, s_v)

    @pl.when(i < n_ctx_blk)
    def _():
        _hy_variant(l_ctx, o_idx, hy_ref, cw_ref, cb_ref, fwd_ref, inv_ref, habc_ref, hd0c_ref, bias_ref, o_ref,
                    *scratch)

    @pl.when(i >= n_ctx_blk)
    def _():
        _hy_variant(l_lat, o_idx, hy_ref, cw_ref, cb_ref, fwd_ref, inv_ref, habl_ref, hd0l_ref, bias_ref, o_ref,
                    *scratch)


def _hy_call(layer, hyb, conv_w, conv_b, fwd, inv, habc, hd0c, habl, hd0l, bias, l_ctx, l_lat, n_ctx_blk):
    n = hyb.shape[0]
    ndc, ndl = habc.shape[2], habl.shape[2]
    pmax = max(l_ctx, l_lat) // CH
    row = lambda i, o: (i, 0)
    fixed = lambda i, o: (0, 0)
    lat_o = lambda i, o: jnp.where(i >= n_ctx_blk, o, 0)
    ctx_o = lambda i, o: jnp.where(i < n_ctx_blk, o, 0)
    return pl.pallas_call(
        functools.partial(_hy_kernel, l_ctx=l_ctx, l_lat=l_lat, n_ctx_blk=n_ctx_blk),
        grid=(n // SEG, HY_ORDER),
        in_specs=[pl.BlockSpec((SEG, 3 * D_B), row),
                  pl.BlockSpec((3, 3 * D_B), fixed), pl.BlockSpec((1, 3 * D_B), fixed),
                  pl.BlockSpec((2 * CH, CH), fixed), pl.BlockSpec((CH, 2 * CH), fixed),
                  pl.BlockSpec((1, 1, ndc, 2, CH, D_B), lambda i, o: (layer, ctx_o(i, o), 0, 0, 0, 0)),
                  pl.BlockSpec((1, 1, ndc, SUB, D_B), lambda i, o: (layer, ctx_o(i, o), 0, 0, 0)),
                  pl.BlockSpec((1, 1, ndl, 2, CH, D_B), lambda i, o: (layer, lat_o(i, o), 0, 0, 0, 0)),
                  pl.BlockSpec((1, 1, ndl, SUB, D_B), lambda i, o: (layer, lat_o(i, o), 0, 0, 0)),
                  pl.BlockSpec((1, 1, D_B), lambda i, o: (o, 0, 0))],
        out_specs=pl.BlockSpec((SEG, D_B), row),
        out_shape=jax.ShapeDtypeStruct((n, D_B), F32),
        scratch_shapes=[pltpu.VMEM((SEG, D_B), F32), pltpu.VMEM((HY_ORDER, SEG, D_B), F32),
                        pltpu.VMEM((pmax, 2 * CH, D_B), F32), pltpu.VMEM((2 * CH, D_B), BF16)],
        compiler_params=_cp(("parallel", "arbitrary")),
        name="hyena",
    )(hyb, conv_w, conv_b, fwd, inv, habc, hd0c, habl, hd0l, bias)


def _row_scan(x, op, fill, reverse):
    t = x.shape[0]
    sub = lax.broadcasted_iota(jnp.int32, (t, 1), 0) & (SUB - 1)
    for s in (1, 2, 4):
        if reverse:
            shifted, keep = pltpu.roll(x, t - s, 0), sub < SUB - s
        else:
            shifted, keep = pltpu.roll(x, s, 0), sub >= s
        x = op(x, jnp.where(keep, shifted, fill))
    n_tiles = t // SUB
    out = [None] * n_tiles
    carry = None
    for i in (reversed(range(n_tiles)) if reverse else range(n_tiles)):
        tile = x[i * SUB:(i + 1) * SUB]
        out[i] = tile if carry is None else op(tile, carry)
        carry = out[i][0:1] if reverse else out[i][SUB - 1:SUB]
    return jnp.concatenate(out, axis=0)


_STK_ONE = 3 * SUB


def _mlstm_prep(d, g_ref, m_old):
    T = CH
    reverse = d == 1
    g = g_ref[...]
    if d == 1:
        g = pltpu.roll(g, LANE - 2 * H_C, 1)
    lane = lax.broadcasted_iota(jnp.int32, (1, LANE), 1)
    head = lane < H_C
    b = pltpu.roll(_row_scan(_log_sigmoid(g), jnp.add, 0.0, reverse), LANE - H_C, 1)
    r = jnp.where(head, g - b, 0.0)
    big_m = jnp.maximum(m_old, _row_scan(r, jnp.maximum, -jnp.inf, reverse))
    last = 0 if reverse else T - 1
    m_last = big_m[last:last + 1, :]
    low = lane < SUB
    p0, p1, p2 = (jnp.where(low, p.astype(F32), 0.0) for p in _split3(-big_m))
    cols = (p0 + pltpu.roll(p1, SUB, 1) + pltpu.roll(p2, 2 * SUB, 1)
            + jnp.where(jnp.logical_and(lane >= _STK_ONE, lane < _STK_ONE + SUB), 1.0, 0.0))
    rowid = lax.broadcasted_iota(jnp.int32, (SUB, 1), 0)
    r8 = r.T[0:SUB, :]
    m_last8 = sum(jnp.where(rowid == h, m_last[:, h:h + 1], 0.0) for h in range(H_C))
    ws8 = jnp.where(rowid < H_C, jnp.exp(r8 - m_last8), 0.0)
    return {"r3": [p.astype(F32) for p in _split3(r8)], "ws8": ws8, "wc": jnp.exp(m_old - m_last),
            "m_new": jnp.where(head, b[last:last + 1, :] + m_last, 0.0),
            "wi": jnp.exp(m_old - big_m), "e": jnp.exp(-(b + big_m)), "cols_b": cols.astype(BF16)}


def _mlstm_variant(carry, qf, vf, ktf, gf, qb_, vb_, ktb_, gb_, hf_ref, hb_ref, co_ref, no_ref, mo_ref,
                   s_cx, s_n, s_m):
    T = CH
    ii = lax.broadcasted_iota(jnp.int32, (T, T), 0)
    jj = lax.broadcasted_iota(jnp.int32, (T, T), 1)
    rowid = lax.broadcasted_iota(jnp.int32, (SUB, 1), 0)
    one_col = jnp.where(lax.broadcasted_iota(jnp.int32, (T, DK), 1) == 0, 1.0, 0.0).astype(BF16)
    prep = []
    for d, g_ref in ((0, gf), (1, gb_ if carry else gf)):
        m_old = s_m[d:d + 1, :] if carry else jnp.zeros((1, LANE), F32)
        prep.append(_mlstm_prep(d, g_ref, m_old))
    refs = ((qf, vf, ktf, hf_ref), (qb_, vb_, ktb_, hb_ref))
    for h in range(H_C):
        sl = slice(h * DK, (h + 1) * DK)
        for d in range(2):
            p = prep[d]
            q_ref, v_ref, kt_ref, h_ref = refs[d]
            idx = d * H_C + h
            if carry or d == 0:
                qb = (q_ref[:, sl] * (DK ** -0.5)).astype(BF16)
                kt = kt_ref[sl, :]
                ktb = kt.astype(BF16)
                v_ext = jnp.concatenate([v_ref[:, sl].astype(BF16), one_col], axis=1)
                s_raw = _dot(qb, ktb)
            tri = (jj >= ii) if d == 1 else (jj <= ii)
            sel = jnp.broadcast_to(jnp.where(rowid == h, 1.0, 0.0), (SUB, T))
            rr = sum(jnp.where(rowid == i, piece[h:h + 1, :], 0.0) for i, piece in enumerate(p["r3"]))
            rmat = jnp.concatenate([sel, sel, sel, rr, jnp.zeros((LANE - 4 * SUB, T), F32)], axis=0)
            expo = _dot(p["cols_b"], rmat.astype(BF16))
            s = s_raw * jnp.exp(jnp.where(tri, expo, -jnp.inf))
            intra = _dot(s.astype(BF16), v_ext)
            num, den = intra[:, :DK], intra[:, DK:DK + 1]
            if carry:
                cx = s_cx[idx]
                inter = _dot(qb, cx.astype(BF16))
                wi = p["wi"][:, h:h + 1]
                num, den = num + wi * inter[:, :DK], den + wi * inter[:, DK:DK + 1]
            h_ref[:, sl] = num / jnp.maximum(jnp.abs(den), p["e"][:, h:h + 1])
            upd = _dot((kt * p["ws8"][h:h + 1, :]).astype(BF16), v_ext)
            n_upd = lax.dot_general(p["ws8"].astype(BF16), ktb, _NT, preferred_element_type=F32)[h:h + 1, :]
            if carry:
                wc = p["wc"][:, h:h + 1]
                s_cx[idx] = wc * cx + upd
                s_n[idx:idx + 1, :] = wc * s_n[idx:idx + 1, :] + n_upd
            else:
                co_ref[0, idx] = upd[:, :DK]
                no_ref[0, idx:idx + 1, :] = n_upd
    m_rows = jnp.concatenate([prep[0]["m_new"], prep[1]["m_new"], jnp.zeros((SUB - 2, LANE), F32)], axis=0)
    if carry:
        s_m[...] = m_rows
    else:
        mo_ref[0] = m_rows


def _mlstm_kernel(*refs, n_ctx_steps, nc_lat):
    cx0_ref, n0_ref, m0_ref = refs[8:11]
    s_cx, s_n, s_m = refs[-3:]
    data = refs[:8] + refs[11:]
    t = pl.program_id(0)
    is_ctx = t < n_ctx_steps

    @pl.when(is_ctx)
    def _():
        _mlstm_variant(False, *data)

    @pl.when(jnp.logical_not(is_ctx))
    def _():
        @pl.when((t - n_ctx_steps) % nc_lat == 0)
        def _():
            s_cx[...] = cx0_ref[0]
            s_n[...] = n0_ref[0]
            s_m[...] = m0_ref[0]

        _mlstm_variant(True, *data)


def _mlstm_call(q, v, kt, gates, cx0, n0, m0, n_ctx_steps, nc_lat):
    n = q.shape[0]
    steps = n // CH
    nst = 2 * H_C

    def bwd_blk(t):
        r = jnp.maximum(t - n_ctx_steps, 0)
        return n_ctx_steps + (r // nc_lat) * nc_lat + (nc_lat - 1 - r % nc_lat)

    out_bwd = lambda t: jnp.where(t < n_ctx_steps, t, bwd_blk(t))
    lat_b = lambda t: jnp.maximum(t - n_ctx_steps, 0) // nc_lat
    ctx_b = lambda t: jnp.minimum(t, n_ctx_steps - 1)
    rows = lambda w, blk: pl.BlockSpec((CH, w), lambda t: (blk(t), 0))
    cols = lambda h, blk: pl.BlockSpec((h, CH), lambda t: (0, blk(t)))
    ident = lambda t: t
    return pl.pallas_call(
        functools.partial(_mlstm_kernel, n_ctx_steps=n_ctx_steps, nc_lat=nc_lat),
        grid=(steps,),
        in_specs=[rows(D_C, ident), rows(D_C, ident), cols(D_C, ident), rows(LANE, ident),
                  rows(D_C, bwd_blk), rows(D_C, bwd_blk), cols(D_C, bwd_blk), rows(LANE, bwd_blk),
                  pl.BlockSpec((1, nst, DK, 2 * DK), lambda t: (lat_b(t), 0, 0, 0)),
                  pl.BlockSpec((1, nst, DK), lambda t: (lat_b(t), 0, 0)),
                  pl.BlockSpec((1, SUB, LANE), lambda t: (lat_b(t), 0, 0))],
        out_specs=[rows(D_C, ident), rows(D_C, out_bwd),
                   pl.BlockSpec((1, nst, DK, DK), lambda t: (ctx_b(t), 0, 0, 0)),
                   pl.BlockSpec((1, nst, DK), lambda t: (ctx_b(t), 0, 0)),
                   pl.BlockSpec((1, SUB, LANE), lambda t: (ctx_b(t), 0, 0))],
        out_shape=[jax.ShapeDtypeStruct((n, D_C), F32), jax.ShapeDtypeStruct((n, D_C), F32),
                   jax.ShapeDtypeStruct((n_ctx_steps, nst, DK, DK), F32),
                   jax.ShapeDtypeStruct((n_ctx_steps, nst, DK), F32),
                   jax.ShapeDtypeStruct((n_ctx_steps, SUB, LANE), F32)],
        scratch_shapes=[pltpu.VMEM((nst, DK, 2 * DK), F32), pltpu.VMEM((nst, DK), F32),
                        pltpu.VMEM((SUB, LANE), F32)],
        compiler_params=_cp(("arbitrary",)),
        name="mlstm",
    )(q, v, kt, gates, q, v, kt, gates, cx0, n0, m0)


def _out_kernel(x_ref, oa_ref, ob_ref, hf_ref, hb_ref, og_ref, mod_ref, mg_ref, w_ref, g_ref, b_ref,
                rw_ref, rb_ref, x1_ref, he_ref, cnt_ref, s_cnt):
    i = pl.program_id(0)

    @pl.when(i == 0)
    def _():
        s_cnt[...] = jnp.zeros(s_cnt.shape, F32)

    m = mod_ref[0]
    mg = mg_ref[...]
    acc = _dot((oa_ref[...] * mg[:, :D_A]).astype(BF16), w_ref[0:D_A, :])
    acc += _dot((ob_ref[...] * mg[:, D_A:D_A + D_B]).astype(BF16), w_ref[D_A:D_A + D_B, :])
    hc = hf_ref[...] + hb_ref[...]
    og = og_ref[...]
    off = D_A + D_B
    for h in range(H_C):
        sl = slice(h * DK, (h + 1) * DK)
        oc = _sigmoid(og[:, sl]) * _rms(hc[:, sl]) * mg[:, off + h * DK:off + (h + 1) * DK]
        acc += _dot(oc.astype(BF16), w_ref[off + h * DK:off + (h + 1) * DK, :])
    x1 = _ln_plain(ALPHA * x_ref[...] + m[2:3] * acc) * g_ref[...] + b_ref[...]
    x1_ref[...] = x1
    h2 = x1 * (1.0 + m[4:5]) + m[3:4]
    he_ref[:, :D_MODEL] = h2

    lg = _dot3(h2, rw_ref[...]) + rb_ref[...]
    col = lax.broadcasted_iota(jnp.int32, lg.shape, 1)
    ninf = -jnp.inf
    lgm = jnp.where(col < N_GROUPS, lg, ninf)
    mx = jnp.max(lgm, -1, keepdims=True)
    gi = jnp.min(jnp.where(lgm == mx, col, LANE), -1, keepdims=True)
    pg_top = 1.0 / jnp.sum(jnp.where(col < N_GROUPS, jnp.exp(lg - mx), 0.0), -1, keepdims=True)
    lo4 = N_GROUPS + E_PER_GROUP * gi
    lem = jnp.where(jnp.logical_and(col >= lo4, col < lo4 + E_PER_GROUP), lg, ninf)
    v1 = jnp.max(lem, -1, keepdims=True)
    i1 = jnp.min(jnp.where(lem == v1, col, LANE), -1, keepdims=True)
    lem2 = jnp.where(col == i1, ninf, lem)
    v2 = jnp.max(lem2, -1, keepdims=True)
    i2 = jnp.min(jnp.where(lem2 == v2, col, LANE), -1, keepdims=True)
    e21 = jnp.exp(v2 - v1)
    w1 = pg_top / (1.0 + e21)
    w2 = pg_top * e21 / (1.0 + e21)
    e1, e2 = i1 - N_GROUPS, i2 - N_GROUPS
    first_lo = e1 < e2
    elo, ehi = jnp.minimum(e1, e2), jnp.maximum(e1, e2)
    wlo, whi = jnp.where(first_lo, w1, w2), jnp.where(first_lo, w2, w1)
    llo, lhi = elo - E_PER_GROUP * gi, ehi - E_PER_GROUP * gi
    cls = gi * N_PAIRS + ((llo * (7 - llo)) >> 1) + lhi - llo - 1

    oh = jnp.where(col == cls, 1.0, 0.0)
    ii = lax.broadcasted_iota(jnp.int32, (TM, TM), 0)
    jj = lax.broadcasted_iota(jnp.int32, (TM, TM), 1)
    before = jnp.where(jj < ii, 1.0, 0.0).astype(BF16)
    cnt = s_cnt[0:1, :]
    rank = jnp.sum(oh * (_dot(before, oh.astype(BF16)) + cnt), -1, keepdims=True)
    cnt = cnt + jnp.sum(oh, 0, keepdims=True)
    s_cnt[...] = jnp.broadcast_to(cnt, s_cnt.shape)
    cnt_ref[...] = jnp.broadcast_to(cnt, cnt_ref.shape)

    info = jnp.zeros(lg.shape, F32)
    for c, val in ((I_CLS, cls.astype(F32)), (I_ELO, elo.astype(F32)), (I_EHI, ehi.astype(F32)),
                   (I_WLO, wlo), (I_WHI, whi), (I_RANK, rank)):
        info = jnp.where(col == c, val, info)
    he_ref[:, D_MODEL:] = info


def _out_call(x, out_a, out_b, hcf, hcb, ogate, mod_l, mix_g, w_out, ln_g, ln_b, rt_w, rt_b, group_of_tile):
    n = x.shape[0]
    row = lambda i: (i, 0)
    fixed = lambda i: (0, 0)
    return pl.pallas_call(
        _out_kernel,
        grid=(n // TM,),
        in_specs=[pl.BlockSpec((TM, D_MODEL), row), pl.BlockSpec((TM, D_A), row), pl.BlockSpec((TM, D_B), row),
                  pl.BlockSpec((TM, D_C), row), pl.BlockSpec((TM, D_C), row), pl.BlockSpec((TM, D_C), row),
                  pl.BlockSpec((1, 6, D_MODEL), lambda i: (group_of_tile(i), 0, 0)),
                  pl.BlockSpec((1, D_MODEL), fixed), pl.BlockSpec((D_MODEL, D_MODEL), fixed),
                  pl.BlockSpec((1, D_MODEL), fixed), pl.BlockSpec((1, D_MODEL), fixed),
                  pl.BlockSpec((D_MODEL, LANE), fixed), pl.BlockSpec((1, LANE), fixed)],
        out_specs=[pl.BlockSpec((TM, D_MODEL), row), pl.BlockSpec((TM, D_EXT), row),
                   pl.BlockSpec((SUB, LANE), fixed)],
        out_shape=[jax.ShapeDtypeStruct((n, D_MODEL), F32), jax.ShapeDtypeStruct((n, D_EXT), F32),
                   jax.ShapeDtypeStruct((SUB, LANE), F32)],
        scratch_shapes=[pltpu.VMEM((SUB, LANE), F32)],
        compiler_params=_cp(("arbitrary",)),
        name="out_proj_router",
    )(x, out_a, out_b, hcf, hcb, ogate, mod_l, mix_g, w_out, ln_g, ln_b, rt_w, rt_b)


def _row_copy(src_ref, src_row, dst_ref, dst_row, sem):
    return pltpu.make_async_copy(src_ref.at[pl.ds(src_row, 1), :], dst_ref.at[pl.ds(dst_row, 1), :], sem)


def _scatter_kernel(dest_ref, pad_ref, na_ref, x_ref, z_ref, o_ref, sem_z, sems):
    n_tiles = o_ref.shape[0] // TM

    def zero_tile(row):
        return pltpu.make_async_copy(z_ref, o_ref.at[pl.ds(pl.multiple_of(row, TM), TM), :], sem_z)

    def for_zero_tiles(fn):
        def per_class(c, carry):
            row = pad_ref[c]

            @pl.when(row >= 0)
            def _():
                fn(zero_tile(row))

            return carry

        lax.fori_loop(0, N_CLASS, per_class, 0)

        def per_idle(t, carry):
            fn(zero_tile(t * TM))
            return carry

        lax.fori_loop(na_ref[0], n_tiles, per_idle, 0)

    @pl.when(pl.program_id(0) == 0)
    def _():
        for_zero_tiles(lambda cp: cp.start())
        for_zero_tiles(lambda cp: cp.wait())

    _pipelined_row_copies(lambda r, sem: _row_copy(x_ref, r, o_ref, dest_ref[r], sem), sems)


def _pipelined_row_copies(make_copy, sems):
    i = pl.program_id(0)

    def each_row(step, fn):
        base = step * TM
        sem = sems.at[step % 2]

        def body(r, carry):
            fn(make_copy(base + r, sem))
            return carry

        lax.fori_loop(0, TM, body, 0, unroll=DMA_UNROLL)

    each_row(i, lambda cp: cp.start())

    @pl.when(i > 0)
    def _():
        each_row(i - 1, lambda cp: cp.wait())

    @pl.when(i == pl.num_programs(0) - 1)
    def _():
        each_row(i, lambda cp: cp.wait())


def _scatter_call(dest, pad_rows, n_act, h_ext, n_tiles_max):
    any_spec = pl.BlockSpec(memory_space=pl.ANY)
    return pl.pallas_call(
        _scatter_kernel,
        grid_spec=pltpu.PrefetchScalarGridSpec(
            num_scalar_prefetch=3, grid=(h_ext.shape[0] // TM,),
            in_specs=[any_spec, any_spec],
            out_specs=any_spec,
            scratch_shapes=[pltpu.SemaphoreType.DMA(()), pltpu.SemaphoreType.DMA((2,))]),
        out_shape=jax.ShapeDtypeStruct((n_tiles_max * TM, D_EXT), F32),
        compiler_params=_cp(("arbitrary",)),
        name="moe_scatter",
    )(dest, pad_rows, n_act, h_ext, jnp.zeros((TM, D_EXT), F32))


def _moe_kernel(ta_ref, tb_ref, ca_ref, cb_ref, na_ref, x_ref, w1a, w3a, w2a, w1b, w3b, w2b, o_ref,
                s1a, s3a, s2a, s1b, s3b, s2b):
    del ta_ref, tb_ref
    t = pl.program_id(0)
    active = t < na_ref[0]

    @pl.when(jnp.logical_not(active))
    def _():
        o_ref[...] = jnp.zeros(o_ref.shape, F32)

    for chg, srcs, dsts in ((ca_ref, (w1a, w3a, w2a), (s1a, s3a, s2a)), (cb_ref, (w1b, w3b, w2b), (s1b, s3b, s2b))):
        @pl.when(jnp.logical_and(active, chg[t] == 1))
        def _():
            for src, dst in zip(srcs, dsts):
                dst[...] = src[0].astype(BF16)

    @pl.when(active)
    def _():
        xe = x_ref[...]
        x = xe[:, :D_MODEL].astype(BF16)

        def expert(w1, w3, w2, gate):
            a = _dot(x, w1[...])
            hm = a * _sigmoid(a) * _dot(x, w3[...]) * gate
            return _dot(hm.astype(BF16), w2[...])

        o_ref[...] = (expert(s1a, s3a, s2a, xe[:, D_MODEL + I_WLO:D_MODEL + I_WLO + 1])
                      + expert(s1b, s3b, s2b, xe[:, D_MODEL + I_WHI:D_MODEL + I_WHI + 1]))


def _moe_call(layer, plan, x_sorted, w1, w3, w2):
    r = x_sorted.shape[0]
    act = lambda t, ta, tb, ca, cb, na: (jnp.minimum(t, na[0] - 1), 0)
    ea = lambda t, ta, tb, ca, cb, na: (layer, ta[t], 0, 0)
    eb = lambda t, ta, tb, ca, cb, na: (layer, tb[t], 0, 0)
    up = lambda m: pl.BlockSpec((None, 1, D_MODEL, D_E), m)
    down = lambda m: pl.BlockSpec((None, 1, D_E, D_MODEL), m)
    return pl.pallas_call(
        _moe_kernel,
        grid_spec=pltpu.PrefetchScalarGridSpec(
            num_scalar_prefetch=5, grid=(r // TM,),
            in_specs=[pl.BlockSpec((TM, D_EXT), act), up(ea), up(ea), down(ea), up(eb), up(eb), down(eb)],
            out_specs=pl.BlockSpec((TM, D_MODEL), lambda t, ta, tb, ca, cb, na: (t, 0)),
            scratch_shapes=[pltpu.VMEM((D_MODEL, D_E), BF16), pltpu.VMEM((D_MODEL, D_E), BF16),
                            pltpu.VMEM((D_E, D_MODEL), BF16)] * 2),
        out_shape=jax.ShapeDtypeStruct((r, D_MODEL), F32),
        compiler_params=_cp(("arbitrary",)),
        name="moe_experts",
    )(plan["tile_a"], plan["tile_b"], plan["chg_a"], plan["chg_b"], plan["n_act"], x_sorted, w1, w3, w2, w1, w3, w2)


def _unsort_kernel(dest_ref, y_ref, o_ref, sems):
    _pipelined_row_copies(lambda r, sem: _row_copy(y_ref, dest_ref[r], o_ref, r, sem), sems)


def _unsort_call(dest, y_sorted):
    n = dest.shape[0]
    any_spec = pl.BlockSpec(memory_space=pl.ANY)
    return pl.pallas_call(
        _unsort_kernel,
        grid_spec=pltpu.PrefetchScalarGridSpec(
            num_scalar_prefetch=1, grid=(n // TM,), in_specs=[any_spec], out_specs=any_spec,
            scratch_shapes=[pltpu.SemaphoreType.DMA((2,))]),
        out_shape=jax.ShapeDtypeStruct((n, D_MODEL), F32),
        compiler_params=_cp(("arbitrary",)),
        name="moe_unsort",
    )(dest, y_sorted)


def _ln2_kernel(x1_ref, f_ref, mod_ref, g_ref, b_ref, *o_refs, n_ctx_tiles):
    m = mod_ref[0]
    y = _ln_plain(ALPHA * x1_ref[...] + m[5:6] * f_ref[...]) * g_ref[...] + b_ref[...]
    if len(o_refs) == 1:
        o_refs[0][...] = y
    else:
        i = pl.program_id(0)

        @pl.when(i < n_ctx_tiles)
        def _():
            o_refs[0][...] = y

        @pl.when(i >= n_ctx_tiles)
        def _():
            o_refs[1][...] = y


def _ln2_call(x1, ffn, mod_l, ln_g, ln_b, group_of_tile, n_ctx_tiles, split):
    n = x1.shape[0]
    row = lambda i: (i, 0)
    fixed = lambda i: (0, 0)
    if split:
        n_ctx = n_ctx_tiles * TM
        out_specs = [pl.BlockSpec((TM, D_MODEL), lambda i: (jnp.minimum(i, n_ctx_tiles - 1), 0)),
                     pl.BlockSpec((TM, D_MODEL), lambda i: (jnp.maximum(i - n_ctx_tiles, 0), 0))]
        out_shape = [jax.ShapeDtypeStruct((n_ctx, D_MODEL), F32), jax.ShapeDtypeStruct((n - n_ctx, D_MODEL), F32)]
    else:
        out_specs = [pl.BlockSpec((TM, D_MODEL), row)]
        out_shape = [jax.ShapeDtypeStruct((n, D_MODEL), F32)]
    return pl.pallas_call(
        functools.partial(_ln2_kernel, n_ctx_tiles=n_ctx_tiles),
        grid=(n // TM,),
        in_specs=[pl.BlockSpec((TM, D_MODEL), row), pl.BlockSpec((TM, D_MODEL), row),
                  pl.BlockSpec((1, 6, D_MODEL), lambda i: (group_of_tile(i), 0, 0)),
                  pl.BlockSpec((1, D_MODEL), fixed), pl.BlockSpec((1, D_MODEL), fixed)],
        out_specs=out_specs,
        out_shape=out_shape,
        compiler_params=_cp(("arbitrary",)),
        name="ln2",
    )(x1, ffn, mod_l, ln_g, ln_b)


def _dft_matrices():
    n2 = 2 * CH
    f = np.arange(CH, dtype=np.float64)[:, None]
    t = np.arange(CH, dtype=np.float64)[None, :]
    ang = 2.0 * np.pi * f * t / n2
    re, im = np.cos(ang), -np.sin(ang)
    im[0, :] = np.cos(np.pi * t[0])
    fwd = np.concatenate([re, im], axis=0)
    scale = np.full((CH, 1), 2.0 / n2)
    scale[0, 0] = 1.0 / n2
    inv = np.concatenate([(re * scale).T, (im * scale).T], axis=1)
    return fwd.astype(np.float32), inv.astype(np.float32)


def _filter_features(L):
    lag = np.arange(-L, L)
    m = np.minimum(np.abs(lag), L - 1)
    t = (np.arange(L, dtype=np.float32) / np.float32(max(L - 1, 1)))[m]
    w = (np.float32(2.0 * math.pi) * np.arange(L, dtype=np.float32) / np.float32(L))[m]
    bands = np.linspace(1e-4, HY_BANDS - 1, HY_BANDS, dtype=np.float32)
    z = np.zeros((2 * L, LANE), np.float32)
    z[:, 0] = t
    z[:, 1:1 + HY_BANDS] = np.cos(w[:, None] * bands)
    z[:, 1 + HY_BANDS:HY_EMB] = -np.sin(w[:, None] * bands)
    lo, hi = math.log(HY_DECAY_TARGET) / 1.5, math.log(HY_DECAY_TARGET) / 0.3
    deltas = np.abs(np.linspace(lo, hi, D_B, dtype=np.float32))
    dec = np.exp(-t[:, None] * deltas)
    return z, dec.astype(np.float32)


def _sincos_2d(rows, cols):
    quarter = D_MODEL // 4
    omega = 1.0 / (10000.0 ** (jnp.arange(quarter, dtype=F32) / quarter))

    def emb(n):
        ang = jnp.arange(n, dtype=F32)[:, None] * omega[None]
        return jnp.concatenate([jnp.sin(ang), jnp.cos(ang)], -1)

    er, ec = emb(rows), emb(cols)
    half = D_MODEL // 2
    pos = jnp.concatenate([jnp.broadcast_to(er[:, None], (rows, cols, half)),
                           jnp.broadcast_to(ec[None], (rows, cols, half))], -1)
    return pos.reshape(rows * cols, D_MODEL)


def _pad_to(x, shape):
    return jnp.pad(x, [(0, s - d) for d, s in zip(x.shape, shape)])


def _block_diag(w):
    eye = jnp.eye(H_A, dtype=w.dtype)
    return jnp.einsum("hij,hg->higj", w, eye).reshape(D_A, D_A)


_PAIR_LO = np.array([0, 0, 0, 1, 1, 2], np.int32)
_PAIR_HI = np.array([1, 2, 3, 2, 3, 3], np.int32)


def _routing_plan(info, counts, n_tiles_max):
    cls = info[:, I_CLS].astype(jnp.int32)
    rank = info[:, I_RANK].astype(jnp.int32)
    cnt = counts[0, :N_CLASS].astype(jnp.int32)
    tiles = (cnt + TM - 1) // TM
    tile_end = jnp.cumsum(tiles)
    row_start = (tile_end - tiles) * TM
    dest = row_start[cls] + rank
    n_act = tile_end[-1]
    t = jnp.minimum(jnp.arange(n_tiles_max, dtype=jnp.int32), n_act - 1)
    tcls = jnp.minimum(jnp.sum((tile_end[None, :] <= t[:, None]).astype(jnp.int32), 1), N_CLASS - 1)
    grp, pair = tcls // N_PAIRS, tcls % N_PAIRS
    tile_a = (grp * E_PER_GROUP + jnp.asarray(_PAIR_LO)[pair]).astype(jnp.int32)
    tile_b = (grp * E_PER_GROUP + jnp.asarray(_PAIR_HI)[pair]).astype(jnp.int32)
    first = jnp.ones((1,), jnp.int32)
    changed = lambda e: jnp.concatenate([first, (e[1:] != e[:-1]).astype(jnp.int32)])
    return {"dest": dest.astype(jnp.int32), "tile_a": tile_a, "tile_b": tile_b,
            "chg_a": changed(tile_a), "chg_b": changed(tile_b),
            "n_act": n_act.reshape(1).astype(jnp.int32),
            "pad_rows": jnp.where(tiles > 0, (tile_end - 1) * TM, -1).astype(jnp.int32)}


def kernel(x_prompt, x_sample, c, state_lru, state_mlstm_C, state_mlstm_n, state_mlstm_m, c_ctx, w_ada, b_ada, w_in, b_in, conv_a_w, conv_a_b, lru_wa, lru_ba, lru_wx, lru_bx, lru_lam, conv_b_w, conv_b_b, hy_w1, hy_b1, hy_w2, hy_b2, hy_freq, hy_w3, hy_bias, mix_g, w_out, ln1_g, ln1_b, rt_wg, rt_bg, rt_we, rt_be, moe_w1, moe_w3, moe_w2, ln2_g, ln2_b):
    B, l_ctx, D = x_prompt.shape
    b_lat, l_lat, _ = x_sample.shape
    n_ctx, n_lat = B * l_ctx, b_lat * l_lat
    n = n_ctx + n_lat
    assert D == D_MODEL and w_in.shape[-1] == D_MAIN + N_GATE
    assert SEG % l_ctx == 0 and l_lat == SEG and l_ctx % CH == 0 and n_ctx % SEG == 0
    assert l_ctx == CH, "the mLSTM step schedule assumes one chunk per context sequence"
    assert 1 + b_lat <= SUB
    n_ctx_blk = n_ctx // SEG
    n_ctx_tiles = n_ctx // TM
    tiles_per_lat = l_lat // TM
    nc_lat = l_lat // CH

    def group_of_tile(i):
        return jnp.where(i < n_ctx_tiles, 0, 1 + (i - n_ctx_tiles) // tiles_per_lat)

    cond = jnp.concatenate([c_ctx[None], c, jnp.zeros((SUB - 1 - b_lat, D), F32)], 0)
    mod = _mod_call(cond, w_ada, b_ada).reshape(DEPTH, SUB, 6, D)
    pos = _sincos_2d(l_lat // GRID_W, GRID_W)
    x = _entry_call(x_prompt.reshape(n_ctx, D), x_sample.reshape(n_lat, D), pos)

    fwd_np, inv_np = _dft_matrices()
    fwd32 = jnp.asarray(fwd_np)
    fwd16, inv16 = fwd32.astype(BF16), jnp.asarray(inv_np).astype(BF16)
    fw1 = _pad_to(hy_w1, (DEPTH, LANE, LANE))
    fb1 = _pad_to(hy_b1[:, None, :], (DEPTH, 1, LANE))
    fw2 = _pad_to(hy_w2, (DEPTH, LANE, LANE))
    fb2 = _pad_to(hy_b2[:, None, :], (DEPTH, 1, LANE))
    ffr = _pad_to(hy_freq[:, None, :], (DEPTH, 1, LANE))
    fw3 = _pad_to(hy_w3, (DEPTH, LANE, HY_ORDER * 2 * D_B))
    spectra = {}
    for L in (l_ctx, l_lat):
        z_np, dec_np = _filter_features(L)
        spectra[L] = _filt_call(L, jnp.asarray(z_np), jnp.asarray(dec_np), fw1, fb1, fw2, fb2, ffr, fw3, fwd32)

    lat_slots = SEG // l_ctx
    st_lru, st_c, st_n, st_m = [], [], [], []
    for l in range(DEPTH):
        w_main = w_in[l, :, :D_MAIN].astype(BF16)
        b_main = b_in[l, None, :D_MAIN]
        k_lo, k_hi = _Z_CUTS[_K_CUT], _Z_CUTS[_K_CUT + 1]
        w_kt = w_in[l, :, k_lo:k_hi].T.astype(BF16)
        b_kt = jnp.broadcast_to(b_in[l, k_lo:k_hi, None], (D_C, TM))
        w_gate = _pad_to(w_in[l, :, D_MAIN:], (D, LANE))
        b_gate = _pad_to(b_in[l, None, D_MAIN:], (1, LANE))
        xa, ya, hyb, q, v, og, gates, kt = _in_call(x, mod[l], w_main, b_main, w_kt, b_kt, w_gate, b_gate,
                                                    group_of_tile)

        lru_w = jnp.concatenate([_block_diag(lru_wa[l, 0]), _block_diag(lru_wx[l, 0]),
                                 _block_diag(lru_wa[l, 1]), _block_diag(lru_wx[l, 1])], 1).astype(BF16)
        lru_b = jnp.concatenate([lru_ba[l, 0], lru_bx[l, 0], lru_ba[l, 1], lru_bx[l, 1]])[None]
        h0_lat = _pad_to(state_lru[:, l][:, None], (b_lat, lat_slots, 2, D_A))
        h0_all = jnp.concatenate([jnp.zeros((n_ctx_blk, lat_slots, 2, D_A), F32), h0_lat], 0)
        out_a, lru_last = _lru_call(xa, ya, conv_a_w[l], conv_a_b[l, None], lru_w, lru_b, lru_lam[l], h0_all,
                                    l_ctx, l_lat, n_ctx_blk)

        habc, hd0c = spectra[l_ctx]
        habl, hd0l = spectra[l_lat]
        out_b = _hy_call(l, hyb, conv_b_w[l], conv_b_b[l, None], fwd16, inv16, habc, hd0c, habl, hd0l,
                         hy_bias[l][:, None, :], l_ctx, l_lat, n_ctx_blk)

        n0 = state_mlstm_n[:, l].reshape(b_lat, 2 * H_C, DK)
        cx0 = jnp.concatenate([state_mlstm_C[:, l].reshape(b_lat, 2 * H_C, DK, DK), n0[..., None],
                               jnp.zeros((b_lat, 2 * H_C, DK, DK - 1), F32)], -1)
        m0 = _pad_to(state_mlstm_m[:, l], (b_lat, SUB, LANE))
        hcf, hcb, c_fin, n_fin, m_fin = _mlstm_call(q, v, kt, gates, cx0, n0, m0, n_ctx // CH, nc_lat)

        rt_w = _pad_to(jnp.concatenate([rt_wg[l], rt_we[l]], 1), (D, LANE))
        rt_b = _pad_to(jnp.concatenate([rt_bg[l], rt_be[l]])[None], (1, LANE))
        x1, h_ext, counts = _out_call(x, out_a, out_b, hcf, hcb, og, mod[l], mix_g[l, None],
                                      w_out[l].astype(BF16), ln1_g[l, None], ln1_b[l, None], rt_w, rt_b,
                                      group_of_tile)

        n_tiles_max = n // TM + N_CLASS
        plan = _routing_plan(h_ext[:, D_MODEL:D_MODEL + SUB], counts, n_tiles_max)
        x_sorted = _scatter_call(plan["dest"], plan["pad_rows"], plan["n_act"], h_ext, n_tiles_max)
        y_sorted = _moe_call(l, plan, x_sorted, moe_w1, moe_w3, moe_w2)
        ffn = _unsort_call(plan["dest"], y_sorted)
        outs = _ln2_call(x1, ffn, mod[l], ln2_g[l, None], ln2_b[l, None], group_of_tile, n_ctx_tiles,
                         split=(l == DEPTH - 1))
        x = outs[0]

        st_lru.append(lru_last[:n_ctx_blk].reshape(B, 2, D_A))
        st_c.append(c_fin.reshape(B, 2, H_C, DK, DK))
        st_n.append(n_fin.reshape(B, 2, H_C, DK))
        st_m.append(m_fin[:, :2, :H_C])

    return (outs[0].reshape(B, l_ctx, D), outs[1].reshape(b_lat, l_lat, D),
            jnp.stack(st_lru, 1), jnp.stack(st_c, 1), jnp.stack(st_n, 1), jnp.stack(st_m, 1))
```

```python
import functools
import math

import numpy as np
import jax
import jax.numpy as jnp
from jax import lax
from jax.experimental import pallas as pl
from jax.experimental.pallas import tpu as pltpu

F32 = jnp.float32
BF16 = jnp.bfloat16

D_MODEL = 1024
DEPTH = 2
GRID_W = 64
D_A = 256
H_A = 4
BA = D_A // H_A
LRU_C = 8.0
D_B = 256
HY_ORDER = 2
HY_BANDS = 16
HY_EMB = 1 + 2 * HY_BANDS
HY_FH = 64
HY_DECAY_TARGET = 1e-2
D_C = 512
H_C = 4
DK = D_C // H_C
N_GROUPS = 4
E_PER_GROUP = 4
N_EXP = N_GROUPS * E_PER_GROUP
N_PAIRS = 6
N_CLASS = N_GROUPS * N_PAIRS
D_E = 512
ALPHA = (2 * DEPTH) ** 0.25
EPS = 1e-6
D_MAIN = 2 * D_A + 3 * D_B + 4 * D_C
N_GATE = 4 * H_C

LANE = 128
SUB = 8
VMEM_LIMIT = 56 * 1024 * 1024

CH = 256
SEG = 2048
TM = 256
D_EXT = D_MODEL + LANE
FS = 64
DMA_UNROLL = 8
RING = 3

I_CLS, I_ELO, I_EHI, I_WLO, I_WHI, I_RANK = range(6)


def _cp(sem, vmem=VMEM_LIMIT):
    return pltpu.CompilerParams(dimension_semantics=sem, vmem_limit_bytes=vmem)


def _dot(a, b):
    return jnp.dot(a, b, preferred_element_type=F32)


def _split2(x):
    hi = x.astype(BF16)
    lo = (x - hi.astype(F32)).astype(BF16)
    return hi, lo


def _dot3(a, b):
    ah, al = _split2(a)
    bh, bl = _split2(b)
    return _dot(ah, bh) + (_dot(ah, bl) + _dot(al, bh))


def _split3(x):
    hi = x.astype(BF16)
    r1 = x - hi.astype(F32)
    mid = r1.astype(BF16)
    lo = (r1 - mid.astype(F32)).astype(BF16)
    return hi, mid, lo


def _sigmoid(x):
    return 1.0 / (1.0 + jnp.exp(-x))


def _log_sigmoid(x):
    return jnp.minimum(x, 0.0) - jnp.log1p(jnp.exp(-jnp.abs(x)))


def _gelu_tanh(x):
    return 0.5 * x * (1.0 + jnp.tanh(math.sqrt(2.0 / math.pi) * (x + 0.044715 * (x * x * x))))


def _ln_plain(x):
    mu = jnp.mean(x, -1, keepdims=True)
    xc = x - mu
    var = jnp.mean(xc * xc, -1, keepdims=True)
    return xc * lax.rsqrt(var + EPS)


def _rms(x):
    return x * lax.rsqrt(jnp.mean(x * x, -1, keepdims=True) + EPS)


def _halo_rows(ref, start, rows):
    total = ref.shape[0]
    prev = ref[pl.ds(pl.multiple_of(jnp.maximum(start - SUB, 0), SUB), SUB), :]
    main = ref[pl.ds(start, rows), :]
    nxt = ref[pl.ds(pl.multiple_of(jnp.minimum(start + rows, total - SUB), SUB), SUB), :]
    return jnp.concatenate([prev, main, nxt], axis=0), main


def _mod_kernel(c_ref, w_ref, b_ref, o_ref):
    c = c_ref[...]
    o_ref[0] = _dot3(c * _sigmoid(c), w_ref[0]) + b_ref[0]


def _mod_call(cond, w_ada, b_ada):
    tn = 1536
    n6 = w_ada.shape[-1]
    return pl.pallas_call(
        _mod_kernel,
        grid=(DEPTH, n6 // tn),
        in_specs=[pl.BlockSpec((SUB, D_MODEL), lambda l, j: (0, 0)),
                  pl.BlockSpec((1, D_MODEL, tn), lambda l, j: (l, 0, j)),
                  pl.BlockSpec((1, 1, tn), lambda l, j: (l, 0, j))],
        out_specs=pl.BlockSpec((1, SUB, tn), lambda l, j: (l, 0, j)),
        out_shape=jax.ShapeDtypeStruct((DEPTH, SUB, n6), F32),
        compiler_params=_cp(("parallel", "parallel")),
        name="adaln_mod",
    )(cond, w_ada, b_ada.reshape(DEPTH, 1, n6))


def _entry_kernel(xc_ref, xl_ref, pos_ref, o_ref, *, n_ctx_tiles):
    i = pl.program_id(0)

    @pl.when(i < n_ctx_tiles)
    def _():
        o_ref[...] = _ln_plain(xc_ref[...])

    @pl.when(i >= n_ctx_tiles)
    def _():
        o_ref[...] = _ln_plain(xl_ref[...] + pos_ref[...])


def _entry_call(xc, xl, pos):
    tm = 512
    n_ctx, n_lat, l_lat = xc.shape[0], xl.shape[0], pos.shape[0]
    nct = n_ctx // tm
    per_seq = l_lat // tm
    return pl.pallas_call(
        functools.partial(_entry_kernel, n_ctx_tiles=nct),
        grid=((n_ctx + n_lat) // tm,),
        in_specs=[pl.BlockSpec((tm, D_MODEL), lambda i: (jnp.minimum(i, nct - 1), 0)),
                  pl.BlockSpec((tm, D_MODEL), lambda i: (jnp.maximum(i - nct, 0), 0)),
                  pl.BlockSpec((tm, D_MODEL), lambda i: (jnp.maximum(i - nct, 0) % per_seq, 0))],
        out_specs=pl.BlockSpec((tm, D_MODEL), lambda i: (i, 0)),
        out_shape=jax.ShapeDtypeStruct((n_ctx + n_lat, D_MODEL), F32),
        compiler_params=_cp(("parallel",)),
        name="entry_ln",
    )(xc, xl, pos)


_Z_CUTS = (0, D_A, 2 * D_A, 2 * D_A + 3 * D_B, 2 * D_A + 3 * D_B + D_C, 2 * D_A + 3 * D_B + 2 * D_C,
           2 * D_A + 3 * D_B + 3 * D_C, D_MAIN)


_K_CUT = 4
_ROW_CUTS = tuple(c for i, c in enumerate(zip(_Z_CUTS[:-1], _Z_CUTS[1:])) if i != _K_CUT)
_NT = (((1,), (1,)), ((), ()))


def _in_kernel(x_ref, mod_ref, w_ref, b_ref, wkt_ref, bkt_ref, wg_ref, bg_ref, *out_refs):
    m = mod_ref[0]
    h = x_ref[...] * (1.0 + m[1:2]) + m[0:1]
    hb = h.astype(BF16)
    for ref, (a, b) in zip(out_refs[:-2], _ROW_CUTS):
        ref[...] = _dot(hb, w_ref[:, a:b]) + b_ref[:, a:b]
    g_ref, kt_ref = out_refs[-2:]
    g_ref[...] = _dot3(h, wg_ref[...]) + bg_ref[...]
    kt_ref[...] = lax.dot_general(wkt_ref[...], hb, _NT, preferred_element_type=F32) + bkt_ref[...]


def _in_call(x, mod_l, w_main, b_main, w_kt, b_kt, w_gate, b_gate, group_of_tile):
    n = x.shape[0]
    widths = [b - a for a, b in _ROW_CUTS] + [LANE]
    row = lambda i: (i, 0)
    fixed = lambda i: (0, 0)
    return pl.pallas_call(
        _in_kernel,
        grid=(n // TM,),
        in_specs=[pl.BlockSpec((TM, D_MODEL), row),
                  pl.BlockSpec((1, 6, D_MODEL), lambda i: (group_of_tile(i), 0, 0)),
                  pl.BlockSpec((D_MODEL, D_MAIN), fixed), pl.BlockSpec((1, D_MAIN), fixed),
                  pl.BlockSpec((D_C, D_MODEL), fixed), pl.BlockSpec((D_C, TM), fixed),
                  pl.BlockSpec((D_MODEL, LANE), fixed), pl.BlockSpec((1, LANE), fixed)],
        out_specs=[pl.BlockSpec((TM, w), row) for w in widths] + [pl.BlockSpec((D_C, TM), lambda i: (0, i))],
        out_shape=[jax.ShapeDtypeStruct((n, w), F32) for w in widths] + [jax.ShapeDtypeStruct((D_C, n), F32)],
        compiler_params=_cp(("parallel",)),
        name="in_proj",
    )(x, mod_l, w_main, b_main, w_kt, b_kt, w_gate, b_gate)


def _lru_variant(L, xa_ref, ya_ref, cw_ref, cb_ref, wg_ref, bg_ref, lam_ref, h0_ref, o_ref, st_ref,
                 s_af, s_bf, s_ab, s_bb):
    nch, nseq, ntile = SEG // CH, SEG // L, L // SUB
    lam = lam_ref[...]
    sp = jnp.maximum(-lam, 0.0) + jnp.log1p(jnp.exp(-jnp.abs(lam)))
    cw = cw_ref[...]
    cb = cb_ref[...]
    row = lax.broadcasted_iota(jnp.int32, (CH, 1), 0)
    sub = row & (SUB - 1)

    def gates_and_tile_scan(c, carry):
        start = pl.multiple_of(c * CH, CH)
        xcat, main = _halo_rows(xa_ref, start, CH)
        tpos = (start + row) & (L - 1)
        xm2 = jnp.where(tpos >= 2, xcat[SUB - 2:SUB - 2 + CH], 0.0)
        xm1 = jnp.where(tpos >= 1, xcat[SUB - 1:SUB - 1 + CH], 0.0)
        xp1 = jnp.where(tpos <= L - 2, xcat[SUB + 1:SUB + 1 + CH], 0.0)
        xc = cw[0:1] * xm2 + cw[1:2] * xm1 + cw[2:3] * main + cw[3:4] * xp1 + cb
        g = _dot(xc.astype(BF16), wg_ref[...]) + bg_ref[...]
        for d, (sa, sb) in enumerate(((s_af, s_bf), (s_ab, s_bb))):
            r = _sigmoid(g[:, 2 * d * D_A:(2 * d + 1) * D_A])
            ig = _sigmoid(g[:, (2 * d + 1) * D_A:(2 * d + 2) * D_A])
            a = jnp.exp(-LRU_C * r * sp[d:d + 1])
            b = jnp.sqrt(1.0 - a * a) * (ig * xc)
            for s in (1, 2, 4):
                if d == 0:
                    keep = sub >= s
                    a_sh, b_sh = pltpu.roll(a, s, 0), pltpu.roll(b, s, 0)
                else:
                    keep = sub < SUB - s
                    a_sh, b_sh = pltpu.roll(a, CH - s, 0), pltpu.roll(b, CH - s, 0)
                b = a * jnp.where(keep, b_sh, 0.0) + b
                a = a * jnp.where(keep, a_sh, 1.0)
            sa[pl.ds(start, CH), :] = a
            sb[pl.ds(start, CH), :] = b
        return carry

    lax.fori_loop(0, nch, gates_and_tile_scan, 0)

    def carry_tiles(k, carry):
        cf, cbk = carry
        nf, nb = [], []
        for s in range(nseq):
            rf = pl.multiple_of(s * L + k * SUB, SUB)
            hf = s_af[pl.ds(rf, SUB), :] * cf[s] + s_bf[pl.ds(rf, SUB), :]
            s_bf[pl.ds(rf, SUB), :] = hf
            nf.append(hf[SUB - 1:SUB, :])
            rb = pl.multiple_of(s * L + (ntile - 1 - k) * SUB, SUB)
            hb = s_ab[pl.ds(rb, SUB), :] * cbk[s] + s_bb[pl.ds(rb, SUB), :]
            s_bb[pl.ds(rb, SUB), :] = hb
            nb.append(hb[0:1, :])
        return tuple(nf), tuple(nb)

    cf0 = tuple(h0_ref[0, s, 0:1, :] for s in range(nseq))
    cb0 = tuple(h0_ref[0, s, 1:2, :] for s in range(nseq))
    cf, cbk = lax.fori_loop(0, ntile, carry_tiles, (cf0, cb0))

    st_ref[...] = jnp.zeros(st_ref.shape, F32)
    for s in range(nseq):
        st_ref[0, s] = jnp.concatenate([cf[s], cbk[s]], axis=0)

    def finish(c, carry):
        start = pl.multiple_of(c * CH, CH)
        h = s_bf[pl.ds(start, CH), :] + s_bb[pl.ds(start, CH), :]
        o_ref[pl.ds(start, CH), :] = _rms(_gelu_tanh(ya_ref[pl.ds(start, CH), :]) * h)
        return carry

    lax.fori_loop(0, nch, finish, 0)


def _lru_kernel(*refs, l_ctx, l_lat, n_ctx_blk):
    i = pl.program_id(0)

    @pl.when(i < n_ctx_blk)
    def _():
        _lru_variant(l_ctx, *refs)

    @pl.when(i >= n_ctx_blk)
    def _():
        _lru_variant(l_lat, *refs)


def _lru_call(xa, ya, conv_w, conv_b, w_gate, b_gate, lam, h0_all, l_ctx, l_lat, n_ctx_blk):
    n = xa.shape[0]
    nblk = n // SEG
    row = lambda i: (i, 0)
    fixed = lambda i: (0, 0)
    slots = SEG // l_ctx
    return pl.pallas_call(
        functools.partial(_lru_kernel, l_ctx=l_ctx, l_lat=l_lat, n_ctx_blk=n_ctx_blk),
        grid=(nblk,),
        in_specs=[pl.BlockSpec((SEG, D_A), row), pl.BlockSpec((SEG, D_A), row),
                  pl.BlockSpec((4, D_A), fixed), pl.BlockSpec((1, D_A), fixed),
                  pl.BlockSpec((D_A, 4 * D_A), fixed), pl.BlockSpec((1, 4 * D_A), fixed),
                  pl.BlockSpec((2, D_A), fixed),
                  pl.BlockSpec((1, slots, 2, D_A), lambda i: (i, 0, 0, 0))],
        out_specs=[pl.BlockSpec((SEG, D_A), row),
                   pl.BlockSpec((1, slots, 2, D_A), lambda i: (i, 0, 0, 0))],
        out_shape=[jax.ShapeDtypeStruct((n, D_A), F32),
                   jax.ShapeDtypeStruct((nblk, slots, 2, D_A), F32)],
        scratch_shapes=[pltpu.VMEM((SEG, D_A), F32) for _ in range(4)],
        compiler_params=_cp(("parallel",)),
        name="rglru",
    )(xa, ya, conv_w, conv_b, w_gate, b_gate, lam, h0_all)


def _filt_kernel(z_ref, dec_ref, w1_ref, b1_ref, w2_ref, b2_ref, fr_ref, w3_ref, fwd_ref,
                 oab_ref, od0_ref, s_k, s_kf, *, L):
    nblk = 2 * L // CH
    d_idx = pl.program_id(1)
    row = lax.broadcasted_iota(jnp.int32, (CH, 1), 0)

    @pl.when(d_idx == 0)
    def _():
        fr = fr_ref[0]

        def taps(c, carry):
            start = pl.multiple_of(c * CH, CH)
            h1 = jnp.sin(fr * (_dot3(z_ref[pl.ds(start, CH), :], w1_ref[0]) + b1_ref[0]))
            h2 = jnp.sin(fr * (_dot3(h1, w2_ref[0]) + b2_ref[0]))
            t = _dot3(h2, w3_ref[0])
            dec = dec_ref[pl.ds(start, CH), :]
            rg = start + row
            for o in range(HY_ORDER):
                fwd_t = t[:, (2 * o) * D_B:(2 * o + 1) * D_B]
                bwd_t = t[:, (2 * o + 1) * D_B:(2 * o + 2) * D_B]
                ko = jnp.where(rg < L, bwd_t, fwd_t) * dec
                s_k[pl.ds(start, CH), o * D_B:(o + 1) * D_B] = jnp.where(rg == 0, 0.0, ko)
            return carry

        lax.fori_loop(0, nblk, taps, 0)
        fwd = fwd_ref[...]

        def spectra(e, carry):
            start = pl.multiple_of(e * CH, CH)
            s_kf[e] = _dot3(fwd, s_k[pl.ds(start, CH), :])
            return carry

        lax.fori_loop(0, nblk, spectra, 0)

    kd = s_kf[d_idx + 1]
    km = s_kf[d_idx]
    k0 = s_k[pl.ds(pl.multiple_of(d_idx * CH, CH), 1), :]
    sgn = jnp.where((row & 1) == 0, 1.0, -1.0)
    a = kd[:CH] + sgn * (km[:CH] - k0)
    b = jnp.where(row == 0, 0.0, kd[CH:] + sgn * km[CH:])
    hn = kd[CH:CH + 1] + km[CH:CH + 1] - k0
    for o in range(HY_ORDER):
        oab_ref[0, o, 0, 0] = a[:, o * D_B:(o + 1) * D_B]
        oab_ref[0, o, 0, 1] = b[:, o * D_B:(o + 1) * D_B]
        od0_ref[0, o, 0] = jnp.broadcast_to(hn[:, o * D_B:(o + 1) * D_B], (SUB, D_B))


def _filt_call(L, z, dec, w1, b1, w2, b2, fr, w3, fwd32):
    nd = 2 * (L // CH) - 1
    fixed = lambda l, d: (0, 0)
    lay3 = lambda l, d: (l, 0, 0)
    return pl.pallas_call(
        functools.partial(_filt_kernel, L=L),
        grid=(DEPTH, nd),
        in_specs=[pl.BlockSpec((2 * L, LANE), fixed), pl.BlockSpec((2 * L, D_B), fixed),
                  pl.BlockSpec((1, LANE, LANE), lay3), pl.BlockSpec((1, 1, LANE), lay3),
                  pl.BlockSpec((1, LANE, LANE), lay3), pl.BlockSpec((1, 1, LANE), lay3),
                  pl.BlockSpec((1, 1, LANE), lay3),
                  pl.BlockSpec((1, LANE, HY_ORDER * 2 * D_B), lay3),
                  pl.BlockSpec((2 * CH, CH), fixed)],
        out_specs=[pl.BlockSpec((1, HY_ORDER, 1, 2, CH, D_B), lambda l, d: (l, 0, d, 0, 0, 0)),
                   pl.BlockSpec((1, HY_ORDER, 1, SUB, D_B), lambda l, d: (l, 0, d, 0, 0))],
        out_shape=[jax.ShapeDtypeStruct((DEPTH, HY_ORDER, nd, 2, CH, D_B), F32),
                   jax.ShapeDtypeStruct((DEPTH, HY_ORDER, nd, SUB, D_B), F32)],
        scratch_shapes=[pltpu.VMEM((2 * L, HY_ORDER * D_B), F32),
                        pltpu.VMEM((2 * L // CH, 2 * CH, HY_ORDER * D_B), F32)],
        compiler_params=_cp(("parallel", "arbitrary")),
        name=f"hyena_filter_{L}",
    )(z, dec, w1, b1, w2, b2, fr, w3, fwd32)


def _hy_variant(L, o_idx, hy_ref, cw_ref, cb_ref, fwd_ref, inv_ref, hab_ref, hd0_ref, bias_ref, o_ref,
                s_y, s_x, s_u, s_v):
    nch, nseq, P = SEG // CH, SEG // L, L // CH
    row = lax.broadcasted_iota(jnp.int32, (CH, 1), 0)
    frow = lax.broadcasted_iota(jnp.int32, (FS, 1), 0)

    @pl.when(o_idx == 0)
    def _():
        cw = cw_ref[...]
        cb = cb_ref[...]

        def short_conv(c, carry):
            start = pl.multiple_of(c * CH, CH)
            xcat, main = _halo_rows(hy_ref, start, CH)
            tpos = (start + row) & (L - 1)
            xm1 = jnp.where(tpos >= 1, xcat[SUB - 1:SUB - 1 + CH], 0.0)
            xp1 = jnp.where(tpos <= L - 2, xcat[SUB + 1:SUB + 1 + CH], 0.0)
            hc = cw[0:1] * xm1 + cw[1:2] * main + cw[2:3] * xp1 + cb
            s_y[pl.ds(start, CH), :] = hc[:, :D_B]
            s_x[0, pl.ds(start, CH), :] = hc[:, D_B:2 * D_B]
            s_x[1, pl.ds(start, CH), :] = hc[:, 2 * D_B:]
            return carry

        lax.fori_loop(0, nch, short_conv, 0)

    bias = bias_ref[0]

    def one_sequence(s, carry):
        base = s * L

        def forward_dft(j, cc):
            r = pl.multiple_of(base + j * CH, CH)
            s_u[j] = _dot(fwd_ref[...], s_y[pl.ds(r, CH), :].astype(BF16))
            return cc

        lax.fori_loop(0, P, forward_dft, 0)

        def output_block(i, cc):
            for fs in range(CH // FS):
                lo = fs * FS

                def accumulate(j, acc):
                    yre, yim = acc
                    d = i - j + (P - 1)
                    ure = s_u[j, lo:lo + FS, :]
                    uim = s_u[j, CH + lo:CH + lo + FS, :]
                    a = hab_ref[0, 0, d, 0, lo:lo + FS, :]
                    b = hab_ref[0, 0, d, 1, lo:lo + FS, :]
                    dd = jnp.where(frow == 0, hd0_ref[0, 0, d, 0:1, :], a) if fs == 0 else a
                    return yre + ure * a - uim * b, yim + ure * b + uim * dd

                zero = jnp.zeros((FS, D_B), F32)
                yre, yim = lax.fori_loop(0, P, accumulate, (zero, zero))
                s_v[lo:lo + FS, :] = yre.astype(BF16)
                s_v[CH + lo:CH + lo + FS, :] = yim.astype(BF16)
            yc = _dot(inv_ref[...], s_v[...])
            r = pl.multiple_of(base + i * CH, CH)
            s_y[pl.ds(r, CH), :] = s_x[o_idx, pl.ds(r, CH), :] * (yc + s_y[pl.ds(r, CH), :] * bias)
            return cc

        lax.fori_loop(0, P, output_block, 0)
        return carry

    lax.fori_loop(0, nseq, one_sequence, 0)

    @pl.when(o_idx == HY_ORDER - 1)
    def _():
        def finish(c, carry):
            start = pl.multiple_of(c * CH, CH)
            o_ref[pl.ds(start, CH), :] = _rms(s_y[pl.ds(start, CH), :])
            return carry

        lax.fori_loop(0, nch, finish, 0)


def _hy_kernel(hy_ref, cw_ref, cb_ref, fwd_ref, inv_ref, habc_ref, hd0c_ref, habl_ref, hd0l_ref, bias_ref,
               o_ref, s_y, s_x, s_u, s_v, *, l_ctx, l_lat, n_ctx_blk):
    i = pl.program_id(0)
    o_idx = pl.program_id(1)
    scratch = (s_y, s_x, s_u, s_v)

    @pl.when(i < n_ctx_blk)
    def _():
        _hy_variant(l_ctx, o_idx, hy_ref, cw_ref, cb_ref, fwd_ref, inv_ref, habc_ref, hd0c_ref, bias_ref, o_ref,
                    *scratch)

    @pl.when(i >= n_ctx_blk)
    def _():
        _hy_variant(l_lat, o_idx, hy_ref, cw_ref, cb_ref, fwd_ref, inv_ref, habl_ref, hd0l_ref, bias_ref, o_ref,
                    *scratch)


def _hy_call(layer, hyb, conv_w, conv_b, fwd, inv, habc, hd0c, habl, hd0l, bias, l_ctx, l_lat, n_ctx_blk):
    n = hyb.shape[0]
    ndc, ndl = habc.shape[2], habl.shape[2]
    pmax = max(l_ctx, l_lat) // CH
    row = lambda i, o: (i, 0)
    fixed = lambda i, o: (0, 0)
    lat_o = lambda i, o: jnp.where(i >= n_ctx_blk, o, 0)
    ctx_o = lambda i, o: jnp.where(i < n_ctx_blk, o, 0)
    return pl.pallas_call(
        functools.partial(_hy_kernel, l_ctx=l_ctx, l_lat=l_lat, n_ctx_blk=n_ctx_blk),
        grid=(n // SEG, HY_ORDER),
        in_specs=[pl.BlockSpec((SEG, 3 * D_B), row),
                  pl.BlockSpec((3, 3 * D_B), fixed), pl.BlockSpec((1, 3 * D_B), fixed),
                  pl.BlockSpec((2 * CH, CH), fixed), pl.BlockSpec((CH, 2 * CH), fixed),
                  pl.BlockSpec((1, 1, ndc, 2, CH, D_B), lambda i, o: (layer, ctx_o(i, o), 0, 0, 0, 0)),
                  pl.BlockSpec((1, 1, ndc, SUB, D_B), lambda i, o: (layer, ctx_o(i, o), 0, 0, 0)),
                  pl.BlockSpec((1, 1, ndl, 2, CH, D_B), lambda i, o: (layer, lat_o(i, o), 0, 0, 0, 0)),
                  pl.BlockSpec((1, 1, ndl, SUB, D_B), lambda i, o: (layer, lat_o(i, o), 0, 0, 0)),
                  pl.BlockSpec((1, 1, D_B), lambda i, o: (o, 0, 0))],
        out_specs=pl.BlockSpec((SEG, D_B), row),
        out_shape=jax.ShapeDtypeStruct((n, D_B), F32),
        scratch_shapes=[pltpu.VMEM((SEG, D_B), F32), pltpu.VMEM((HY_ORDER, SEG, D_B), F32),
                        pltpu.VMEM((pmax, 2 * CH, D_B), F32), pltpu.VMEM((2 * CH, D_B), BF16)],
        compiler_params=_cp(("parallel", "arbitrary")),
        name="hyena",
    )(hyb, conv_w, conv_b, fwd, inv, habc, hd0c, habl, hd0l, bias)


def _row_scan(x, op, fill, reverse):
    t = x.shape[0]
    sub = lax.broadcasted_iota(jnp.int32, (t, 1), 0) & (SUB - 1)
    for s in (1, 2, 4):
        if reverse:
            shifted, keep = pltpu.roll(x, t - s, 0), sub < SUB - s
        else:
            shifted, keep = pltpu.roll(x, s, 0), sub >= s
        x = op(x, jnp.where(keep, shifted, fill))
    n_tiles = t // SUB
    out = [None] * n_tiles
    carry = None
    for i in (reversed(range(n_tiles)) if reverse else range(n_tiles)):
        tile = x[i * SUB:(i + 1) * SUB]
        out[i] = tile if carry is None else op(tile, carry)
        carry = out[i][0:1] if reverse else out[i][SUB - 1:SUB]
    return jnp.concatenate(out, axis=0)


_STK_ONE = 3 * SUB


def _mlstm_prep(d, g_ref, m_old):
    T = CH
    reverse = d == 1
    g = g_ref[...]
    if d == 1:
        g = pltpu.roll(g, LANE - 2 * H_C, 1)
    lane = lax.broadcasted_iota(jnp.int32, (1, LANE), 1)
    head = lane < H_C
    b = pltpu.roll(_row_scan(_log_sigmoid(g), jnp.add, 0.0, reverse), LANE - H_C, 1)
    r = jnp.where(head, g - b, 0.0)
    big_m = jnp.maximum(m_old, _row_scan(r, jnp.maximum, -jnp.inf, reverse))
    last = 0 if reverse else T - 1
    m_last = big_m[last:last + 1, :]
    low = lane < SUB
    p0, p1, p2 = (jnp.where(low, p.astype(F32), 0.0) for p in _split3(-big_m))
    cols = (p0 + pltpu.roll(p1, SUB, 1) + pltpu.roll(p2, 2 * SUB, 1)
            + jnp.where(jnp.logical_and(lane >= _STK_ONE, lane < _STK_ONE + SUB), 1.0, 0.0))
    rowid = lax.broadcasted_iota(jnp.int32, (SUB, 1), 0)
    r8 = r.T[0:SUB, :]
    m_last8 = sum(jnp.where(rowid == h, m_last[:, h:h + 1], 0.0) for h in range(H_C))
    ws8 = jnp.where(rowid < H_C, jnp.exp(r8 - m_last8), 0.0)
    return {"r3": [p.astype(F32) for p in _split3(r8)], "ws8": ws8, "wc": jnp.exp(m_old - m_last),
            "m_new": jnp.where(head, b[last:last + 1, :] + m_last, 0.0),
            "wi": jnp.exp(m_old - big_m), "e": jnp.exp(-(b + big_m)), "cols_b": cols.astype(BF16)}


def _mlstm_variant(carry, qf, vf, ktf, gf, qb_, vb_, ktb_, gb_, hf_ref, hb_ref, co_ref, no_ref, mo_ref,
                   s_cx, s_n, s_m):
    T = CH
    ii = lax.broadcasted_iota(jnp.int32, (T, T), 0)
    jj = lax.broadcasted_iota(jnp.int32, (T, T), 1)
    rowid = lax.broadcasted_iota(jnp.int32, (SUB, 1), 0)
    one_col = jnp.where(lax.broadcasted_iota(jnp.int32, (T, DK), 1) == 0, 1.0, 0.0).astype(BF16)
    prep = []
    for d, g_ref in ((0, gf), (1, gb_ if carry else gf)):
        m_old = s_m[d:d + 1, :] if carry else jnp.zeros((1, LANE), F32)
        prep.append(_mlstm_prep(d, g_ref, m_old))
    refs = ((qf, vf, ktf, hf_ref), (qb_, vb_, ktb_, hb_ref))
    for h in range(H_C):
        sl = slice(h * DK, (h + 1) * DK)
        for d in range(2):
            p = prep[d]
            q_ref, v_ref, kt_ref, h_ref = refs[d]
            idx = d * H_C + h
            if carry or d == 0:
                qb = (q_ref[:, sl] * (DK ** -0.5)).astype(BF16)
                kt = kt_ref[sl, :]
                ktb = kt.astype(BF16)
                v_ext = jnp.concatenate([v_ref[:, sl].astype(BF16), one_col], axis=1)
                s_raw = _dot(qb, ktb)
            tri = (jj >= ii) if d == 1 else (jj <= ii)
            sel = jnp.broadcast_to(jnp.where(rowid == h, 1.0, 0.0), (SUB, T))
            rr = sum(jnp.where(rowid == i, piece[h:h + 1, :], 0.0) for i, piece in enumerate(p["r3"]))
            rmat = jnp.concatenate([sel, sel, sel, rr, jnp.zeros((LANE - 4 * SUB, T), F32)], axis=0)
            expo = _dot(p["cols_b"], rmat.astype(BF16))
            s = s_raw * jnp.exp(jnp.where(tri, expo, -jnp.inf))
            intra = _dot(s.astype(BF16), v_ext)
            num, den = intra[:, :DK], intra[:, DK:DK + 1]
            if carry:
                cx = s_cx[idx]
                inter = _dot(qb, cx.astype(BF16))
                wi = p["wi"][:, h:h + 1]
                num, den = num + wi * inter[:, :DK], den + wi * inter[:, DK:DK + 1]
            h_ref[:, sl] = num / jnp.maximum(jnp.abs(den), p["e"][:, h:h + 1])
            upd = _dot((kt * p["ws8"][h:h + 1, :]).astype(BF16), v_ext)
            n_upd = lax.dot_general(p["ws8"].astype(BF16), ktb, _NT, preferred_element_type=F32)[h:h + 1, :]
            if carry:
                wc = p["wc"][:, h:h + 1]
                s_cx[idx] = wc * cx + upd
                s_n[idx:idx + 1, :] = wc * s_n[idx:idx + 1, :] + n_upd
            else:
                co_ref[0, idx] = upd[:, :DK]
                no_ref[0, idx:idx + 1, :] = n_upd
    m_rows = jnp.concatenate([prep[0]["m_new"], prep[1]["m_new"], jnp.zeros((SUB - 2, LANE), F32)], axis=0)
    if carry:
        s_m[...] = m_rows
    else:
        mo_ref[0] = m_rows


def _mlstm_kernel(*refs, n_ctx_steps, nc_lat):
    cx0_ref, n0_ref, m0_ref = refs[8:11]
    s_cx, s_n, s_m = refs[-3:]
    data = refs[:8] + refs[11:]
    t = pl.program_id(0)
    is_ctx = t < n_ctx_steps

    @pl.when(is_ctx)
    def _():
        _mlstm_variant(False, *data)

    @pl.when(jnp.logical_not(is_ctx))
    def _():
        @pl.when((t - n_ctx_steps) % nc_lat == 0)
        def _():
            s_cx[...] = cx0_ref[0]
            s_n[...] = n0_ref[0]
            s_m[...] = m0_ref[0]

        _mlstm_variant(True, *data)


def _mlstm_call(q, v, kt, gates, cx0, n0, m0, n_ctx_steps, nc_lat):
    n = q.shape[0]
    steps = n // CH
    nst = 2 * H_C

    def bwd_blk(t):
        r = jnp.maximum(t - n_ctx_steps, 0)
        return n_ctx_steps + (r // nc_lat) * nc_lat + (nc_lat - 1 - r % nc_lat)

    out_bwd = lambda t: jnp.where(t < n_ctx_steps, t, bwd_blk(t))
    lat_b = lambda t: jnp.maximum(t - n_ctx_steps, 0) // nc_lat
    ctx_b = lambda t: jnp.minimum(t, n_ctx_steps - 1)
    rows = lambda w, blk: pl.BlockSpec((CH, w), lambda t: (blk(t), 0))
    cols = lambda h, blk: pl.BlockSpec((h, CH), lambda t: (0, blk(t)))
    ident = lambda t: t
    return pl.pallas_call(
        functools.partial(_mlstm_kernel, n_ctx_steps=n_ctx_steps, nc_lat=nc_lat),
        grid=(steps,),
        in_specs=[rows(D_C, ident), rows(D_C, ident), cols(D_C, ident), rows(LANE, ident),
                  rows(D_C, bwd_blk), rows(D_C, bwd_blk), cols(D_C, bwd_blk), rows(LANE, bwd_blk),
                  pl.BlockSpec((1, nst, DK, 2 * DK), lambda t: (lat_b(t), 0, 0, 0)),
                  pl.BlockSpec((1, nst, DK), lambda t: (lat_b(t), 0, 0)),
                  pl.BlockSpec((1, SUB, LANE), lambda t: (lat_b(t), 0, 0))],
        out_specs=[rows(D_C, ident), rows(D_C, out_bwd),
                   pl.BlockSpec((1, nst, DK, DK), lambda t: (ctx_b(t), 0, 0, 0)),
                   pl.BlockSpec((1, nst, DK), lambda t: (ctx_b(t), 0, 0)),
                   pl.BlockSpec((1, SUB, LANE), lambda t: (ctx_b(t), 0, 0))],
        out_shape=[jax.ShapeDtypeStruct((n, D_C), F32), jax.ShapeDtypeStruct((n, D_C), F32),
                   jax.ShapeDtypeStruct((n_ctx_steps, nst, DK, DK), F32),
                   jax.ShapeDtypeStruct((n_ctx_steps, nst, DK), F32),
                   jax.ShapeDtypeStruct((n_ctx_steps, SUB, LANE), F32)],
        scratch_shapes=[pltpu.VMEM((nst, DK, 2 * DK), F32), pltpu.VMEM((nst, DK), F32),
                        pltpu.VMEM((SUB, LANE), F32)],
        compiler_params=_cp(("arbitrary",)),
        name="mlstm",
    )(q, v, kt, gates, q, v, kt, gates, cx0, n0, m0)


def _out_kernel(x_ref, oa_ref, ob_ref, hf_ref, hb_ref, og_ref, mod_ref, mg_ref, w_ref, g_ref, b_ref,
                rw_ref, rb_ref, x1_ref, he_ref, cnt_ref, s_cnt):
    i = pl.program_id(0)

    @pl.when(i == 0)
    def _():
        s_cnt[...] = jnp.zeros(s_cnt.shape, F32)

    m = mod_ref[0]
    mg = mg_ref[...]
    acc = _dot((oa_ref[...] * mg[:, :D_A]).astype(BF16), w_ref[0:D_A, :])
    acc += _dot((ob_ref[...] * mg[:, D_A:D_A + D_B]).astype(BF16), w_ref[D_A:D_A + D_B, :])
    hc = hf_ref[...] + hb_ref[...]
    og = og_ref[...]
    off = D_A + D_B
    for h in range(H_C):
        sl = slice(h * DK, (h + 1) * DK)
        oc = _sigmoid(og[:, sl]) * _rms(hc[:, sl]) * mg[:, off + h * DK:off + (h + 1) * DK]
        acc += _dot(oc.astype(BF16), w_ref[off + h * DK:off + (h + 1) * DK, :])
    x1 = _ln_plain(ALPHA * x_ref[...] + m[2:3] * acc) * g_ref[...] + b_ref[...]
    x1_ref[...] = x1
    h2 = x1 * (1.0 + m[4:5]) + m[3:4]
    he_ref[:, :D_MODEL] = h2

    lg = _dot3(h2, rw_ref[...]) + rb_ref[...]
    col = lax.broadcasted_iota(jnp.int32, lg.shape, 1)
    ninf = -jnp.inf
    lgm = jnp.where(col < N_GROUPS, lg, ninf)
    mx = jnp.max(lgm, -1, keepdims=True)
    gi = jnp.min(jnp.where(lgm == mx, col, LANE), -1, keepdims=True)
    pg_top = 1.0 / jnp.sum(jnp.where(col < N_GROUPS, jnp.exp(lg - mx), 0.0), -1, keepdims=True)
    lo4 = N_GROUPS + E_PER_GROUP * gi
    lem = jnp.where(jnp.logical_and(col >= lo4, col < lo4 + E_PER_GROUP), lg, ninf)
    v1 = jnp.max(lem, -1, keepdims=True)
    i1 = jnp.min(jnp.where(lem == v1, col, LANE), -1, keepdims=True)
    lem2 = jnp.where(col == i1, ninf, lem)
    v2 = jnp.max(lem2, -1, keepdims=True)
    i2 = jnp.min(jnp.where(lem2 == v2, col, LANE), -1, keepdims=True)
    e21 = jnp.exp(v2 - v1)
    w1 = pg_top / (1.0 + e21)
    w2 = pg_top * e21 / (1.0 + e21)
    e1, e2 = i1 - N_GROUPS, i2 - N_GROUPS
    first_lo = e1 < e2
    elo, ehi = jnp.minimum(e1, e2), jnp.maximum(e1, e2)
    wlo, whi = jnp.where(first_lo, w1, w2), jnp.where(first_lo, w2, w1)
    llo, lhi = elo - E_PER_GROUP * gi, ehi - E_PER_GROUP * gi
    cls = gi * N_PAIRS + ((llo * (7 - llo)) >> 1) + lhi - llo - 1

    oh = jnp.where(col == cls, 1.0, 0.0)
    ii = lax.broadcasted_iota(jnp.int32, (TM, TM), 0)
    jj = lax.broadcasted_iota(jnp.int32, (TM, TM), 1)
    before = jnp.where(jj < ii, 1.0, 0.0).astype(BF16)
    cnt = s_cnt[0:1, :]
    rank = jnp.sum(oh * (_dot(before, oh.astype(BF16)) + cnt), -1, keepdims=True)
    cnt = cnt + jnp.sum(oh, 0, keepdims=True)
    s_cnt[...] = jnp.broadcast_to(cnt, s_cnt.shape)
    cnt_ref[...] = jnp.broadcast_to(cnt, cnt_ref.shape)

    info = jnp.zeros(lg.shape, F32)
    for c, val in ((I_CLS, cls.astype(F32)), (I_ELO, elo.astype(F32)), (I_EHI, ehi.astype(F32)),
                   (I_WLO, wlo), (I_WHI, whi), (I_RANK, rank)):
        info = jnp.where(col == c, val, info)
    he_ref[:, D_MODEL:] = info


def _out_call(x, out_a, out_b, hcf, hcb, ogate, mod_l, mix_g, w_out, ln_g, ln_b, rt_w, rt_b, group_of_tile):
    n = x.shape[0]
    row = lambda i: (i, 0)
    fixed = lambda i: (0, 0)
    return pl.pallas_call(
        _out_kernel,
        grid=(n // TM,),
        in_specs=[pl.BlockSpec((TM, D_MODEL), row), pl.BlockSpec((TM, D_A), row), pl.BlockSpec((TM, D_B), row),
                  pl.BlockSpec((TM, D_C), row), pl.BlockSpec((TM, D_C), row), pl.BlockSpec((TM, D_C), row),
                  pl.BlockSpec((1, 6, D_MODEL), lambda i: (group_of_tile(i), 0, 0)),
                  pl.BlockSpec((1, D_MODEL), fixed), pl.BlockSpec((D_MODEL, D_MODEL), fixed),
                  pl.BlockSpec((1, D_MODEL), fixed), pl.BlockSpec((1, D_MODEL), fixed),
                  pl.BlockSpec((D_MODEL, LANE), fixed), pl.BlockSpec((1, LANE), fixed)],
        out_specs=[pl.BlockSpec((TM, D_MODEL), row), pl.BlockSpec((TM, D_EXT), row),
                   pl.BlockSpec((SUB, LANE), fixed)],
        out_shape=[jax.ShapeDtypeStruct((n, D_MODEL), F32), jax.ShapeDtypeStruct((n, D_EXT), F32),
                   jax.ShapeDtypeStruct((SUB, LANE), F32)],
        scratch_shapes=[pltpu.VMEM((SUB, LANE), F32)],
        compiler_params=_cp(("arbitrary",)),
        name="out_proj_router",
    )(x, out_a, out_b, hcf, hcb, ogate, mod_l, mix_g, w_out, ln_g, ln_b, rt_w, rt_b)


def _row_copy(src_ref, src_row, dst_ref, dst_row, sem):
    return pltpu.make_async_copy(src_ref.at[pl.ds(src_row, 1), :], dst_ref.at[pl.ds(dst_row, 1), :], sem)


def _scatter_kernel(dest_ref, pad_ref, na_ref, x_ref, z_ref, o_ref, ring, sem_z, load_sems, row_sems):
    n_tiles = o_ref.shape[0] // TM

    def zero_tile(row):
        return pltpu.make_async_copy(z_ref, o_ref.at[pl.ds(pl.multiple_of(row, TM), TM), :], sem_z)

    def for_zero_tiles(fn):
        def per_class(c, carry):
            row = pad_ref[c]

            @pl.when(row >= 0)
            def _():
                fn(zero_tile(row))

            return carry

        lax.fori_loop(0, N_CLASS, per_class, 0)

        def per_idle(t, carry):
            fn(zero_tile(t * TM))
            return carry

        lax.fori_loop(na_ref[0], n_tiles, per_idle, 0)

    i = pl.program_id(0)
    n_steps = x_ref.shape[0] // TM

    def load(step):
        return pltpu.make_async_copy(x_ref.at[pl.ds(pl.multiple_of(step * TM, TM), TM), :],
                                     ring.at[step % RING], load_sems.at[step % RING])

    def start_rows(step):
        slot = step % RING

        def body(r, carry):
            _row_copy(ring.at[slot], r, o_ref, dest_ref[step * TM + r], row_sems.at[slot]).start()
            return carry

        lax.fori_loop(0, TM, body, 0, unroll=DMA_UNROLL)

    def wait_rows(step):
        slot = step % RING
        pltpu.make_async_copy(ring.at[slot], o_ref.at[pl.ds(0, TM), :], row_sems.at[slot]).wait()

    @pl.when(i == 0)
    def _():
        for_zero_tiles(lambda cp: cp.start())
        for_zero_tiles(lambda cp: cp.wait())
        load(0).start()

    @pl.when(i >= RING - 1)
    def _():
        wait_rows(i - (RING - 1))

    @pl.when(i + 1 < n_steps)
    def _():
        load(i + 1).start()

    load(i).wait()
    start_rows(i)

    @pl.when(i == n_steps - 1)
    def _():
        for back in range(min(RING - 1, n_steps) - 1, -1, -1):
            wait_rows(i - back)


def _scatter_call(dest, pad_rows, n_act, h_ext, n_tiles_max):
    any_spec = pl.BlockSpec(memory_space=pl.ANY)
    return pl.pallas_call(
        _scatter_kernel,
        grid_spec=pltpu.PrefetchScalarGridSpec(
            num_scalar_prefetch=3, grid=(h_ext.shape[0] // TM,),
            in_specs=[any_spec, any_spec],
            out_specs=any_spec,
            scratch_shapes=[pltpu.VMEM((RING, TM, D_EXT), F32), pltpu.SemaphoreType.DMA(()),
                            pltpu.SemaphoreType.DMA((RING,)), pltpu.SemaphoreType.DMA((RING,))]),
        out_shape=jax.ShapeDtypeStruct((n_tiles_max * TM, D_EXT), F32),
        compiler_params=_cp(("arbitrary",)),
        name="moe_scatter",
    )(dest, pad_rows, n_act, h_ext, jnp.zeros((TM, D_EXT), F32))


def _moe_kernel(ta_ref, tb_ref, ca_ref, cb_ref, na_ref, x_ref, w1a, w3a, w2a, w1b, w3b, w2b, o_ref,
                s1a, s3a, s2a, s1b, s3b, s2b):
    del ta_ref, tb_ref
    t = pl.program_id(0)
    active = t < na_ref[0]

    @pl.when(jnp.logical_not(active))
    def _():
        o_ref[...] = jnp.zeros(o_ref.shape, F32)

    for chg, srcs, dsts in ((ca_ref, (w1a, w3a, w2a), (s1a, s3a, s2a)), (cb_ref, (w1b, w3b, w2b), (s1b, s3b, s2b))):
        @pl.when(jnp.logical_and(active, chg[t] == 1))
        def _():
            for src, dst in zip(srcs, dsts):
                dst[...] = src[0].astype(BF16)

    @pl.when(active)
    def _():
        xe = x_ref[...]
        x = xe[:, :D_MODEL].astype(BF16)

        def expert(w1, w3, w2, gate):
            a = _dot(x, w1[...])
            hm = a * _sigmoid(a) * _dot(x, w3[...]) * gate
            return _dot(hm.astype(BF16), w2[...])

        o_ref[...] = (expert(s1a, s3a, s2a, xe[:, D_MODEL + I_WLO:D_MODEL + I_WLO + 1])
                      + expert(s1b, s3b, s2b, xe[:, D_MODEL + I_WHI:D_MODEL + I_WHI + 1]))


def _moe_call(layer, plan, x_sorted, w1, w3, w2):
    r = x_sorted.shape[0]
    act = lambda t, ta, tb, ca, cb, na: (jnp.minimum(t, na[0] - 1), 0)
    ea = lambda t, ta, tb, ca, cb, na: (layer, ta[t], 0, 0)
    eb = lambda t, ta, tb, ca, cb, na: (layer, tb[t], 0, 0)
    up = lambda m: pl.BlockSpec((None, 1, D_MODEL, D_E), m)
    down = lambda m: pl.BlockSpec((None, 1, D_E, D_MODEL), m)
    return pl.pallas_call(
        _moe_kernel,
        grid_spec=pltpu.PrefetchScalarGridSpec(
            num_scalar_prefetch=5, grid=(r // TM,),
            in_specs=[pl.BlockSpec((TM, D_EXT), act), up(ea), up(ea), down(ea), up(eb), up(eb), down(eb)],
            out_specs=pl.BlockSpec((TM, D_MODEL), lambda t, ta, tb, ca, cb, na: (t, 0)),
            scratch_shapes=[pltpu.VMEM((D_MODEL, D_E), BF16), pltpu.VMEM((D_MODEL, D_E), BF16),
                            pltpu.VMEM((D_E, D_MODEL), BF16)] * 2),
        out_shape=jax.ShapeDtypeStruct((r, D_MODEL), F32),
        compiler_params=_cp(("arbitrary",)),
        name="moe_experts",
    )(plan["tile_a"], plan["tile_b"], plan["chg_a"], plan["chg_b"], plan["n_act"], x_sorted, w1, w3, w2, w1, w3, w2)


def _ln2_kernel(dest_ref, x1_ref, mod_ref, g_ref, b_ref, y_ref, *rest, n_ctx_tiles):
    o_refs, (buf, sems) = rest[:-2], rest[-2:]
    i = pl.program_id(0)
    n_steps = pl.num_programs(0)

    def start_rows(step):
        slot = step % 2

        def body(r, carry):
            _row_copy(y_ref, dest_ref[step * TM + r], buf.at[slot], r, sems.at[slot]).start()
            return carry

        lax.fori_loop(0, TM, body, 0, unroll=DMA_UNROLL)

    @pl.when(i == 0)
    def _():
        start_rows(0)

    @pl.when(i + 1 < n_steps)
    def _():
        start_rows(i + 1)

    slot = i % 2
    pltpu.make_async_copy(y_ref.at[pl.ds(0, TM), :], buf.at[slot], sems.at[slot]).wait()
    m = mod_ref[0]
    y = _ln_plain(ALPHA * x1_ref[...] + m[5:6] * buf[slot]) * g_ref[...] + b_ref[...]
    if len(o_refs) == 1:
        o_refs[0][...] = y
    else:
        @pl.when(i < n_ctx_tiles)
        def _():
            o_refs[0][...] = y

        @pl.when(i >= n_ctx_tiles)
        def _():
            o_refs[1][...] = y


def _ln2_call(dest, x1, y_sorted, mod_l, ln_g, ln_b, group_of_tile, n_ctx_tiles, split):
    n = x1.shape[0]
    row = lambda i, d: (i, 0)
    fixed = lambda i, d: (0, 0)
    if split:
        n_ctx = n_ctx_tiles * TM
        out_specs = [pl.BlockSpec((TM, D_MODEL), lambda i, d: (jnp.minimum(i, n_ctx_tiles - 1), 0)),
                     pl.BlockSpec((TM, D_MODEL), lambda i, d: (jnp.maximum(i - n_ctx_tiles, 0), 0))]
        out_shape = [jax.ShapeDtypeStruct((n_ctx, D_MODEL), F32), jax.ShapeDtypeStruct((n - n_ctx, D_MODEL), F32)]
    else:
        out_specs = [pl.BlockSpec((TM, D_MODEL), row)]
        out_shape = [jax.ShapeDtypeStruct((n, D_MODEL), F32)]
    return pl.pallas_call(
        functools.partial(_ln2_kernel, n_ctx_tiles=n_ctx_tiles),
        grid_spec=pltpu.PrefetchScalarGridSpec(
            num_scalar_prefetch=1, grid=(n // TM,),
            in_specs=[pl.BlockSpec((TM, D_MODEL), row),
                      pl.BlockSpec((1, 6, D_MODEL), lambda i, d: (group_of_tile(i), 0, 0)),
                      pl.BlockSpec((1, D_MODEL), fixed), pl.BlockSpec((1, D_MODEL), fixed),
                      pl.BlockSpec(memory_space=pl.ANY)],
            out_specs=out_specs,
            scratch_shapes=[pltpu.VMEM((2, TM, D_MODEL), F32), pltpu.SemaphoreType.DMA((2,))]),
        out_shape=out_shape,
        compiler_params=_cp(("arbitrary",)),
        name="moe_gather_ln2",
    )(dest, x1, mod_l, ln_g, ln_b, y_sorted)


def _dft_matrices():
    n2 = 2 * CH
    f = np.arange(CH, dtype=np.float64)[:, None]
    t = np.arange(CH, dtype=np.float64)[None, :]
    ang = 2.0 * np.pi * f * t / n2
    re, im = np.cos(ang), -np.sin(ang)
    im[0, :] = np.cos(np.pi * t[0])
    fwd = np.concatenate([re, im], axis=0)
    scale = np.full((CH, 1), 2.0 / n2)
    scale[0, 0] = 1.0 / n2
    inv = np.concatenate([(re * scale).T, (im * scale).T], axis=1)
    return fwd.astype(np.float32), inv.astype(np.float32)


def _filter_features(L):
    lag = np.arange(-L, L)
    m = np.minimum(np.abs(lag), L - 1)
    t = (np.arange(L, dtype=np.float32) / np.float32(max(L - 1, 1)))[m]
    w = (np.float32(2.0 * math.pi) * np.arange(L, dtype=np.float32) / np.float32(L))[m]
    bands = np.linspace(1e-4, HY_BANDS - 1, HY_BANDS, dtype=np.float32)
    z = np.zeros((2 * L, LANE), np.float32)
    z[:, 0] = t
    z[:, 1:1 + HY_BANDS] = np.cos(w[:, None] * bands)
    z[:, 1 + HY_BANDS:HY_EMB] = -np.sin(w[:, None] * bands)
    lo, hi = math.log(HY_DECAY_TARGET) / 1.5, math.log(HY_DECAY_TARGET) / 0.3
    deltas = np.abs(np.linspace(lo, hi, D_B, dtype=np.float32))
    dec = np.exp(-t[:, None] * deltas)
    return z, dec.astype(np.float32)


def _sincos_2d(rows, cols):
    quarter = D_MODEL // 4
    omega = 1.0 / (10000.0 ** (jnp.arange(quarter, dtype=F32) / quarter))

    def emb(n):
        ang = jnp.arange(n, dtype=F32)[:, None] * omega[None]
        return jnp.concatenate([jnp.sin(ang), jnp.cos(ang)], -1)

    er, ec = emb(rows), emb(cols)
    half = D_MODEL // 2
    pos = jnp.concatenate([jnp.broadcast_to(er[:, None], (rows, cols, half)),
                           jnp.broadcast_to(ec[None], (rows, cols, half))], -1)
    return pos.reshape(rows * cols, D_MODEL)


def _pad_to(x, shape):
    return jnp.pad(x, [(0, s - d) for d, s in zip(x.shape, shape)])


def _block_diag(w):
    eye = jnp.eye(H_A, dtype=w.dtype)
    return jnp.einsum("hij,hg->higj", w, eye).reshape(D_A, D_A)


_PAIR_LO = np.array([0, 0, 0, 1, 1, 2], np.int32)
_PAIR_HI = np.array([1, 2, 3, 2, 3, 3], np.int32)


def _routing_plan(info, counts, n_tiles_max):
    cls = info[:, I_CLS].astype(jnp.int32)
    rank = info[:, I_RANK].astype(jnp.int32)
    cnt = counts[0, :N_CLASS].astype(jnp.int32)
    tiles = (cnt + TM - 1) // TM
    tile_end = jnp.cumsum(tiles)
    row_start = (tile_end - tiles) * TM
    dest = row_start[cls] + rank
    n_act = tile_end[-1]
    t = jnp.minimum(jnp.arange(n_tiles_max, dtype=jnp.int32), n_act - 1)
    tcls = jnp.minimum(jnp.sum((tile_end[None, :] <= t[:, None]).astype(jnp.int32), 1), N_CLASS - 1)
    grp, pair = tcls // N_PAIRS, tcls % N_PAIRS
    tile_a = (grp * E_PER_GROUP + jnp.asarray(_PAIR_LO)[pair]).astype(jnp.int32)
    tile_b = (grp * E_PER_GROUP + jnp.asarray(_PAIR_HI)[pair]).astype(jnp.int32)
    first = jnp.ones((1,), jnp.int32)
    changed = lambda e: jnp.concatenate([first, (e[1:] != e[:-1]).astype(jnp.int32)])
    return {"dest": dest.astype(jnp.int32), "tile_a": tile_a, "tile_b": tile_b,
            "chg_a": changed(tile_a), "chg_b": changed(tile_b),
            "n_act": n_act.reshape(1).astype(jnp.int32),
            "pad_rows": jnp.where(tiles > 0, (tile_end - 1) * TM, -1).astype(jnp.int32)}


def kernel(x_prompt, x_sample, c, state_lru, state_mlstm_C, state_mlstm_n, state_mlstm_m, c_ctx, w_ada, b_ada, w_in, b_in, conv_a_w, conv_a_b, lru_wa, lru_ba, lru_wx, lru_bx, lru_lam, conv_b_w, conv_b_b, hy_w1, hy_b1, hy_w2, hy_b2, hy_freq, hy_w3, hy_bias, mix_g, w_out, ln1_g, ln1_b, rt_wg, rt_bg, rt_we, rt_be, moe_w1, moe_w3, moe_w2, ln2_g, ln2_b):
    B, l_ctx, D = x_prompt.shape
    b_lat, l_lat, _ = x_sample.shape
    n_ctx, n_lat = B * l_ctx, b_lat * l_lat
    n = n_ctx + n_lat
    assert D == D_MODEL and w_in.shape[-1] == D_MAIN + N_GATE
    assert SEG % l_ctx == 0 and l_lat == SEG and l_ctx % CH == 0 and n_ctx % SEG == 0
    assert l_ctx == CH, "the mLSTM step schedule assumes one chunk per context sequence"
    assert 1 + b_lat <= SUB
    n_ctx_blk = n_ctx // SEG
    n_ctx_tiles = n_ctx // TM
    tiles_per_lat = l_lat // TM
    nc_lat = l_lat // CH

    def group_of_tile(i):
        return jnp.where(i < n_ctx_tiles, 0, 1 + (i - n_ctx_tiles) // tiles_per_lat)

    cond = jnp.concatenate([c_ctx[None], c, jnp.zeros((SUB - 1 - b_lat, D), F32)], 0)
    mod = _mod_call(cond, w_ada, b_ada).reshape(DEPTH, SUB, 6, D)
    pos = _sincos_2d(l_lat // GRID_W, GRID_W)
    x = _entry_call(x_prompt.reshape(n_ctx, D), x_sample.reshape(n_lat, D), pos)

    fwd_np, inv_np = _dft_matrices()
    fwd32 = jnp.asarray(fwd_np)
    fwd16, inv16 = fwd32.astype(BF16), jnp.asarray(inv_np).astype(BF16)
    fw1 = _pad_to(hy_w1, (DEPTH, LANE, LANE))
    fb1 = _pad_to(hy_b1[:, None, :], (DEPTH, 1, LANE))
    fw2 = _pad_to(hy_w2, (DEPTH, LANE, LANE))
    fb2 = _pad_to(hy_b2[:, None, :], (DEPTH, 1, LANE))
    ffr = _pad_to(hy_freq[:, None, :], (DEPTH, 1, LANE))
    fw3 = _pad_to(hy_w3, (DEPTH, LANE, HY_ORDER * 2 * D_B))
    spectra = {}
    for L in (l_ctx, l_lat):
        z_np, dec_np = _filter_features(L)
        spectra[L] = _filt_call(L, jnp.asarray(z_np), jnp.asarray(dec_np), fw1, fb1, fw2, fb2, ffr, fw3, fwd32)

    lat_slots = SEG // l_ctx
    st_lru, st_c, st_n, st_m = [], [], [], []
    for l in range(DEPTH):
        w_main = w_in[l, :, :D_MAIN].astype(BF16)
        b_main = b_in[l, None, :D_MAIN]
        k_lo, k_hi = _Z_CUTS[_K_CUT], _Z_CUTS[_K_CUT + 1]
        w_kt = w_in[l, :, k_lo:k_hi].T.astype(BF16)
        b_kt = jnp.broadcast_to(b_in[l, k_lo:k_hi, None], (D_C, TM))
        w_gate = _pad_to(w_in[l, :, D_MAIN:], (D, LANE))
        b_gate = _pad_to(b_in[l, None, D_MAIN:], (1, LANE))
        xa, ya, hyb, q, v, og, gates, kt = _in_call(x, mod[l], w_main, b_main, w_kt, b_kt, w_gate, b_gate,
                                                    group_of_tile)

        lru_w = jnp.concatenate([_block_diag(lru_wa[l, 0]), _block_diag(lru_wx[l, 0]),
                                 _block_diag(lru_wa[l, 1]), _block_diag(lru_wx[l, 1])], 1).astype(BF16)
        lru_b = jnp.concatenate([lru_ba[l, 0], lru_bx[l, 0], lru_ba[l, 1], lru_bx[l, 1]])[None]
        h0_lat = _pad_to(state_lru[:, l][:, None], (b_lat, lat_slots, 2, D_A))
        h0_all = jnp.concatenate([jnp.zeros((n_ctx_blk, lat_slots, 2, D_A), F32), h0_lat], 0)
        out_a, lru_last = _lru_call(xa, ya, conv_a_w[l], conv_a_b[l, None], lru_w, lru_b, lru_lam[l], h0_all,
                                    l_ctx, l_lat, n_ctx_blk)

        habc, hd0c = spectra[l_ctx]
        habl, hd0l = spectra[l_lat]
        out_b = _hy_call(l, hyb, conv_b_w[l], conv_b_b[l, None], fwd16, inv16, habc, hd0c, habl, hd0l,
                         hy_bias[l][:, None, :], l_ctx, l_lat, n_ctx_blk)

        n0 = state_mlstm_n[:, l].reshape(b_lat, 2 * H_C, DK)
        cx0 = jnp.concatenate([state_mlstm_C[:, l].reshape(b_lat, 2 * H_C, DK, DK), n0[..., None],
                               jnp.zeros((b_lat, 2 * H_C, DK, DK - 1), F32)], -1)
        m0 = _pad_to(state_mlstm_m[:, l], (b_lat, SUB, LANE))
        hcf, hcb, c_fin, n_fin, m_fin = _mlstm_call(q, v, kt, gates, cx0, n0, m0, n_ctx // CH, nc_lat)

        rt_w = _pad_to(jnp.concatenate([rt_wg[l], rt_we[l]], 1), (D, LANE))
        rt_b = _pad_to(jnp.concatenate([rt_bg[l], rt_be[l]])[None], (1, LANE))
        x1, h_ext, counts = _out_call(x, out_a, out_b, hcf, hcb, og, mod[l], mix_g[l, None],
                                      w_out[l].astype(BF16), ln1_g[l, None], ln1_b[l, None], rt_w, rt_b,
                                      group_of_tile)

        n_tiles_max = n // TM + N_CLASS
        plan = _routing_plan(h_ext[:, D_MODEL:D_MODEL + SUB], counts, n_tiles_max)
        x_sorted = _scatter_call(plan["dest"], plan["pad_rows"], plan["n_act"], h_ext, n_tiles_max)
        y_sorted = _moe_call(l, plan, x_sorted, moe_w1, moe_w3, moe_w2)
        outs = _ln2_call(plan["dest"], x1, y_sorted, mod[l], ln2_g[l, None], ln2_b[l, None], group_of_tile,
                         n_ctx_tiles, split=(l == DEPTH - 1))
        x = outs[0]

        st_lru.append(lru_last[:n_ctx_blk].reshape(B, 2, D_A))
        st_c.append(c_fin.reshape(B, 2, H_C, DK, DK))
        st_n.append(n_fin.reshape(B, 2, H_C, DK))
        st_m.append(m_fin[:, :2, :H_C])

    return (outs[0].reshape(B, l_ctx, D), outs[1].reshape(b_lat, l_lat, D),
            jnp.stack(st_lru, 1), jnp.stack(st_c, 1), jnp.stack(st_n, 1), jnp.stack(st_m, 1))
```

```python
import functools
import math

import numpy as np
import jax
import jax.numpy as jnp
from jax import lax
from jax.experimental import pallas as pl
from jax.experimental.pallas import tpu as pltpu

F32 = jnp.float32
BF16 = jnp.bfloat16

D_MODEL = 1024
DEPTH = 2
GRID_W = 64
D_A = 256
H_A = 4
BA = D_A // H_A
LRU_C = 8.0
D_B = 256
HY_ORDER = 2
HY_BANDS = 16
HY_EMB = 1 + 2 * HY_BANDS
HY_FH = 64
HY_DECAY_TARGET = 1e-2
D_C = 512
H_C = 4
DK = D_C // H_C
N_GROUPS = 4
E_PER_GROUP = 4
N_EXP = N_GROUPS * E_PER_GROUP
N_PAIRS = 6
N_CLASS = N_GROUPS * N_PAIRS
D_E = 512
ALPHA = (2 * DEPTH) ** 0.25
EPS = 1e-6
D_MAIN = 2 * D_A + 3 * D_B + 4 * D_C
N_GATE = 4 * H_C

LANE = 128
SUB = 8
VMEM_LIMIT = 56 * 1024 * 1024

CH = 256
SEG = 2048
TM = 256
TM_IN = 512
D_EXT = D_MODEL + LANE
FS = 64
DMA_UNROLL = 8
RING = 3

I_CLS, I_ELO, I_EHI, I_WLO, I_WHI, I_RANK = range(6)


def _cp(sem, vmem=VMEM_LIMIT):
    return pltpu.CompilerParams(dimension_semantics=sem, vmem_limit_bytes=vmem)


def _dot(a, b):
    return jnp.dot(a, b, preferred_element_type=F32)


def _split2(x):
    hi = x.astype(BF16)
    lo = (x - hi.astype(F32)).astype(BF16)
    return hi, lo


def _dot3(a, b):
    ah, al = _split2(a)
    bh, bl = _split2(b)
    return _dot(ah, bh) + (_dot(ah, bl) + _dot(al, bh))


def _split3(x):
    hi = x.astype(BF16)
    r1 = x - hi.astype(F32)
    mid = r1.astype(BF16)
    lo = (r1 - mid.astype(F32)).astype(BF16)
    return hi, mid, lo


def _sigmoid(x):
    return 1.0 / (1.0 + jnp.exp(-x))


def _log_sigmoid(x):
    return jnp.minimum(x, 0.0) - jnp.log1p(jnp.exp(-jnp.abs(x)))


def _gelu_tanh(x):
    return 0.5 * x * (1.0 + jnp.tanh(math.sqrt(2.0 / math.pi) * (x + 0.044715 * (x * x * x))))


def _ln_plain(x):
    mu = jnp.mean(x, -1, keepdims=True)
    xc = x - mu
    var = jnp.mean(xc * xc, -1, keepdims=True)
    return xc * lax.rsqrt(var + EPS)


def _rms(x):
    return x * lax.rsqrt(jnp.mean(x * x, -1, keepdims=True) + EPS)


def _halo_rows(ref, start, rows):
    total = ref.shape[0]
    prev = ref[pl.ds(pl.multiple_of(jnp.maximum(start - SUB, 0), SUB), SUB), :]
    main = ref[pl.ds(start, rows), :]
    nxt = ref[pl.ds(pl.multiple_of(jnp.minimum(start + rows, total - SUB), SUB), SUB), :]
    return jnp.concatenate([prev, main, nxt], axis=0), main


def _mod_kernel(c_ref, w_ref, b_ref, o_ref):
    c = c_ref[...]
    o_ref[0] = _dot3(c * _sigmoid(c), w_ref[0]) + b_ref[0]


def _mod_call(cond, w_ada, b_ada):
    tn = 1536
    n6 = w_ada.shape[-1]
    return pl.pallas_call(
        _mod_kernel,
        grid=(DEPTH, n6 // tn),
        in_specs=[pl.BlockSpec((SUB, D_MODEL), lambda l, j: (0, 0)),
                  pl.BlockSpec((1, D_MODEL, tn), lambda l, j: (l, 0, j)),
                  pl.BlockSpec((1, 1, tn), lambda l, j: (l, 0, j))],
        out_specs=pl.BlockSpec((1, SUB, tn), lambda l, j: (l, 0, j)),
        out_shape=jax.ShapeDtypeStruct((DEPTH, SUB, n6), F32),
        compiler_params=_cp(("parallel", "parallel")),
        name="adaln_mod",
    )(cond, w_ada, b_ada.reshape(DEPTH, 1, n6))


def _entry_kernel(xc_ref, xl_ref, pos_ref, o_ref, *, n_ctx_tiles):
    i = pl.program_id(0)

    @pl.when(i < n_ctx_tiles)
    def _():
        o_ref[...] = _ln_plain(xc_ref[...])

    @pl.when(i >= n_ctx_tiles)
    def _():
        o_ref[...] = _ln_plain(xl_ref[...] + pos_ref[...])


def _entry_call(xc, xl, pos):
    tm = 512
    n_ctx, n_lat, l_lat = xc.shape[0], xl.shape[0], pos.shape[0]
    nct = n_ctx // tm
    per_seq = l_lat // tm
    return pl.pallas_call(
        functools.partial(_entry_kernel, n_ctx_tiles=nct),
        grid=((n_ctx + n_lat) // tm,),
        in_specs=[pl.BlockSpec((tm, D_MODEL), lambda i: (jnp.minimum(i, nct - 1), 0)),
                  pl.BlockSpec((tm, D_MODEL), lambda i: (jnp.maximum(i - nct, 0), 0)),
                  pl.BlockSpec((tm, D_MODEL), lambda i: (jnp.maximum(i - nct, 0) % per_seq, 0))],
        out_specs=pl.BlockSpec((tm, D_MODEL), lambda i: (i, 0)),
        out_shape=jax.ShapeDtypeStruct((n_ctx + n_lat, D_MODEL), F32),
        compiler_params=_cp(("parallel",)),
        name="entry_ln",
    )(xc, xl, pos)


_Z_CUTS = (0, D_A, 2 * D_A, 2 * D_A + 3 * D_B, 2 * D_A + 3 * D_B + D_C, 2 * D_A + 3 * D_B + 2 * D_C,
           2 * D_A + 3 * D_B + 3 * D_C, D_MAIN)


_K_CUT = 4
_ROW_CUTS = tuple(c for i, c in enumerate(zip(_Z_CUTS[:-1], _Z_CUTS[1:])) if i != _K_CUT)
_NT = (((1,), (1,)), ((), ()))


def _in_kernel(x_ref, mod_ref, w_ref, b_ref, wkt_ref, bkt_ref, wg_ref, bg_ref, *refs):
    out_refs, w16 = refs[:-1], refs[-1]

    @pl.when(pl.program_id(0) == 0)
    def _():
        w16[...] = w_ref[0, :, :D_MAIN].astype(BF16)

    m = mod_ref[0]
    h = x_ref[...] * (1.0 + m[1:2]) + m[0:1]
    hb = h.astype(BF16)
    for ref, (a, b) in zip(out_refs[:-2], _ROW_CUTS):
        ref[...] = _dot(hb, w16[:, a:b]) + b_ref[:, a:b]
    g_ref, kt_ref = out_refs[-2:]
    g_ref[...] = _dot3(h, wg_ref[...]) + bg_ref[...]
    kt_ref[...] = lax.dot_general(wkt_ref[...], hb, _NT, preferred_element_type=F32) + bkt_ref[...]


def _in_call(layer, x, mod_l, w_in, b_main, w_kt, b_kt, w_gate, b_gate, group_of_tile):
    n = x.shape[0]
    widths = [b - a for a, b in _ROW_CUTS] + [LANE]
    row = lambda i: (i, 0)
    fixed = lambda i: (0, 0)
    return pl.pallas_call(
        _in_kernel,
        grid=(n // TM_IN,),
        in_specs=[pl.BlockSpec((TM_IN, D_MODEL), row),
                  pl.BlockSpec((1, 6, D_MODEL), lambda i: (group_of_tile(i), 0, 0)),
                  pl.BlockSpec((1, D_MODEL, w_in.shape[-1]), lambda i: (layer, 0, 0), pipeline_mode=pl.Buffered(1)),
                  pl.BlockSpec((1, D_MAIN), fixed),
                  pl.BlockSpec((D_C, D_MODEL), fixed), pl.BlockSpec((D_C, TM_IN), fixed),
                  pl.BlockSpec((D_MODEL, LANE), fixed), pl.BlockSpec((1, LANE), fixed)],
        out_specs=[pl.BlockSpec((TM_IN, w), row) for w in widths] + [pl.BlockSpec((D_C, TM_IN), lambda i: (0, i))],
        out_shape=[jax.ShapeDtypeStruct((n, w), F32) for w in widths] + [jax.ShapeDtypeStruct((D_C, n), F32)],
        scratch_shapes=[pltpu.VMEM((D_MODEL, D_MAIN), BF16)],
        compiler_params=_cp(("arbitrary",)),
        name="in_proj",
    )(x, mod_l, w_in, b_main, w_kt, b_kt, w_gate, b_gate)


def _lru_variant(L, xa_ref, ya_ref, cw_ref, cb_ref, wg_ref, bg_ref, lam_ref, h0_ref, o_ref, st_ref,
                 s_af, s_bf, s_ab, s_bb):
    nch, nseq, ntile = SEG // CH, SEG // L, L // SUB
    lam = lam_ref[...]
    sp = jnp.maximum(-lam, 0.0) + jnp.log1p(jnp.exp(-jnp.abs(lam)))
    cw = cw_ref[...]
    cb = cb_ref[...]
    row = lax.broadcasted_iota(jnp.int32, (CH, 1), 0)
    sub = row & (SUB - 1)

    def gates_and_tile_scan(c, carry):
        start = pl.multiple_of(c * CH, CH)
        xcat, main = _halo_rows(xa_ref, start, CH)
        tpos = (start + row) & (L - 1)
        xm2 = jnp.where(tpos >= 2, xcat[SUB - 2:SUB - 2 + CH], 0.0)
        xm1 = jnp.where(tpos >= 1, xcat[SUB - 1:SUB - 1 + CH], 0.0)
        xp1 = jnp.where(tpos <= L - 2, xcat[SUB + 1:SUB + 1 + CH], 0.0)
        xc = cw[0:1] * xm2 + cw[1:2] * xm1 + cw[2:3] * main + cw[3:4] * xp1 + cb
        g = _dot(xc.astype(BF16), wg_ref[...]) + bg_ref[...]
        for d, (sa, sb) in enumerate(((s_af, s_bf), (s_ab, s_bb))):
            r = _sigmoid(g[:, 2 * d * D_A:(2 * d + 1) * D_A])
            ig = _sigmoid(g[:, (2 * d + 1) * D_A:(2 * d + 2) * D_A])
            a = jnp.exp(-LRU_C * r * sp[d:d + 1])
            b = jnp.sqrt(1.0 - a * a) * (ig * xc)
            for s in (1, 2, 4):
                if d == 0:
                    keep = sub >= s
                    a_sh, b_sh = pltpu.roll(a, s, 0), pltpu.roll(b, s, 0)
                else:
                    keep = sub < SUB - s
                    a_sh, b_sh = pltpu.roll(a, CH - s, 0), pltpu.roll(b, CH - s, 0)
                b = a * jnp.where(keep, b_sh, 0.0) + b
                a = a * jnp.where(keep, a_sh, 1.0)
            sa[pl.ds(start, CH), :] = a
            sb[pl.ds(start, CH), :] = b
        return carry

    lax.fori_loop(0, nch, gates_and_tile_scan, 0)

    def carry_tiles(k, carry):
        cf, cbk = carry
        nf, nb = [], []
        for s in range(nseq):
            rf = pl.multiple_of(s * L + k * SUB, SUB)
            hf = s_af[pl.ds(rf, SUB), :] * cf[s] + s_bf[pl.ds(rf, SUB), :]
            s_bf[pl.ds(rf, SUB), :] = hf
            nf.append(hf[SUB - 1:SUB, :])
            rb = pl.multiple_of(s * L + (ntile - 1 - k) * SUB, SUB)
            hb = s_ab[pl.ds(rb, SUB), :] * cbk[s] + s_bb[pl.ds(rb, SUB), :]
            s_bb[pl.ds(rb, SUB), :] = hb
            nb.append(hb[0:1, :])
        return tuple(nf), tuple(nb)

    cf0 = tuple(h0_ref[0, s, 0:1, :] for s in range(nseq))
    cb0 = tuple(h0_ref[0, s, 1:2, :] for s in range(nseq))
    cf, cbk = lax.fori_loop(0, ntile, carry_tiles, (cf0, cb0))

    st_ref[...] = jnp.zeros(st_ref.shape, F32)
    for s in range(nseq):
        st_ref[0, s] = jnp.concatenate([cf[s], cbk[s]], axis=0)

    def finish(c, carry):
        start = pl.multiple_of(c * CH, CH)
        h = s_bf[pl.ds(start, CH), :] + s_bb[pl.ds(start, CH), :]
        o_ref[pl.ds(start, CH), :] = _rms(_gelu_tanh(ya_ref[pl.ds(start, CH), :]) * h)
        return carry

    lax.fori_loop(0, nch, finish, 0)


def _lru_kernel(*refs, l_ctx, l_lat, n_ctx_blk):
    i = pl.program_id(0)

    @pl.when(i < n_ctx_blk)
    def _():
        _lru_variant(l_ctx, *refs)

    @pl.when(i >= n_ctx_blk)
    def _():
        _lru_variant(l_lat, *refs)


def _lru_call(xa, ya, conv_w, conv_b, w_gate, b_gate, lam, h0_all, l_ctx, l_lat, n_ctx_blk):
    n = xa.shape[0]
    nblk = n // SEG
    row = lambda i: (i, 0)
    fixed = lambda i: (0, 0)
    slots = SEG // l_ctx
    return pl.pallas_call(
        functools.partial(_lru_kernel, l_ctx=l_ctx, l_lat=l_lat, n_ctx_blk=n_ctx_blk),
        grid=(nblk,),
        in_specs=[pl.BlockSpec((SEG, D_A), row), pl.BlockSpec((SEG, D_A), row),
                  pl.BlockSpec((4, D_A), fixed), pl.BlockSpec((1, D_A), fixed),
                  pl.BlockSpec((D_A, 4 * D_A), fixed), pl.BlockSpec((1, 4 * D_A), fixed),
                  pl.BlockSpec((2, D_A), fixed),
                  pl.BlockSpec((1, slots, 2, D_A), lambda i: (i, 0, 0, 0))],
        out_specs=[pl.BlockSpec((SEG, D_A), row),
                   pl.BlockSpec((1, slots, 2, D_A), lambda i: (i, 0, 0, 0))],
        out_shape=[jax.ShapeDtypeStruct((n, D_A), F32),
                   jax.ShapeDtypeStruct((nblk, slots, 2, D_A), F32)],
        scratch_shapes=[pltpu.VMEM((SEG, D_A), F32) for _ in range(4)],
        compiler_params=_cp(("parallel",)),
        name="rglru",
    )(xa, ya, conv_w, conv_b, w_gate, b_gate, lam, h0_all)


def _filt_kernel(z_ref, dec_ref, w1_ref, b1_ref, w2_ref, b2_ref, fr_ref, w3_ref, fwd_ref,
                 oab_ref, od0_ref, s_k, s_kf, *, L):
    nblk = 2 * L // CH
    d_idx = pl.program_id(1)
    row = lax.broadcasted_iota(jnp.int32, (CH, 1), 0)

    @pl.when(d_idx == 0)
    def _():
        fr = fr_ref[0]

        def taps(c, carry):
            start = pl.multiple_of(c * CH, CH)
            h1 = jnp.sin(fr * (_dot3(z_ref[pl.ds(start, CH), :], w1_ref[0]) + b1_ref[0]))
            h2 = jnp.sin(fr * (_dot3(h1, w2_ref[0]) + b2_ref[0]))
            t = _dot3(h2, w3_ref[0])
            dec = dec_ref[pl.ds(start, CH), :]
            rg = start + row
            for o in range(HY_ORDER):
                fwd_t = t[:, (2 * o) * D_B:(2 * o + 1) * D_B]
                bwd_t = t[:, (2 * o + 1) * D_B:(2 * o + 2) * D_B]
                ko = jnp.where(rg < L, bwd_t, fwd_t) * dec
                s_k[pl.ds(start, CH), o * D_B:(o + 1) * D_B] = jnp.where(rg == 0, 0.0, ko)
            return carry

        lax.fori_loop(0, nblk, taps, 0)
        fwd = fwd_ref[...]

        def spectra(e, carry):
            start = pl.multiple_of(e * CH, CH)
            s_kf[e] = _dot3(fwd, s_k[pl.ds(start, CH), :])
            return carry

        lax.fori_loop(0, nblk, spectra, 0)

    kd = s_kf[d_idx + 1]
    km = s_kf[d_idx]
    k0 = s_k[pl.ds(pl.multiple_of(d_idx * CH, CH), 1), :]
    sgn = jnp.where((row & 1) == 0, 1.0, -1.0)
    a = kd[:CH] + sgn * (km[:CH] - k0)
    b = jnp.where(row == 0, 0.0, kd[CH:] + sgn * km[CH:])
    hn = kd[CH:CH + 1] + km[CH:CH + 1] - k0
    for o in range(HY_ORDER):
        oab_ref[0, o, 0, 0] = a[:, o * D_B:(o + 1) * D_B]
        oab_ref[0, o, 0, 1] = b[:, o * D_B:(o + 1) * D_B]
        od0_ref[0, o, 0] = jnp.broadcast_to(hn[:, o * D_B:(o + 1) * D_B], (SUB, D_B))


def _filt_call(L, z, dec, w1, b1, w2, b2, fr, w3, fwd32):
    nd = 2 * (L // CH) - 1
    fixed = lambda l, d: (0, 0)
    lay3 = lambda l, d: (l, 0, 0)
    return pl.pallas_call(
        functools.partial(_filt_kernel, L=L),
        grid=(DEPTH, nd),
        in_specs=[pl.BlockSpec((2 * L, LANE), fixed), pl.BlockSpec((2 * L, D_B), fixed),
                  pl.BlockSpec((1, LANE, LANE), lay3), pl.BlockSpec((1, 1, LANE), lay3),
                  pl.BlockSpec((1, LANE, LANE), lay3), pl.BlockSpec((1, 1, LANE), lay3),
                  pl.BlockSpec((1, 1, LANE), lay3),
                  pl.BlockSpec((1, LANE, HY_ORDER * 2 * D_B), lay3),
                  pl.BlockSpec((2 * CH, CH), fixed)],
        out_specs=[pl.BlockSpec((1, HY_ORDER, 1, 2, CH, D_B), lambda l, d: (l, 0, d, 0, 0, 0)),
                   pl.BlockSpec((1, HY_ORDER, 1, SUB, D_B), lambda l, d: (l, 0, d, 0, 0))],
        out_shape=[jax.ShapeDtypeStruct((DEPTH, HY_ORDER, nd, 2, CH, D_B), F32),
                   jax.ShapeDtypeStruct((DEPTH, HY_ORDER, nd, SUB, D_B), F32)],
        scratch_shapes=[pltpu.VMEM((2 * L, HY_ORDER * D_B), F32),
                        pltpu.VMEM((2 * L // CH, 2 * CH, HY_ORDER * D_B), F32)],
        compiler_params=_cp(("parallel", "arbitrary")),
        name=f"hyena_filter_{L}",
    )(z, dec, w1, b1, w2, b2, fr, w3, fwd32)


def _hy_variant(L, o_idx, hy_ref, cw_ref, cb_ref, fwd_ref, inv_ref, hab_ref, hd0_ref, bias_ref, o_ref,
                s_y, s_x, s_u, s_v):
    nch, nseq, P = SEG // CH, SEG // L, L // CH
    row = lax.broadcasted_iota(jnp.int32, (CH, 1), 0)
    frow = lax.broadcasted_iota(jnp.int32, (FS, 1), 0)

    @pl.when(o_idx == 0)
    def _():
        cw = cw_ref[...]
        cb = cb_ref[...]

        def short_conv(c, carry):
            start = pl.multiple_of(c * CH, CH)
            xcat, main = _halo_rows(hy_ref, start, CH)
            tpos = (start + row) & (L - 1)
            xm1 = jnp.where(tpos >= 1, xcat[SUB - 1:SUB - 1 + CH], 0.0)
            xp1 = jnp.where(tpos <= L - 2, xcat[SUB + 1:SUB + 1 + CH], 0.0)
            hc = cw[0:1] * xm1 + cw[1:2] * main + cw[2:3] * xp1 + cb
            s_y[pl.ds(start, CH), :] = hc[:, :D_B]
            s_x[0, pl.ds(start, CH), :] = hc[:, D_B:2 * D_B]
            s_x[1, pl.ds(start, CH), :] = hc[:, 2 * D_B:]
            return carry

        lax.fori_loop(0, nch, short_conv, 0)

    bias = bias_ref[0]

    def one_sequence(s, carry):
        base = s * L

        def forward_dft(j, cc):
            r = pl.multiple_of(base + j * CH, CH)
            s_u[j] = _dot(fwd_ref[...], s_y[pl.ds(r, CH), :].astype(BF16))
            return cc

        lax.fori_loop(0, P, forward_dft, 0)

        def output_block(i, cc):
            for fs in range(CH // FS):
                lo = fs * FS

                def accumulate(j, acc):
                    yre, yim = acc
                    d = i - j + (P - 1)
                    ure = s_u[j, lo:lo + FS, :]
                    uim = s_u[j, CH + lo:CH + lo + FS, :]
                    a = hab_ref[0, 0, d, 0, lo:lo + FS, :]
                    b = hab_ref[0, 0, d, 1, lo:lo + FS, :]
                    dd = jnp.where(frow == 0, hd0_ref[0, 0, d, 0:1, :], a) if fs == 0 else a
                    return yre + ure * a - uim * b, yim + ure * b + uim * dd

                zero = jnp.zeros((FS, D_B), F32)
                yre, yim = lax.fori_loop(0, P, accumulate, (zero, zero))
                s_v[lo:lo + FS, :] = yre.astype(BF16)
                s_v[CH + lo:CH + lo + FS, :] = yim.astype(BF16)
            yc = _dot(inv_ref[...], s_v[...])
            r = pl.multiple_of(base + i * CH, CH)
            s_y[pl.ds(r, CH), :] = s_x[o_idx, pl.ds(r, CH), :] * (yc + s_y[pl.ds(r, CH), :] * bias)
            return cc

        lax.fori_loop(0, P, output_block, 0)
        return carry

    lax.fori_loop(0, nseq, one_sequence, 0)

    @pl.when(o_idx == HY_ORDER - 1)
    def _():
        def finish(c, carry):
            start = pl.multiple_of(c * CH, CH)
            o_ref[pl.ds(start, CH), :] = _rms(s_y[pl.ds(start, CH), :])
            return carry

        lax.fori_loop(0, nch, finish, 0)


def _hy_kernel(hy_ref, cw_ref, cb_ref, fwd_ref, inv_ref, habc_ref, hd0c_ref, habl_ref, hd0l_ref, bias_ref,
               o_ref, s_y, s_x, s_u, s_v, *, l_ctx, l_lat, n_ctx_blk):
    i = pl.program_id(0)
    o_idx = pl.program_id(1)
    scratch = (s_y, s_x, s_u, s_v)

    @pl.when(i < n_ctx_blk)
    def _():
        _hy_variant(l_ctx, o_idx, hy_ref, cw_ref, cb_ref, fwd_ref, inv_ref, habc_ref, hd0c_ref, bias_ref, o_ref,
                    *scratch)

    @pl.when(i >= n_ctx_blk)
    def _():
        _hy_variant(l_lat, o_idx, hy_ref, cw_ref, cb_ref, fwd_ref, inv_ref, habl_ref, hd0l_ref, bias_ref, o_ref,
                    *scratch)


def _hy_call(layer, hyb, conv_w, conv_b, fwd, inv, habc, hd0c, habl, hd0l, bias, l_ctx, l_lat, n_ctx_blk):
    n = hyb.shape[0]
    ndc, ndl = habc.shape[2], habl.shape[2]
    pmax = max(l_ctx, l_lat) // CH
    row = lambda i, o: (i, 0)
    fixed = lambda i, o: (0, 0)
    lat_o = lambda i, o: jnp.where(i >= n_ctx_blk, o, 0)
    ctx_o = lambda i, o: jnp.where(i < n_ctx_blk, o, 0)
    return pl.pallas_call(
        functools.partial(_hy_kernel, l_ctx=l_ctx, l_lat=l_lat, n_ctx_blk=n_ctx_blk),
        grid=(n // SEG, HY_ORDER),
        in_specs=[pl.BlockSpec((SEG, 3 * D_B), row),
                  pl.BlockSpec((3, 3 * D_B), fixed), pl.BlockSpec((1, 3 * D_B), fixed),
                  pl.BlockSpec((2 * CH, CH), fixed), pl.BlockSpec((CH, 2 * CH), fixed),
                  pl.BlockSpec((1, 1, ndc, 2, CH, D_B), lambda i, o: (layer, ctx_o(i, o), 0, 0, 0, 0)),
                  pl.BlockSpec((1, 1, ndc, SUB, D_B), lambda i, o: (layer, ctx_o(i, o), 0, 0, 0)),
                  pl.BlockSpec((1, 1, ndl, 2, CH, D_B), lambda i, o: (layer, lat_o(i, o), 0, 0, 0, 0)),
                  pl.BlockSpec((1, 1, ndl, SUB, D_B), lambda i, o: (layer, lat_o(i, o), 0, 0, 0)),
                  pl.BlockSpec((1, 1, D_B), lambda i, o: (o, 0, 0))],
        out_specs=pl.BlockSpec((SEG, D_B), row),
        out_shape=jax.ShapeDtypeStruct((n, D_B), F32),
        scratch_shapes=[pltpu.VMEM((SEG, D_B), F32), pltpu.VMEM((HY_ORDER, SEG, D_B), F32),
                        pltpu.VMEM((pmax, 2 * CH, D_B), F32), pltpu.VMEM((2 * CH, D_B), BF16)],
        compiler_params=_cp(("parallel", "arbitrary")),
        name="hyena",
    )(hyb, conv_w, conv_b, fwd, inv, habc, hd0c, habl, hd0l, bias)


def _row_scan(x, op, fill, reverse):
    t = x.shape[0]
    sub = lax.broadcasted_iota(jnp.int32, (t, 1), 0) & (SUB - 1)
    for s in (1, 2, 4):
        if reverse:
            shifted, keep = pltpu.roll(x, t - s, 0), sub < SUB - s
        else:
            shifted, keep = pltpu.roll(x, s, 0), sub >= s
        x = op(x, jnp.where(keep, shifted, fill))
    n_tiles = t // SUB
    out = [None] * n_tiles
    carry = None
    for i in (reversed(range(n_tiles)) if reverse else range(n_tiles)):
        tile = x[i * SUB:(i + 1) * SUB]
        out[i] = tile if carry is None else op(tile, carry)
        carry = out[i][0:1] if reverse else out[i][SUB - 1:SUB]
    return jnp.concatenate(out, axis=0)


_STK_ONE = 3 * SUB


def _mlstm_prep(d, g_ref, m_old):
    T = CH
    reverse = d == 1
    g = g_ref[...]
    if d == 1:
        g = pltpu.roll(g, LANE - 2 * H_C, 1)
    lane = lax.broadcasted_iota(jnp.int32, (1, LANE), 1)
    head = lane < H_C
    b = pltpu.roll(_row_scan(_log_sigmoid(g), jnp.add, 0.0, reverse), LANE - H_C, 1)
    r = jnp.where(head, g - b, 0.0)
    big_m = jnp.maximum(m_old, _row_scan(r, jnp.maximum, -jnp.inf, reverse))
    last = 0 if reverse else T - 1
    m_last = big_m[last:last + 1, :]
    low = lane < SUB
    p0, p1, p2 = (jnp.where(low, p.astype(F32), 0.0) for p in _split3(-big_m))
    cols = (p0 + pltpu.roll(p1, SUB, 1) + pltpu.roll(p2, 2 * SUB, 1)
            + jnp.where(jnp.logical_and(lane >= _STK_ONE, lane < _STK_ONE + SUB), 1.0, 0.0))
    rowid = lax.broadcasted_iota(jnp.int32, (SUB, 1), 0)
    r8 = r.T[0:SUB, :]
    m_last8 = sum(jnp.where(rowid == h, m_last[:, h:h + 1], 0.0) for h in range(H_C))
    ws8 = jnp.where(rowid < H_C, jnp.exp(r8 - m_last8), 0.0)
    return {"r3": [p.astype(F32) for p in _split3(r8)], "ws8": ws8, "wc": jnp.exp(m_old - m_last),
            "m_new": jnp.where(head, b[last:last + 1, :] + m_last, 0.0),
            "wi": jnp.exp(m_old - big_m), "e": jnp.exp(-(b + big_m)), "cols_b": cols.astype(BF16)}


def _mlstm_variant(carry, qf, vf, ktf, gf, qb_, vb_, ktb_, gb_, hf_ref, hb_ref, co_ref, no_ref, mo_ref,
                   s_cx, s_n, s_m):
    T = CH
    ii = lax.broadcasted_iota(jnp.int32, (T, T), 0)
    jj = lax.broadcasted_iota(jnp.int32, (T, T), 1)
    rowid = lax.broadcasted_iota(jnp.int32, (SUB, 1), 0)
    one_col = jnp.where(lax.broadcasted_iota(jnp.int32, (T, DK), 1) == 0, 1.0, 0.0).astype(BF16)
    prep = []
    for d, g_ref in ((0, gf), (1, gb_ if carry else gf)):
        m_old = s_m[d:d + 1, :] if carry else jnp.zeros((1, LANE), F32)
        prep.append(_mlstm_prep(d, g_ref, m_old))
    refs = ((qf, vf, ktf, hf_ref), (qb_, vb_, ktb_, hb_ref))
    for h in range(H_C):
        sl = slice(h * DK, (h + 1) * DK)
        for d in range(2):
            p = prep[d]
            q_ref, v_ref, kt_ref, h_ref = refs[d]
            idx = d * H_C + h
            if carry or d == 0:
                qb = (q_ref[:, sl] * (DK ** -0.5)).astype(BF16)
                kt = kt_ref[sl, :]
                ktb = kt.astype(BF16)
                v_ext = jnp.concatenate([v_ref[:, sl].astype(BF16), one_col], axis=1)
                s_raw = _dot(qb, ktb)
            tri = (jj >= ii) if d == 1 else (jj <= ii)
            sel = jnp.broadcast_to(jnp.where(rowid == h, 1.0, 0.0), (SUB, T))
            rr = sum(jnp.where(rowid == i, piece[h:h + 1, :], 0.0) for i, piece in enumerate(p["r3"]))
            rmat = jnp.concatenate([sel, sel, sel, rr, jnp.zeros((LANE - 4 * SUB, T), F32)], axis=0)
            expo = _dot(p["cols_b"], rmat.astype(BF16))
            s = s_raw * jnp.exp(jnp.where(tri, expo, -jnp.inf))
            intra = _dot(s.astype(BF16), v_ext)
            num, den = intra[:, :DK], intra[:, DK:DK + 1]
            if carry:
                cx = s_cx[idx]
                inter = _dot(qb, cx.astype(BF16))
                wi = p["wi"][:, h:h + 1]
                num, den = num + wi * inter[:, :DK], den + wi * inter[:, DK:DK + 1]
            h_ref[:, sl] = num / jnp.maximum(jnp.abs(den), p["e"][:, h:h + 1])
            upd = _dot((kt * p["ws8"][h:h + 1, :]).astype(BF16), v_ext)
            n_upd = lax.dot_general(p["ws8"].astype(BF16), ktb, _NT, preferred_element_type=F32)[h:h + 1, :]
            if carry:
                wc = p["wc"][:, h:h + 1]
                s_cx[idx] = wc * cx + upd
                s_n[idx:idx + 1, :] = wc * s_n[idx:idx + 1, :] + n_upd
            else:
                co_ref[0, idx] = upd[:, :DK]
                no_ref[0, idx:idx + 1, :] = n_upd
    m_rows = jnp.concatenate([prep[0]["m_new"], prep[1]["m_new"], jnp.zeros((SUB - 2, LANE), F32)], axis=0)
    if carry:
        s_m[...] = m_rows
    else:
        mo_ref[0] = m_rows


def _mlstm_kernel(*refs, n_ctx_steps, nc_lat):
    cx0_ref, n0_ref, m0_ref = refs[8:11]
    s_cx, s_n, s_m = refs[-3:]
    data = refs[:8] + refs[11:]
    t = pl.program_id(0)
    is_ctx = t < n_ctx_steps

    @pl.when(is_ctx)
    def _():
        _mlstm_variant(False, *data)

    @pl.when(jnp.logical_not(is_ctx))
    def _():
        @pl.when((t - n_ctx_steps) % nc_lat == 0)
        def _():
            s_cx[...] = cx0_ref[0]
            s_n[...] = n0_ref[0]
            s_m[...] = m0_ref[0]

        _mlstm_variant(True, *data)


def _mlstm_call(q, v, kt, gates, cx0, n0, m0, n_ctx_steps, nc_lat):
    n = q.shape[0]
    steps = n // CH
    nst = 2 * H_C

    def bwd_blk(t):
        r = jnp.maximum(t - n_ctx_steps, 0)
        return n_ctx_steps + (r // nc_lat) * nc_lat + (nc_lat - 1 - r % nc_lat)

    out_bwd = lambda t: jnp.where(t < n_ctx_steps, t, bwd_blk(t))
    lat_b = lambda t: jnp.maximum(t - n_ctx_steps, 0) // nc_lat
    ctx_b = lambda t: jnp.minimum(t, n_ctx_steps - 1)
    rows = lambda w, blk: pl.BlockSpec((CH, w), lambda t: (blk(t), 0))
    cols = lambda h, blk: pl.BlockSpec((h, CH), lambda t: (0, blk(t)))
    ident = lambda t: t
    return pl.pallas_call(
        functools.partial(_mlstm_kernel, n_ctx_steps=n_ctx_steps, nc_lat=nc_lat),
        grid=(steps,),
        in_specs=[rows(D_C, ident), rows(D_C, ident), cols(D_C, ident), rows(LANE, ident),
                  rows(D_C, bwd_blk), rows(D_C, bwd_blk), cols(D_C, bwd_blk), rows(LANE, bwd_blk),
                  pl.BlockSpec((1, nst, DK, 2 * DK), lambda t: (lat_b(t), 0, 0, 0)),
                  pl.BlockSpec((1, nst, DK), lambda t: (lat_b(t), 0, 0)),
                  pl.BlockSpec((1, SUB, LANE), lambda t: (lat_b(t), 0, 0))],
        out_specs=[rows(D_C, ident), rows(D_C, out_bwd),
                   pl.BlockSpec((1, nst, DK, DK), lambda t: (ctx_b(t), 0, 0, 0)),
                   pl.BlockSpec((1, nst, DK), lambda t: (ctx_b(t), 0, 0)),
                   pl.BlockSpec((1, SUB, LANE), lambda t: (ctx_b(t), 0, 0))],
        out_shape=[jax.ShapeDtypeStruct((n, D_C), F32), jax.ShapeDtypeStruct((n, D_C), F32),
                   jax.ShapeDtypeStruct((n_ctx_steps, nst, DK, DK), F32),
                   jax.ShapeDtypeStruct((n_ctx_steps, nst, DK), F32),
                   jax.ShapeDtypeStruct((n_ctx_steps, SUB, LANE), F32)],
        scratch_shapes=[pltpu.VMEM((nst, DK, 2 * DK), F32), pltpu.VMEM((nst, DK), F32),
                        pltpu.VMEM((SUB, LANE), F32)],
        compiler_params=_cp(("arbitrary",)),
        name="mlstm",
    )(q, v, kt, gates, q, v, kt, gates, cx0, n0, m0)


def _out_kernel(x_ref, oa_ref, ob_ref, hf_ref, hb_ref, og_ref, mod_ref, mg_ref, w_ref, g_ref, b_ref,
                rw_ref, rb_ref, x1_ref, he_ref, cnt_ref, it_ref, s_cnt):
    i = pl.program_id(0)

    @pl.when(i == 0)
    def _():
        s_cnt[...] = jnp.zeros(s_cnt.shape, F32)

    m = mod_ref[0]
    mg = mg_ref[...]
    acc = _dot((oa_ref[...] * mg[:, :D_A]).astype(BF16), w_ref[0:D_A, :])
    acc += _dot((ob_ref[...] * mg[:, D_A:D_A + D_B]).astype(BF16), w_ref[D_A:D_A + D_B, :])
    hc = hf_ref[...] + hb_ref[...]
    og = og_ref[...]
    off = D_A + D_B
    for h in range(H_C):
        sl = slice(h * DK, (h + 1) * DK)
        oc = _sigmoid(og[:, sl]) * _rms(hc[:, sl]) * mg[:, off + h * DK:off + (h + 1) * DK]
        acc += _dot(oc.astype(BF16), w_ref[off + h * DK:off + (h + 1) * DK, :])
    x1 = _ln_plain(ALPHA * x_ref[...] + m[2:3] * acc) * g_ref[...] + b_ref[...]
    x1_ref[...] = x1
    h2 = x1 * (1.0 + m[4:5]) + m[3:4]
    he_ref[:, :D_MODEL] = h2

    lg = _dot3(h2, rw_ref[...]) + rb_ref[...]
    col = lax.broadcasted_iota(jnp.int32, lg.shape, 1)
    ninf = -jnp.inf
    lgm = jnp.where(col < N_GROUPS, lg, ninf)
    mx = jnp.max(lgm, -1, keepdims=True)
    gi = jnp.min(jnp.where(lgm == mx, col, LANE), -1, keepdims=True)
    pg_top = 1.0 / jnp.sum(jnp.where(col < N_GROUPS, jnp.exp(lg - mx), 0.0), -1, keepdims=True)
    lo4 = N_GROUPS + E_PER_GROUP * gi
    lem = jnp.where(jnp.logical_and(col >= lo4, col < lo4 + E_PER_GROUP), lg, ninf)
    v1 = jnp.max(lem, -1, keepdims=True)
    i1 = jnp.min(jnp.where(lem == v1, col, LANE), -1, keepdims=True)
    lem2 = jnp.where(col == i1, ninf, lem)
    v2 = jnp.max(lem2, -1, keepdims=True)
    i2 = jnp.min(jnp.where(lem2 == v2, col, LANE), -1, keepdims=True)
    e21 = jnp.exp(v2 - v1)
    w1 = pg_top / (1.0 + e21)
    w2 = pg_top * e21 / (1.0 + e21)
    e1, e2 = i1 - N_GROUPS, i2 - N_GROUPS
    first_lo = e1 < e2
    elo, ehi = jnp.minimum(e1, e2), jnp.maximum(e1, e2)
    wlo, whi = jnp.where(first_lo, w1, w2), jnp.where(first_lo, w2, w1)
    llo, lhi = elo - E_PER_GROUP * gi, ehi - E_PER_GROUP * gi
    cls = gi * N_PAIRS + ((llo * (7 - llo)) >> 1) + lhi - llo - 1

    oh = jnp.where(col == cls, 1.0, 0.0)
    ii = lax.broadcasted_iota(jnp.int32, (TM, TM), 0)
    jj = lax.broadcasted_iota(jnp.int32, (TM, TM), 1)
    before = jnp.where(jj < ii, 1.0, 0.0).astype(BF16)
    cnt = s_cnt[0:1, :]
    rank = jnp.sum(oh * (_dot(before, oh.astype(BF16)) + cnt), -1, keepdims=True)
    cnt = cnt + jnp.sum(oh, 0, keepdims=True)
    s_cnt[...] = jnp.broadcast_to(cnt, s_cnt.shape)
    cnt_ref[...] = jnp.broadcast_to(cnt, cnt_ref.shape)

    info = jnp.zeros(lg.shape, F32)
    for c, val in ((I_CLS, cls.astype(F32)), (I_ELO, elo.astype(F32)), (I_EHI, ehi.astype(F32)),
                   (I_WLO, wlo), (I_WHI, whi), (I_RANK, rank)):
        info = jnp.where(col == c, val, info)
    he_ref[:, D_MODEL:] = info
    it_ref[...] = info.T[:SUB, :]


def _out_call(x, out_a, out_b, hcf, hcb, ogate, mod_l, mix_g, w_out, ln_g, ln_b, rt_w, rt_b, group_of_tile):
    n = x.shape[0]
    row = lambda i: (i, 0)
    fixed = lambda i: (0, 0)
    return pl.pallas_call(
        _out_kernel,
        grid=(n // TM,),
        in_specs=[pl.BlockSpec((TM, D_MODEL), row), pl.BlockSpec((TM, D_A), row), pl.BlockSpec((TM, D_B), row),
                  pl.BlockSpec((TM, D_C), row), pl.BlockSpec((TM, D_C), row), pl.BlockSpec((TM, D_C), row),
                  pl.BlockSpec((1, 6, D_MODEL), lambda i: (group_of_tile(i), 0, 0)),
                  pl.BlockSpec((1, D_MODEL), fixed), pl.BlockSpec((D_MODEL, D_MODEL), fixed),
                  pl.BlockSpec((1, D_MODEL), fixed), pl.BlockSpec((1, D_MODEL), fixed),
                  pl.BlockSpec((D_MODEL, LANE), fixed), pl.BlockSpec((1, LANE), fixed)],
        out_specs=[pl.BlockSpec((TM, D_MODEL), row), pl.BlockSpec((TM, D_EXT), row),
                   pl.BlockSpec((SUB, LANE), fixed), pl.BlockSpec((SUB, TM), lambda i: (0, i))],
        out_shape=[jax.ShapeDtypeStruct((n, D_MODEL), F32), jax.ShapeDtypeStruct((n, D_EXT), F32),
                   jax.ShapeDtypeStruct((SUB, LANE), F32), jax.ShapeDtypeStruct((SUB, n), F32)],
        scratch_shapes=[pltpu.VMEM((SUB, LANE), F32)],
        compiler_params=_cp(("arbitrary",)),
        name="out_proj_router",
    )(x, out_a, out_b, hcf, hcb, ogate, mod_l, mix_g, w_out, ln_g, ln_b, rt_w, rt_b)


def _row_copy(src_ref, src_row, dst_ref, dst_row, sem):
    return pltpu.make_async_copy(src_ref.at[pl.ds(src_row, 1), :], dst_ref.at[pl.ds(dst_row, 1), :], sem)


def _sorted_row(cls_ref, rank_ref, start_ref, t):
    return start_ref[cls_ref[t]] + rank_ref[t]


def _scatter_kernel(cls_ref, rank_ref, start_ref, pad_ref, na_ref, x_ref, o_ref, ring, z_ref, sem_z, load_sems,
                    row_sems):
    n_tiles = o_ref.shape[0] // TM

    def zero_tile(row):
        return pltpu.make_async_copy(z_ref, o_ref.at[pl.ds(pl.multiple_of(row, TM), TM), :], sem_z)

    def for_zero_tiles(fn):
        def per_class(c, carry):
            row = pad_ref[c]

            @pl.when(row >= 0)
            def _():
                fn(zero_tile(row))

            return carry

        lax.fori_loop(0, N_CLASS, per_class, 0)

        def per_idle(t, carry):
            fn(zero_tile(t * TM))
            return carry

        lax.fori_loop(na_ref[0], n_tiles, per_idle, 0)

    i = pl.program_id(0)
    n_steps = x_ref.shape[0] // TM

    def load(step):
        return pltpu.make_async_copy(x_ref.at[pl.ds(pl.multiple_of(step * TM, TM), TM), :],
                                     ring.at[step % RING], load_sems.at[step % RING])

    def start_rows(step):
        slot = step % RING

        def body(r, carry):
            dst = _sorted_row(cls_ref, rank_ref, start_ref, step * TM + r)
            _row_copy(ring.at[slot], r, o_ref, dst, row_sems.at[slot]).start()
            return carry

        lax.fori_loop(0, TM, body, 0, unroll=DMA_UNROLL)

    def wait_rows(step):
        slot = step % RING
        pltpu.make_async_copy(ring.at[slot], o_ref.at[pl.ds(0, TM), :], row_sems.at[slot]).wait()

    @pl.when(i == 0)
    def _():
        z_ref[...] = jnp.zeros(z_ref.shape, F32)
        for_zero_tiles(lambda cp: cp.start())
        for_zero_tiles(lambda cp: cp.wait())
        load(0).start()

    @pl.when(i >= RING - 1)
    def _():
        wait_rows(i - (RING - 1))

    @pl.when(i + 1 < n_steps)
    def _():
        load(i + 1).start()

    load(i).wait()
    start_rows(i)

    @pl.when(i == n_steps - 1)
    def _():
        for back in range(min(RING - 1, n_steps) - 1, -1, -1):
            wait_rows(i - back)


def _scatter_call(plan, h_ext, n_tiles_max):
    any_spec = pl.BlockSpec(memory_space=pl.ANY)
    return pl.pallas_call(
        _scatter_kernel,
        grid_spec=pltpu.PrefetchScalarGridSpec(
            num_scalar_prefetch=5, grid=(h_ext.shape[0] // TM,),
            in_specs=[any_spec],
            out_specs=any_spec,
            scratch_shapes=[pltpu.VMEM((RING, TM, D_EXT), F32), pltpu.VMEM((TM, D_EXT), F32),
                            pltpu.SemaphoreType.DMA(()),
                            pltpu.SemaphoreType.DMA((RING,)), pltpu.SemaphoreType.DMA((RING,))]),
        out_shape=jax.ShapeDtypeStruct((n_tiles_max * TM, D_EXT), F32),
        compiler_params=_cp(("arbitrary",)),
        name="moe_scatter",
    )(plan["cls"], plan["rank"], plan["row_start"], plan["pad_rows"], plan["n_act"], h_ext)


def _moe_kernel(ta_ref, tb_ref, ca_ref, cb_ref, na_ref, x_ref, w1a, w3a, w2a, w1b, w3b, w2b, o_ref,
                s1a, s3a, s2a, s1b, s3b, s2b):
    del ta_ref, tb_ref
    t = pl.program_id(0)
    active = t < na_ref[0]

    @pl.when(jnp.logical_not(active))
    def _():
        o_ref[...] = jnp.zeros(o_ref.shape, F32)

    for chg, srcs, dsts in ((ca_ref, (w1a, w3a, w2a), (s1a, s3a, s2a)), (cb_ref, (w1b, w3b, w2b), (s1b, s3b, s2b))):
        @pl.when(jnp.logical_and(active, chg[t] == 1))
        def _():
            for src, dst in zip(srcs, dsts):
                dst[...] = src[0].astype(BF16)

    @pl.when(active)
    def _():
        xe = x_ref[...]
        x = xe[:, :D_MODEL].astype(BF16)

        def expert(w1, w3, w2, gate):
            a = _dot(x, w1[...])
            hm = a * _sigmoid(a) * _dot(x, w3[...]) * gate
            return _dot(hm.astype(BF16), w2[...])

        o_ref[...] = (expert(s1a, s3a, s2a, xe[:, D_MODEL + I_WLO:D_MODEL + I_WLO + 1])
                      + expert(s1b, s3b, s2b, xe[:, D_MODEL + I_WHI:D_MODEL + I_WHI + 1]))


def _moe_call(layer, plan, x_sorted, w1, w3, w2):
    r = x_sorted.shape[0]
    act = lambda t, ta, tb, ca, cb, na: (jnp.minimum(t, na[0] - 1), 0)
    ea = lambda t, ta, tb, ca, cb, na: (layer, ta[t], 0, 0)
    eb = lambda t, ta, tb, ca, cb, na: (layer, tb[t], 0, 0)
    up = lambda m: pl.BlockSpec((None, 1, D_MODEL, D_E), m)
    down = lambda m: pl.BlockSpec((None, 1, D_E, D_MODEL), m)
    return pl.pallas_call(
        _moe_kernel,
        grid_spec=pltpu.PrefetchScalarGridSpec(
            num_scalar_prefetch=5, grid=(r // TM,),
            in_specs=[pl.BlockSpec((TM, D_EXT), act), up(ea), up(ea), down(ea), up(eb), up(eb), down(eb)],
            out_specs=pl.BlockSpec((TM, D_MODEL), lambda t, ta, tb, ca, cb, na: (t, 0)),
            scratch_shapes=[pltpu.VMEM((D_MODEL, D_E), BF16), pltpu.VMEM((D_MODEL, D_E), BF16),
                            pltpu.VMEM((D_E, D_MODEL), BF16)] * 2),
        out_shape=jax.ShapeDtypeStruct((r, D_MODEL), F32),
        compiler_params=_cp(("arbitrary",)),
        name="moe_experts",
    )(plan["tile_a"], plan["tile_b"], plan["chg_a"], plan["chg_b"], plan["n_act"], x_sorted, w1, w3, w2, w1, w3, w2)


def _ln2_kernel(cls_ref, rank_ref, start_ref, x1_ref, mod_ref, g_ref, b_ref, y_ref, *rest, n_ctx_tiles):
    o_refs, (buf, sems) = rest[:-2], rest[-2:]
    i = pl.program_id(0)
    n_steps = pl.num_programs(0)

    def start_rows(step):
        slot = step % 2

        def body(r, carry):
            src = _sorted_row(cls_ref, rank_ref, start_ref, step * TM + r)
            _row_copy(y_ref, src, buf.at[slot], r, sems.at[slot]).start()
            return carry

        lax.fori_loop(0, TM, body, 0, unroll=DMA_UNROLL)

    @pl.when(i == 0)
    def _():
        start_rows(0)

    @pl.when(i + 1 < n_steps)
    def _():
        start_rows(i + 1)

    slot = i % 2
    pltpu.make_async_copy(y_ref.at[pl.ds(0, TM), :], buf.at[slot], sems.at[slot]).wait()
    m = mod_ref[0]
    y = _ln_plain(ALPHA * x1_ref[...] + m[5:6] * buf[slot]) * g_ref[...] + b_ref[...]
    if len(o_refs) == 1:
        o_refs[0][...] = y
    else:
        @pl.when(i < n_ctx_tiles)
        def _():
            o_refs[0][...] = y

        @pl.when(i >= n_ctx_tiles)
        def _():
            o_refs[1][...] = y


def _ln2_call(plan, x1, y_sorted, mod_l, ln_g, ln_b, group_of_tile, n_ctx_tiles, split):
    n = x1.shape[0]
    row = lambda i, *_: (i, 0)
    fixed = lambda i, *_: (0, 0)
    if split:
        n_ctx = n_ctx_tiles * TM
        out_specs = [pl.BlockSpec((TM, D_MODEL), lambda i, *_: (jnp.minimum(i, n_ctx_tiles - 1), 0)),
                     pl.BlockSpec((TM, D_MODEL), lambda i, *_: (jnp.maximum(i - n_ctx_tiles, 0), 0))]
        out_shape = [jax.ShapeDtypeStruct((n_ctx, D_MODEL), F32), jax.ShapeDtypeStruct((n - n_ctx, D_MODEL), F32)]
    else:
        out_specs = [pl.BlockSpec((TM, D_MODEL), row)]
        out_shape = [jax.ShapeDtypeStruct((n, D_MODEL), F32)]
    return pl.pallas_call(
        functools.partial(_ln2_kernel, n_ctx_tiles=n_ctx_tiles),
        grid_spec=pltpu.PrefetchScalarGridSpec(
            num_scalar_prefetch=3, grid=(n // TM,),
            in_specs=[pl.BlockSpec((TM, D_MODEL), row),
                      pl.BlockSpec((1, 6, D_MODEL), lambda i, *_: (group_of_tile(i), 0, 0)),
                      pl.BlockSpec((1, D_MODEL), fixed), pl.BlockSpec((1, D_MODEL), fixed),
                      pl.BlockSpec(memory_space=pl.ANY)],
            out_specs=out_specs,
            scratch_shapes=[pltpu.VMEM((2, TM, D_MODEL), F32), pltpu.SemaphoreType.DMA((2,))]),
        out_shape=out_shape,
        compiler_params=_cp(("arbitrary",)),
        name="moe_gather_ln2",
    )(plan["cls"], plan["rank"], plan["row_start"], x1, mod_l, ln_g, ln_b, y_sorted)


def _dft_matrices():
    n2 = 2 * CH
    f = np.arange(CH, dtype=np.float64)[:, None]
    t = np.arange(CH, dtype=np.float64)[None, :]
    ang = 2.0 * np.pi * f * t / n2
    re, im = np.cos(ang), -np.sin(ang)
    im[0, :] = np.cos(np.pi * t[0])
    fwd = np.concatenate([re, im], axis=0)
    scale = np.full((CH, 1), 2.0 / n2)
    scale[0, 0] = 1.0 / n2
    inv = np.concatenate([(re * scale).T, (im * scale).T], axis=1)
    return fwd.astype(np.float32), inv.astype(np.float32)


def _filter_features(L):
    lag = np.arange(-L, L)
    m = np.minimum(np.abs(lag), L - 1)
    t = (np.arange(L, dtype=np.float32) / np.float32(max(L - 1, 1)))[m]
    w = (np.float32(2.0 * math.pi) * np.arange(L, dtype=np.float32) / np.float32(L))[m]
    bands = np.linspace(1e-4, HY_BANDS - 1, HY_BANDS, dtype=np.float32)
    z = np.zeros((2 * L, LANE), np.float32)
    z[:, 0] = t
    z[:, 1:1 + HY_BANDS] = np.cos(w[:, None] * bands)
    z[:, 1 + HY_BANDS:HY_EMB] = -np.sin(w[:, None] * bands)
    lo, hi = math.log(HY_DECAY_TARGET) / 1.5, math.log(HY_DECAY_TARGET) / 0.3
    deltas = np.abs(np.linspace(lo, hi, D_B, dtype=np.float32))
    dec = np.exp(-t[:, None] * deltas)
    return z, dec.astype(np.float32)


def _sincos_2d(rows, cols):
    quarter = D_MODEL // 4
    omega = 1.0 / (10000.0 ** (jnp.arange(quarter, dtype=F32) / quarter))

    def emb(n):
        ang = jnp.arange(n, dtype=F32)[:, None] * omega[None]
        return jnp.concatenate([jnp.sin(ang), jnp.cos(ang)], -1)

    er, ec = emb(rows), emb(cols)
    half = D_MODEL // 2
    pos = jnp.concatenate([jnp.broadcast_to(er[:, None], (rows, cols, half)),
                           jnp.broadcast_to(ec[None], (rows, cols, half))], -1)
    return pos.reshape(rows * cols, D_MODEL)


def _pad_to(x, shape):
    return jnp.pad(x, [(0, s - d) for d, s in zip(x.shape, shape)])


def _block_diag(w):
    eye = jnp.eye(H_A, dtype=w.dtype)
    return jnp.einsum("hij,hg->higj", w, eye).reshape(D_A, D_A)


_PAIR_LO = np.array([0, 0, 0, 1, 1, 2], np.int32)
_PAIR_HI = np.array([1, 2, 3, 2, 3, 3], np.int32)


def _routing_plan(info_t, counts, n_tiles_max):
    cnt = counts[0, :N_CLASS].astype(jnp.int32)
    tiles = (cnt + TM - 1) // TM
    tile_end = jnp.cumsum(tiles)
    n_act = tile_end[-1]
    t = jnp.minimum(jnp.arange(n_tiles_max, dtype=jnp.int32), n_act - 1)
    tcls = jnp.minimum(jnp.sum((tile_end[None, :] <= t[:, None]).astype(jnp.int32), 1), N_CLASS - 1)
    grp, pair = tcls // N_PAIRS, tcls % N_PAIRS
    tile_a = (grp * E_PER_GROUP + jnp.asarray(_PAIR_LO)[pair]).astype(jnp.int32)
    tile_b = (grp * E_PER_GROUP + jnp.asarray(_PAIR_HI)[pair]).astype(jnp.int32)
    first = jnp.ones((1,), jnp.int32)
    changed = lambda e: jnp.concatenate([first, (e[1:] != e[:-1]).astype(jnp.int32)])
    return {"cls": info_t[I_CLS].astype(jnp.int32), "rank": info_t[I_RANK].astype(jnp.int32),
            "row_start": ((tile_end - tiles) * TM).astype(jnp.int32), "tile_a": tile_a, "tile_b": tile_b,
            "chg_a": changed(tile_a), "chg_b": changed(tile_b),
            "n_act": n_act.reshape(1).astype(jnp.int32),
            "pad_rows": jnp.where(tiles > 0, (tile_end - 1) * TM, -1).astype(jnp.int32)}


def kernel(x_prompt, x_sample, c, state_lru, state_mlstm_C, state_mlstm_n, state_mlstm_m, c_ctx, w_ada, b_ada, w_in, b_in, conv_a_w, conv_a_b, lru_wa, lru_ba, lru_wx, lru_bx, lru_lam, conv_b_w, conv_b_b, hy_w1, hy_b1, hy_w2, hy_b2, hy_freq, hy_w3, hy_bias, mix_g, w_out, ln1_g, ln1_b, rt_wg, rt_bg, rt_we, rt_be, moe_w1, moe_w3, moe_w2, ln2_g, ln2_b):
    B, l_ctx, D = x_prompt.shape
    b_lat, l_lat, _ = x_sample.shape
    n_ctx, n_lat = B * l_ctx, b_lat * l_lat
    n = n_ctx + n_lat
    assert D == D_MODEL and w_in.shape[-1] == D_MAIN + N_GATE
    assert SEG % l_ctx == 0 and l_lat == SEG and l_ctx % CH == 0 and n_ctx % SEG == 0
    assert l_ctx == CH, "the mLSTM step schedule assumes one chunk per context sequence"
    assert 1 + b_lat <= SUB
    n_ctx_blk = n_ctx // SEG
    n_ctx_tiles = n_ctx // TM
    tiles_per_lat = l_lat // TM
    nc_lat = l_lat // CH

    def group_of(tile_rows):
        first_lat, per_seq = n_ctx // tile_rows, l_lat // tile_rows
        return lambda i: jnp.where(i < first_lat, 0, 1 + (i - first_lat) // per_seq)

    group_of_tile = group_of(TM)

    cond = jnp.concatenate([c_ctx[None], c, jnp.zeros((SUB - 1 - b_lat, D), F32)], 0)
    mod = _mod_call(cond, w_ada, b_ada).reshape(DEPTH, SUB, 6, D)
    pos = _sincos_2d(l_lat // GRID_W, GRID_W)
    x = _entry_call(x_prompt.reshape(n_ctx, D), x_sample.reshape(n_lat, D), pos)

    fwd_np, inv_np = _dft_matrices()
    fwd32 = jnp.asarray(fwd_np)
    fwd16, inv16 = fwd32.astype(BF16), jnp.asarray(inv_np).astype(BF16)
    fw1 = _pad_to(hy_w1, (DEPTH, LANE, LANE))
    fb1 = _pad_to(hy_b1[:, None, :], (DEPTH, 1, LANE))
    fw2 = _pad_to(hy_w2, (DEPTH, LANE, LANE))
    fb2 = _pad_to(hy_b2[:, None, :], (DEPTH, 1, LANE))
    ffr = _pad_to(hy_freq[:, None, :], (DEPTH, 1, LANE))
    fw3 = _pad_to(hy_w3, (DEPTH, LANE, HY_ORDER * 2 * D_B))
    spectra = {}
    for L in (l_ctx, l_lat):
        z_np, dec_np = _filter_features(L)
        spectra[L] = _filt_call(L, jnp.asarray(z_np), jnp.asarray(dec_np), fw1, fb1, fw2, fb2, ffr, fw3, fwd32)

    lat_slots = SEG // l_ctx
    st_lru, st_c, st_n, st_m = [], [], [], []
    for l in range(DEPTH):
        b_main = b_in[l, None, :D_MAIN]
        k_lo, k_hi = _Z_CUTS[_K_CUT], _Z_CUTS[_K_CUT + 1]
        w_kt = w_in[l, :, k_lo:k_hi].T.astype(BF16)
        b_kt = jnp.broadcast_to(b_in[l, k_lo:k_hi, None], (D_C, TM_IN))
        w_gate = _pad_to(w_in[l, :, D_MAIN:], (D, LANE))
        b_gate = _pad_to(b_in[l, None, D_MAIN:], (1, LANE))
        xa, ya, hyb, q, v, og, gates, kt = _in_call(l, x, mod[l], w_in, b_main, w_kt, b_kt, w_gate, b_gate,
                                                    group_of(TM_IN))

        lru_w = jnp.concatenate([_block_diag(lru_wa[l, 0]), _block_diag(lru_wx[l, 0]),
                                 _block_diag(lru_wa[l, 1]), _block_diag(lru_wx[l, 1])], 1).astype(BF16)
        lru_b = jnp.concatenate([lru_ba[l, 0], lru_bx[l, 0], lru_ba[l, 1], lru_bx[l, 1]])[None]
        h0_lat = _pad_to(state_lru[:, l][:, None], (b_lat, lat_slots, 2, D_A))
        h0_all = jnp.concatenate([jnp.zeros((n_ctx_blk, lat_slots, 2, D_A), F32), h0_lat], 0)
        out_a, lru_last = _lru_call(xa, ya, conv_a_w[l], conv_a_b[l, None], lru_w, lru_b, lru_lam[l], h0_all,
                                    l_ctx, l_lat, n_ctx_blk)

        habc, hd0c = spectra[l_ctx]
        habl, hd0l = spectra[l_lat]
        out_b = _hy_call(l, hyb, conv_b_w[l], conv_b_b[l, None], fwd16, inv16, habc, hd0c, habl, hd0l,
                         hy_bias[l][:, None, :], l_ctx, l_lat, n_ctx_blk)

        n0 = state_mlstm_n[:, l].reshape(b_lat, 2 * H_C, DK)
        cx0 = jnp.concatenate([state_mlstm_C[:, l].reshape(b_lat, 2 * H_C, DK, DK), n0[..., None],
                               jnp.zeros((b_lat, 2 * H_C, DK, DK - 1), F32)], -1)
        m0 = _pad_to(state_mlstm_m[:, l], (b_lat, SUB, LANE))
        hcf, hcb, c_fin, n_fin, m_fin = _mlstm_call(q, v, kt, gates, cx0, n0, m0, n_ctx // CH, nc_lat)

        rt_w = _pad_to(jnp.concatenate([rt_wg[l], rt_we[l]], 1), (D, LANE))
        rt_b = _pad_to(jnp.concatenate([rt_bg[l], rt_be[l]])[None], (1, LANE))
        x1, h_ext, counts, info_t = _out_call(x, out_a, out_b, hcf, hcb, og, mod[l], mix_g[l, None],
                                              w_out[l].astype(BF16), ln1_g[l, None], ln1_b[l, None], rt_w, rt_b,
                                              group_of_tile)

        n_tiles_max = n // TM + N_CLASS
        plan = _routing_plan(info_t, counts, n_tiles_max)
        x_sorted = _scatter_call(plan, h_ext, n_tiles_max)
        y_sorted = _moe_call(l, plan, x_sorted, moe_w1, moe_w3, moe_w2)
        outs = _ln2_call(plan, x1, y_sorted, mod[l], ln2_g[l, None], ln2_b[l, None], group_of_tile,
                         n_ctx_tiles, split=(l == DEPTH - 1))
        x = outs[0]

        st_lru.append(lru_last[:n_ctx_blk].reshape(B, 2, D_A))
        st_c.append(c_fin.reshape(B, 2, H_C, DK, DK))
        st_n.append(n_fin.reshape(B, 2, H_C, DK))
        st_m.append(m_fin[:, :2, :H_C])

    return (outs[0].reshape(B, l_ctx, D), outs[1].reshape(b_lat, l_lat, D),
            jnp.stack(st_lru, 1), jnp.stack(st_c, 1), jnp.stack(st_n, 1), jnp.stack(st_m, 1))
```

```python
import functools
import math

import numpy as np
import jax
import jax.numpy as jnp
from jax import lax
from jax.experimental import pallas as pl
from jax.experimental.pallas import tpu as pltpu

F32 = jnp.float32
BF16 = jnp.bfloat16

D_MODEL = 1024
DEPTH = 2
GRID_W = 64
D_A = 256
H_A = 4
BA = D_A // H_A
LRU_C = 8.0
D_B = 256
HY_ORDER = 2
HY_BANDS = 16
HY_EMB = 1 + 2 * HY_BANDS
HY_FH = 64
HY_DECAY_TARGET = 1e-2
D_C = 512
H_C = 4
DK = D_C // H_C
N_GROUPS = 4
E_PER_GROUP = 4
N_EXP = N_GROUPS * E_PER_GROUP
N_PAIRS = 6
N_CLASS = N_GROUPS * N_PAIRS
D_E = 512
ALPHA = (2 * DEPTH) ** 0.25
EPS = 1e-6
D_MAIN = 2 * D_A + 3 * D_B + 4 * D_C
N_GATE = 4 * H_C

LANE = 128
SUB = 8
VMEM_LIMIT = 56 * 1024 * 1024

CH = 256
SEG = 2048
TM = 256
TM_IN = 512
D_EXT = D_MODEL + LANE
FS = 64
DMA_UNROLL = 8
RING = 3

I_CLS, I_ELO, I_EHI, I_WLO, I_WHI, I_RANK = range(6)


def _cp(sem, vmem=VMEM_LIMIT):
    return pltpu.CompilerParams(dimension_semantics=sem, vmem_limit_bytes=vmem)


def _dot(a, b):
    return jnp.dot(a, b, preferred_element_type=F32)


def _split2(x):
    hi = x.astype(BF16)
    lo = (x - hi.astype(F32)).astype(BF16)
    return hi, lo


def _dot3(a, b):
    ah, al = _split2(a)
    bh, bl = _split2(b)
    return _dot(ah, bh) + (_dot(ah, bl) + _dot(al, bh))


def _split3(x):
    hi = x.astype(BF16)
    r1 = x - hi.astype(F32)
    mid = r1.astype(BF16)
    lo = (r1 - mid.astype(F32)).astype(BF16)
    return hi, mid, lo


def _sigmoid(x):
    return 1.0 / (1.0 + jnp.exp(-x))


def _log_sigmoid(x):
    return jnp.minimum(x, 0.0) - jnp.log1p(jnp.exp(-jnp.abs(x)))


def _gelu_tanh(x):
    return 0.5 * x * (1.0 + jnp.tanh(math.sqrt(2.0 / math.pi) * (x + 0.044715 * (x * x * x))))


def _ln_plain(x):
    mu = jnp.mean(x, -1, keepdims=True)
    xc = x - mu
    var = jnp.mean(xc * xc, -1, keepdims=True)
    return xc * lax.rsqrt(var + EPS)


def _rms(x):
    return x * lax.rsqrt(jnp.mean(x * x, -1, keepdims=True) + EPS)


def _halo_rows(ref, start, rows):
    total = ref.shape[0]
    prev = ref[pl.ds(pl.multiple_of(jnp.maximum(start - SUB, 0), SUB), SUB), :]
    main = ref[pl.ds(start, rows), :]
    nxt = ref[pl.ds(pl.multiple_of(jnp.minimum(start + rows, total - SUB), SUB), SUB), :]
    return jnp.concatenate([prev, main, nxt], axis=0), main


def _mod_kernel(c_ref, w_ref, b_ref, o_ref):
    c = c_ref[...]
    o_ref[0] = _dot3(c * _sigmoid(c), w_ref[0]) + b_ref[0]


def _mod_call(cond, w_ada, b_ada):
    tn = 1536
    n6 = w_ada.shape[-1]
    return pl.pallas_call(
        _mod_kernel,
        grid=(DEPTH, n6 // tn),
        in_specs=[pl.BlockSpec((SUB, D_MODEL), lambda l, j: (0, 0)),
                  pl.BlockSpec((1, D_MODEL, tn), lambda l, j: (l, 0, j)),
                  pl.BlockSpec((1, 1, tn), lambda l, j: (l, 0, j))],
        out_specs=pl.BlockSpec((1, SUB, tn), lambda l, j: (l, 0, j)),
        out_shape=jax.ShapeDtypeStruct((DEPTH, SUB, n6), F32),
        compiler_params=_cp(("parallel", "parallel")),
        name="adaln_mod",
    )(cond, w_ada, b_ada.reshape(DEPTH, 1, n6))


def _entry_kernel(xc_ref, xl_ref, pos_ref, o_ref, *, n_ctx_tiles):
    i = pl.program_id(0)

    @pl.when(i < n_ctx_tiles)
    def _():
        o_ref[...] = _ln_plain(xc_ref[...])

    @pl.when(i >= n_ctx_tiles)
    def _():
        o_ref[...] = _ln_plain(xl_ref[...] + pos_ref[...])


def _entry_call(xc, xl, pos):
    tm = 512
    n_ctx, n_lat, l_lat = xc.shape[0], xl.shape[0], pos.shape[0]
    nct = n_ctx // tm
    per_seq = l_lat // tm
    return pl.pallas_call(
        functools.partial(_entry_kernel, n_ctx_tiles=nct),
        grid=((n_ctx + n_lat) // tm,),
        in_specs=[pl.BlockSpec((tm, D_MODEL), lambda i: (jnp.minimum(i, nct - 1), 0)),
                  pl.BlockSpec((tm, D_MODEL), lambda i: (jnp.maximum(i - nct, 0), 0)),
                  pl.BlockSpec((tm, D_MODEL), lambda i: (jnp.maximum(i - nct, 0) % per_seq, 0))],
        out_specs=pl.BlockSpec((tm, D_MODEL), lambda i: (i, 0)),
        out_shape=jax.ShapeDtypeStruct((n_ctx + n_lat, D_MODEL), F32),
        compiler_params=_cp(("parallel",)),
        name="entry_ln",
    )(xc, xl, pos)


_Z_CUTS = (0, D_A, 2 * D_A, 2 * D_A + 3 * D_B, 2 * D_A + 3 * D_B + D_C, 2 * D_A + 3 * D_B + 2 * D_C,
           2 * D_A + 3 * D_B + 3 * D_C, D_MAIN)


_K_CUT = 4
_ROW_CUTS = tuple(c for i, c in enumerate(zip(_Z_CUTS[:-1], _Z_CUTS[1:])) if i != _K_CUT)
_NT = (((1,), (1,)), ((), ()))


def _in_kernel(x_ref, mod_ref, w_ref, b_ref, wkt_ref, bkt_ref, wg_ref, bg_ref, *refs):
    out_refs, w16 = refs[:-1], refs[-1]

    @pl.when(pl.program_id(0) == 0)
    def _():
        w16[...] = w_ref[0, :, :D_MAIN].astype(BF16)

    m = mod_ref[0]
    h = x_ref[...] * (1.0 + m[1:2]) + m[0:1]
    hb = h.astype(BF16)
    for ref, (a, b) in zip(out_refs[:-2], _ROW_CUTS):
        ref[...] = _dot(hb, w16[:, a:b]) + b_ref[:, a:b]
    g_ref, kt_ref = out_refs[-2:]
    g_ref[...] = _dot3(h, wg_ref[...]) + bg_ref[...]
    kt_ref[...] = lax.dot_general(wkt_ref[...], hb, _NT, preferred_element_type=F32) + bkt_ref[...]


def _in_call(layer, x, mod_l, w_in, b_main, w_kt, b_kt, w_gate, b_gate, group_of_tile):
    n = x.shape[0]
    widths = [b - a for a, b in _ROW_CUTS] + [LANE]
    row = lambda i: (i, 0)
    fixed = lambda i: (0, 0)
    return pl.pallas_call(
        _in_kernel,
        grid=(n // TM_IN,),
        in_specs=[pl.BlockSpec((TM_IN, D_MODEL), row),
                  pl.BlockSpec((1, 6, D_MODEL), lambda i: (group_of_tile(i), 0, 0)),
                  pl.BlockSpec((1, D_MODEL, w_in.shape[-1]), lambda i: (layer, 0, 0), pipeline_mode=pl.Buffered(1)),
                  pl.BlockSpec((1, D_MAIN), fixed),
                  pl.BlockSpec((D_C, D_MODEL), fixed), pl.BlockSpec((D_C, TM_IN), fixed),
                  pl.BlockSpec((D_MODEL, LANE), fixed), pl.BlockSpec((1, LANE), fixed)],
        out_specs=[pl.BlockSpec((TM_IN, w), row) for w in widths] + [pl.BlockSpec((D_C, TM_IN), lambda i: (0, i))],
        out_shape=[jax.ShapeDtypeStruct((n, w), F32) for w in widths] + [jax.ShapeDtypeStruct((D_C, n), F32)],
        scratch_shapes=[pltpu.VMEM((D_MODEL, D_MAIN), BF16)],
        compiler_params=_cp(("arbitrary",)),
        name="in_proj",
    )(x, mod_l, w_in, b_main, w_kt, b_kt, w_gate, b_gate)


def _lru_variant(L, xa_ref, ya_ref, cw_ref, cb_ref, wg_ref, bg_ref, lam_ref, h0_ref, o_ref, st_ref,
                 s_af, s_bf, s_ab, s_bb):
    nch, nseq, ntile = SEG // CH, SEG // L, L // SUB
    lam = lam_ref[...]
    sp = jnp.maximum(-lam, 0.0) + jnp.log1p(jnp.exp(-jnp.abs(lam)))
    cw = cw_ref[...]
    cb = cb_ref[...]
    row = lax.broadcasted_iota(jnp.int32, (CH, 1), 0)
    sub3 = lax.broadcasted_iota(jnp.int32, (1, SUB, 1), 1)

    def gates_and_tile_scan(c, carry):
        start = pl.multiple_of(c * CH, CH)
        xcat, main = _halo_rows(xa_ref, start, CH)
        tpos = (start + row) & (L - 1)
        xm2 = jnp.where(tpos >= 2, xcat[SUB - 2:SUB - 2 + CH], 0.0)
        xm1 = jnp.where(tpos >= 1, xcat[SUB - 1:SUB - 1 + CH], 0.0)
        xp1 = jnp.where(tpos <= L - 2, xcat[SUB + 1:SUB + 1 + CH], 0.0)
        xc = cw[0:1] * xm2 + cw[1:2] * xm1 + cw[2:3] * main + cw[3:4] * xp1 + cb
        g = _dot(xc.astype(BF16), wg_ref[...]) + bg_ref[...]
        for d, (sa, sb) in enumerate(((s_af, s_bf), (s_ab, s_bb))):
            r = _sigmoid(g[:, 2 * d * D_A:(2 * d + 1) * D_A])
            ig = _sigmoid(g[:, (2 * d + 1) * D_A:(2 * d + 2) * D_A])
            a = jnp.exp(-LRU_C * r * sp[d:d + 1])
            b = jnp.sqrt(1.0 - a * a) * (ig * xc)
            a3, b3 = a.reshape(CH // SUB, SUB, D_A), b.reshape(CH // SUB, SUB, D_A)
            for s in (1, 2, 4):
                shift, keep = (s, sub3 >= s) if d == 0 else (SUB - s, sub3 < SUB - s)
                b3 = a3 * jnp.where(keep, pltpu.roll(b3, shift, 1), 0.0) + b3
                a3 = a3 * jnp.where(keep, pltpu.roll(a3, shift, 1), 1.0)
            sa[pl.ds(start, CH), :] = a3.reshape(CH, D_A)
            sb[pl.ds(start, CH), :] = b3.reshape(CH, D_A)
        return carry

    lax.fori_loop(0, nch, gates_and_tile_scan, 0)

    def carry_tiles(k, carry):
        cf, cbk = carry
        nf, nb = [], []
        for s in range(nseq):
            rf = pl.multiple_of(s * L + k * SUB, SUB)
            hf = s_af[pl.ds(rf, SUB), :] * cf[s] + s_bf[pl.ds(rf, SUB), :]
            s_bf[pl.ds(rf, SUB), :] = hf
            nf.append(hf[SUB - 1:SUB, :])
            rb = pl.multiple_of(s * L + (ntile - 1 - k) * SUB, SUB)
            hb = s_ab[pl.ds(rb, SUB), :] * cbk[s] + s_bb[pl.ds(rb, SUB), :]
            s_bb[pl.ds(rb, SUB), :] = hb
            nb.append(hb[0:1, :])
        return tuple(nf), tuple(nb)

    cf0 = tuple(h0_ref[0, s, 0:1, :] for s in range(nseq))
    cb0 = tuple(h0_ref[0, s, 1:2, :] for s in range(nseq))
    cf, cbk = lax.fori_loop(0, ntile, carry_tiles, (cf0, cb0))

    st_ref[...] = jnp.zeros(st_ref.shape, F32)
    for s in range(nseq):
        st_ref[0, s] = jnp.concatenate([cf[s], cbk[s]], axis=0)

    def finish(c, carry):
        start = pl.multiple_of(c * CH, CH)
        h = s_bf[pl.ds(start, CH), :] + s_bb[pl.ds(start, CH), :]
        o_ref[pl.ds(start, CH), :] = _rms(_gelu_tanh(ya_ref[pl.ds(start, CH), :]) * h)
        return carry

    lax.fori_loop(0, nch, finish, 0)


def _lru_kernel(*refs, l_ctx, l_lat, n_ctx_blk):
    i = pl.program_id(0)

    @pl.when(i < n_ctx_blk)
    def _():
        _lru_variant(l_ctx, *refs)

    @pl.when(i >= n_ctx_blk)
    def _():
        _lru_variant(l_lat, *refs)


def _lru_call(xa, ya, conv_w, conv_b, w_gate, b_gate, lam, h0_all, l_ctx, l_lat, n_ctx_blk):
    n = xa.shape[0]
    nblk = n // SEG
    row = lambda i: (i, 0)
    fixed = lambda i: (0, 0)
    slots = SEG // l_ctx
    return pl.pallas_call(
        functools.partial(_lru_kernel, l_ctx=l_ctx, l_lat=l_lat, n_ctx_blk=n_ctx_blk),
        grid=(nblk,),
        in_specs=[pl.BlockSpec((SEG, D_A), row), pl.BlockSpec((SEG, D_A), row),
                  pl.BlockSpec((4, D_A), fixed), pl.BlockSpec((1, D_A), fixed),
                  pl.BlockSpec((D_A, 4 * D_A), fixed), pl.BlockSpec((1, 4 * D_A), fixed),
                  pl.BlockSpec((2, D_A), fixed),
                  pl.BlockSpec((1, slots, 2, D_A), lambda i: (i, 0, 0, 0))],
        out_specs=[pl.BlockSpec((SEG, D_A), row),
                   pl.BlockSpec((1, slots, 2, D_A), lambda i: (i, 0, 0, 0))],
        out_shape=[jax.ShapeDtypeStruct((n, D_A), F32),
                   jax.ShapeDtypeStruct((nblk, slots, 2, D_A), F32)],
        scratch_shapes=[pltpu.VMEM((SEG, D_A), F32) for _ in range(4)],
        compiler_params=_cp(("parallel",)),
        name="rglru",
    )(xa, ya, conv_w, conv_b, w_gate, b_gate, lam, h0_all)


def _filt_kernel(z_ref, dec_ref, w1_ref, b1_ref, w2_ref, b2_ref, fr_ref, w3_ref, fwd_ref,
                 oab_ref, od0_ref, s_k, s_kf, *, L):
    nblk = 2 * L // CH
    d_idx = pl.program_id(1)
    row = lax.broadcasted_iota(jnp.int32, (CH, 1), 0)

    @pl.when(d_idx == 0)
    def _():
        fr = fr_ref[0]

        def taps(c, carry):
            start = pl.multiple_of(c * CH, CH)
            h1 = jnp.sin(fr * (_dot3(z_ref[pl.ds(start, CH), :], w1_ref[0]) + b1_ref[0]))
            h2 = jnp.sin(fr * (_dot3(h1, w2_ref[0]) + b2_ref[0]))
            t = _dot3(h2, w3_ref[0])
            dec = dec_ref[pl.ds(start, CH), :]
            rg = start + row
            for o in range(HY_ORDER):
                fwd_t = t[:, (2 * o) * D_B:(2 * o + 1) * D_B]
                bwd_t = t[:, (2 * o + 1) * D_B:(2 * o + 2) * D_B]
                ko = jnp.where(rg < L, bwd_t, fwd_t) * dec
                s_k[pl.ds(start, CH), o * D_B:(o + 1) * D_B] = jnp.where(rg == 0, 0.0, ko)
            return carry

        lax.fori_loop(0, nblk, taps, 0)
        fwd = fwd_ref[...]

        def spectra(e, carry):
            start = pl.multiple_of(e * CH, CH)
            s_kf[e] = _dot3(fwd, s_k[pl.ds(start, CH), :])
            return carry

        lax.fori_loop(0, nblk, spectra, 0)

    kd = s_kf[d_idx + 1]
    km = s_kf[d_idx]
    k0 = s_k[pl.ds(pl.multiple_of(d_idx * CH, CH), 1), :]
    sgn = jnp.where((row & 1) == 0, 1.0, -1.0)
    a = kd[:CH] + sgn * (km[:CH] - k0)
    b = jnp.where(row == 0, 0.0, kd[CH:] + sgn * km[CH:])
    hn = kd[CH:CH + 1] + km[CH:CH + 1] - k0
    for o in range(HY_ORDER):
        oab_ref[0, o, 0, 0] = a[:, o * D_B:(o + 1) * D_B]
        oab_ref[0, o, 0, 1] = b[:, o * D_B:(o + 1) * D_B]
        od0_ref[0, o, 0] = jnp.broadcast_to(hn[:, o * D_B:(o + 1) * D_B], (SUB, D_B))


def _filt_call(L, z, dec, w1, b1, w2, b2, fr, w3, fwd32):
    nd = 2 * (L // CH) - 1
    fixed = lambda l, d: (0, 0)
    lay3 = lambda l, d: (l, 0, 0)
    return pl.pallas_call(
        functools.partial(_filt_kernel, L=L),
        grid=(DEPTH, nd),
        in_specs=[pl.BlockSpec((2 * L, LANE), fixed), pl.BlockSpec((2 * L, D_B), fixed),
                  pl.BlockSpec((1, LANE, LANE), lay3), pl.BlockSpec((1, 1, LANE), lay3),
                  pl.BlockSpec((1, LANE, LANE), lay3), pl.BlockSpec((1, 1, LANE), lay3),
                  pl.BlockSpec((1, 1, LANE), lay3),
                  pl.BlockSpec((1, LANE, HY_ORDER * 2 * D_B), lay3),
                  pl.BlockSpec((2 * CH, CH), fixed)],
        out_specs=[pl.BlockSpec((1, HY_ORDER, 1, 2, CH, D_B), lambda l, d: (l, 0, d, 0, 0, 0)),
                   pl.BlockSpec((1, HY_ORDER, 1, SUB, D_B), lambda l, d: (l, 0, d, 0, 0))],
        out_shape=[jax.ShapeDtypeStruct((DEPTH, HY_ORDER, nd, 2, CH, D_B), F32),
                   jax.ShapeDtypeStruct((DEPTH, HY_ORDER, nd, SUB, D_B), F32)],
        scratch_shapes=[pltpu.VMEM((2 * L, HY_ORDER * D_B), F32),
                        pltpu.VMEM((2 * L // CH, 2 * CH, HY_ORDER * D_B), F32)],
        compiler_params=_cp(("parallel", "arbitrary")),
        name=f"hyena_filter_{L}",
    )(z, dec, w1, b1, w2, b2, fr, w3, fwd32)


def _hy_variant(L, o_idx, hy_ref, cw_ref, cb_ref, fwd_ref, inv_ref, hab_ref, hd0_ref, bias_ref, o_ref,
                s_y, s_x, s_u, s_v):
    nch, nseq, P = SEG // CH, SEG // L, L // CH
    row = lax.broadcasted_iota(jnp.int32, (CH, 1), 0)
    frow = lax.broadcasted_iota(jnp.int32, (FS, 1), 0)

    @pl.when(o_idx == 0)
    def _():
        cw = cw_ref[...]
        cb = cb_ref[...]

        def short_conv(c, carry):
            start = pl.multiple_of(c * CH, CH)
            xcat, main = _halo_rows(hy_ref, start, CH)
            tpos = (start + row) & (L - 1)
            xm1 = jnp.where(tpos >= 1, xcat[SUB - 1:SUB - 1 + CH], 0.0)
            xp1 = jnp.where(tpos <= L - 2, xcat[SUB + 1:SUB + 1 + CH], 0.0)
            hc = cw[0:1] * xm1 + cw[1:2] * main + cw[2:3] * xp1 + cb
            s_y[pl.ds(start, CH), :] = hc[:, :D_B]
            s_x[0, pl.ds(start, CH), :] = hc[:, D_B:2 * D_B]
            s_x[1, pl.ds(start, CH), :] = hc[:, 2 * D_B:]
            return carry

        lax.fori_loop(0, nch, short_conv, 0)

    bias = bias_ref[0]

    def loop(n, body, init):
        return body(0, init) if n == 1 else lax.fori_loop(0, n, body, init)

    def one_sequence(s, slot):
        base = s * L
        u0 = slot * P

        def forward_dft(j, cc):
            r = pl.multiple_of(base + j * CH, CH)
            s_u[u0 + j] = _dot(fwd_ref[...], s_y[pl.ds(r, CH), :].astype(BF16))
            return cc

        loop(P, forward_dft, 0)

        def output_block(i, cc):
            for fs in range(CH // FS):
                lo = fs * FS

                def accumulate(j, acc):
                    yre, yim = acc
                    d = i - j + (P - 1)
                    ure = s_u[u0 + j, lo:lo + FS, :]
                    uim = s_u[u0 + j, CH + lo:CH + lo + FS, :]
                    a = hab_ref[0, 0, d, 0, lo:lo + FS, :]
                    b = hab_ref[0, 0, d, 1, lo:lo + FS, :]
                    dd = jnp.where(frow == 0, hd0_ref[0, 0, d, 0:1, :], a) if fs == 0 else a
                    return yre + ure * a - uim * b, yim + ure * b + uim * dd

                zero = jnp.zeros((FS, D_B), F32)
                yre, yim = loop(P, accumulate, (zero, zero))
                s_v[slot, lo:lo + FS, :] = yre.astype(BF16)
                s_v[slot, CH + lo:CH + lo + FS, :] = yim.astype(BF16)
            yc = _dot(inv_ref[...], s_v[slot])
            r = pl.multiple_of(base + i * CH, CH)
            s_y[pl.ds(r, CH), :] = s_x[o_idx, pl.ds(r, CH), :] * (yc + s_y[pl.ds(r, CH), :] * bias)
            return cc

        loop(P, output_block, 0)

    def single_block_pair(p, carry):
        rows = [pl.multiple_of((2 * p + k) * L, CH) for k in range(2)]
        ys = [s_y[pl.ds(r, CH), :] for r in rows]
        gates = [s_x[o_idx, pl.ds(r, CH), :] for r in rows]
        a = hab_ref[0, 0, 0, 0]
        b = hab_ref[0, 0, 0, 1]
        dd = jnp.where(row == 0, hd0_ref[0, 0, 0, 0:1, :], a)
        outs = []
        for y, gate in zip(ys, gates):
            u = _dot(fwd_ref[...], y.astype(BF16))
            ure, uim = u[:CH], u[CH:]
            v = jnp.concatenate([ure * a - uim * b, ure * b + uim * dd], axis=0).astype(BF16)
            outs.append(gate * (_dot(inv_ref[...], v) + y * bias))
        for r, out in zip(rows, outs):
            s_y[pl.ds(r, CH), :] = out
        return carry

    if P == 1 and nseq % 2 == 0:
        lax.fori_loop(0, nseq // 2, single_block_pair, 0)
    else:
        lax.fori_loop(0, nseq, lambda s, carry: (one_sequence(s, 0), carry)[1], 0)

    @pl.when(o_idx == HY_ORDER - 1)
    def _():
        def finish(c, carry):
            start = pl.multiple_of(c * CH, CH)
            o_ref[pl.ds(start, CH), :] = _rms(s_y[pl.ds(start, CH), :])
            return carry

        lax.fori_loop(0, nch, finish, 0)


def _hy_kernel(hy_ref, cw_ref, cb_ref, fwd_ref, inv_ref, habc_ref, hd0c_ref, habl_ref, hd0l_ref, bias_ref,
               o_ref, s_y, s_x, s_u, s_v, *, l_ctx, l_lat, n_ctx_blk):
    i = pl.program_id(0)
    o_idx = pl.program_id(1)
    scratch = (s_y, s_x, s_u, s_v)

    @pl.when(i < n_ctx_blk)
    def _():
        _hy_variant(l_ctx, o_idx, hy_ref, cw_ref, cb_ref, fwd_ref, inv_ref, habc_ref, hd0c_ref, bias_ref, o_ref,
                    *scratch)

    @pl.when(i >= n_ctx_blk)
    def _():
        _hy_variant(l_lat, o_idx, hy_ref, cw_ref, cb_ref, fwd_ref, inv_ref, habl_ref, hd0l_ref, bias_ref, o_ref,
                    *scratch)


def _hy_call(layer, hyb, conv_w, conv_b, fwd, inv, habc, hd0c, habl, hd0l, bias, l_ctx, l_lat, n_ctx_blk):
    n = hyb.shape[0]
    ndc, ndl = habc.shape[2], habl.shape[2]
    pmax = max(l_ctx, l_lat) // CH
    row = lambda i, o: (i, 0)
    fixed = lambda i, o: (0, 0)
    lat_o = lambda i, o: jnp.where(i >= n_ctx_blk, o, 0)
    ctx_o = lambda i, o: jnp.where(i < n_ctx_blk, o, 0)
    return pl.pallas_call(
        functools.partial(_hy_kernel, l_ctx=l_ctx, l_lat=l_lat, n_ctx_blk=n_ctx_blk),
        grid=(n // SEG, HY_ORDER),
        in_specs=[pl.BlockSpec((SEG, 3 * D_B), row),
                  pl.BlockSpec((3, 3 * D_B), fixed), pl.BlockSpec((1, 3 * D_B), fixed),
                  pl.BlockSpec((2 * CH, CH), fixed), pl.BlockSpec((CH, 2 * CH), fixed),
                  pl.BlockSpec((1, 1, ndc, 2, CH, D_B), lambda i, o: (layer, ctx_o(i, o), 0, 0, 0, 0)),
                  pl.BlockSpec((1, 1, ndc, SUB, D_B), lambda i, o: (layer, ctx_o(i, o), 0, 0, 0)),
                  pl.BlockSpec((1, 1, ndl, 2, CH, D_B), lambda i, o: (layer, lat_o(i, o), 0, 0, 0, 0)),
                  pl.BlockSpec((1, 1, ndl, SUB, D_B), lambda i, o: (layer, lat_o(i, o), 0, 0, 0)),
                  pl.BlockSpec((1, 1, D_B), lambda i, o: (o, 0, 0))],
        out_specs=pl.BlockSpec((SEG, D_B), row),
        out_shape=jax.ShapeDtypeStruct((n, D_B), F32),
        scratch_shapes=[pltpu.VMEM((SEG, D_B), F32), pltpu.VMEM((HY_ORDER, SEG, D_B), F32),
                        pltpu.VMEM((max(pmax, 2), 2 * CH, D_B), F32), pltpu.VMEM((2, 2 * CH, D_B), BF16)],
        compiler_params=_cp(("parallel", "arbitrary")),
        name="hyena",
    )(hyb, conv_w, conv_b, fwd, inv, habc, hd0c, habl, hd0l, bias)


def _row_scan(x, op, fill, reverse):
    t, width = x.shape
    n_tiles = t // SUB
    sub = lax.broadcasted_iota(jnp.int32, (1, SUB, 1), 1)
    x3 = x.reshape(n_tiles, SUB, width)
    for s in (1, 2, 4):
        shift, keep = (SUB - s, sub < SUB - s) if reverse else (s, sub >= s)
        x3 = op(x3, jnp.where(keep, pltpu.roll(x3, shift, 1), fill))
    x = x3.reshape(t, width)
    out = [None] * n_tiles
    carry = None
    for i in (reversed(range(n_tiles)) if reverse else range(n_tiles)):
        tile = x[i * SUB:(i + 1) * SUB]
        out[i] = tile if carry is None else op(tile, carry)
        carry = out[i][0:1] if reverse else out[i][SUB - 1:SUB]
    return jnp.concatenate(out, axis=0)


_STK_ONE = 3 * SUB


def _mlstm_prep(d, g_ref, m_old):
    T = CH
    reverse = d == 1
    g = g_ref[...]
    if d == 1:
        g = pltpu.roll(g, LANE - 2 * H_C, 1)
    lane = lax.broadcasted_iota(jnp.int32, (1, LANE), 1)
    head = lane < H_C
    b = pltpu.roll(_row_scan(_log_sigmoid(g), jnp.add, 0.0, reverse), LANE - H_C, 1)
    r = jnp.where(head, g - b, 0.0)
    big_m = jnp.maximum(m_old, _row_scan(r, jnp.maximum, -jnp.inf, reverse))
    last = 0 if reverse else T - 1
    m_last = big_m[last:last + 1, :]
    low = lane < SUB
    p0, p1, p2 = (jnp.where(low, p.astype(F32), 0.0) for p in _split3(-big_m))
    cols = (p0 + pltpu.roll(p1, SUB, 1) + pltpu.roll(p2, 2 * SUB, 1)
            + jnp.where(jnp.logical_and(lane >= _STK_ONE, lane < _STK_ONE + SUB), 1.0, 0.0))
    rowid = lax.broadcasted_iota(jnp.int32, (SUB, 1), 0)
    r8 = r.T[0:SUB, :]
    m_last8 = sum(jnp.where(rowid == h, m_last[:, h:h + 1], 0.0) for h in range(H_C))
    ws8 = jnp.where(rowid < H_C, jnp.exp(r8 - m_last8), 0.0)
    return {"r3": [p.astype(F32) for p in _split3(r8)], "ws8": ws8, "wc": jnp.exp(m_old - m_last),
            "m_new": jnp.where(head, b[last:last + 1, :] + m_last, 0.0),
            "wi": jnp.exp(m_old - big_m), "e": jnp.exp(-(b + big_m)), "cols_b": cols.astype(BF16)}


def _mlstm_variant(carry, qf, vf, ktf, gf, qb_, vb_, ktb_, gb_, hf_ref, hb_ref, co_ref, no_ref, mo_ref,
                   s_cx, s_n, s_m):
    T = CH
    ii = lax.broadcasted_iota(jnp.int32, (T, T), 0)
    jj = lax.broadcasted_iota(jnp.int32, (T, T), 1)
    rowid = lax.broadcasted_iota(jnp.int32, (SUB, 1), 0)
    one_col = jnp.where(lax.broadcasted_iota(jnp.int32, (T, DK), 1) == 0, 1.0, 0.0).astype(BF16)
    prep = []
    for d, g_ref in ((0, gf), (1, gb_ if carry else gf)):
        m_old = s_m[d:d + 1, :] if carry else jnp.zeros((1, LANE), F32)
        prep.append(_mlstm_prep(d, g_ref, m_old))
    refs = ((qf, vf, ktf, hf_ref), (qb_, vb_, ktb_, hb_ref))
    for h in range(H_C):
        sl = slice(h * DK, (h + 1) * DK)
        for d in range(2):
            p = prep[d]
            q_ref, v_ref, kt_ref, h_ref = refs[d]
            idx = d * H_C + h
            if carry or d == 0:
                qb = (q_ref[:, sl] * (DK ** -0.5)).astype(BF16)
                kt = kt_ref[sl, :]
                ktb = kt.astype(BF16)
                v_ext = jnp.concatenate([v_ref[:, sl].astype(BF16), one_col], axis=1)
                s_raw = _dot(qb, ktb)
            tri = (jj >= ii) if d == 1 else (jj <= ii)
            sel = jnp.broadcast_to(jnp.where(rowid == h, 1.0, 0.0), (SUB, T))
            rr = sum(jnp.where(rowid == i, piece[h:h + 1, :], 0.0) for i, piece in enumerate(p["r3"]))
            rmat = jnp.concatenate([sel, sel, sel, rr, jnp.zeros((LANE - 4 * SUB, T), F32)], axis=0)
            expo = _dot(p["cols_b"], rmat.astype(BF16))
            s = s_raw * jnp.exp(jnp.where(tri, expo, -jnp.inf))
            intra = _dot(s.astype(BF16), v_ext)
            num, den = intra[:, :DK], intra[:, DK:DK + 1]
            if carry:
                cx = s_cx[idx]
                inter = _dot(qb, cx.astype(BF16))
                wi = p["wi"][:, h:h + 1]
                num, den = num + wi * inter[:, :DK], den + wi * inter[:, DK:DK + 1]
            h_ref[:, sl] = num / jnp.maximum(jnp.abs(den), p["e"][:, h:h + 1])
            upd = _dot((kt * p["ws8"][h:h + 1, :]).astype(BF16), v_ext)
            n_upd = lax.dot_general(p["ws8"].astype(BF16), ktb, _NT, preferred_element_type=F32)[h:h + 1, :]
            if carry:
                wc = p["wc"][:, h:h + 1]
                s_cx[idx] = wc * cx + upd
                s_n[idx:idx + 1, :] = wc * s_n[idx:idx + 1, :] + n_upd
            else:
                co_ref[0, idx] = upd[:, :DK]
                no_ref[0, idx:idx + 1, :] = n_upd
    m_rows = jnp.concatenate([prep[0]["m_new"], prep[1]["m_new"], jnp.zeros((SUB - 2, LANE), F32)], axis=0)
    if carry:
        s_m[...] = m_rows
    else:
        mo_ref[0] = m_rows


def _mlstm_kernel(*refs, n_ctx_steps, nc_lat):
    cx0_ref, n0_ref, m0_ref = refs[8:11]
    s_cx, s_n, s_m = refs[-3:]
    data = refs[:8] + refs[11:]
    t = pl.program_id(0)
    is_ctx = t < n_ctx_steps

    @pl.when(is_ctx)
    def _():
        _mlstm_variant(False, *data)

    @pl.when(jnp.logical_not(is_ctx))
    def _():
        @pl.when((t - n_ctx_steps) % nc_lat == 0)
        def _():
            s_cx[...] = cx0_ref[0]
            s_n[...] = n0_ref[0]
            s_m[...] = m0_ref[0]

        _mlstm_variant(True, *data)


def _mlstm_call(q, v, kt, gates, cx0, n0, m0, n_ctx_steps, nc_lat):
    n = q.shape[0]
    steps = n // CH
    nst = 2 * H_C

    def bwd_blk(t):
        r = jnp.maximum(t - n_ctx_steps, 0)
        return n_ctx_steps + (r // nc_lat) * nc_lat + (nc_lat - 1 - r % nc_lat)

    out_bwd = lambda t: jnp.where(t < n_ctx_steps, t, bwd_blk(t))
    lat_b = lambda t: jnp.maximum(t - n_ctx_steps, 0) // nc_lat
    ctx_b = lambda t: jnp.minimum(t, n_ctx_steps - 1)
    rows = lambda w, blk: pl.BlockSpec((CH, w), lambda t: (blk(t), 0))
    cols = lambda h, blk: pl.BlockSpec((h, CH), lambda t: (0, blk(t)))
    ident = lambda t: t
    return pl.pallas_call(
        functools.partial(_mlstm_kernel, n_ctx_steps=n_ctx_steps, nc_lat=nc_lat),
        grid=(steps,),
        in_specs=[rows(D_C, ident), rows(D_C, ident), cols(D_C, ident), rows(LANE, ident),
                  rows(D_C, bwd_blk), rows(D_C, bwd_blk), cols(D_C, bwd_blk), rows(LANE, bwd_blk),
                  pl.BlockSpec((1, nst, DK, 2 * DK), lambda t: (lat_b(t), 0, 0, 0)),
                  pl.BlockSpec((1, nst, DK), lambda t: (lat_b(t), 0, 0)),
                  pl.BlockSpec((1, SUB, LANE), lambda t: (lat_b(t), 0, 0))],
        out_specs=[rows(D_C, ident), rows(D_C, out_bwd),
                   pl.BlockSpec((1, nst, DK, DK), lambda t: (ctx_b(t), 0, 0, 0)),
                   pl.BlockSpec((1, nst, DK), lambda t: (ctx_b(t), 0, 0)),
                   pl.BlockSpec((1, SUB, LANE), lambda t: (ctx_b(t), 0, 0))],
        out_shape=[jax.ShapeDtypeStruct((n, D_C), F32), jax.ShapeDtypeStruct((n, D_C), F32),
                   jax.ShapeDtypeStruct((n_ctx_steps, nst, DK, DK), F32),
                   jax.ShapeDtypeStruct((n_ctx_steps, nst, DK), F32),
                   jax.ShapeDtypeStruct((n_ctx_steps, SUB, LANE), F32)],
        scratch_shapes=[pltpu.VMEM((nst, DK, 2 * DK), F32), pltpu.VMEM((nst, DK), F32),
                        pltpu.VMEM((SUB, LANE), F32)],
        compiler_params=_cp(("arbitrary",)),
        name="mlstm",
    )(q, v, kt, gates, q, v, kt, gates, cx0, n0, m0)


def _out_kernel(x_ref, oa_ref, ob_ref, hf_ref, hb_ref, og_ref, mod_ref, mg_ref, w_ref, g_ref, b_ref,
                rw_ref, rb_ref, x1_ref, he_ref, cnt_ref, it_ref, s_cnt):
    i = pl.program_id(0)

    @pl.when(i == 0)
    def _():
        s_cnt[...] = jnp.zeros(s_cnt.shape, F32)

    m = mod_ref[0]
    mg = mg_ref[...]
    acc = _dot((oa_ref[...] * mg[:, :D_A]).astype(BF16), w_ref[0:D_A, :])
    acc += _dot((ob_ref[...] * mg[:, D_A:D_A + D_B]).astype(BF16), w_ref[D_A:D_A + D_B, :])
    hc = hf_ref[...] + hb_ref[...]
    og = og_ref[...]
    off = D_A + D_B
    for h in range(H_C):
        sl = slice(h * DK, (h + 1) * DK)
        oc = _sigmoid(og[:, sl]) * _rms(hc[:, sl]) * mg[:, off + h * DK:off + (h + 1) * DK]
        acc += _dot(oc.astype(BF16), w_ref[off + h * DK:off + (h + 1) * DK, :])
    x1 = _ln_plain(ALPHA * x_ref[...] + m[2:3] * acc) * g_ref[...] + b_ref[...]
    x1_ref[...] = x1
    h2 = x1 * (1.0 + m[4:5]) + m[3:4]
    he_ref[:, :D_MODEL] = h2

    lg = _dot3(h2, rw_ref[...]) + rb_ref[...]
    col = lax.broadcasted_iota(jnp.int32, lg.shape, 1)
    ninf = -jnp.inf
    lgm = jnp.where(col < N_GROUPS, lg, ninf)
    mx = jnp.max(lgm, -1, keepdims=True)
    gi = jnp.min(jnp.where(lgm == mx, col, LANE), -1, keepdims=True)
    pg_top = 1.0 / jnp.sum(jnp.where(col < N_GROUPS, jnp.exp(lg - mx), 0.0), -1, keepdims=True)
    lo4 = N_GROUPS + E_PER_GROUP * gi
    lem = jnp.where(jnp.logical_and(col >= lo4, col < lo4 + E_PER_GROUP), lg, ninf)
    v1 = jnp.max(lem, -1, keepdims=True)
    i1 = jnp.min(jnp.where(lem == v1, col, LANE), -1, keepdims=True)
    lem2 = jnp.where(col == i1, ninf, lem)
    v2 = jnp.max(lem2, -1, keepdims=True)
    i2 = jnp.min(jnp.where(lem2 == v2, col, LANE), -1, keepdims=True)
    e21 = jnp.exp(v2 - v1)
    w1 = pg_top / (1.0 + e21)
    w2 = pg_top * e21 / (1.0 + e21)
    e1, e2 = i1 - N_GROUPS, i2 - N_GROUPS
    first_lo = e1 < e2
    elo, ehi = jnp.minimum(e1, e2), jnp.maximum(e1, e2)
    wlo, whi = jnp.where(first_lo, w1, w2), jnp.where(first_lo, w2, w1)
    llo, lhi = elo - E_PER_GROUP * gi, ehi - E_PER_GROUP * gi
    cls = gi * N_PAIRS + ((llo * (7 - llo)) >> 1) + lhi - llo - 1

    oh = jnp.where(col == cls, 1.0, 0.0)
    ii = lax.broadcasted_iota(jnp.int32, (TM, TM), 0)
    jj = lax.broadcasted_iota(jnp.int32, (TM, TM), 1)
    before = jnp.where(jj < ii, 1.0, 0.0).astype(BF16)
    cnt = s_cnt[0:1, :]
    rank = jnp.sum(oh * (_dot(before, oh.astype(BF16)) + cnt), -1, keepdims=True)
    cnt = cnt + jnp.sum(oh, 0, keepdims=True)
    s_cnt[...] = jnp.broadcast_to(cnt, s_cnt.shape)
    cnt_ref[...] = jnp.broadcast_to(cnt, cnt_ref.shape)

    info = jnp.zeros(lg.shape, F32)
    for c, val in ((I_CLS, cls.astype(F32)), (I_ELO, elo.astype(F32)), (I_EHI, ehi.astype(F32)),
                   (I_WLO, wlo), (I_WHI, whi), (I_RANK, rank)):
        info = jnp.where(col == c, val, info)
    he_ref[:, D_MODEL:] = info
    it_ref[...] = info.T[:SUB, :]


def _out_call(x, out_a, out_b, hcf, hcb, ogate, mod_l, mix_g, w_out, ln_g, ln_b, rt_w, rt_b, group_of_tile):
    n = x.shape[0]
    row = lambda i: (i, 0)
    fixed = lambda i: (0, 0)
    return pl.pallas_call(
        _out_kernel,
        grid=(n // TM,),
        in_specs=[pl.BlockSpec((TM, D_MODEL), row), pl.BlockSpec((TM, D_A), row), pl.BlockSpec((TM, D_B), row),
                  pl.BlockSpec((TM, D_C), row), pl.BlockSpec((TM, D_C), row), pl.BlockSpec((TM, D_C), row),
                  pl.BlockSpec((1, 6, D_MODEL), lambda i: (group_of_tile(i), 0, 0)),
                  pl.BlockSpec((1, D_MODEL), fixed), pl.BlockSpec((D_MODEL, D_MODEL), fixed),
                  pl.BlockSpec((1, D_MODEL), fixed), pl.BlockSpec((1, D_MODEL), fixed),
                  pl.BlockSpec((D_MODEL, LANE), fixed), pl.BlockSpec((1, LANE), fixed)],
        out_specs=[pl.BlockSpec((TM, D_MODEL), row), pl.BlockSpec((TM, D_EXT), row),
                   pl.BlockSpec((SUB, LANE), fixed), pl.BlockSpec((SUB, TM), lambda i: (0, i))],
        out_shape=[jax.ShapeDtypeStruct((n, D_MODEL), F32), jax.ShapeDtypeStruct((n, D_EXT), F32),
                   jax.ShapeDtypeStruct((SUB, LANE), F32), jax.ShapeDtypeStruct((SUB, n), F32)],
        scratch_shapes=[pltpu.VMEM((SUB, LANE), F32)],
        compiler_params=_cp(("arbitrary",)),
        name="out_proj_router",
    )(x, out_a, out_b, hcf, hcb, ogate, mod_l, mix_g, w_out, ln_g, ln_b, rt_w, rt_b)


def _row_copy(src_ref, src_row, dst_ref, dst_row, sem):
    return pltpu.make_async_copy(src_ref.at[pl.ds(src_row, 1), :], dst_ref.at[pl.ds(dst_row, 1), :], sem)


def _sorted_row(cls_ref, rank_ref, start_ref, t):
    return start_ref[cls_ref[t]] + rank_ref[t]


def _scatter_kernel(cls_ref, rank_ref, start_ref, pad_ref, na_ref, x_ref, o_ref, ring, z_ref, sem_z, load_sems,
                    row_sems):
    n_tiles = o_ref.shape[0] // TM

    def zero_tile(row):
        return pltpu.make_async_copy(z_ref, o_ref.at[pl.ds(pl.multiple_of(row, TM), TM), :], sem_z)

    def for_zero_tiles(fn):
        def per_class(c, carry):
            row = pad_ref[c]

            @pl.when(row >= 0)
            def _():
                fn(zero_tile(row))

            return carry

        lax.fori_loop(0, N_CLASS, per_class, 0)

        def per_idle(t, carry):
            fn(zero_tile(t * TM))
            return carry

        lax.fori_loop(na_ref[0], n_tiles, per_idle, 0)

    i = pl.program_id(0)
    n_steps = x_ref.shape[0] // TM

    def load(step):
        return pltpu.make_async_copy(x_ref.at[pl.ds(pl.multiple_of(step * TM, TM), TM), :],
                                     ring.at[step % RING], load_sems.at[step % RING])

    def start_rows(step):
        slot = step % RING

        def body(r, carry):
            dst = _sorted_row(cls_ref, rank_ref, start_ref, step * TM + r)
            _row_copy(ring.at[slot], r, o_ref, dst, row_sems.at[slot]).start()
            return carry

        lax.fori_loop(0, TM, body, 0, unroll=DMA_UNROLL)

    def wait_rows(step):
        slot = step % RING
        pltpu.make_async_copy(ring.at[slot], o_ref.at[pl.ds(0, TM), :], row_sems.at[slot]).wait()

    @pl.when(i == 0)
    def _():
        z_ref[...] = jnp.zeros(z_ref.shape, F32)
        for_zero_tiles(lambda cp: cp.start())
        for_zero_tiles(lambda cp: cp.wait())
        load(0).start()

    @pl.when(i >= RING - 1)
    def _():
        wait_rows(i - (RING - 1))

    @pl.when(i + 1 < n_steps)
    def _():
        load(i + 1).start()

    load(i).wait()
    start_rows(i)

    @pl.when(i == n_steps - 1)
    def _():
        for back in range(min(RING - 1, n_steps) - 1, -1, -1):
            wait_rows(i - back)


def _scatter_call(plan, h_ext, n_tiles_max):
    any_spec = pl.BlockSpec(memory_space=pl.ANY)
    return pl.pallas_call(
        _scatter_kernel,
        grid_spec=pltpu.PrefetchScalarGridSpec(
            num_scalar_prefetch=5, grid=(h_ext.shape[0] // TM,),
            in_specs=[any_spec],
            out_specs=any_spec,
            scratch_shapes=[pltpu.VMEM((RING, TM, D_EXT), F32), pltpu.VMEM((TM, D_EXT), F32),
                            pltpu.SemaphoreType.DMA(()),
                            pltpu.SemaphoreType.DMA((RING,)), pltpu.SemaphoreType.DMA((RING,))]),
        out_shape=jax.ShapeDtypeStruct((n_tiles_max * TM, D_EXT), F32),
        compiler_params=_cp(("arbitrary",)),
        name="moe_scatter",
    )(plan["cls"], plan["rank"], plan["row_start"], plan["pad_rows"], plan["n_act"], h_ext)


def _moe_kernel(ta_ref, tb_ref, ca_ref, cb_ref, nv_ref, na_ref, x_ref, w1a, w3a, w2a, w1b, w3b, w2b, o_ref,
                s1a, s3a, s2a, s1b, s3b, s2b):
    del ta_ref, tb_ref
    t = pl.program_id(0)
    active = t < na_ref[0]
    half = TM // 2

    @pl.when(jnp.logical_not(active))
    def _():
        o_ref[...] = jnp.zeros(o_ref.shape, F32)

    for chg, srcs, dsts in ((ca_ref, (w1a, w3a, w2a), (s1a, s3a, s2a)), (cb_ref, (w1b, w3b, w2b), (s1b, s3b, s2b))):
        @pl.when(jnp.logical_and(active, chg[t] == 1))
        def _():
            for src, dst in zip(srcs, dsts):
                dst[...] = src[0].astype(BF16)

    def run(rows):
        xe = x_ref[0:rows, :]
        x = xe[:, :D_MODEL].astype(BF16)

        def expert(w1, w3, w2, gate):
            a = _dot(x, w1[...])
            hm = a * _sigmoid(a) * _dot(x, w3[...]) * gate
            return _dot(hm.astype(BF16), w2[...])

        o_ref[0:rows, :] = (expert(s1a, s3a, s2a, xe[:, D_MODEL + I_WLO:D_MODEL + I_WLO + 1])
                            + expert(s1b, s3b, s2b, xe[:, D_MODEL + I_WHI:D_MODEL + I_WHI + 1]))

    @pl.when(jnp.logical_and(active, nv_ref[t] > half))
    def _():
        run(TM)

    @pl.when(jnp.logical_and(active, nv_ref[t] <= half))
    def _():
        run(half)
        o_ref[half:, :] = jnp.zeros((TM - half, D_MODEL), F32)


def _moe_call(layer, plan, x_sorted, w1, w3, w2):
    r = x_sorted.shape[0]
    act = lambda t, ta, tb, ca, cb, nv, na: (jnp.minimum(t, na[0] - 1), 0)
    ea = lambda t, ta, tb, ca, cb, nv, na: (layer, ta[t], 0, 0)
    eb = lambda t, ta, tb, ca, cb, nv, na: (layer, tb[t], 0, 0)
    up = lambda m: pl.BlockSpec((None, 1, D_MODEL, D_E), m)
    down = lambda m: pl.BlockSpec((None, 1, D_E, D_MODEL), m)
    return pl.pallas_call(
        _moe_kernel,
        grid_spec=pltpu.PrefetchScalarGridSpec(
            num_scalar_prefetch=6, grid=(r // TM,),
            in_specs=[pl.BlockSpec((TM, D_EXT), act), up(ea), up(ea), down(ea), up(eb), up(eb), down(eb)],
            out_specs=pl.BlockSpec((TM, D_MODEL), lambda t, *_: (t, 0)),
            scratch_shapes=[pltpu.VMEM((D_MODEL, D_E), BF16), pltpu.VMEM((D_MODEL, D_E), BF16),
                            pltpu.VMEM((D_E, D_MODEL), BF16)] * 2),
        out_shape=jax.ShapeDtypeStruct((r, D_MODEL), F32),
        compiler_params=_cp(("arbitrary",)),
        name="moe_experts",
    )(plan["tile_a"], plan["tile_b"], plan["chg_a"], plan["chg_b"], plan["valid"], plan["n_act"],
      x_sorted, w1, w3, w2, w1, w3, w2)


def _ln2_kernel(cls_ref, rank_ref, start_ref, x1_ref, mod_ref, g_ref, b_ref, y_ref, *rest, n_ctx_tiles):
    o_refs, (buf, sems) = rest[:-2], rest[-2:]
    i = pl.program_id(0)
    n_steps = pl.num_programs(0)

    def start_rows(step):
        slot = step % 2

        def body(r, carry):
            src = _sorted_row(cls_ref, rank_ref, start_ref, step * TM + r)
            _row_copy(y_ref, src, buf.at[slot], r, sems.at[slot]).start()
            return carry

        lax.fori_loop(0, TM, body, 0, unroll=DMA_UNROLL)

    @pl.when(i == 0)
    def _():
        start_rows(0)

    @pl.when(i + 1 < n_steps)
    def _():
        start_rows(i + 1)

    slot = i % 2
    pltpu.make_async_copy(y_ref.at[pl.ds(0, TM), :], buf.at[slot], sems.at[slot]).wait()
    m = mod_ref[0]
    y = _ln_plain(ALPHA * x1_ref[...] + m[5:6] * buf[slot]) * g_ref[...] + b_ref[...]
    if len(o_refs) == 1:
        o_refs[0][...] = y
    else:
        @pl.when(i < n_ctx_tiles)
        def _():
            o_refs[0][...] = y

        @pl.when(i >= n_ctx_tiles)
        def _():
            o_refs[1][...] = y


def _ln2_call(plan, x1, y_sorted, mod_l, ln_g, ln_b, group_of_tile, n_ctx_tiles, split):
    n = x1.shape[0]
    row = lambda i, *_: (i, 0)
    fixed = lambda i, *_: (0, 0)
    if split:
        n_ctx = n_ctx_tiles * TM
        out_specs = [pl.BlockSpec((TM, D_MODEL), lambda i, *_: (jnp.minimum(i, n_ctx_tiles - 1), 0)),
                     pl.BlockSpec((TM, D_MODEL), lambda i, *_: (jnp.maximum(i - n_ctx_tiles, 0), 0))]
        out_shape = [jax.ShapeDtypeStruct((n_ctx, D_MODEL), F32), jax.ShapeDtypeStruct((n - n_ctx, D_MODEL), F32)]
    else:
        out_specs = [pl.BlockSpec((TM, D_MODEL), row)]
        out_shape = [jax.ShapeDtypeStruct((n, D_MODEL), F32)]
    return pl.pallas_call(
        functools.partial(_ln2_kernel, n_ctx_tiles=n_ctx_tiles),
        grid_spec=pltpu.PrefetchScalarGridSpec(
            num_scalar_prefetch=3, grid=(n // TM,),
            in_specs=[pl.BlockSpec((TM, D_MODEL), row),
                      pl.BlockSpec((1, 6, D_MODEL), lambda i, *_: (group_of_tile(i), 0, 0)),
                      pl.BlockSpec((1, D_MODEL), fixed), pl.BlockSpec((1, D_MODEL), fixed),
                      pl.BlockSpec(memory_space=pl.ANY)],
            out_specs=out_specs,
            scratch_shapes=[pltpu.VMEM((2, TM, D_MODEL), F32), pltpu.SemaphoreType.DMA((2,))]),
        out_shape=out_shape,
        compiler_params=_cp(("arbitrary",)),
        name="moe_gather_ln2",
    )(plan["cls"], plan["rank"], plan["row_start"], x1, mod_l, ln_g, ln_b, y_sorted)


def _dft_matrices():
    n2 = 2 * CH
    f = np.arange(CH, dtype=np.float64)[:, None]
    t = np.arange(CH, dtype=np.float64)[None, :]
    ang = 2.0 * np.pi * f * t / n2
    re, im = np.cos(ang), -np.sin(ang)
    im[0, :] = np.cos(np.pi * t[0])
    fwd = np.concatenate([re, im], axis=0)
    scale = np.full((CH, 1), 2.0 / n2)
    scale[0, 0] = 1.0 / n2
    inv = np.concatenate([(re * scale).T, (im * scale).T], axis=1)
    return fwd.astype(np.float32), inv.astype(np.float32)


def _filter_features(L):
    lag = np.arange(-L, L)
    m = np.minimum(np.abs(lag), L - 1)
    t = (np.arange(L, dtype=np.float32) / np.float32(max(L - 1, 1)))[m]
    w = (np.float32(2.0 * math.pi) * np.arange(L, dtype=np.float32) / np.float32(L))[m]
    bands = np.linspace(1e-4, HY_BANDS - 1, HY_BANDS, dtype=np.float32)
    z = np.zeros((2 * L, LANE), np.float32)
    z[:, 0] = t
    z[:, 1:1 + HY_BANDS] = np.cos(w[:, None] * bands)
    z[:, 1 + HY_BANDS:HY_EMB] = -np.sin(w[:, None] * bands)
    lo, hi = math.log(HY_DECAY_TARGET) / 1.5, math.log(HY_DECAY_TARGET) / 0.3
    deltas = np.abs(np.linspace(lo, hi, D_B, dtype=np.float32))
    dec = np.exp(-t[:, None] * deltas)
    return z, dec.astype(np.float32)


def _sincos_2d(rows, cols):
    quarter = D_MODEL // 4
    omega = 1.0 / (10000.0 ** (jnp.arange(quarter, dtype=F32) / quarter))

    def emb(n):
        ang = jnp.arange(n, dtype=F32)[:, None] * omega[None]
        return jnp.concatenate([jnp.sin(ang), jnp.cos(ang)], -1)

    er, ec = emb(rows), emb(cols)
    half = D_MODEL // 2
    pos = jnp.concatenate([jnp.broadcast_to(er[:, None], (rows, cols, half)),
                           jnp.broadcast_to(ec[None], (rows, cols, half))], -1)
    return pos.reshape(rows * cols, D_MODEL)


def _pad_to(x, shape):
    return jnp.pad(x, [(0, s - d) for d, s in zip(x.shape, shape)])


def _block_diag(w):
    eye = jnp.eye(H_A, dtype=w.dtype)
    return jnp.einsum("hij,hg->higj", w, eye).reshape(D_A, D_A)


_PAIR_LO = np.array([0, 0, 0, 1, 1, 2], np.int32)
_PAIR_HI = np.array([1, 2, 3, 2, 3, 3], np.int32)


def _routing_plan(info_t, counts, n_tiles_max):
    cnt = counts[0, :N_CLASS].astype(jnp.int32)
    tiles = (cnt + TM - 1) // TM
    tile_end = jnp.cumsum(tiles)
    n_act = tile_end[-1]
    t = jnp.minimum(jnp.arange(n_tiles_max, dtype=jnp.int32), n_act - 1)
    tcls = jnp.minimum(jnp.sum((tile_end[None, :] <= t[:, None]).astype(jnp.int32), 1), N_CLASS - 1)
    grp, pair = tcls // N_PAIRS, tcls % N_PAIRS
    tile_a = (grp * E_PER_GROUP + jnp.asarray(_PAIR_LO)[pair]).astype(jnp.int32)
    tile_b = (grp * E_PER_GROUP + jnp.asarray(_PAIR_HI)[pair]).astype(jnp.int32)
    valid = jnp.clip(cnt[tcls] - (t - (tile_end - tiles)[tcls]) * TM, 0, TM).astype(jnp.int32)
    first = jnp.ones((1,), jnp.int32)
    changed = lambda e: jnp.concatenate([first, (e[1:] != e[:-1]).astype(jnp.int32)])
    return {"cls": info_t[I_CLS].astype(jnp.int32), "rank": info_t[I_RANK].astype(jnp.int32),
            "row_start": ((tile_end - tiles) * TM).astype(jnp.int32), "tile_a": tile_a, "tile_b": tile_b,
            "chg_a": changed(tile_a), "chg_b": changed(tile_b), "valid": valid,
            "n_act": n_act.reshape(1).astype(jnp.int32),
            "pad_rows": jnp.where(tiles > 0, (tile_end - 1) * TM, -1).astype(jnp.int32)}


def kernel(x_prompt, x_sample, c, state_lru, state_mlstm_C, state_mlstm_n, state_mlstm_m, c_ctx, w_ada, b_ada, w_in, b_in, conv_a_w, conv_a_b, lru_wa, lru_ba, lru_wx, lru_bx, lru_lam, conv_b_w, conv_b_b, hy_w1, hy_b1, hy_w2, hy_b2, hy_freq, hy_w3, hy_bias, mix_g, w_out, ln1_g, ln1_b, rt_wg, rt_bg, rt_we, rt_be, moe_w1, moe_w3, moe_w2, ln2_g, ln2_b):
    B, l_ctx, D = x_prompt.shape
    b_lat, l_lat, _ = x_sample.shape
    n_ctx, n_lat = B * l_ctx, b_lat * l_lat
    n = n_ctx + n_lat
    assert D == D_MODEL and w_in.shape[-1] == D_MAIN + N_GATE
    assert SEG % l_ctx == 0 and l_lat == SEG and l_ctx % CH == 0 and n_ctx % SEG == 0
    assert l_ctx == CH, "the mLSTM step schedule assumes one chunk per context sequence"
    assert 1 + b_lat <= SUB
    n_ctx_blk = n_ctx // SEG
    n_ctx_tiles = n_ctx // TM
    tiles_per_lat = l_lat // TM
    nc_lat = l_lat // CH

    def group_of(tile_rows):
        first_lat, per_seq = n_ctx // tile_rows, l_lat // tile_rows
        return lambda i: jnp.where(i < first_lat, 0, 1 + (i - first_lat) // per_seq)

    group_of_tile = group_of(TM)

    cond = jnp.concatenate([c_ctx[None], c, jnp.zeros((SUB - 1 - b_lat, D), F32)], 0)
    mod = _mod_call(cond, w_ada, b_ada).reshape(DEPTH, SUB, 6, D)
    pos = _sincos_2d(l_lat // GRID_W, GRID_W)
    x = _entry_call(x_prompt.reshape(n_ctx, D), x_sample.reshape(n_lat, D), pos)

    fwd_np, inv_np = _dft_matrices()
    fwd32 = jnp.asarray(fwd_np)
    fwd16, inv16 = fwd32.astype(BF16), jnp.asarray(inv_np).astype(BF16)
    fw1 = _pad_to(hy_w1, (DEPTH, LANE, LANE))
    fb1 = _pad_to(hy_b1[:, None, :], (DEPTH, 1, LANE))
    fw2 = _pad_to(hy_w2, (DEPTH, LANE, LANE))
    fb2 = _pad_to(hy_b2[:, None, :], (DEPTH, 1, LANE))
    ffr = _pad_to(hy_freq[:, None, :], (DEPTH, 1, LANE))
    fw3 = _pad_to(hy_w3, (DEPTH, LANE, HY_ORDER * 2 * D_B))
    spectra = {}
    for L in (l_ctx, l_lat):
        z_np, dec_np = _filter_features(L)
        spectra[L] = _filt_call(L, jnp.asarray(z_np), jnp.asarray(dec_np), fw1, fb1, fw2, fb2, ffr, fw3, fwd32)

    lat_slots = SEG // l_ctx
    st_lru, st_c, st_n, st_m = [], [], [], []
    for l in range(DEPTH):
        b_main = b_in[l, None, :D_MAIN]
        k_lo, k_hi = _Z_CUTS[_K_CUT], _Z_CUTS[_K_CUT + 1]
        w_kt = w_in[l, :, k_lo:k_hi].T.astype(BF16)
        b_kt = jnp.broadcast_to(b_in[l, k_lo:k_hi, None], (D_C, TM_IN))
        w_gate = _pad_to(w_in[l, :, D_MAIN:], (D, LANE))
        b_gate = _pad_to(b_in[l, None, D_MAIN:], (1, LANE))
        xa, ya, hyb, q, v, og, gates, kt = _in_call(l, x, mod[l], w_in, b_main, w_kt, b_kt, w_gate, b_gate,
                                                    group_of(TM_IN))

        lru_w = jnp.concatenate([_block_diag(lru_wa[l, 0]), _block_diag(lru_wx[l, 0]),
                                 _block_diag(lru_wa[l, 1]), _block_diag(lru_wx[l, 1])], 1).astype(BF16)
        lru_b = jnp.concatenate([lru_ba[l, 0], lru_bx[l, 0], lru_ba[l, 1], lru_bx[l, 1]])[None]
        h0_lat = _pad_to(state_lru[:, l][:, None], (b_lat, lat_slots, 2, D_A))
        h0_all = jnp.concatenate([jnp.zeros((n_ctx_blk, lat_slots, 2, D_A), F32), h0_lat], 0)
        out_a, lru_last = _lru_call(xa, ya, conv_a_w[l], conv_a_b[l, None], lru_w, lru_b, lru_lam[l], h0_all,
                                    l_ctx, l_lat, n_ctx_blk)

        habc, hd0c = spectra[l_ctx]
        habl, hd0l = spectra[l_lat]
        out_b = _hy_call(l, hyb, conv_b_w[l], conv_b_b[l, None], fwd16, inv16, habc, hd0c, habl, hd0l,
                         hy_bias[l][:, None, :], l_ctx, l_lat, n_ctx_blk)

        n0 = state_mlstm_n[:, l].reshape(b_lat, 2 * H_C, DK)
        cx0 = jnp.concatenate([state_mlstm_C[:, l].reshape(b_lat, 2 * H_C, DK, DK), n0[..., None],
                               jnp.zeros((b_lat, 2 * H_C, DK, DK - 1), F32)], -1)
        m0 = _pad_to(state_mlstm_m[:, l], (b_lat, SUB, LANE))
        hcf, hcb, c_fin, n_fin, m_fin = _mlstm_call(q, v, kt, gates, cx0, n0, m0, n_ctx // CH, nc_lat)

        rt_w = _pad_to(jnp.concatenate([rt_wg[l], rt_we[l]], 1), (D, LANE))
        rt_b = _pad_to(jnp.concatenate([rt_bg[l], rt_be[l]])[None], (1, LANE))
        x1, h_ext, counts, info_t = _out_call(x, out_a, out_b, hcf, hcb, og, mod[l], mix_g[l, None],
                                              w_out[l].astype(BF16), ln1_g[l, None], ln1_b[l, None], rt_w, rt_b,
                                              group_of_tile)

        n_tiles_max = n // TM + N_CLASS
        plan = _routing_plan(info_t, counts, n_tiles_max)
        x_sorted = _scatter_call(plan, h_ext, n_tiles_max)
        y_sorted = _moe_call(l, plan, x_sorted, moe_w1, moe_w3, moe_w2)
        outs = _ln2_call(plan, x1, y_sorted, mod[l], ln2_g[l, None], ln2_b[l, None], group_of_tile,
                         n_ctx_tiles, split=(l == DEPTH - 1))
        x = outs[0]

        st_lru.append(lru_last[:n_ctx_blk].reshape(B, 2, D_A))
        st_c.append(c_fin.reshape(B, 2, H_C, DK, DK))
        st_n.append(n_fin.reshape(B, 2, H_C, DK))
        st_m.append(m_fin[:, :2, :H_C])

    return (outs[0].reshape(B, l_ctx, D), outs[1].reshape(b_lat, l_lat, D),
            jnp.stack(st_lru, 1), jnp.stack(st_c, 1), jnp.stack(st_n, 1), jnp.stack(st_m, 1))
```

```python
import functools
import math

import numpy as np
import jax
import jax.numpy as jnp
from jax import lax
from jax.experimental import pallas as pl
from jax.experimental.pallas import tpu as pltpu

F32 = jnp.float32
BF16 = jnp.bfloat16

D_MODEL = 1024
DEPTH = 2
GRID_W = 64
D_A = 256
H_A = 4
BA = D_A // H_A
LRU_C = 8.0
D_B = 256
HY_ORDER = 2
HY_BANDS = 16
HY_EMB = 1 + 2 * HY_BANDS
HY_FH = 64
HY_DECAY_TARGET = 1e-2
D_C = 512
H_C = 4
DK = D_C // H_C
N_GROUPS = 4
E_PER_GROUP = 4
N_EXP = N_GROUPS * E_PER_GROUP
N_PAIRS = 6
N_CLASS = N_GROUPS * N_PAIRS
D_E = 512
ALPHA = (2 * DEPTH) ** 0.25
EPS = 1e-6
D_MAIN = 2 * D_A + 3 * D_B + 4 * D_C
N_GATE = 4 * H_C

LANE = 128
SUB = 8
VMEM_LIMIT = 56 * 1024 * 1024

CH = 256
SEG = 2048
TM = 256
TM_IN = 512
D_EXT = D_MODEL + LANE
FS = 64
DMA_UNROLL = 8
RING = 3

I_CLS, I_ELO, I_EHI, I_WLO, I_WHI, I_RANK = range(6)


def _cp(sem, vmem=VMEM_LIMIT):
    return pltpu.CompilerParams(dimension_semantics=sem, vmem_limit_bytes=vmem)


def _dot(a, b):
    return jnp.dot(a, b, preferred_element_type=F32)


def _split2(x):
    hi = x.astype(BF16)
    lo = (x - hi.astype(F32)).astype(BF16)
    return hi, lo


def _dot3(a, b):
    ah, al = _split2(a)
    bh, bl = _split2(b)
    return _dot(ah, bh) + (_dot(ah, bl) + _dot(al, bh))


def _split3(x):
    hi = x.astype(BF16)
    r1 = x - hi.astype(F32)
    mid = r1.astype(BF16)
    lo = (r1 - mid.astype(F32)).astype(BF16)
    return hi, mid, lo


def _sigmoid(x):
    return 1.0 / (1.0 + jnp.exp(-x))


def _log_sigmoid(x):
    return jnp.minimum(x, 0.0) - jnp.log1p(jnp.exp(-jnp.abs(x)))


def _gelu_tanh(x):
    return 0.5 * x * (1.0 + jnp.tanh(math.sqrt(2.0 / math.pi) * (x + 0.044715 * (x * x * x))))


def _ln_plain(x):
    mu = jnp.mean(x, -1, keepdims=True)
    xc = x - mu
    var = jnp.mean(xc * xc, -1, keepdims=True)
    return xc * lax.rsqrt(var + EPS)


def _rms(x):
    return x * lax.rsqrt(jnp.mean(x * x, -1, keepdims=True) + EPS)


def _halo_rows(ref, start, rows):
    total = ref.shape[0]
    prev = ref[pl.ds(pl.multiple_of(jnp.maximum(start - SUB, 0), SUB), SUB), :]
    main = ref[pl.ds(start, rows), :]
    nxt = ref[pl.ds(pl.multiple_of(jnp.minimum(start + rows, total - SUB), SUB), SUB), :]
    return jnp.concatenate([prev, main, nxt], axis=0), main


def _mod_kernel(c_ref, w_ref, b_ref, o_ref):
    c = c_ref[...]
    o_ref[0] = _dot3(c * _sigmoid(c), w_ref[0]) + b_ref[0]


def _mod_call(cond, w_ada, b_ada):
    tn = 1536
    n6 = w_ada.shape[-1]
    return pl.pallas_call(
        _mod_kernel,
        grid=(DEPTH, n6 // tn),
        in_specs=[pl.BlockSpec((SUB, D_MODEL), lambda l, j: (0, 0)),
                  pl.BlockSpec((1, D_MODEL, tn), lambda l, j: (l, 0, j)),
                  pl.BlockSpec((1, 1, tn), lambda l, j: (l, 0, j))],
        out_specs=pl.BlockSpec((1, SUB, tn), lambda l, j: (l, 0, j)),
        out_shape=jax.ShapeDtypeStruct((DEPTH, SUB, n6), F32),
        compiler_params=_cp(("parallel", "parallel")),
        name="adaln_mod",
    )(cond, w_ada, b_ada.reshape(DEPTH, 1, n6))


def _entry_kernel(xc_ref, xl_ref, pos_ref, o_ref, *, n_ctx_tiles):
    i = pl.program_id(0)

    @pl.when(i < n_ctx_tiles)
    def _():
        o_ref[...] = _ln_plain(xc_ref[...])

    @pl.when(i >= n_ctx_tiles)
    def _():
        o_ref[...] = _ln_plain(xl_ref[...] + pos_ref[...])


def _entry_call(xc, xl, pos):
    tm = 512
    n_ctx, n_lat, l_lat = xc.shape[0], xl.shape[0], pos.shape[0]
    nct = n_ctx // tm
    per_seq = l_lat // tm
    return pl.pallas_call(
        functools.partial(_entry_kernel, n_ctx_tiles=nct),
        grid=((n_ctx + n_lat) // tm,),
        in_specs=[pl.BlockSpec((tm, D_MODEL), lambda i: (jnp.minimum(i, nct - 1), 0)),
                  pl.BlockSpec((tm, D_MODEL), lambda i: (jnp.maximum(i - nct, 0), 0)),
                  pl.BlockSpec((tm, D_MODEL), lambda i: (jnp.maximum(i - nct, 0) % per_seq, 0))],
        out_specs=pl.BlockSpec((tm, D_MODEL), lambda i: (i, 0)),
        out_shape=jax.ShapeDtypeStruct((n_ctx + n_lat, D_MODEL), F32),
        compiler_params=_cp(("parallel",)),
        name="entry_ln",
    )(xc, xl, pos)


_Z_CUTS = (0, D_A, 2 * D_A, 2 * D_A + 3 * D_B, 2 * D_A + 3 * D_B + D_C, 2 * D_A + 3 * D_B + 2 * D_C,
           2 * D_A + 3 * D_B + 3 * D_C, D_MAIN)


_K_CUT = 4
_ROW_CUTS = tuple(c for i, c in enumerate(zip(_Z_CUTS[:-1], _Z_CUTS[1:])) if i != _K_CUT)
_NT = (((1,), (1,)), ((), ()))


def _in_kernel(x_ref, mod_ref, w_ref, b_ref, bkt_ref, wg_ref, bg_ref, *refs):
    out_refs, (w16, wkt_ref) = refs[:-2], refs[-2:]

    @pl.when(pl.program_id(0) == 0)
    def _():
        w16[...] = w_ref[0, :, :D_MAIN].astype(BF16)
        wkt_ref[...] = w_ref[0, :, _Z_CUTS[_K_CUT]:_Z_CUTS[_K_CUT + 1]].T.astype(BF16)

    m = mod_ref[0]
    h = x_ref[...] * (1.0 + m[1:2]) + m[0:1]
    hb = h.astype(BF16)
    for ref, (a, b) in zip(out_refs[:-2], _ROW_CUTS):
        ref[...] = _dot(hb, w16[:, a:b]) + b_ref[:, a:b]
    g_ref, kt_ref = out_refs[-2:]
    g_ref[...] = _dot3(h, wg_ref[...]) + bg_ref[...]
    kt_ref[...] = lax.dot_general(wkt_ref[...], hb, _NT, preferred_element_type=F32) + bkt_ref[...]


def _in_call(layer, x, mod_l, w_in, b_main, b_kt, w_gate, b_gate, group_of_tile):
    n = x.shape[0]
    widths = [b - a for a, b in _ROW_CUTS] + [LANE]
    row = lambda i: (i, 0)
    fixed = lambda i: (0, 0)
    return pl.pallas_call(
        _in_kernel,
        grid=(n // TM_IN,),
        in_specs=[pl.BlockSpec((TM_IN, D_MODEL), row),
                  pl.BlockSpec((1, 6, D_MODEL), lambda i: (group_of_tile(i), 0, 0)),
                  pl.BlockSpec((1, D_MODEL, w_in.shape[-1]), lambda i: (layer, 0, 0), pipeline_mode=pl.Buffered(1)),
                  pl.BlockSpec((1, D_MAIN), fixed),
                  pl.BlockSpec((D_C, TM_IN), fixed),
                  pl.BlockSpec((D_MODEL, LANE), fixed), pl.BlockSpec((1, LANE), fixed)],
        out_specs=[pl.BlockSpec((TM_IN, w), row) for w in widths] + [pl.BlockSpec((D_C, TM_IN), lambda i: (0, i))],
        out_shape=[jax.ShapeDtypeStruct((n, w), F32) for w in widths] + [jax.ShapeDtypeStruct((D_C, n), F32)],
        scratch_shapes=[pltpu.VMEM((D_MODEL, D_MAIN), BF16), pltpu.VMEM((D_C, D_MODEL), BF16)],
        compiler_params=_cp(("arbitrary",)),
        name="in_proj",
    )(x, mod_l, w_in, b_main, b_kt, w_gate, b_gate)


def _lru_variant(L, xa_ref, ya_ref, cw_ref, cb_ref, wg_ref, bg_ref, lam_ref, h0_ref, o_ref, st_ref,
                 s_af, s_bf, s_ab, s_bb):
    nch, nseq, ntile = SEG // CH, SEG // L, L // SUB
    lam = lam_ref[...]
    sp = jnp.maximum(-lam, 0.0) + jnp.log1p(jnp.exp(-jnp.abs(lam)))
    cw = cw_ref[...]
    cb = cb_ref[...]
    row = lax.broadcasted_iota(jnp.int32, (CH, 1), 0)
    sub3 = lax.broadcasted_iota(jnp.int32, (1, SUB, 1), 1)

    def gates_and_tile_scan(c, carry):
        start = pl.multiple_of(c * CH, CH)
        xcat, main = _halo_rows(xa_ref, start, CH)
        tpos = (start + row) & (L - 1)
        xm2 = jnp.where(tpos >= 2, xcat[SUB - 2:SUB - 2 + CH], 0.0)
        xm1 = jnp.where(tpos >= 1, xcat[SUB - 1:SUB - 1 + CH], 0.0)
        xp1 = jnp.where(tpos <= L - 2, xcat[SUB + 1:SUB + 1 + CH], 0.0)
        xc = cw[0:1] * xm2 + cw[1:2] * xm1 + cw[2:3] * main + cw[3:4] * xp1 + cb
        g = _dot(xc.astype(BF16), wg_ref[...]) + bg_ref[...]
        for d, (sa, sb) in enumerate(((s_af, s_bf), (s_ab, s_bb))):
            r = _sigmoid(g[:, 2 * d * D_A:(2 * d + 1) * D_A])
            ig = _sigmoid(g[:, (2 * d + 1) * D_A:(2 * d + 2) * D_A])
            a = jnp.exp(-LRU_C * r * sp[d:d + 1])
            b = jnp.sqrt(1.0 - a * a) * (ig * xc)
            a3, b3 = a.reshape(CH // SUB, SUB, D_A), b.reshape(CH // SUB, SUB, D_A)
            for s in (1, 2, 4):
                shift, keep = (s, sub3 >= s) if d == 0 else (SUB - s, sub3 < SUB - s)
                b3 = a3 * jnp.where(keep, pltpu.roll(b3, shift, 1), 0.0) + b3
                a3 = a3 * jnp.where(keep, pltpu.roll(a3, shift, 1), 1.0)
            sa[pl.ds(start, CH), :] = a3.reshape(CH, D_A)
            sb[pl.ds(start, CH), :] = b3.reshape(CH, D_A)
        return carry

    lax.fori_loop(0, nch, gates_and_tile_scan, 0)

    def carry_tiles(k, carry):
        cf, cbk = carry
        nf, nb = [], []
        for s in range(nseq):
            rf = pl.multiple_of(s * L + k * SUB, SUB)
            hf = s_af[pl.ds(rf, SUB), :] * cf[s] + s_bf[pl.ds(rf, SUB), :]
            s_bf[pl.ds(rf, SUB), :] = hf
            nf.append(hf[SUB - 1:SUB, :])
            rb = pl.multiple_of(s * L + (ntile - 1 - k) * SUB, SUB)
            hb = s_ab[pl.ds(rb, SUB), :] * cbk[s] + s_bb[pl.ds(rb, SUB), :]
            s_bb[pl.ds(rb, SUB), :] = hb
            nb.append(hb[0:1, :])
        return tuple(nf), tuple(nb)

    cf0 = tuple(h0_ref[0, s, 0:1, :] for s in range(nseq))
    cb0 = tuple(h0_ref[0, s, 1:2, :] for s in range(nseq))
    cf, cbk = lax.fori_loop(0, ntile, carry_tiles, (cf0, cb0))

    st_ref[...] = jnp.zeros(st_ref.shape, F32)
    for s in range(nseq):
        st_ref[0, s] = jnp.concatenate([cf[s], cbk[s]], axis=0)

    def finish(c, carry):
        start = pl.multiple_of(c * CH, CH)
        h = s_bf[pl.ds(start, CH), :] + s_bb[pl.ds(start, CH), :]
        o_ref[pl.ds(start, CH), :] = _rms(_gelu_tanh(ya_ref[pl.ds(start, CH), :]) * h)
        return carry

    lax.fori_loop(0, nch, finish, 0)


def _lru_kernel(*refs, l_ctx, l_lat, n_ctx_blk):
    i = pl.program_id(0)

    @pl.when(i < n_ctx_blk)
    def _():
        _lru_variant(l_ctx, *refs)

    @pl.when(i >= n_ctx_blk)
    def _():
        _lru_variant(l_lat, *refs)


def _lru_call(xa, ya, conv_w, conv_b, w_gate, b_gate, lam, h0_all, l_ctx, l_lat, n_ctx_blk):
    n = xa.shape[0]
    nblk = n // SEG
    row = lambda i: (i, 0)
    fixed = lambda i: (0, 0)
    slots = SEG // l_ctx
    return pl.pallas_call(
        functools.partial(_lru_kernel, l_ctx=l_ctx, l_lat=l_lat, n_ctx_blk=n_ctx_blk),
        grid=(nblk,),
        in_specs=[pl.BlockSpec((SEG, D_A), row), pl.BlockSpec((SEG, D_A), row),
                  pl.BlockSpec((4, D_A), fixed), pl.BlockSpec((1, D_A), fixed),
                  pl.BlockSpec((D_A, 4 * D_A), fixed), pl.BlockSpec((1, 4 * D_A), fixed),
                  pl.BlockSpec((2, D_A), fixed),
                  pl.BlockSpec((1, slots, 2, D_A), lambda i: (i, 0, 0, 0))],
        out_specs=[pl.BlockSpec((SEG, D_A), row),
                   pl.BlockSpec((1, slots, 2, D_A), lambda i: (i, 0, 0, 0))],
        out_shape=[jax.ShapeDtypeStruct((n, D_A), F32),
                   jax.ShapeDtypeStruct((nblk, slots, 2, D_A), F32)],
        scratch_shapes=[pltpu.VMEM((SEG, D_A), F32) for _ in range(4)],
        compiler_params=_cp(("parallel",)),
        name="rglru",
    )(xa, ya, conv_w, conv_b, w_gate, b_gate, lam, h0_all)


def _filt_kernel(z_ref, dec_ref, w1_ref, b1_ref, w2_ref, b2_ref, fr_ref, w3_ref, fwd_ref,
                 oab_ref, od0_ref, s_k, s_kf, *, L):
    nblk = 2 * L // CH
    d_idx = pl.program_id(1)
    row = lax.broadcasted_iota(jnp.int32, (CH, 1), 0)

    @pl.when(d_idx == 0)
    def _():
        fr = fr_ref[0]

        def taps(c, carry):
            start = pl.multiple_of(c * CH, CH)
            h1 = jnp.sin(fr * (_dot3(z_ref[pl.ds(start, CH), :], w1_ref[0]) + b1_ref[0]))
            h2 = jnp.sin(fr * (_dot3(h1, w2_ref[0]) + b2_ref[0]))
            t = _dot3(h2, w3_ref[0])
            dec = dec_ref[pl.ds(start, CH), :]
            rg = start + row
            for o in range(HY_ORDER):
                fwd_t = t[:, (2 * o) * D_B:(2 * o + 1) * D_B]
                bwd_t = t[:, (2 * o + 1) * D_B:(2 * o + 2) * D_B]
                ko = jnp.where(rg < L, bwd_t, fwd_t) * dec
                s_k[pl.ds(start, CH), o * D_B:(o + 1) * D_B] = jnp.where(rg == 0, 0.0, ko)
            return carry

        lax.fori_loop(0, nblk, taps, 0)
        fwd = fwd_ref[...]

        def spectra(e, carry):
            start = pl.multiple_of(e * CH, CH)
            s_kf[e] = _dot3(fwd, s_k[pl.ds(start, CH), :])
            return carry

        lax.fori_loop(0, nblk, spectra, 0)

    kd = s_kf[d_idx + 1]
    km = s_kf[d_idx]
    k0 = s_k[pl.ds(pl.multiple_of(d_idx * CH, CH), 1), :]
    sgn = jnp.where((row & 1) == 0, 1.0, -1.0)
    a = kd[:CH] + sgn * (km[:CH] - k0)
    b = jnp.where(row == 0, 0.0, kd[CH:] + sgn * km[CH:])
    hn = kd[CH:CH + 1] + km[CH:CH + 1] - k0
    for o in range(HY_ORDER):
        oab_ref[0, o, 0, 0] = a[:, o * D_B:(o + 1) * D_B]
        oab_ref[0, o, 0, 1] = b[:, o * D_B:(o + 1) * D_B]
        od0_ref[0, o, 0] = jnp.broadcast_to(hn[:, o * D_B:(o + 1) * D_B], (SUB, D_B))


def _filt_call(L, z, dec, w1, b1, w2, b2, fr, w3, fwd32):
    nd = 2 * (L // CH) - 1
    fixed = lambda l, d: (0, 0)
    lay3 = lambda l, d: (l, 0, 0)
    return pl.pallas_call(
        functools.partial(_filt_kernel, L=L),
        grid=(DEPTH, nd),
        in_specs=[pl.BlockSpec((2 * L, LANE), fixed), pl.BlockSpec((2 * L, D_B), fixed),
                  pl.BlockSpec((1, LANE, LANE), lay3), pl.BlockSpec((1, 1, LANE), lay3),
                  pl.BlockSpec((1, LANE, LANE), lay3), pl.BlockSpec((1, 1, LANE), lay3),
                  pl.BlockSpec((1, 1, LANE), lay3),
                  pl.BlockSpec((1, LANE, HY_ORDER * 2 * D_B), lay3),
                  pl.BlockSpec((2 * CH, CH), fixed)],
        out_specs=[pl.BlockSpec((1, HY_ORDER, 1, 2, CH, D_B), lambda l, d: (l, 0, d, 0, 0, 0)),
                   pl.BlockSpec((1, HY_ORDER, 1, SUB, D_B), lambda l, d: (l, 0, d, 0, 0))],
        out_shape=[jax.ShapeDtypeStruct((DEPTH, HY_ORDER, nd, 2, CH, D_B), F32),
                   jax.ShapeDtypeStruct((DEPTH, HY_ORDER, nd, SUB, D_B), F32)],
        scratch_shapes=[pltpu.VMEM((2 * L, HY_ORDER * D_B), F32),
                        pltpu.VMEM((2 * L // CH, 2 * CH, HY_ORDER * D_B), F32)],
        compiler_params=_cp(("parallel", "arbitrary")),
        name=f"hyena_filter_{L}",
    )(z, dec, w1, b1, w2, b2, fr, w3, fwd32)


def _hy_variant(L, o_idx, hy_ref, cw_ref, cb_ref, fwd_ref, inv_ref, hab_ref, hd0_ref, bias_ref, o_ref,
                s_y, s_x, s_u, s_v):
    nch, nseq, P = SEG // CH, SEG // L, L // CH
    row = lax.broadcasted_iota(jnp.int32, (CH, 1), 0)
    frow = lax.broadcasted_iota(jnp.int32, (FS, 1), 0)

    @pl.when(o_idx == 0)
    def _():
        cw = cw_ref[...]
        cb = cb_ref[...]

        def short_conv(c, carry):
            start = pl.multiple_of(c * CH, CH)
            xcat, main = _halo_rows(hy_ref, start, CH)
            tpos = (start + row) & (L - 1)
            xm1 = jnp.where(tpos >= 1, xcat[SUB - 1:SUB - 1 + CH], 0.0)
            xp1 = jnp.where(tpos <= L - 2, xcat[SUB + 1:SUB + 1 + CH], 0.0)
            hc = cw[0:1] * xm1 + cw[1:2] * main + cw[2:3] * xp1 + cb
            s_y[pl.ds(start, CH), :] = hc[:, :D_B]
            s_x[0, pl.ds(start, CH), :] = hc[:, D_B:2 * D_B]
            s_x[1, pl.ds(start, CH), :] = hc[:, 2 * D_B:]
            return carry

        lax.fori_loop(0, nch, short_conv, 0)

    bias = bias_ref[0]

    def loop(n, body, init):
        return body(0, init) if n == 1 else lax.fori_loop(0, n, body, init)

    def one_sequence(s, slot):
        base = s * L
        u0 = slot * P

        def forward_dft(j, cc):
            r = pl.multiple_of(base + j * CH, CH)
            s_u[u0 + j] = _dot(fwd_ref[...], s_y[pl.ds(r, CH), :].astype(BF16))
            return cc

        loop(P, forward_dft, 0)

        def output_block(i, cc):
            for fs in range(CH // FS):
                lo = fs * FS

                def accumulate(j, acc):
                    yre, yim = acc
                    d = i - j + (P - 1)
                    ure = s_u[u0 + j, lo:lo + FS, :]
                    uim = s_u[u0 + j, CH + lo:CH + lo + FS, :]
                    a = hab_ref[0, 0, d, 0, lo:lo + FS, :]
                    b = hab_ref[0, 0, d, 1, lo:lo + FS, :]
                    dd = jnp.where(frow == 0, hd0_ref[0, 0, d, 0:1, :], a) if fs == 0 else a
                    return yre + ure * a - uim * b, yim + ure * b + uim * dd

                zero = jnp.zeros((FS, D_B), F32)
                yre, yim = loop(P, accumulate, (zero, zero))
                s_v[slot, lo:lo + FS, :] = yre.astype(BF16)
                s_v[slot, CH + lo:CH + lo + FS, :] = yim.astype(BF16)
            yc = _dot(inv_ref[...], s_v[slot])
            r = pl.multiple_of(base + i * CH, CH)
            s_y[pl.ds(r, CH), :] = s_x[o_idx, pl.ds(r, CH), :] * (yc + s_y[pl.ds(r, CH), :] * bias)
            return cc

        loop(P, output_block, 0)

    def single_block_pair(p, carry):
        rows = [pl.multiple_of((2 * p + k) * L, CH) for k in range(2)]
        ys = [s_y[pl.ds(r, CH), :] for r in rows]
        gates = [s_x[o_idx, pl.ds(r, CH), :] for r in rows]
        a = hab_ref[0, 0, 0, 0]
        b = hab_ref[0, 0, 0, 1]
        dd = jnp.where(row == 0, hd0_ref[0, 0, 0, 0:1, :], a)
        outs = []
        for y, gate in zip(ys, gates):
            u = _dot(fwd_ref[...], y.astype(BF16))
            ure, uim = u[:CH], u[CH:]
            v = jnp.concatenate([ure * a - uim * b, ure * b + uim * dd], axis=0).astype(BF16)
            outs.append(gate * (_dot(inv_ref[...], v) + y * bias))
        for r, out in zip(rows, outs):
            s_y[pl.ds(r, CH), :] = out
        return carry

    if P == 1 and nseq % 2 == 0:
        lax.fori_loop(0, nseq // 2, single_block_pair, 0)
    else:
        lax.fori_loop(0, nseq, lambda s, carry: (one_sequence(s, 0), carry)[1], 0)

    @pl.when(o_idx == HY_ORDER - 1)
    def _():
        def finish(c, carry):
            start = pl.multiple_of(c * CH, CH)
            o_ref[pl.ds(start, CH), :] = _rms(s_y[pl.ds(start, CH), :])
            return carry

        lax.fori_loop(0, nch, finish, 0)


def _hy_kernel(hy_ref, cw_ref, cb_ref, fwd_ref, inv_ref, habc_ref, hd0c_ref, habl_ref, hd0l_ref, bias_ref,
               o_ref, s_y, s_x, s_u, s_v, *, l_ctx, l_lat, n_ctx_blk):
    i = pl.program_id(0)
    o_idx = pl.program_id(1)
    scratch = (s_y, s_x, s_u, s_v)

    @pl.when(i < n_ctx_blk)
    def _():
        _hy_variant(l_ctx, o_idx, hy_ref, cw_ref, cb_ref, fwd_ref, inv_ref, habc_ref, hd0c_ref, bias_ref, o_ref,
                    *scratch)

    @pl.when(i >= n_ctx_blk)
    def _():
        _hy_variant(l_lat, o_idx, hy_ref, cw_ref, cb_ref, fwd_ref, inv_ref, habl_ref, hd0l_ref, bias_ref, o_ref,
                    *scratch)


def _hy_call(layer, hyb, conv_w, conv_b, fwd, inv, habc, hd0c, habl, hd0l, bias, l_ctx, l_lat, n_ctx_blk):
    n = hyb.shape[0]
    ndc, ndl = habc.shape[2], habl.shape[2]
    pmax = max(l_ctx, l_lat) // CH
    row = lambda i, o: (i, 0)
    fixed = lambda i, o: (0, 0)
    lat_o = lambda i, o: jnp.where(i >= n_ctx_blk, o, 0)
    ctx_o = lambda i, o: jnp.where(i < n_ctx_blk, o, 0)
    return pl.pallas_call(
        functools.partial(_hy_kernel, l_ctx=l_ctx, l_lat=l_lat, n_ctx_blk=n_ctx_blk),
        grid=(n // SEG, HY_ORDER),
        in_specs=[pl.BlockSpec((SEG, 3 * D_B), row),
                  pl.BlockSpec((3, 3 * D_B), fixed), pl.BlockSpec((1, 3 * D_B), fixed),
                  pl.BlockSpec((2 * CH, CH), fixed), pl.BlockSpec((CH, 2 * CH), fixed),
                  pl.BlockSpec((1, 1, ndc, 2, CH, D_B), lambda i, o: (layer, ctx_o(i, o), 0, 0, 0, 0)),
                  pl.BlockSpec((1, 1, ndc, SUB, D_B), lambda i, o: (layer, ctx_o(i, o), 0, 0, 0)),
                  pl.BlockSpec((1, 1, ndl, 2, CH, D_B), lambda i, o: (layer, lat_o(i, o), 0, 0, 0, 0)),
                  pl.BlockSpec((1, 1, ndl, SUB, D_B), lambda i, o: (layer, lat_o(i, o), 0, 0, 0)),
                  pl.BlockSpec((1, 1, D_B), lambda i, o: (o, 0, 0))],
        out_specs=pl.BlockSpec((SEG, D_B), row),
        out_shape=jax.ShapeDtypeStruct((n, D_B), F32),
        scratch_shapes=[pltpu.VMEM((SEG, D_B), F32), pltpu.VMEM((HY_ORDER, SEG, D_B), F32),
                        pltpu.VMEM((max(pmax, 2), 2 * CH, D_B), F32), pltpu.VMEM((2, 2 * CH, D_B), BF16)],
        compiler_params=_cp(("parallel", "arbitrary")),
        name="hyena",
    )(hyb, conv_w, conv_b, fwd, inv, habc, hd0c, habl, hd0l, bias)


def _row_scan(x, op, fill, reverse):
    t, width = x.shape
    n_tiles = t // SUB
    sub = lax.broadcasted_iota(jnp.int32, (1, SUB, 1), 1)
    x3 = x.reshape(n_tiles, SUB, width)
    for s in (1, 2, 4):
        shift, keep = (SUB - s, sub < SUB - s) if reverse else (s, sub >= s)
        x3 = op(x3, jnp.where(keep, pltpu.roll(x3, shift, 1), fill))
    x = x3.reshape(t, width)
    out = [None] * n_tiles
    carry = None
    for i in (reversed(range(n_tiles)) if reverse else range(n_tiles)):
        tile = x[i * SUB:(i + 1) * SUB]
        out[i] = tile if carry is None else op(tile, carry)
        carry = out[i][0:1] if reverse else out[i][SUB - 1:SUB]
    return jnp.concatenate(out, axis=0)


_STK_ONE = 3 * SUB


def _mlstm_prep(d, g_ref, m_old):
    T = CH
    reverse = d == 1
    g = g_ref[...]
    if d == 1:
        g = pltpu.roll(g, LANE - 2 * H_C, 1)
    lane = lax.broadcasted_iota(jnp.int32, (1, LANE), 1)
    head = lane < H_C
    b = pltpu.roll(_row_scan(_log_sigmoid(g), jnp.add, 0.0, reverse), LANE - H_C, 1)
    r = jnp.where(head, g - b, 0.0)
    big_m = jnp.maximum(m_old, _row_scan(r, jnp.maximum, -jnp.inf, reverse))
    last = 0 if reverse else T - 1
    m_last = big_m[last:last + 1, :]
    low = lane < SUB
    p0, p1, p2 = (jnp.where(low, p.astype(F32), 0.0) for p in _split3(-big_m))
    cols = (p0 + pltpu.roll(p1, SUB, 1) + pltpu.roll(p2, 2 * SUB, 1)
            + jnp.where(jnp.logical_and(lane >= _STK_ONE, lane < _STK_ONE + SUB), 1.0, 0.0))
    rowid = lax.broadcasted_iota(jnp.int32, (SUB, 1), 0)
    r8 = r.T[0:SUB, :]
    m_last8 = sum(jnp.where(rowid == h, m_last[:, h:h + 1], 0.0) for h in range(H_C))
    ws8 = jnp.where(rowid < H_C, jnp.exp(r8 - m_last8), 0.0)
    return {"r3": [p.astype(F32) for p in _split3(r8)], "ws8": ws8, "wc": jnp.exp(m_old - m_last),
            "m_new": jnp.where(head, b[last:last + 1, :] + m_last, 0.0),
            "wi": jnp.exp(m_old - big_m), "e": jnp.exp(-(b + big_m)), "cols_b": cols.astype(BF16)}


def _mlstm_variant(carry, qf, vf, ktf, gf, qb_, vb_, ktb_, gb_, hf_ref, hb_ref, co_ref, no_ref, mo_ref,
                   s_cx, s_n, s_m):
    T = CH
    ii = lax.broadcasted_iota(jnp.int32, (T, T), 0)
    jj = lax.broadcasted_iota(jnp.int32, (T, T), 1)
    rowid = lax.broadcasted_iota(jnp.int32, (SUB, 1), 0)
    one_col = jnp.where(lax.broadcasted_iota(jnp.int32, (T, DK), 1) == 0, 1.0, 0.0).astype(BF16)
    prep = []
    for d, g_ref in ((0, gf), (1, gb_ if carry else gf)):
        m_old = s_m[d:d + 1, :] if carry else jnp.zeros((1, LANE), F32)
        prep.append(_mlstm_prep(d, g_ref, m_old))
    refs = ((qf, vf, ktf, hf_ref), (qb_, vb_, ktb_, hb_ref))
    for h in range(H_C):
        sl = slice(h * DK, (h + 1) * DK)
        for d in range(2):
            p = prep[d]
            q_ref, v_ref, kt_ref, h_ref = refs[d]
            idx = d * H_C + h
            if carry or d == 0:
                qb = (q_ref[:, sl] * (DK ** -0.5)).astype(BF16)
                kt = kt_ref[sl, :]
                ktb = kt.astype(BF16)
                v_ext = jnp.concatenate([v_ref[:, sl].astype(BF16), one_col], axis=1)
                s_raw = _dot(qb, ktb)
            tri = (jj >= ii) if d == 1 else (jj <= ii)
            sel = jnp.broadcast_to(jnp.where(rowid == h, 1.0, 0.0), (SUB, T))
            rr = sum(jnp.where(rowid == i, piece[h:h + 1, :], 0.0) for i, piece in enumerate(p["r3"]))
            rmat = jnp.concatenate([sel, sel, sel, rr, jnp.zeros((LANE - 4 * SUB, T), F32)], axis=0)
            expo = _dot(p["cols_b"], rmat.astype(BF16))
            s = s_raw * jnp.exp(jnp.where(tri, expo, -jnp.inf))
            intra = _dot(s.astype(BF16), v_ext)
            num, den = intra[:, :DK], intra[:, DK:DK + 1]
            if carry:
                cx = s_cx[idx]
                inter = _dot(qb, cx.astype(BF16))
                wi = p["wi"][:, h:h + 1]
                num, den = num + wi * inter[:, :DK], den + wi * inter[:, DK:DK + 1]
            h_ref[:, sl] = num / jnp.maximum(jnp.abs(den), p["e"][:, h:h + 1])
            upd = _dot((kt * p["ws8"][h:h + 1, :]).astype(BF16), v_ext)
            n_upd = lax.dot_general(p["ws8"].astype(BF16), ktb, _NT, preferred_element_type=F32)[h:h + 1, :]
            if carry:
                wc = p["wc"][:, h:h + 1]
                s_cx[idx] = wc * cx + upd
                s_n[idx:idx + 1, :] = wc * s_n[idx:idx + 1, :] + n_upd
            else:
                co_ref[0, idx] = upd[:, :DK]
                no_ref[0, idx:idx + 1, :] = n_upd
    m_rows = jnp.concatenate([prep[0]["m_new"], prep[1]["m_new"], jnp.zeros((SUB - 2, LANE), F32)], axis=0)
    if carry:
        s_m[...] = m_rows
    else:
        mo_ref[0] = m_rows


def _mlstm_kernel(*refs, n_ctx_steps, nc_lat):
    cx0_ref, n0_ref, m0_ref = refs[8:11]
    s_cx, s_n, s_m = refs[-3:]
    data = refs[:8] + refs[11:]
    t = pl.program_id(0)
    is_ctx = t < n_ctx_steps

    @pl.when(is_ctx)
    def _():
        _mlstm_variant(False, *data)

    @pl.when(jnp.logical_not(is_ctx))
    def _():
        @pl.when((t - n_ctx_steps) % nc_lat == 0)
        def _():
            s_cx[...] = cx0_ref[0]
            s_n[...] = n0_ref[0]
            s_m[...] = m0_ref[0]

        _mlstm_variant(True, *data)


def _mlstm_call(q, v, kt, gates, cx0, n0, m0, n_ctx_steps, nc_lat):
    n = q.shape[0]
    steps = n // CH
    nst = 2 * H_C

    def bwd_blk(t):
        r = jnp.maximum(t - n_ctx_steps, 0)
        return n_ctx_steps + (r // nc_lat) * nc_lat + (nc_lat - 1 - r % nc_lat)

    out_bwd = lambda t: jnp.where(t < n_ctx_steps, t, bwd_blk(t))
    lat_b = lambda t: jnp.maximum(t - n_ctx_steps, 0) // nc_lat
    ctx_b = lambda t: jnp.minimum(t, n_ctx_steps - 1)
    rows = lambda w, blk: pl.BlockSpec((CH, w), lambda t: (blk(t), 0))
    cols = lambda h, blk: pl.BlockSpec((h, CH), lambda t: (0, blk(t)))
    ident = lambda t: t
    return pl.pallas_call(
        functools.partial(_mlstm_kernel, n_ctx_steps=n_ctx_steps, nc_lat=nc_lat),
        grid=(steps,),
        in_specs=[rows(D_C, ident), rows(D_C, ident), cols(D_C, ident), rows(LANE, ident),
                  rows(D_C, bwd_blk), rows(D_C, bwd_blk), cols(D_C, bwd_blk), rows(LANE, bwd_blk),
                  pl.BlockSpec((1, nst, DK, 2 * DK), lambda t: (lat_b(t), 0, 0, 0)),
                  pl.BlockSpec((1, nst, DK), lambda t: (lat_b(t), 0, 0)),
                  pl.BlockSpec((1, SUB, LANE), lambda t: (lat_b(t), 0, 0))],
        out_specs=[rows(D_C, ident), rows(D_C, out_bwd),
                   pl.BlockSpec((1, nst, DK, DK), lambda t: (ctx_b(t), 0, 0, 0)),
                   pl.BlockSpec((1, nst, DK), lambda t: (ctx_b(t), 0, 0)),
                   pl.BlockSpec((1, SUB, LANE), lambda t: (ctx_b(t), 0, 0))],
        out_shape=[jax.ShapeDtypeStruct((n, D_C), F32), jax.ShapeDtypeStruct((n, D_C), F32),
                   jax.ShapeDtypeStruct((n_ctx_steps, nst, DK, DK), F32),
                   jax.ShapeDtypeStruct((n_ctx_steps, nst, DK), F32),
                   jax.ShapeDtypeStruct((n_ctx_steps, SUB, LANE), F32)],
        scratch_shapes=[pltpu.VMEM((nst, DK, 2 * DK), F32), pltpu.VMEM((nst, DK), F32),
                        pltpu.VMEM((SUB, LANE), F32)],
        compiler_params=_cp(("arbitrary",)),
        name="mlstm",
    )(q, v, kt, gates, q, v, kt, gates, cx0, n0, m0)


def _out_kernel(x_ref, oa_ref, ob_ref, hf_ref, hb_ref, og_ref, mod_ref, mg_ref, w_ref, g_ref, b_ref,
                rw_ref, rb_ref, x1_ref, he_ref, cnt_ref, it_ref, s_cnt):
    i = pl.program_id(0)

    @pl.when(i == 0)
    def _():
        s_cnt[...] = jnp.zeros(s_cnt.shape, F32)

    m = mod_ref[0]
    mg = mg_ref[...]
    acc = _dot((oa_ref[...] * mg[:, :D_A]).astype(BF16), w_ref[0:D_A, :])
    acc += _dot((ob_ref[...] * mg[:, D_A:D_A + D_B]).astype(BF16), w_ref[D_A:D_A + D_B, :])
    hc = hf_ref[...] + hb_ref[...]
    og = og_ref[...]
    off = D_A + D_B
    for h in range(H_C):
        sl = slice(h * DK, (h + 1) * DK)
        oc = _sigmoid(og[:, sl]) * _rms(hc[:, sl]) * mg[:, off + h * DK:off + (h + 1) * DK]
        acc += _dot(oc.astype(BF16), w_ref[off + h * DK:off + (h + 1) * DK, :])
    x1 = _ln_plain(ALPHA * x_ref[...] + m[2:3] * acc) * g_ref[...] + b_ref[...]
    x1_ref[...] = x1
    h2 = x1 * (1.0 + m[4:5]) + m[3:4]
    he_ref[:, :D_MODEL] = h2

    lg = _dot3(h2, rw_ref[...]) + rb_ref[...]
    col = lax.broadcasted_iota(jnp.int32, lg.shape, 1)
    ninf = -jnp.inf
    lgm = jnp.where(col < N_GROUPS, lg, ninf)
    mx = jnp.max(lgm, -1, keepdims=True)
    gi = jnp.min(jnp.where(lgm == mx, col, LANE), -1, keepdims=True)
    pg_top = 1.0 / jnp.sum(jnp.where(col < N_GROUPS, jnp.exp(lg - mx), 0.0), -1, keepdims=True)
    lo4 = N_GROUPS + E_PER_GROUP * gi
    lem = jnp.where(jnp.logical_and(col >= lo4, col < lo4 + E_PER_GROUP), lg, ninf)
    v1 = jnp.max(lem, -1, keepdims=True)
    i1 = jnp.min(jnp.where(lem == v1, col, LANE), -1, keepdims=True)
    lem2 = jnp.where(col == i1, ninf, lem)
    v2 = jnp.max(lem2, -1, keepdims=True)
    i2 = jnp.min(jnp.where(lem2 == v2, col, LANE), -1, keepdims=True)
    e21 = jnp.exp(v2 - v1)
    w1 = pg_top / (1.0 + e21)
    w2 = pg_top * e21 / (1.0 + e21)
    e1, e2 = i1 - N_GROUPS, i2 - N_GROUPS
    first_lo = e1 < e2
    elo, ehi = jnp.minimum(e1, e2), jnp.maximum(e1, e2)
    wlo, whi = jnp.where(first_lo, w1, w2), jnp.where(first_lo, w2, w1)
    llo, lhi = elo - E_PER_GROUP * gi, ehi - E_PER_GROUP * gi
    cls = gi * N_PAIRS + ((llo * (7 - llo)) >> 1) + lhi - llo - 1

    oh = jnp.where(col == cls, 1.0, 0.0)
    ii = lax.broadcasted_iota(jnp.int32, (TM, TM), 0)
    jj = lax.broadcasted_iota(jnp.int32, (TM, TM), 1)
    before = jnp.where(jj < ii, 1.0, 0.0).astype(BF16)
    cnt = s_cnt[0:1, :]
    rank = jnp.sum(oh * (_dot(before, oh.astype(BF16)) + cnt), -1, keepdims=True)
    cnt = cnt + jnp.sum(oh, 0, keepdims=True)
    s_cnt[...] = jnp.broadcast_to(cnt, s_cnt.shape)
    cnt_ref[...] = jnp.broadcast_to(cnt, cnt_ref.shape)

    info = jnp.zeros(lg.shape, F32)
    for c, val in ((I_CLS, cls.astype(F32)), (I_ELO, elo.astype(F32)), (I_EHI, ehi.astype(F32)),
                   (I_WLO, wlo), (I_WHI, whi), (I_RANK, rank)):
        info = jnp.where(col == c, val, info)
    he_ref[:, D_MODEL:] = info
    it_ref[...] = info.T[:SUB, :]


def _out_call(x, out_a, out_b, hcf, hcb, ogate, mod_l, mix_g, w_out, ln_g, ln_b, rt_w, rt_b, group_of_tile):
    n = x.shape[0]
    row = lambda i: (i, 0)
    fixed = lambda i: (0, 0)
    return pl.pallas_call(
        _out_kernel,
        grid=(n // TM,),
        in_specs=[pl.BlockSpec((TM, D_MODEL), row), pl.BlockSpec((TM, D_A), row), pl.BlockSpec((TM, D_B), row),
                  pl.BlockSpec((TM, D_C), row), pl.BlockSpec((TM, D_C), row), pl.BlockSpec((TM, D_C), row),
                  pl.BlockSpec((1, 6, D_MODEL), lambda i: (group_of_tile(i), 0, 0)),
                  pl.BlockSpec((1, D_MODEL), fixed), pl.BlockSpec((D_MODEL, D_MODEL), fixed),
                  pl.BlockSpec((1, D_MODEL), fixed), pl.BlockSpec((1, D_MODEL), fixed),
                  pl.BlockSpec((D_MODEL, LANE), fixed), pl.BlockSpec((1, LANE), fixed)],
        out_specs=[pl.BlockSpec((TM, D_MODEL), row), pl.BlockSpec((TM, D_EXT), row),
                   pl.BlockSpec((SUB, LANE), fixed), pl.BlockSpec((SUB, TM), lambda i: (0, i))],
        out_shape=[jax.ShapeDtypeStruct((n, D_MODEL), F32), jax.ShapeDtypeStruct((n, D_EXT), F32),
                   jax.ShapeDtypeStruct((SUB, LANE), F32), jax.ShapeDtypeStruct((SUB, n), F32)],
        scratch_shapes=[pltpu.VMEM((SUB, LANE), F32)],
        compiler_params=_cp(("arbitrary",)),
        name="out_proj_router",
    )(x, out_a, out_b, hcf, hcb, ogate, mod_l, mix_g, w_out, ln_g, ln_b, rt_w, rt_b)


def _row_copy(src_ref, src_row, dst_ref, dst_row, sem):
    return pltpu.make_async_copy(src_ref.at[pl.ds(src_row, 1), :], dst_ref.at[pl.ds(dst_row, 1), :], sem)


def _sorted_row(cls_ref, rank_ref, start_ref, t):
    return start_ref[cls_ref[t]] + rank_ref[t]


def _scatter_kernel(cls_ref, rank_ref, start_ref, pad_ref, na_ref, x_ref, o_ref, ring, z_ref, sem_z, load_sems,
                    row_sems):
    n_tiles = o_ref.shape[0] // TM

    def zero_tile(row):
        return pltpu.make_async_copy(z_ref, o_ref.at[pl.ds(pl.multiple_of(row, TM), TM), :], sem_z)

    def for_zero_tiles(fn):
        def per_class(c, carry):
            row = pad_ref[c]

            @pl.when(row >= 0)
            def _():
                fn(zero_tile(row))

            return carry

        lax.fori_loop(0, N_CLASS, per_class, 0)

        def per_idle(t, carry):
            fn(zero_tile(t * TM))
            return carry

        lax.fori_loop(na_ref[0], n_tiles, per_idle, 0)

    i = pl.program_id(0)
    n_steps = x_ref.shape[0] // TM

    def load(step):
        return pltpu.make_async_copy(x_ref.at[pl.ds(pl.multiple_of(step * TM, TM), TM), :],
                                     ring.at[step % RING], load_sems.at[step % RING])

    def start_rows(step):
        slot = step % RING

        def body(r, carry):
            dst = _sorted_row(cls_ref, rank_ref, start_ref, step * TM + r)
            _row_copy(ring.at[slot], r, o_ref, dst, row_sems.at[slot]).start()
            return carry

        lax.fori_loop(0, TM, body, 0, unroll=DMA_UNROLL)

    def wait_rows(step):
        slot = step % RING
        pltpu.make_async_copy(ring.at[slot], o_ref.at[pl.ds(0, TM), :], row_sems.at[slot]).wait()

    @pl.when(i == 0)
    def _():
        z_ref[...] = jnp.zeros(z_ref.shape, F32)
        for_zero_tiles(lambda cp: cp.start())
        for_zero_tiles(lambda cp: cp.wait())
        load(0).start()

    @pl.when(i >= RING - 1)
    def _():
        wait_rows(i - (RING - 1))

    @pl.when(i + 1 < n_steps)
    def _():
        load(i + 1).start()

    load(i).wait()
    start_rows(i)

    @pl.when(i == n_steps - 1)
    def _():
        for back in range(min(RING - 1, n_steps) - 1, -1, -1):
            wait_rows(i - back)


def _scatter_call(plan, h_ext, n_tiles_max):
    any_spec = pl.BlockSpec(memory_space=pl.ANY)
    return pl.pallas_call(
        _scatter_kernel,
        grid_spec=pltpu.PrefetchScalarGridSpec(
            num_scalar_prefetch=5, grid=(h_ext.shape[0] // TM,),
            in_specs=[any_spec],
            out_specs=any_spec,
            scratch_shapes=[pltpu.VMEM((RING, TM, D_EXT), F32), pltpu.VMEM((TM, D_EXT), F32),
                            pltpu.SemaphoreType.DMA(()),
                            pltpu.SemaphoreType.DMA((RING,)), pltpu.SemaphoreType.DMA((RING,))]),
        out_shape=jax.ShapeDtypeStruct((n_tiles_max * TM, D_EXT), F32),
        compiler_params=_cp(("arbitrary",)),
        name="moe_scatter",
    )(plan["cls"], plan["rank"], plan["row_start"], plan["pad_rows"], plan["n_act"], h_ext)


def _moe_kernel(tg_ref, lo_ref, hi_ref, cg_ref, nv_ref, na_ref, x_ref, w1_ref, w3_ref, w2_ref, o_ref, s1, s3, s2):
    del tg_ref
    t = pl.program_id(0)
    active = t < na_ref[0]
    half = TM // 2

    @pl.when(jnp.logical_not(active))
    def _():
        o_ref[...] = jnp.zeros(o_ref.shape, F32)

    @pl.when(jnp.logical_and(active, cg_ref[t] == 1))
    def _():
        for e in range(E_PER_GROUP):
            s1[e] = w1_ref[e].astype(BF16)
            s3[e] = w3_ref[e].astype(BF16)
            s2[e] = w2_ref[e].astype(BF16)

    def run(rows):
        xe = x_ref[0:rows, :]
        x = xe[:, :D_MODEL].astype(BF16)

        def expert(e, gate):
            a = _dot(x, s1[e])
            hm = a * _sigmoid(a) * _dot(x, s3[e]) * gate
            return _dot(hm.astype(BF16), s2[e])

        o_ref[0:rows, :] = (expert(lo_ref[t], xe[:, D_MODEL + I_WLO:D_MODEL + I_WLO + 1])
                            + expert(hi_ref[t], xe[:, D_MODEL + I_WHI:D_MODEL + I_WHI + 1]))

    @pl.when(jnp.logical_and(active, nv_ref[t] > half))
    def _():
        run(TM)

    @pl.when(jnp.logical_and(active, nv_ref[t] <= half))
    def _():
        run(half)
        o_ref[half:, :] = jnp.zeros((TM - half, D_MODEL), F32)


def _moe_call(layer, plan, x_sorted, w1, w3, w2):
    r = x_sorted.shape[0]
    act = lambda t, tg, lo, hi, cg, nv, na: (jnp.minimum(t, na[0] - 1), 0)
    grp = lambda t, tg, lo, hi, cg, nv, na: (layer, tg[t], 0, 0)
    up = pl.BlockSpec((None, E_PER_GROUP, D_MODEL, D_E), grp, pipeline_mode=pl.Buffered(1))
    down = pl.BlockSpec((None, E_PER_GROUP, D_E, D_MODEL), grp, pipeline_mode=pl.Buffered(1))
    return pl.pallas_call(
        _moe_kernel,
        grid_spec=pltpu.PrefetchScalarGridSpec(
            num_scalar_prefetch=6, grid=(r // TM,),
            in_specs=[pl.BlockSpec((TM, D_EXT), act), up, up, down],
            out_specs=pl.BlockSpec((TM, D_MODEL), lambda t, *_: (t, 0)),
            scratch_shapes=[pltpu.VMEM((E_PER_GROUP, D_MODEL, D_E), BF16),
                            pltpu.VMEM((E_PER_GROUP, D_MODEL, D_E), BF16),
                            pltpu.VMEM((E_PER_GROUP, D_E, D_MODEL), BF16)]),
        out_shape=jax.ShapeDtypeStruct((r, D_MODEL), F32),
        compiler_params=_cp(("arbitrary",)),
        name="moe_experts",
    )(plan["tile_grp"], plan["tile_lo"], plan["tile_hi"], plan["chg_grp"], plan["valid"], plan["n_act"],
      x_sorted, w1, w3, w2)


def _ln2_kernel(cls_ref, rank_ref, start_ref, x1_ref, mod_ref, g_ref, b_ref, y_ref, *rest, n_ctx_tiles):
    o_refs, (buf, sems) = rest[:-2], rest[-2:]
    i = pl.program_id(0)
    n_steps = pl.num_programs(0)

    def start_rows(step):
        slot = step % 2

        def body(r, carry):
            src = _sorted_row(cls_ref, rank_ref, start_ref, step * TM + r)
            _row_copy(y_ref, src, buf.at[slot], r, sems.at[slot]).start()
            return carry

        lax.fori_loop(0, TM, body, 0, unroll=DMA_UNROLL)

    @pl.when(i == 0)
    def _():
        start_rows(0)

    @pl.when(i + 1 < n_steps)
    def _():
        start_rows(i + 1)

    slot = i % 2
    pltpu.make_async_copy(y_ref.at[pl.ds(0, TM), :], buf.at[slot], sems.at[slot]).wait()
    m = mod_ref[0]
    y = _ln_plain(ALPHA * x1_ref[...] + m[5:6] * buf[slot]) * g_ref[...] + b_ref[...]
    if len(o_refs) == 1:
        o_refs[0][...] = y
    else:
        @pl.when(i < n_ctx_tiles)
        def _():
            o_refs[0][...] = y

        @pl.when(i >= n_ctx_tiles)
        def _():
            o_refs[1][...] = y


def _ln2_call(plan, x1, y_sorted, mod_l, ln_g, ln_b, group_of_tile, n_ctx_tiles, split):
    n = x1.shape[0]
    row = lambda i, *_: (i, 0)
    fixed = lambda i, *_: (0, 0)
    if split:
        n_ctx = n_ctx_tiles * TM
        out_specs = [pl.BlockSpec((TM, D_MODEL), lambda i, *_: (jnp.minimum(i, n_ctx_tiles - 1), 0)),
                     pl.BlockSpec((TM, D_MODEL), lambda i, *_: (jnp.maximum(i - n_ctx_tiles, 0), 0))]
        out_shape = [jax.ShapeDtypeStruct((n_ctx, D_MODEL), F32), jax.ShapeDtypeStruct((n - n_ctx, D_MODEL), F32)]
    else:
        out_specs = [pl.BlockSpec((TM, D_MODEL), row)]
        out_shape = [jax.ShapeDtypeStruct((n, D_MODEL), F32)]
    return pl.pallas_call(
        functools.partial(_ln2_kernel, n_ctx_tiles=n_ctx_tiles),
        grid_spec=pltpu.PrefetchScalarGridSpec(
            num_scalar_prefetch=3, grid=(n // TM,),
            in_specs=[pl.BlockSpec((TM, D_MODEL), row),
                      pl.BlockSpec((1, 6, D_MODEL), lambda i, *_: (group_of_tile(i), 0, 0)),
                      pl.BlockSpec((1, D_MODEL), fixed), pl.BlockSpec((1, D_MODEL), fixed),
                      pl.BlockSpec(memory_space=pl.ANY)],
            out_specs=out_specs,
            scratch_shapes=[pltpu.VMEM((2, TM, D_MODEL), F32), pltpu.SemaphoreType.DMA((2,))]),
        out_shape=out_shape,
        compiler_params=_cp(("arbitrary",)),
        name="moe_gather_ln2",
    )(plan["cls"], plan["rank"], plan["row_start"], x1, mod_l, ln_g, ln_b, y_sorted)


def _dft_matrices():
    n2 = 2 * CH
    f = np.arange(CH, dtype=np.float64)[:, None]
    t = np.arange(CH, dtype=np.float64)[None, :]
    ang = 2.0 * np.pi * f * t / n2
    re, im = np.cos(ang), -np.sin(ang)
    im[0, :] = np.cos(np.pi * t[0])
    fwd = np.concatenate([re, im], axis=0)
    scale = np.full((CH, 1), 2.0 / n2)
    scale[0, 0] = 1.0 / n2
    inv = np.concatenate([(re * scale).T, (im * scale).T], axis=1)
    return fwd.astype(np.float32), inv.astype(np.float32)


def _filter_features(L):
    lag = np.arange(-L, L)
    m = np.minimum(np.abs(lag), L - 1)
    t = (np.arange(L, dtype=np.float32) / np.float32(max(L - 1, 1)))[m]
    w = (np.float32(2.0 * math.pi) * np.arange(L, dtype=np.float32) / np.float32(L))[m]
    bands = np.linspace(1e-4, HY_BANDS - 1, HY_BANDS, dtype=np.float32)
    z = np.zeros((2 * L, LANE), np.float32)
    z[:, 0] = t
    z[:, 1:1 + HY_BANDS] = np.cos(w[:, None] * bands)
    z[:, 1 + HY_BANDS:HY_EMB] = -np.sin(w[:, None] * bands)
    lo, hi = math.log(HY_DECAY_TARGET) / 1.5, math.log(HY_DECAY_TARGET) / 0.3
    deltas = np.abs(np.linspace(lo, hi, D_B, dtype=np.float32))
    dec = np.exp(-t[:, None] * deltas)
    return z, dec.astype(np.float32)


def _sincos_2d(rows, cols):
    quarter = D_MODEL // 4
    omega = 1.0 / (10000.0 ** (jnp.arange(quarter, dtype=F32) / quarter))

    def emb(n):
        ang = jnp.arange(n, dtype=F32)[:, None] * omega[None]
        return jnp.concatenate([jnp.sin(ang), jnp.cos(ang)], -1)

    er, ec = emb(rows), emb(cols)
    half = D_MODEL // 2
    pos = jnp.concatenate([jnp.broadcast_to(er[:, None], (rows, cols, half)),
                           jnp.broadcast_to(ec[None], (rows, cols, half))], -1)
    return pos.reshape(rows * cols, D_MODEL)


def _pad_to(x, shape):
    return jnp.pad(x, [(0, s - d) for d, s in zip(x.shape, shape)])


def _block_diag(w):
    eye = jnp.eye(H_A, dtype=w.dtype)
    return jnp.einsum("hij,hg->higj", w, eye).reshape(D_A, D_A)


_PAIR_LO = np.array([0, 0, 0, 1, 1, 2], np.int32)
_PAIR_HI = np.array([1, 2, 3, 2, 3, 3], np.int32)


def _routing_plan(info_t, counts, n_tiles_max):
    cnt = counts[0, :N_CLASS].astype(jnp.int32)
    tiles = (cnt + TM - 1) // TM
    tile_end = jnp.cumsum(tiles)
    n_act = tile_end[-1]
    t = jnp.minimum(jnp.arange(n_tiles_max, dtype=jnp.int32), n_act - 1)
    tcls = jnp.minimum(jnp.sum((tile_end[None, :] <= t[:, None]).astype(jnp.int32), 1), N_CLASS - 1)
    grp, pair = (tcls // N_PAIRS).astype(jnp.int32), tcls % N_PAIRS
    valid = jnp.clip(cnt[tcls] - (t - (tile_end - tiles)[tcls]) * TM, 0, TM).astype(jnp.int32)
    first = jnp.ones((1,), jnp.int32)
    changed = lambda e: jnp.concatenate([first, (e[1:] != e[:-1]).astype(jnp.int32)])
    return {"cls": info_t[I_CLS].astype(jnp.int32), "rank": info_t[I_RANK].astype(jnp.int32),
            "row_start": ((tile_end - tiles) * TM).astype(jnp.int32), "tile_grp": grp, "chg_grp": changed(grp),
            "tile_lo": jnp.asarray(_PAIR_LO)[pair], "tile_hi": jnp.asarray(_PAIR_HI)[pair], "valid": valid,
            "n_act": n_act.reshape(1).astype(jnp.int32),
            "pad_rows": jnp.where(tiles > 0, (tile_end - 1) * TM, -1).astype(jnp.int32)}


def kernel(x_prompt, x_sample, c, state_lru, state_mlstm_C, state_mlstm_n, state_mlstm_m, c_ctx, w_ada, b_ada, w_in, b_in, conv_a_w, conv_a_b, lru_wa, lru_ba, lru_wx, lru_bx, lru_lam, conv_b_w, conv_b_b, hy_w1, hy_b1, hy_w2, hy_b2, hy_freq, hy_w3, hy_bias, mix_g, w_out, ln1_g, ln1_b, rt_wg, rt_bg, rt_we, rt_be, moe_w1, moe_w3, moe_w2, ln2_g, ln2_b):
    B, l_ctx, D = x_prompt.shape
    b_lat, l_lat, _ = x_sample.shape
    n_ctx, n_lat = B * l_ctx, b_lat * l_lat
    n = n_ctx + n_lat
    assert D == D_MODEL and w_in.shape[-1] == D_MAIN + N_GATE
    assert SEG % l_ctx == 0 and l_lat == SEG and l_ctx % CH == 0 and n_ctx % SEG == 0
    assert l_ctx == CH, "the mLSTM step schedule assumes one chunk per context sequence"
    assert 1 + b_lat <= SUB
    n_ctx_blk = n_ctx // SEG
    n_ctx_tiles = n_ctx // TM
    tiles_per_lat = l_lat // TM
    nc_lat = l_lat // CH

    def group_of(tile_rows):
        first_lat, per_seq = n_ctx // tile_rows, l_lat // tile_rows
        return lambda i: jnp.where(i < first_lat, 0, 1 + (i - first_lat) // per_seq)

    group_of_tile = group_of(TM)

    cond = jnp.concatenate([c_ctx[None], c, jnp.zeros((SUB - 1 - b_lat, D), F32)], 0)
    mod = _mod_call(cond, w_ada, b_ada).reshape(DEPTH, SUB, 6, D)
    pos = _sincos_2d(l_lat // GRID_W, GRID_W)
    x = _entry_call(x_prompt.reshape(n_ctx, D), x_sample.reshape(n_lat, D), pos)

    fwd_np, inv_np = _dft_matrices()
    fwd32 = jnp.asarray(fwd_np)
    fwd16, inv16 = fwd32.astype(BF16), jnp.asarray(inv_np).astype(BF16)
    fw1 = _pad_to(hy_w1, (DEPTH, LANE, LANE))
    fb1 = _pad_to(hy_b1[:, None, :], (DEPTH, 1, LANE))
    fw2 = _pad_to(hy_w2, (DEPTH, LANE, LANE))
    fb2 = _pad_to(hy_b2[:, None, :], (DEPTH, 1, LANE))
    ffr = _pad_to(hy_freq[:, None, :], (DEPTH, 1, LANE))
    fw3 = _pad_to(hy_w3, (DEPTH, LANE, HY_ORDER * 2 * D_B))
    spectra = {}
    for L in (l_ctx, l_lat):
        z_np, dec_np = _filter_features(L)
        spectra[L] = _filt_call(L, jnp.asarray(z_np), jnp.asarray(dec_np), fw1, fb1, fw2, fb2, ffr, fw3, fwd32)

    lat_slots = SEG // l_ctx
    st_lru, st_c, st_n, st_m = [], [], [], []
    for l in range(DEPTH):
        b_main = b_in[l, None, :D_MAIN]
        k_lo, k_hi = _Z_CUTS[_K_CUT], _Z_CUTS[_K_CUT + 1]
        b_kt = jnp.broadcast_to(b_in[l, k_lo:k_hi, None], (D_C, TM_IN))
        w_gate = _pad_to(w_in[l, :, D_MAIN:], (D, LANE))
        b_gate = _pad_to(b_in[l, None, D_MAIN:], (1, LANE))
        xa, ya, hyb, q, v, og, gates, kt = _in_call(l, x, mod[l], w_in, b_main, b_kt, w_gate, b_gate,
                                                    group_of(TM_IN))

        lru_w = jnp.concatenate([_block_diag(lru_wa[l, 0]), _block_diag(lru_wx[l, 0]),
                                 _block_diag(lru_wa[l, 1]), _block_diag(lru_wx[l, 1])], 1).astype(BF16)
        lru_b = jnp.concatenate([lru_ba[l, 0], lru_bx[l, 0], lru_ba[l, 1], lru_bx[l, 1]])[None]
        h0_lat = _pad_to(state_lru[:, l][:, None], (b_lat, lat_slots, 2, D_A))
        h0_all = jnp.concatenate([jnp.zeros((n_ctx_blk, lat_slots, 2, D_A), F32), h0_lat], 0)
        out_a, lru_last = _lru_call(xa, ya, conv_a_w[l], conv_a_b[l, None], lru_w, lru_b, lru_lam[l], h0_all,
                                    l_ctx, l_lat, n_ctx_blk)

        habc, hd0c = spectra[l_ctx]
        habl, hd0l = spectra[l_lat]
        out_b = _hy_call(l, hyb, conv_b_w[l], conv_b_b[l, None], fwd16, inv16, habc, hd0c, habl, hd0l,
                         hy_bias[l][:, None, :], l_ctx, l_lat, n_ctx_blk)

        n0 = state_mlstm_n[:, l].reshape(b_lat, 2 * H_C, DK)
        cx0 = jnp.concatenate([state_mlstm_C[:, l].reshape(b_lat, 2 * H_C, DK, DK), n0[..., None],
                               jnp.zeros((b_lat, 2 * H_C, DK, DK - 1), F32)], -1)
        m0 = _pad_to(state_mlstm_m[:, l], (b_lat, SUB, LANE))
        hcf, hcb, c_fin, n_fin, m_fin = _mlstm_call(q, v, kt, gates, cx0, n0, m0, n_ctx // CH, nc_lat)

        rt_w = _pad_to(jnp.concatenate([rt_wg[l], rt_we[l]], 1), (D, LANE))
        rt_b = _pad_to(jnp.concatenate([rt_bg[l], rt_be[l]])[None], (1, LANE))
        x1, h_ext, counts, info_t = _out_call(x, out_a, out_b, hcf, hcb, og, mod[l], mix_g[l, None],
                                              w_out[l].astype(BF16), ln1_g[l, None], ln1_b[l, None], rt_w, rt_b,
                                              group_of_tile)

        n_tiles_max = n // TM + N_CLASS
        plan = _routing_plan(info_t, counts, n_tiles_max)
        x_sorted = _scatter_call(plan, h_ext, n_tiles_max)
        y_sorted = _moe_call(l, plan, x_sorted, moe_w1, moe_w3, moe_w2)
        outs = _ln2_call(plan, x1, y_sorted, mod[l], ln2_g[l, None], ln2_b[l, None], group_of_tile,
                         n_ctx_tiles, split=(l == DEPTH - 1))
        x = outs[0]

        st_lru.append(lru_last[:n_ctx_blk].reshape(B, 2, D_A))
        st_c.append(c_fin.reshape(B, 2, H_C, DK, DK))
        st_n.append(n_fin.reshape(B, 2, H_C, DK))
        st_m.append(m_fin[:, :2, :H_C])

    return (outs[0].reshape(B, l_ctx, D), outs[1].reshape(b_lat, l_lat, D),
            jnp.stack(st_lru, 1), jnp.stack(st_c, 1), jnp.stack(st_n, 1), jnp.stack(st_m, 1))
```

```python
import functools
import math

import numpy as np
import jax
import jax.numpy as jnp
from jax import lax
from jax.experimental import pallas as pl
from jax.experimental.pallas import tpu as pltpu

F32 = jnp.float32
BF16 = jnp.bfloat16

D_MODEL = 1024
DEPTH = 2
GRID_W = 64
D_A = 256
H_A = 4
BA = D_A // H_A
LRU_C = 8.0
D_B = 256
HY_ORDER = 2
HY_BANDS = 16
HY_EMB = 1 + 2 * HY_BANDS
HY_FH = 64
HY_DECAY_TARGET = 1e-2
D_C = 512
H_C = 4
DK = D_C // H_C
N_GROUPS = 4
E_PER_GROUP = 4
N_EXP = N_GROUPS * E_PER_GROUP
N_PAIRS = 6
N_CLASS = N_GROUPS * N_PAIRS
D_E = 512
ALPHA = (2 * DEPTH) ** 0.25
EPS = 1e-6
D_MAIN = 2 * D_A + 3 * D_B + 4 * D_C
N_GATE = 4 * H_C

LANE = 128
SUB = 8
VMEM_LIMIT = 56 * 1024 * 1024

CH = 256
SEG = 2048
TM = 256
TM_IN = 512
TM_OUT = 512
D_EXT = D_MODEL + LANE
FS = 64
DMA_UNROLL = 8
RING = 3

I_CLS, I_ELO, I_EHI, I_WLO, I_WHI, I_RANK = range(6)


def _cp(sem, vmem=VMEM_LIMIT):
    return pltpu.CompilerParams(dimension_semantics=sem, vmem_limit_bytes=vmem)


def _dot(a, b):
    return jnp.dot(a, b, preferred_element_type=F32)


def _split2(x):
    hi = x.astype(BF16)
    lo = (x - hi.astype(F32)).astype(BF16)
    return hi, lo


def _dot3(a, b):
    ah, al = _split2(a)
    bh, bl = _split2(b)
    return _dot(ah, bh) + (_dot(ah, bl) + _dot(al, bh))


def _split3(x):
    hi = x.astype(BF16)
    r1 = x - hi.astype(F32)
    mid = r1.astype(BF16)
    lo = (r1 - mid.astype(F32)).astype(BF16)
    return hi, mid, lo


def _sigmoid(x):
    return 1.0 / (1.0 + jnp.exp(-x))


def _log_sigmoid(x):
    return jnp.minimum(x, 0.0) - jnp.log1p(jnp.exp(-jnp.abs(x)))


def _gelu_tanh(x):
    return 0.5 * x * (1.0 + jnp.tanh(math.sqrt(2.0 / math.pi) * (x + 0.044715 * (x * x * x))))


def _ln_plain(x):
    mu = jnp.mean(x, -1, keepdims=True)
    xc = x - mu
    var = jnp.mean(xc * xc, -1, keepdims=True)
    return xc * lax.rsqrt(var + EPS)


def _rms(x):
    return x * lax.rsqrt(jnp.mean(x * x, -1, keepdims=True) + EPS)


def _halo_rows(ref, start, rows):
    total = ref.shape[0]
    prev = ref[pl.ds(pl.multiple_of(jnp.maximum(start - SUB, 0), SUB), SUB), :]
    main = ref[pl.ds(start, rows), :]
    nxt = ref[pl.ds(pl.multiple_of(jnp.minimum(start + rows, total - SUB), SUB), SUB), :]
    return jnp.concatenate([prev, main, nxt], axis=0), main


def _mod_kernel(c_ref, w_ref, b_ref, o_ref):
    c = c_ref[...]
    o_ref[0] = _dot3(c * _sigmoid(c), w_ref[0]) + b_ref[0]


def _mod_call(cond, w_ada, b_ada):
    tn = 1536
    n6 = w_ada.shape[-1]
    return pl.pallas_call(
        _mod_kernel,
        grid=(DEPTH, n6 // tn),
        in_specs=[pl.BlockSpec((SUB, D_MODEL), lambda l, j: (0, 0)),
                  pl.BlockSpec((1, D_MODEL, tn), lambda l, j: (l, 0, j)),
                  pl.BlockSpec((1, 1, tn), lambda l, j: (l, 0, j))],
        out_specs=pl.BlockSpec((1, SUB, tn), lambda l, j: (l, 0, j)),
        out_shape=jax.ShapeDtypeStruct((DEPTH, SUB, n6), F32),
        compiler_params=_cp(("parallel", "parallel")),
        name="adaln_mod",
    )(cond, w_ada, b_ada.reshape(DEPTH, 1, n6))


def _entry_kernel(xc_ref, xl_ref, pos_ref, o_ref, *, n_ctx_tiles):
    i = pl.program_id(0)

    @pl.when(i < n_ctx_tiles)
    def _():
        o_ref[...] = _ln_plain(xc_ref[...])

    @pl.when(i >= n_ctx_tiles)
    def _():
        o_ref[...] = _ln_plain(xl_ref[...] + pos_ref[...])


def _entry_call(xc, xl, pos):
    tm = 512
    n_ctx, n_lat, l_lat = xc.shape[0], xl.shape[0], pos.shape[0]
    nct = n_ctx // tm
    per_seq = l_lat // tm
    return pl.pallas_call(
        functools.partial(_entry_kernel, n_ctx_tiles=nct),
        grid=((n_ctx + n_lat) // tm,),
        in_specs=[pl.BlockSpec((tm, D_MODEL), lambda i: (jnp.minimum(i, nct - 1), 0)),
                  pl.BlockSpec((tm, D_MODEL), lambda i: (jnp.maximum(i - nct, 0), 0)),
                  pl.BlockSpec((tm, D_MODEL), lambda i: (jnp.maximum(i - nct, 0) % per_seq, 0))],
        out_specs=pl.BlockSpec((tm, D_MODEL), lambda i: (i, 0)),
        out_shape=jax.ShapeDtypeStruct((n_ctx + n_lat, D_MODEL), F32),
        compiler_params=_cp(("parallel",)),
        name="entry_ln",
    )(xc, xl, pos)


_Z_CUTS = (0, D_A, 2 * D_A, 2 * D_A + 3 * D_B, 2 * D_A + 3 * D_B + D_C, 2 * D_A + 3 * D_B + 2 * D_C,
           2 * D_A + 3 * D_B + 3 * D_C, D_MAIN)


_K_CUT = 4
_ROW_CUTS = tuple(c for i, c in enumerate(zip(_Z_CUTS[:-1], _Z_CUTS[1:])) if i != _K_CUT)
_NT = (((1,), (1,)), ((), ()))


def _in_kernel(x_ref, mod_ref, wt_ref, b_ref, bkt_ref, wg_ref, bg_ref, *refs):
    out_refs, wt16 = refs[:-1], refs[-1]

    @pl.when(pl.program_id(0) == 0)
    def _():
        wt16[...] = wt_ref[0, :D_MAIN, :].astype(BF16)

    m = mod_ref[0]
    h = x_ref[...] * (1.0 + m[1:2]) + m[0:1]
    hb = h.astype(BF16)
    dg = lambda a, b: lax.dot_general(a, b, _NT, preferred_element_type=F32)
    for ref, (a, b) in zip(out_refs[:-2], _ROW_CUTS):
        ref[...] = dg(hb, wt16[a:b, :]) + b_ref[:, a:b]
    g_ref, kt_ref = out_refs[-2:]
    g_ref[...] = _dot3(h, wg_ref[...]) + bg_ref[...]
    kt_ref[...] = dg(wt16[_Z_CUTS[_K_CUT]:_Z_CUTS[_K_CUT + 1], :], hb) + bkt_ref[...]


def _in_call(layer, x, mod_l, w_in_t, b_main, b_kt, w_gate, b_gate, group_of_tile):
    n = x.shape[0]
    widths = [b - a for a, b in _ROW_CUTS] + [LANE]
    row = lambda i: (i, 0)
    fixed = lambda i: (0, 0)
    return pl.pallas_call(
        _in_kernel,
        grid=(n // TM_IN,),
        in_specs=[pl.BlockSpec((TM_IN, D_MODEL), row),
                  pl.BlockSpec((1, 6, D_MODEL), lambda i: (group_of_tile(i), 0, 0)),
                  pl.BlockSpec((1, w_in_t.shape[1], D_MODEL), lambda i: (layer, 0, 0), pipeline_mode=pl.Buffered(1)),
                  pl.BlockSpec((1, D_MAIN), fixed),
                  pl.BlockSpec((D_C, TM_IN), fixed),
                  pl.BlockSpec((D_MODEL, LANE), fixed), pl.BlockSpec((1, LANE), fixed)],
        out_specs=[pl.BlockSpec((TM_IN, w), row) for w in widths] + [pl.BlockSpec((D_C, TM_IN), lambda i: (0, i))],
        out_shape=[jax.ShapeDtypeStruct((n, w), F32) for w in widths] + [jax.ShapeDtypeStruct((D_C, n), F32)],
        scratch_shapes=[pltpu.VMEM((D_MAIN, D_MODEL), BF16)],
        compiler_params=_cp(("arbitrary",)),
        name="in_proj",
    )(x, mod_l, w_in_t, b_main, b_kt, w_gate, b_gate)


def _lru_variant(L, xa_ref, ya_ref, cw_ref, cb_ref, wg_ref, bg_ref, lam_ref, h0_ref, o_ref, st_ref,
                 s_af, s_bf, s_ab, s_bb):
    nch, nseq, ntile = SEG // CH, SEG // L, L // SUB
    lam = lam_ref[...]
    sp = jnp.maximum(-lam, 0.0) + jnp.log1p(jnp.exp(-jnp.abs(lam)))
    cw = cw_ref[...]
    cb = cb_ref[...]
    row = lax.broadcasted_iota(jnp.int32, (CH, 1), 0)
    sub3 = lax.broadcasted_iota(jnp.int32, (1, SUB, 1), 1)

    def gates_and_tile_scan(c, carry):
        start = pl.multiple_of(c * CH, CH)
        xcat, main = _halo_rows(xa_ref, start, CH)
        tpos = (start + row) & (L - 1)
        xm2 = jnp.where(tpos >= 2, xcat[SUB - 2:SUB - 2 + CH], 0.0)
        xm1 = jnp.where(tpos >= 1, xcat[SUB - 1:SUB - 1 + CH], 0.0)
        xp1 = jnp.where(tpos <= L - 2, xcat[SUB + 1:SUB + 1 + CH], 0.0)
        xc = cw[0:1] * xm2 + cw[1:2] * xm1 + cw[2:3] * main + cw[3:4] * xp1 + cb
        g = _dot(xc.astype(BF16), wg_ref[...]) + bg_ref[...]
        for d, (sa, sb) in enumerate(((s_af, s_bf), (s_ab, s_bb))):
            r = _sigmoid(g[:, 2 * d * D_A:(2 * d + 1) * D_A])
            ig = _sigmoid(g[:, (2 * d + 1) * D_A:(2 * d + 2) * D_A])
            a = jnp.exp(-LRU_C * r * sp[d:d + 1])
            b = jnp.sqrt(1.0 - a * a) * (ig * xc)
            a3, b3 = a.reshape(CH // SUB, SUB, D_A), b.reshape(CH // SUB, SUB, D_A)
            for s in (1, 2, 4):
                shift, keep = (s, sub3 >= s) if d == 0 else (SUB - s, sub3 < SUB - s)
                b3 = a3 * jnp.where(keep, pltpu.roll(b3, shift, 1), 0.0) + b3
                a3 = a3 * jnp.where(keep, pltpu.roll(a3, shift, 1), 1.0)
            sa[pl.ds(start, CH), :] = a3.reshape(CH, D_A)
            sb[pl.ds(start, CH), :] = b3.reshape(CH, D_A)
        return carry

    lax.fori_loop(0, nch, gates_and_tile_scan, 0)

    def carry_tiles(k, carry):
        cf, cbk = carry
        nf, nb = [], []
        for s in range(nseq):
            rf = pl.multiple_of(s * L + k * SUB, SUB)
            hf = s_af[pl.ds(rf, SUB), :] * cf[s] + s_bf[pl.ds(rf, SUB), :]
            s_bf[pl.ds(rf, SUB), :] = hf
            nf.append(hf[SUB - 1:SUB, :])
            rb = pl.multiple_of(s * L + (ntile - 1 - k) * SUB, SUB)
            hb = s_ab[pl.ds(rb, SUB), :] * cbk[s] + s_bb[pl.ds(rb, SUB), :]
            s_bb[pl.ds(rb, SUB), :] = hb
            nb.append(hb[0:1, :])
        return tuple(nf), tuple(nb)

    cf0 = tuple(h0_ref[0, s, 0:1, :] for s in range(nseq))
    cb0 = tuple(h0_ref[0, s, 1:2, :] for s in range(nseq))
    cf, cbk = lax.fori_loop(0, ntile, carry_tiles, (cf0, cb0))

    st_ref[...] = jnp.zeros(st_ref.shape, F32)
    for s in range(nseq):
        st_ref[0, s] = jnp.concatenate([cf[s], cbk[s]], axis=0)

    def finish(c, carry):
        start = pl.multiple_of(c * CH, CH)
        h = s_bf[pl.ds(start, CH), :] + s_bb[pl.ds(start, CH), :]
        o_ref[pl.ds(start, CH), :] = _rms(_gelu_tanh(ya_ref[pl.ds(start, CH), :]) * h)
        return carry

    lax.fori_loop(0, nch, finish, 0)


def _lru_kernel(*refs, l_ctx, l_lat, n_ctx_blk):
    i = pl.program_id(0)

    @pl.when(i < n_ctx_blk)
    def _():
        _lru_variant(l_ctx, *refs)

    @pl.when(i >= n_ctx_blk)
    def _():
        _lru_variant(l_lat, *refs)


def _lru_call(xa, ya, conv_w, conv_b, w_gate, b_gate, lam, h0_all, l_ctx, l_lat, n_ctx_blk):
    n = xa.shape[0]
    nblk = n // SEG
    row = lambda i: (i, 0)
    fixed = lambda i: (0, 0)
    slots = SEG // l_ctx
    return pl.pallas_call(
        functools.partial(_lru_kernel, l_ctx=l_ctx, l_lat=l_lat, n_ctx_blk=n_ctx_blk),
        grid=(nblk,),
        in_specs=[pl.BlockSpec((SEG, D_A), row), pl.BlockSpec((SEG, D_A), row),
                  pl.BlockSpec((4, D_A), fixed), pl.BlockSpec((1, D_A), fixed),
                  pl.BlockSpec((D_A, 4 * D_A), fixed), pl.BlockSpec((1, 4 * D_A), fixed),
                  pl.BlockSpec((2, D_A), fixed),
                  pl.BlockSpec((1, slots, 2, D_A), lambda i: (i, 0, 0, 0))],
        out_specs=[pl.BlockSpec((SEG, D_A), row),
                   pl.BlockSpec((1, slots, 2, D_A), lambda i: (i, 0, 0, 0))],
        out_shape=[jax.ShapeDtypeStruct((n, D_A), F32),
                   jax.ShapeDtypeStruct((nblk, slots, 2, D_A), F32)],
        scratch_shapes=[pltpu.VMEM((SEG, D_A), F32) for _ in range(4)],
        compiler_params=_cp(("parallel",)),
        name="rglru",
    )(xa, ya, conv_w, conv_b, w_gate, b_gate, lam, h0_all)


def _filt_kernel(z_ref, dec_ref, w1_ref, b1_ref, w2_ref, b2_ref, fr_ref, w3_ref, fwd_ref,
                 oab_ref, od0_ref, s_k, s_kf, *, L):
    nblk = 2 * L // CH
    d_idx = pl.program_id(1)
    row = lax.broadcasted_iota(jnp.int32, (CH, 1), 0)

    @pl.when(d_idx == 0)
    def _():
        fr = fr_ref[0]

        def taps(c, carry):
            start = pl.multiple_of(c * CH, CH)
            h1 = jnp.sin(fr * (_dot3(z_ref[pl.ds(start, CH), :], w1_ref[0]) + b1_ref[0]))
            h2 = jnp.sin(fr * (_dot3(h1, w2_ref[0]) + b2_ref[0]))
            t = _dot3(h2, w3_ref[0])
            dec = dec_ref[pl.ds(start, CH), :]
            rg = start + row
            for o in range(HY_ORDER):
                fwd_t = t[:, (2 * o) * D_B:(2 * o + 1) * D_B]
                bwd_t = t[:, (2 * o + 1) * D_B:(2 * o + 2) * D_B]
                ko = jnp.where(rg < L, bwd_t, fwd_t) * dec
                s_k[pl.ds(start, CH), o * D_B:(o + 1) * D_B] = jnp.where(rg == 0, 0.0, ko)
            return carry

        lax.fori_loop(0, nblk, taps, 0)
        fwd = fwd_ref[...]

        def spectra(e, carry):
            start = pl.multiple_of(e * CH, CH)
            s_kf[e] = _dot3(fwd, s_k[pl.ds(start, CH), :])
            return carry

        lax.fori_loop(0, nblk, spectra, 0)

    kd = s_kf[d_idx + 1]
    km = s_kf[d_idx]
    k0 = s_k[pl.ds(pl.multiple_of(d_idx * CH, CH), 1), :]
    sgn = jnp.where((row & 1) == 0, 1.0, -1.0)
    a = kd[:CH] + sgn * (km[:CH] - k0)
    b = jnp.where(row == 0, 0.0, kd[CH:] + sgn * km[CH:])
    hn = kd[CH:CH + 1] + km[CH:CH + 1] - k0
    for o in range(HY_ORDER):
        oab_ref[0, o, 0, 0] = a[:, o * D_B:(o + 1) * D_B]
        oab_ref[0, o, 0, 1] = b[:, o * D_B:(o + 1) * D_B]
        od0_ref[0, o, 0] = jnp.broadcast_to(hn[:, o * D_B:(o + 1) * D_B], (SUB, D_B))


def _filt_call(L, z, dec, w1, b1, w2, b2, fr, w3, fwd32):
    nd = 2 * (L // CH) - 1
    fixed = lambda l, d: (0, 0)
    lay3 = lambda l, d: (l, 0, 0)
    return pl.pallas_call(
        functools.partial(_filt_kernel, L=L),
        grid=(DEPTH, nd),
        in_specs=[pl.BlockSpec((2 * L, LANE), fixed), pl.BlockSpec((2 * L, D_B), fixed),
                  pl.BlockSpec((1, LANE, LANE), lay3), pl.BlockSpec((1, 1, LANE), lay3),
                  pl.BlockSpec((1, LANE, LANE), lay3), pl.BlockSpec((1, 1, LANE), lay3),
                  pl.BlockSpec((1, 1, LANE), lay3),
                  pl.BlockSpec((1, LANE, HY_ORDER * 2 * D_B), lay3),
                  pl.BlockSpec((2 * CH, CH), fixed)],
        out_specs=[pl.BlockSpec((1, HY_ORDER, 1, 2, CH, D_B), lambda l, d: (l, 0, d, 0, 0, 0)),
                   pl.BlockSpec((1, HY_ORDER, 1, SUB, D_B), lambda l, d: (l, 0, d, 0, 0))],
        out_shape=[jax.ShapeDtypeStruct((DEPTH, HY_ORDER, nd, 2, CH, D_B), F32),
                   jax.ShapeDtypeStruct((DEPTH, HY_ORDER, nd, SUB, D_B), F32)],
        scratch_shapes=[pltpu.VMEM((2 * L, HY_ORDER * D_B), F32),
                        pltpu.VMEM((2 * L // CH, 2 * CH, HY_ORDER * D_B), F32)],
        compiler_params=_cp(("parallel", "arbitrary")),
        name=f"hyena_filter_{L}",
    )(z, dec, w1, b1, w2, b2, fr, w3, fwd32)


def _hy_variant(L, o_idx, hy_ref, cw_ref, cb_ref, fwd_ref, inv_ref, hab_ref, hd0_ref, bias_ref, o_ref,
                s_y, s_x, s_u, s_v):
    nch, nseq, P = SEG // CH, SEG // L, L // CH
    row = lax.broadcasted_iota(jnp.int32, (CH, 1), 0)
    frow = lax.broadcasted_iota(jnp.int32, (FS, 1), 0)

    @pl.when(o_idx == 0)
    def _():
        cw = cw_ref[...]
        cb = cb_ref[...]

        def short_conv(c, carry):
            start = pl.multiple_of(c * CH, CH)
            xcat, main = _halo_rows(hy_ref, start, CH)
            tpos = (start + row) & (L - 1)
            xm1 = jnp.where(tpos >= 1, xcat[SUB - 1:SUB - 1 + CH], 0.0)
            xp1 = jnp.where(tpos <= L - 2, xcat[SUB + 1:SUB + 1 + CH], 0.0)
            hc = cw[0:1] * xm1 + cw[1:2] * main + cw[2:3] * xp1 + cb
            s_y[pl.ds(start, CH), :] = hc[:, :D_B]
            s_x[0, pl.ds(start, CH), :] = hc[:, D_B:2 * D_B]
            s_x[1, pl.ds(start, CH), :] = hc[:, 2 * D_B:]
            return carry

        lax.fori_loop(0, nch, short_conv, 0)

    bias = bias_ref[0]

    def loop(n, body, init):
        return body(0, init) if n == 1 else lax.fori_loop(0, n, body, init)

    def one_sequence(s, slot):
        base = s * L
        u0 = slot * P

        def forward_dft(j, cc):
            r = pl.multiple_of(base + j * CH, CH)
            s_u[u0 + j] = _dot(fwd_ref[...], s_y[pl.ds(r, CH), :].astype(BF16))
            return cc

        loop(P, forward_dft, 0)

        def output_block(i, cc):
            for fs in range(CH // FS):
                lo = fs * FS

                def accumulate(j, acc):
                    yre, yim = acc
                    d = i - j + (P - 1)
                    ure = s_u[u0 + j, lo:lo + FS, :]
                    uim = s_u[u0 + j, CH + lo:CH + lo + FS, :]
                    a = hab_ref[0, 0, d, 0, lo:lo + FS, :]
                    b = hab_ref[0, 0, d, 1, lo:lo + FS, :]
                    dd = jnp.where(frow == 0, hd0_ref[0, 0, d, 0:1, :], a) if fs == 0 else a
                    return yre + ure * a - uim * b, yim + ure * b + uim * dd

                zero = jnp.zeros((FS, D_B), F32)
                yre, yim = loop(P, accumulate, (zero, zero))
                s_v[slot, lo:lo + FS, :] = yre.astype(BF16)
                s_v[slot, CH + lo:CH + lo + FS, :] = yim.astype(BF16)
            yc = _dot(inv_ref[...], s_v[slot])
            r = pl.multiple_of(base + i * CH, CH)
            s_y[pl.ds(r, CH), :] = s_x[o_idx, pl.ds(r, CH), :] * (yc + s_y[pl.ds(r, CH), :] * bias)
            return cc

        loop(P, output_block, 0)

    def single_block_pair(p, carry):
        rows = [pl.multiple_of((2 * p + k) * L, CH) for k in range(2)]
        ys = [s_y[pl.ds(r, CH), :] for r in rows]
        gates = [s_x[o_idx, pl.ds(r, CH), :] for r in rows]
        a = hab_ref[0, 0, 0, 0]
        b = hab_ref[0, 0, 0, 1]
        dd = jnp.where(row == 0, hd0_ref[0, 0, 0, 0:1, :], a)
        outs = []
        for y, gate in zip(ys, gates):
            u = _dot(fwd_ref[...], y.astype(BF16))
            ure, uim = u[:CH], u[CH:]
            v = jnp.concatenate([ure * a - uim * b, ure * b + uim * dd], axis=0).astype(BF16)
            outs.append(gate * (_dot(inv_ref[...], v) + y * bias))
        for r, out in zip(rows, outs):
            s_y[pl.ds(r, CH), :] = out
        return carry

    if P == 1 and nseq % 2 == 0:
        lax.fori_loop(0, nseq // 2, single_block_pair, 0)
    else:
        lax.fori_loop(0, nseq, lambda s, carry: (one_sequence(s, 0), carry)[1], 0)

    @pl.when(o_idx == HY_ORDER - 1)
    def _():
        def finish(c, carry):
            start = pl.multiple_of(c * CH, CH)
            o_ref[pl.ds(start, CH), :] = _rms(s_y[pl.ds(start, CH), :])
            return carry

        lax.fori_loop(0, nch, finish, 0)


def _hy_kernel(hy_ref, cw_ref, cb_ref, fwd_ref, inv_ref, habc_ref, hd0c_ref, habl_ref, hd0l_ref, bias_ref,
               o_ref, s_y, s_x, s_u, s_v, *, l_ctx, l_lat, n_ctx_blk):
    i = pl.program_id(0)
    o_idx = pl.program_id(1)
    scratch = (s_y, s_x, s_u, s_v)

    @pl.when(i < n_ctx_blk)
    def _():
        _hy_variant(l_ctx, o_idx, hy_ref, cw_ref, cb_ref, fwd_ref, inv_ref, habc_ref, hd0c_ref, bias_ref, o_ref,
                    *scratch)

    @pl.when(i >= n_ctx_blk)
    def _():
        _hy_variant(l_lat, o_idx, hy_ref, cw_ref, cb_ref, fwd_ref, inv_ref, habl_ref, hd0l_ref, bias_ref, o_ref,
                    *scratch)


def _hy_call(layer, hyb, conv_w, conv_b, fwd, inv, habc, hd0c, habl, hd0l, bias, l_ctx, l_lat, n_ctx_blk):
    n = hyb.shape[0]
    ndc, ndl = habc.shape[2], habl.shape[2]
    pmax = max(l_ctx, l_lat) // CH
    row = lambda i, o: (i, 0)
    fixed = lambda i, o: (0, 0)
    lat_o = lambda i, o: jnp.where(i >= n_ctx_blk, o, 0)
    ctx_o = lambda i, o: jnp.where(i < n_ctx_blk, o, 0)
    return pl.pallas_call(
        functools.partial(_hy_kernel, l_ctx=l_ctx, l_lat=l_lat, n_ctx_blk=n_ctx_blk),
        grid=(n // SEG, HY_ORDER),
        in_specs=[pl.BlockSpec((SEG, 3 * D_B), row),
                  pl.BlockSpec((3, 3 * D_B), fixed), pl.BlockSpec((1, 3 * D_B), fixed),
                  pl.BlockSpec((2 * CH, CH), fixed), pl.BlockSpec((CH, 2 * CH), fixed),
                  pl.BlockSpec((1, 1, ndc, 2, CH, D_B), lambda i, o: (layer, ctx_o(i, o), 0, 0, 0, 0)),
                  pl.BlockSpec((1, 1, ndc, SUB, D_B), lambda i, o: (layer, ctx_o(i, o), 0, 0, 0)),
                  pl.BlockSpec((1, 1, ndl, 2, CH, D_B), lambda i, o: (layer, lat_o(i, o), 0, 0, 0, 0)),
                  pl.BlockSpec((1, 1, ndl, SUB, D_B), lambda i, o: (layer, lat_o(i, o), 0, 0, 0)),
                  pl.BlockSpec((1, 1, D_B), lambda i, o: (o, 0, 0))],
        out_specs=pl.BlockSpec((SEG, D_B), row),
        out_shape=jax.ShapeDtypeStruct((n, D_B), F32),
        scratch_shapes=[pltpu.VMEM((SEG, D_B), F32), pltpu.VMEM((HY_ORDER, SEG, D_B), F32),
                        pltpu.VMEM((max(pmax, 2), 2 * CH, D_B), F32), pltpu.VMEM((2, 2 * CH, D_B), BF16)],
        compiler_params=_cp(("parallel", "arbitrary")),
        name="hyena",
    )(hyb, conv_w, conv_b, fwd, inv, habc, hd0c, habl, hd0l, bias)


def _row_scan(x, op, fill, reverse):
    t, width = x.shape
    n_tiles = t // SUB
    sub = lax.broadcasted_iota(jnp.int32, (1, SUB, 1), 1)
    x3 = x.reshape(n_tiles, SUB, width)
    for s in (1, 2, 4):
        shift, keep = (SUB - s, sub < SUB - s) if reverse else (s, sub >= s)
        x3 = op(x3, jnp.where(keep, pltpu.roll(x3, shift, 1), fill))
    x = x3.reshape(t, width)
    out = [None] * n_tiles
    carry = None
    for i in (reversed(range(n_tiles)) if reverse else range(n_tiles)):
        tile = x[i * SUB:(i + 1) * SUB]
        out[i] = tile if carry is None else op(tile, carry)
        carry = out[i][0:1] if reverse else out[i][SUB - 1:SUB]
    return jnp.concatenate(out, axis=0)


_STK_ONE = 3 * SUB


def _mlstm_prep(d, g_ref, m_old):
    T = CH
    reverse = d == 1
    g = g_ref[...]
    if d == 1:
        g = pltpu.roll(g, LANE - 2 * H_C, 1)
    lane = lax.broadcasted_iota(jnp.int32, (1, LANE), 1)
    head = lane < H_C
    b = pltpu.roll(_row_scan(_log_sigmoid(g), jnp.add, 0.0, reverse), LANE - H_C, 1)
    r = jnp.where(head, g - b, 0.0)
    big_m = jnp.maximum(m_old, _row_scan(r, jnp.maximum, -jnp.inf, reverse))
    last = 0 if reverse else T - 1
    m_last = big_m[last:last + 1, :]
    low = lane < SUB
    p0, p1, p2 = (jnp.where(low, p.astype(F32), 0.0) for p in _split3(-big_m))
    cols = (p0 + pltpu.roll(p1, SUB, 1) + pltpu.roll(p2, 2 * SUB, 1)
            + jnp.where(jnp.logical_and(lane >= _STK_ONE, lane < _STK_ONE + SUB), 1.0, 0.0))
    rowid = lax.broadcasted_iota(jnp.int32, (SUB, 1), 0)
    r8 = r.T[0:SUB, :]
    m_last8 = sum(jnp.where(rowid == h, m_last[:, h:h + 1], 0.0) for h in range(H_C))
    ws8 = jnp.where(rowid < H_C, jnp.exp(r8 - m_last8), 0.0)
    return {"r3": [p.astype(F32) for p in _split3(r8)], "ws8": ws8, "wc": jnp.exp(m_old - m_last),
            "m_new": jnp.where(head, b[last:last + 1, :] + m_last, 0.0),
            "wi": jnp.exp(m_old - big_m), "e": jnp.exp(-(b + big_m)), "cols_b": cols.astype(BF16)}


def _mlstm_variant(carry, qf, vf, ktf, gf, qb_, vb_, ktb_, gb_, hf_ref, hb_ref, co_ref, no_ref, mo_ref,
                   s_cx, s_n, s_m):
    T = CH
    ii = lax.broadcasted_iota(jnp.int32, (T, T), 0)
    jj = lax.broadcasted_iota(jnp.int32, (T, T), 1)
    rowid = lax.broadcasted_iota(jnp.int32, (SUB, 1), 0)
    one_col = jnp.where(lax.broadcasted_iota(jnp.int32, (T, DK), 1) == 0, 1.0, 0.0).astype(BF16)
    prep = []
    for d, g_ref in ((0, gf), (1, gb_ if carry else gf)):
        m_old = s_m[d:d + 1, :] if carry else jnp.zeros((1, LANE), F32)
        prep.append(_mlstm_prep(d, g_ref, m_old))
    refs = ((qf, vf, ktf, hf_ref), (qb_, vb_, ktb_, hb_ref))
    for h in range(H_C):
        sl = slice(h * DK, (h + 1) * DK)
        for d in range(2):
            p = prep[d]
            q_ref, v_ref, kt_ref, h_ref = refs[d]
            idx = d * H_C + h
            if carry or d == 0:
                qb = (q_ref[:, sl] * (DK ** -0.5)).astype(BF16)
                kt = kt_ref[sl, :]
                ktb = kt.astype(BF16)
                v_ext = jnp.concatenate([v_ref[:, sl].astype(BF16), one_col], axis=1)
                s_raw = _dot(qb, ktb)
            tri = (jj >= ii) if d == 1 else (jj <= ii)
            sel = jnp.broadcast_to(jnp.where(rowid == h, 1.0, 0.0), (SUB, T))
            rr = sum(jnp.where(rowid == i, piece[h:h + 1, :], 0.0) for i, piece in enumerate(p["r3"]))
            rmat = jnp.concatenate([sel, sel, sel, rr, jnp.zeros((LANE - 4 * SUB, T), F32)], axis=0)
            expo = _dot(p["cols_b"], rmat.astype(BF16))
            s = s_raw * jnp.exp(jnp.where(tri, expo, -jnp.inf))
            intra = _dot(s.astype(BF16), v_ext)
            num, den = intra[:, :DK], intra[:, DK:DK + 1]
            if carry:
                cx = s_cx[idx]
                inter = _dot(qb, cx.astype(BF16))
                wi = p["wi"][:, h:h + 1]
                num, den = num + wi * inter[:, :DK], den + wi * inter[:, DK:DK + 1]
            h_ref[:, sl] = num / jnp.maximum(jnp.abs(den), p["e"][:, h:h + 1])
            upd = _dot((kt * p["ws8"][h:h + 1, :]).astype(BF16), v_ext)
            n_upd = lax.dot_general(p["ws8"].astype(BF16), ktb, _NT, preferred_element_type=F32)[h:h + 1, :]
            if carry:
                wc = p["wc"][:, h:h + 1]
                s_cx[idx] = wc * cx + upd
                s_n[idx:idx + 1, :] = wc * s_n[idx:idx + 1, :] + n_upd
            else:
                co_ref[0, idx] = upd[:, :DK]
                no_ref[0, idx:idx + 1, :] = n_upd
    m_rows = jnp.concatenate([prep[0]["m_new"], prep[1]["m_new"], jnp.zeros((SUB - 2, LANE), F32)], axis=0)
    if carry:
        s_m[...] = m_rows
    else:
        mo_ref[0] = m_rows


def _mlstm_kernel(*refs, n_ctx_steps, nc_lat):
    cx0_ref, n0_ref, m0_ref = refs[8:11]
    s_cx, s_n, s_m = refs[-3:]
    data = refs[:8] + refs[11:]
    t = pl.program_id(0)
    is_ctx = t < n_ctx_steps

    @pl.when(is_ctx)
    def _():
        _mlstm_variant(False, *data)

    @pl.when(jnp.logical_not(is_ctx))
    def _():
        @pl.when((t - n_ctx_steps) % nc_lat == 0)
        def _():
            s_cx[...] = cx0_ref[0]
            s_n[...] = n0_ref[0]
            s_m[...] = m0_ref[0]

        _mlstm_variant(True, *data)


def _mlstm_call(q, v, kt, gates, cx0, n0, m0, n_ctx_steps, nc_lat):
    n = q.shape[0]
    steps = n // CH
    nst = 2 * H_C

    def bwd_blk(t):
        r = jnp.maximum(t - n_ctx_steps, 0)
        return n_ctx_steps + (r // nc_lat) * nc_lat + (nc_lat - 1 - r % nc_lat)

    out_bwd = lambda t: jnp.where(t < n_ctx_steps, t, bwd_blk(t))
    lat_b = lambda t: jnp.maximum(t - n_ctx_steps, 0) // nc_lat
    ctx_b = lambda t: jnp.minimum(t, n_ctx_steps - 1)
    rows = lambda w, blk: pl.BlockSpec((CH, w), lambda t: (blk(t), 0))
    cols = lambda h, blk: pl.BlockSpec((h, CH), lambda t: (0, blk(t)))
    ident = lambda t: t
    return pl.pallas_call(
        functools.partial(_mlstm_kernel, n_ctx_steps=n_ctx_steps, nc_lat=nc_lat),
        grid=(steps,),
        in_specs=[rows(D_C, ident), rows(D_C, ident), cols(D_C, ident), rows(LANE, ident),
                  rows(D_C, bwd_blk), rows(D_C, bwd_blk), cols(D_C, bwd_blk), rows(LANE, bwd_blk),
                  pl.BlockSpec((1, nst, DK, 2 * DK), lambda t: (lat_b(t), 0, 0, 0)),
                  pl.BlockSpec((1, nst, DK), lambda t: (lat_b(t), 0, 0)),
                  pl.BlockSpec((1, SUB, LANE), lambda t: (lat_b(t), 0, 0))],
        out_specs=[rows(D_C, ident), rows(D_C, out_bwd),
                   pl.BlockSpec((1, nst, DK, DK), lambda t: (ctx_b(t), 0, 0, 0)),
                   pl.BlockSpec((1, nst, DK), lambda t: (ctx_b(t), 0, 0)),
                   pl.BlockSpec((1, SUB, LANE), lambda t: (ctx_b(t), 0, 0))],
        out_shape=[jax.ShapeDtypeStruct((n, D_C), F32), jax.ShapeDtypeStruct((n, D_C), F32),
                   jax.ShapeDtypeStruct((n_ctx_steps, nst, DK, DK), F32),
                   jax.ShapeDtypeStruct((n_ctx_steps, nst, DK), F32),
                   jax.ShapeDtypeStruct((n_ctx_steps, SUB, LANE), F32)],
        scratch_shapes=[pltpu.VMEM((nst, DK, 2 * DK), F32), pltpu.VMEM((nst, DK), F32),
                        pltpu.VMEM((SUB, LANE), F32)],
        compiler_params=_cp(("arbitrary",)),
        name="mlstm",
    )(q, v, kt, gates, q, v, kt, gates, cx0, n0, m0)


def _out_kernel(x_ref, oa_ref, ob_ref, hf_ref, hb_ref, og_ref, mod_ref, mg_ref, w_ref, g_ref, b_ref,
                rw_ref, rb_ref, x1_ref, he_ref, cnt_ref, it_ref, s_cnt):
    i = pl.program_id(0)

    @pl.when(i == 0)
    def _():
        s_cnt[...] = jnp.zeros(s_cnt.shape, F32)

    m = mod_ref[0]
    mg = mg_ref[...]
    acc = _dot((oa_ref[...] * mg[:, :D_A]).astype(BF16), w_ref[0:D_A, :])
    acc += _dot((ob_ref[...] * mg[:, D_A:D_A + D_B]).astype(BF16), w_ref[D_A:D_A + D_B, :])
    hc = hf_ref[...] + hb_ref[...]
    og = og_ref[...]
    off = D_A + D_B
    for h in range(H_C):
        sl = slice(h * DK, (h + 1) * DK)
        oc = _sigmoid(og[:, sl]) * _rms(hc[:, sl]) * mg[:, off + h * DK:off + (h + 1) * DK]
        acc += _dot(oc.astype(BF16), w_ref[off + h * DK:off + (h + 1) * DK, :])
    x1 = _ln_plain(ALPHA * x_ref[...] + m[2:3] * acc) * g_ref[...] + b_ref[...]
    x1_ref[...] = x1
    h2 = x1 * (1.0 + m[4:5]) + m[3:4]
    he_ref[:, :D_MODEL] = h2

    lg = _dot3(h2, rw_ref[...]) + rb_ref[...]
    col = lax.broadcasted_iota(jnp.int32, lg.shape, 1)
    ninf = -jnp.inf
    lgm = jnp.where(col < N_GROUPS, lg, ninf)
    mx = jnp.max(lgm, -1, keepdims=True)
    gi = jnp.min(jnp.where(lgm == mx, col, LANE), -1, keepdims=True)
    pg_top = 1.0 / jnp.sum(jnp.where(col < N_GROUPS, jnp.exp(lg - mx), 0.0), -1, keepdims=True)
    lo4 = N_GROUPS + E_PER_GROUP * gi
    lem = jnp.where(jnp.logical_and(col >= lo4, col < lo4 + E_PER_GROUP), lg, ninf)
    v1 = jnp.max(lem, -1, keepdims=True)
    i1 = jnp.min(jnp.where(lem == v1, col, LANE), -1, keepdims=True)
    lem2 = jnp.where(col == i1, ninf, lem)
    v2 = jnp.max(lem2, -1, keepdims=True)
    i2 = jnp.min(jnp.where(lem2 == v2, col, LANE), -1, keepdims=True)
    e21 = jnp.exp(v2 - v1)
    w1 = pg_top / (1.0 + e21)
    w2 = pg_top * e21 / (1.0 + e21)
    e1, e2 = i1 - N_GROUPS, i2 - N_GROUPS
    first_lo = e1 < e2
    elo, ehi = jnp.minimum(e1, e2), jnp.maximum(e1, e2)
    wlo, whi = jnp.where(first_lo, w1, w2), jnp.where(first_lo, w2, w1)
    llo, lhi = elo - E_PER_GROUP * gi, ehi - E_PER_GROUP * gi
    cls = gi * N_PAIRS + ((llo * (7 - llo)) >> 1) + lhi - llo - 1

    oh = jnp.where(col == cls, 1.0, 0.0)
    ii = lax.broadcasted_iota(jnp.int32, (TM_OUT, TM_OUT), 0)
    jj = lax.broadcasted_iota(jnp.int32, (TM_OUT, TM_OUT), 1)
    before = jnp.where(jj < ii, 1.0, 0.0).astype(BF16)
    cnt = s_cnt[0:1, :]
    rank = jnp.sum(oh * (_dot(before, oh.astype(BF16)) + cnt), -1, keepdims=True)
    cnt = cnt + jnp.sum(oh, 0, keepdims=True)
    s_cnt[...] = jnp.broadcast_to(cnt, s_cnt.shape)
    cnt_ref[...] = jnp.broadcast_to(cnt, cnt_ref.shape)

    info = jnp.zeros(lg.shape, F32)
    for c, val in ((I_CLS, cls.astype(F32)), (I_ELO, elo.astype(F32)), (I_EHI, ehi.astype(F32)),
                   (I_WLO, wlo), (I_WHI, whi), (I_RANK, rank)):
        info = jnp.where(col == c, val, info)
    he_ref[:, D_MODEL:] = info
    it_ref[...] = info.T[:SUB, :]


def _out_call(x, out_a, out_b, hcf, hcb, ogate, mod_l, mix_g, w_out, ln_g, ln_b, rt_w, rt_b, group_of_tile):
    n = x.shape[0]
    row = lambda i: (i, 0)
    fixed = lambda i: (0, 0)
    return pl.pallas_call(
        _out_kernel,
        grid=(n // TM_OUT,),
        in_specs=[pl.BlockSpec((TM_OUT, D_MODEL), row), pl.BlockSpec((TM_OUT, D_A), row),
                  pl.BlockSpec((TM_OUT, D_B), row),
                  pl.BlockSpec((TM_OUT, D_C), row), pl.BlockSpec((TM_OUT, D_C), row), pl.BlockSpec((TM_OUT, D_C), row),
                  pl.BlockSpec((1, 6, D_MODEL), lambda i: (group_of_tile(i), 0, 0)),
                  pl.BlockSpec((1, D_MODEL), fixed), pl.BlockSpec((D_MODEL, D_MODEL), fixed),
                  pl.BlockSpec((1, D_MODEL), fixed), pl.BlockSpec((1, D_MODEL), fixed),
                  pl.BlockSpec((D_MODEL, LANE), fixed), pl.BlockSpec((1, LANE), fixed)],
        out_specs=[pl.BlockSpec((TM_OUT, D_MODEL), row), pl.BlockSpec((TM_OUT, D_EXT), row),
                   pl.BlockSpec((SUB, LANE), fixed), pl.BlockSpec((SUB, TM_OUT), lambda i: (0, i))],
        out_shape=[jax.ShapeDtypeStruct((n, D_MODEL), F32), jax.ShapeDtypeStruct((n, D_EXT), F32),
                   jax.ShapeDtypeStruct((SUB, LANE), F32), jax.ShapeDtypeStruct((SUB, n), F32)],
        scratch_shapes=[pltpu.VMEM((SUB, LANE), F32)],
        compiler_params=_cp(("arbitrary",)),
        name="out_proj_router",
    )(x, out_a, out_b, hcf, hcb, ogate, mod_l, mix_g, w_out, ln_g, ln_b, rt_w, rt_b)


def _row_copy(src_ref, src_row, dst_ref, dst_row, sem):
    return pltpu.make_async_copy(src_ref.at[pl.ds(src_row, 1), :], dst_ref.at[pl.ds(dst_row, 1), :], sem)


def _sorted_row(cls_ref, rank_ref, start_ref, t):
    return start_ref[cls_ref[t]] + rank_ref[t]


def _scatter_kernel(cls_ref, rank_ref, start_ref, pad_ref, na_ref, x_ref, o_ref, ring, z_ref, sem_z, load_sems,
                    row_sems):
    n_tiles = o_ref.shape[0] // TM

    def zero_tile(row):
        return pltpu.make_async_copy(z_ref, o_ref.at[pl.ds(pl.multiple_of(row, TM), TM), :], sem_z)

    def for_zero_tiles(fn):
        def per_class(c, carry):
            row = pad_ref[c]

            @pl.when(row >= 0)
            def _():
                fn(zero_tile(row))

            return carry

        lax.fori_loop(0, N_CLASS, per_class, 0)

        def per_idle(t, carry):
            fn(zero_tile(t * TM))
            return carry

        lax.fori_loop(na_ref[0], n_tiles, per_idle, 0)

    i = pl.program_id(0)
    n_steps = x_ref.shape[0] // TM

    def load(step):
        return pltpu.make_async_copy(x_ref.at[pl.ds(pl.multiple_of(step * TM, TM), TM), :],
                                     ring.at[step % RING], load_sems.at[step % RING])

    def start_rows(step):
        slot = step % RING

        def body(r, carry):
            dst = _sorted_row(cls_ref, rank_ref, start_ref, step * TM + r)
            _row_copy(ring.at[slot], r, o_ref, dst, row_sems.at[slot]).start()
            return carry

        lax.fori_loop(0, TM, body, 0, unroll=DMA_UNROLL)

    def wait_rows(step):
        slot = step % RING
        pltpu.make_async_copy(ring.at[slot], o_ref.at[pl.ds(0, TM), :], row_sems.at[slot]).wait()

    @pl.when(i == 0)
    def _():
        z_ref[...] = jnp.zeros(z_ref.shape, F32)
        for_zero_tiles(lambda cp: cp.start())
        for_zero_tiles(lambda cp: cp.wait())
        load(0).start()

    @pl.when(i >= RING - 1)
    def _():
        wait_rows(i - (RING - 1))

    @pl.when(i + 1 < n_steps)
    def _():
        load(i + 1).start()

    load(i).wait()
    start_rows(i)

    @pl.when(i == n_steps - 1)
    def _():
        for back in range(min(RING - 1, n_steps) - 1, -1, -1):
            wait_rows(i - back)


def _scatter_call(plan, h_ext, n_tiles_max):
    any_spec = pl.BlockSpec(memory_space=pl.ANY)
    return pl.pallas_call(
        _scatter_kernel,
        grid_spec=pltpu.PrefetchScalarGridSpec(
            num_scalar_prefetch=5, grid=(h_ext.shape[0] // TM,),
            in_specs=[any_spec],
            out_specs=any_spec,
            scratch_shapes=[pltpu.VMEM((RING, TM, D_EXT), F32), pltpu.VMEM((TM, D_EXT), F32),
                            pltpu.SemaphoreType.DMA(()),
                            pltpu.SemaphoreType.DMA((RING,)), pltpu.SemaphoreType.DMA((RING,))]),
        out_shape=jax.ShapeDtypeStruct((n_tiles_max * TM, D_EXT), F32),
        compiler_params=_cp(("arbitrary",)),
        name="moe_scatter",
    )(plan["cls"], plan["rank"], plan["row_start"], plan["pad_rows"], plan["n_act"], h_ext)


def _moe_kernel(tg_ref, lo_ref, hi_ref, cg_ref, nv_ref, na_ref, x_ref, w1_ref, w3_ref, w2_ref, o_ref, s1, s3, s2):
    del tg_ref
    t = pl.program_id(0)
    active = t < na_ref[0]
    half = TM // 2

    @pl.when(jnp.logical_not(active))
    def _():
        o_ref[...] = jnp.zeros(o_ref.shape, F32)

    @pl.when(jnp.logical_and(active, cg_ref[t] == 1))
    def _():
        for e in range(E_PER_GROUP):
            s1[e] = w1_ref[e].astype(BF16)
            s3[e] = w3_ref[e].astype(BF16)
            s2[e] = w2_ref[e].astype(BF16)

    def run(rows):
        xe = x_ref[0:rows, :]
        x = xe[:, :D_MODEL].astype(BF16)

        def expert(e, gate):
            a = _dot(x, s1[e])
            hm = a * _sigmoid(a) * _dot(x, s3[e]) * gate
            return _dot(hm.astype(BF16), s2[e])

        o_ref[0:rows, :] = (expert(lo_ref[t], xe[:, D_MODEL + I_WLO:D_MODEL + I_WLO + 1])
                            + expert(hi_ref[t], xe[:, D_MODEL + I_WHI:D_MODEL + I_WHI + 1]))

    @pl.when(jnp.logical_and(active, nv_ref[t] > half))
    def _():
        run(TM)

    @pl.when(jnp.logical_and(active, nv_ref[t] <= half))
    def _():
        run(half)
        o_ref[half:, :] = jnp.zeros((TM - half, D_MODEL), F32)


def _moe_call(layer, plan, x_sorted, w1, w3, w2):
    r = x_sorted.shape[0]
    act = lambda t, tg, lo, hi, cg, nv, na: (jnp.minimum(t, na[0] - 1), 0)
    grp = lambda t, tg, lo, hi, cg, nv, na: (layer, tg[t], 0, 0)
    up = pl.BlockSpec((None, E_PER_GROUP, D_MODEL, D_E), grp, pipeline_mode=pl.Buffered(1))
    down = pl.BlockSpec((None, E_PER_GROUP, D_E, D_MODEL), grp, pipeline_mode=pl.Buffered(1))
    return pl.pallas_call(
        _moe_kernel,
        grid_spec=pltpu.PrefetchScalarGridSpec(
            num_scalar_prefetch=6, grid=(r // TM,),
            in_specs=[pl.BlockSpec((TM, D_EXT), act), up, up, down],
            out_specs=pl.BlockSpec((TM, D_MODEL), lambda t, *_: (t, 0)),
            scratch_shapes=[pltpu.VMEM((E_PER_GROUP, D_MODEL, D_E), BF16),
                            pltpu.VMEM((E_PER_GROUP, D_MODEL, D_E), BF16),
                            pltpu.VMEM((E_PER_GROUP, D_E, D_MODEL), BF16)]),
        out_shape=jax.ShapeDtypeStruct((r, D_MODEL), F32),
        compiler_params=_cp(("arbitrary",)),
        name="moe_experts",
    )(plan["tile_grp"], plan["tile_lo"], plan["tile_hi"], plan["chg_grp"], plan["valid"], plan["n_act"],
      x_sorted, w1, w3, w2)


def _ln2_kernel(cls_ref, rank_ref, start_ref, x1_ref, mod_ref, g_ref, b_ref, y_ref, *rest, n_ctx_tiles):
    o_refs, (buf, sems) = rest[:-2], rest[-2:]
    i = pl.program_id(0)
    n_steps = pl.num_programs(0)

    def start_rows(step):
        slot = step % 2

        def body(r, carry):
            src = _sorted_row(cls_ref, rank_ref, start_ref, step * TM + r)
            _row_copy(y_ref, src, buf.at[slot], r, sems.at[slot]).start()
            return carry

        lax.fori_loop(0, TM, body, 0, unroll=DMA_UNROLL)

    @pl.when(i == 0)
    def _():
        start_rows(0)

    @pl.when(i + 1 < n_steps)
    def _():
        start_rows(i + 1)

    slot = i % 2
    pltpu.make_async_copy(y_ref.at[pl.ds(0, TM), :], buf.at[slot], sems.at[slot]).wait()
    m = mod_ref[0]
    y = _ln_plain(ALPHA * x1_ref[...] + m[5:6] * buf[slot]) * g_ref[...] + b_ref[...]
    if len(o_refs) == 1:
        o_refs[0][...] = y
    else:
        @pl.when(i < n_ctx_tiles)
        def _():
            o_refs[0][...] = y

        @pl.when(i >= n_ctx_tiles)
        def _():
            o_refs[1][...] = y


def _ln2_call(plan, x1, y_sorted, mod_l, ln_g, ln_b, group_of_tile, n_ctx_tiles, split):
    n = x1.shape[0]
    row = lambda i, *_: (i, 0)
    fixed = lambda i, *_: (0, 0)
    if split:
        n_ctx = n_ctx_tiles * TM
        out_specs = [pl.BlockSpec((TM, D_MODEL), lambda i, *_: (jnp.minimum(i, n_ctx_tiles - 1), 0)),
                     pl.BlockSpec((TM, D_MODEL), lambda i, *_: (jnp.maximum(i - n_ctx_tiles, 0), 0))]
        out_shape = [jax.ShapeDtypeStruct((n_ctx, D_MODEL), F32), jax.ShapeDtypeStruct((n - n_ctx, D_MODEL), F32)]
    else:
        out_specs = [pl.BlockSpec((TM, D_MODEL), row)]
        out_shape = [jax.ShapeDtypeStruct((n, D_MODEL), F32)]
    return pl.pallas_call(
        functools.partial(_ln2_kernel, n_ctx_tiles=n_ctx_tiles),
        grid_spec=pltpu.PrefetchScalarGridSpec(
            num_scalar_prefetch=3, grid=(n // TM,),
            in_specs=[pl.BlockSpec((TM, D_MODEL), row),
                      pl.BlockSpec((1, 6, D_MODEL), lambda i, *_: (group_of_tile(i), 0, 0)),
                      pl.BlockSpec((1, D_MODEL), fixed), pl.BlockSpec((1, D_MODEL), fixed),
                      pl.BlockSpec(memory_space=pl.ANY)],
            out_specs=out_specs,
            scratch_shapes=[pltpu.VMEM((2, TM, D_MODEL), F32), pltpu.SemaphoreType.DMA((2,))]),
        out_shape=out_shape,
        compiler_params=_cp(("arbitrary",)),
        name="moe_gather_ln2",
    )(plan["cls"], plan["rank"], plan["row_start"], x1, mod_l, ln_g, ln_b, y_sorted)


def _dft_matrices():
    n2 = 2 * CH
    f = np.arange(CH, dtype=np.float64)[:, None]
    t = np.arange(CH, dtype=np.float64)[None, :]
    ang = 2.0 * np.pi * f * t / n2
    re, im = np.cos(ang), -np.sin(ang)
    im[0, :] = np.cos(np.pi * t[0])
    fwd = np.concatenate([re, im], axis=0)
    scale = np.full((CH, 1), 2.0 / n2)
    scale[0, 0] = 1.0 / n2
    inv = np.concatenate([(re * scale).T, (im * scale).T], axis=1)
    return fwd.astype(np.float32), inv.astype(np.float32)


def _filter_features(L):
    lag = np.arange(-L, L)
    m = np.minimum(np.abs(lag), L - 1)
    t = (np.arange(L, dtype=np.float32) / np.float32(max(L - 1, 1)))[m]
    w = (np.float32(2.0 * math.pi) * np.arange(L, dtype=np.float32) / np.float32(L))[m]
    bands = np.linspace(1e-4, HY_BANDS - 1, HY_BANDS, dtype=np.float32)
    z = np.zeros((2 * L, LANE), np.float32)
    z[:, 0] = t
    z[:, 1:1 + HY_BANDS] = np.cos(w[:, None] * bands)
    z[:, 1 + HY_BANDS:HY_EMB] = -np.sin(w[:, None] * bands)
    lo, hi = math.log(HY_DECAY_TARGET) / 1.5, math.log(HY_DECAY_TARGET) / 0.3
    deltas = np.abs(np.linspace(lo, hi, D_B, dtype=np.float32))
    dec = np.exp(-t[:, None] * deltas)
    return z, dec.astype(np.float32)


def _sincos_2d(rows, cols):
    quarter = D_MODEL // 4
    omega = 1.0 / (10000.0 ** (jnp.arange(quarter, dtype=F32) / quarter))

    def emb(n):
        ang = jnp.arange(n, dtype=F32)[:, None] * omega[None]
        return jnp.concatenate([jnp.sin(ang), jnp.cos(ang)], -1)

    er, ec = emb(rows), emb(cols)
    half = D_MODEL // 2
    pos = jnp.concatenate([jnp.broadcast_to(er[:, None], (rows, cols, half)),
                           jnp.broadcast_to(ec[None], (rows, cols, half))], -1)
    return pos.reshape(rows * cols, D_MODEL)


def _pad_to(x, shape):
    return jnp.pad(x, [(0, s - d) for d, s in zip(x.shape, shape)])


def _block_diag(w):
    eye = jnp.eye(H_A, dtype=w.dtype)
    return jnp.einsum("hij,hg->higj", w, eye).reshape(D_A, D_A)


_PAIR_LO = np.array([0, 0, 0, 1, 1, 2], np.int32)
_PAIR_HI = np.array([1, 2, 3, 2, 3, 3], np.int32)


def _routing_plan(info_t, counts, n_tiles_max):
    cnt = counts[0, :N_CLASS].astype(jnp.int32)
    tiles = (cnt + TM - 1) // TM
    tile_end = jnp.cumsum(tiles)
    n_act = tile_end[-1]
    t = jnp.minimum(jnp.arange(n_tiles_max, dtype=jnp.int32), n_act - 1)
    tcls = jnp.minimum(jnp.sum((tile_end[None, :] <= t[:, None]).astype(jnp.int32), 1), N_CLASS - 1)
    grp, pair = (tcls // N_PAIRS).astype(jnp.int32), tcls % N_PAIRS
    valid = jnp.clip(cnt[tcls] - (t - (tile_end - tiles)[tcls]) * TM, 0, TM).astype(jnp.int32)
    first = jnp.ones((1,), jnp.int32)
    changed = lambda e: jnp.concatenate([first, (e[1:] != e[:-1]).astype(jnp.int32)])
    return {"cls": info_t[I_CLS].astype(jnp.int32), "rank": info_t[I_RANK].astype(jnp.int32),
            "row_start": ((tile_end - tiles) * TM).astype(jnp.int32), "tile_grp": grp, "chg_grp": changed(grp),
            "tile_lo": jnp.asarray(_PAIR_LO)[pair], "tile_hi": jnp.asarray(_PAIR_HI)[pair], "valid": valid,
            "n_act": n_act.reshape(1).astype(jnp.int32),
            "pad_rows": jnp.where(tiles > 0, (tile_end - 1) * TM, -1).astype(jnp.int32)}


def kernel(x_prompt, x_sample, c, state_lru, state_mlstm_C, state_mlstm_n, state_mlstm_m, c_ctx, w_ada, b_ada, w_in, b_in, conv_a_w, conv_a_b, lru_wa, lru_ba, lru_wx, lru_bx, lru_lam, conv_b_w, conv_b_b, hy_w1, hy_b1, hy_w2, hy_b2, hy_freq, hy_w3, hy_bias, mix_g, w_out, ln1_g, ln1_b, rt_wg, rt_bg, rt_we, rt_be, moe_w1, moe_w3, moe_w2, ln2_g, ln2_b):
    B, l_ctx, D = x_prompt.shape
    b_lat, l_lat, _ = x_sample.shape
    n_ctx, n_lat = B * l_ctx, b_lat * l_lat
    n = n_ctx + n_lat
    assert D == D_MODEL and w_in.shape[-1] == D_MAIN + N_GATE
    assert SEG % l_ctx == 0 and l_lat == SEG and l_ctx % CH == 0 and n_ctx % SEG == 0
    assert l_ctx == CH, "the mLSTM step schedule assumes one chunk per context sequence"
    assert 1 + b_lat <= SUB
    n_ctx_blk = n_ctx // SEG
    n_ctx_tiles = n_ctx // TM
    tiles_per_lat = l_lat // TM
    nc_lat = l_lat // CH

    def group_of(tile_rows):
        first_lat, per_seq = n_ctx // tile_rows, l_lat // tile_rows
        return lambda i: jnp.where(i < first_lat, 0, 1 + (i - first_lat) // per_seq)

    group_of_tile = group_of(TM)

    cond = jnp.concatenate([c_ctx[None], c, jnp.zeros((SUB - 1 - b_lat, D), F32)], 0)
    mod = _mod_call(cond, w_ada, b_ada).reshape(DEPTH, SUB, 6, D)
    pos = _sincos_2d(l_lat // GRID_W, GRID_W)
    x = _entry_call(x_prompt.reshape(n_ctx, D), x_sample.reshape(n_lat, D), pos)

    fwd_np, inv_np = _dft_matrices()
    fwd32 = jnp.asarray(fwd_np)
    fwd16, inv16 = fwd32.astype(BF16), jnp.asarray(inv_np).astype(BF16)
    fw1 = _pad_to(hy_w1, (DEPTH, LANE, LANE))
    fb1 = _pad_to(hy_b1[:, None, :], (DEPTH, 1, LANE))
    fw2 = _pad_to(hy_w2, (DEPTH, LANE, LANE))
    fb2 = _pad_to(hy_b2[:, None, :], (DEPTH, 1, LANE))
    ffr = _pad_to(hy_freq[:, None, :], (DEPTH, 1, LANE))
    fw3 = _pad_to(hy_w3, (DEPTH, LANE, HY_ORDER * 2 * D_B))
    spectra = {}
    for L in (l_ctx, l_lat):
        z_np, dec_np = _filter_features(L)
        spectra[L] = _filt_call(L, jnp.asarray(z_np), jnp.asarray(dec_np), fw1, fb1, fw2, fb2, ffr, fw3, fwd32)

    w_in_t = jnp.swapaxes(w_in, 1, 2)
    lat_slots = SEG // l_ctx
    st_lru, st_c, st_n, st_m = [], [], [], []
    for l in range(DEPTH):
        b_main = b_in[l, None, :D_MAIN]
        k_lo, k_hi = _Z_CUTS[_K_CUT], _Z_CUTS[_K_CUT + 1]
        b_kt = jnp.broadcast_to(b_in[l, k_lo:k_hi, None], (D_C, TM_IN))
        w_gate = _pad_to(w_in[l, :, D_MAIN:], (D, LANE))
        b_gate = _pad_to(b_in[l, None, D_MAIN:], (1, LANE))
        xa, ya, hyb, q, v, og, gates, kt = _in_call(l, x, mod[l], w_in_t, b_main, b_kt, w_gate, b_gate,
                                                    group_of(TM_IN))

        lru_w = jnp.concatenate([_block_diag(lru_wa[l, 0]), _block_diag(lru_wx[l, 0]),
                                 _block_diag(lru_wa[l, 1]), _block_diag(lru_wx[l, 1])], 1).astype(BF16)
        lru_b = jnp.concatenate([lru_ba[l, 0], lru_bx[l, 0], lru_ba[l, 1], lru_bx[l, 1]])[None]
        h0_lat = _pad_to(state_lru[:, l][:, None], (b_lat, lat_slots, 2, D_A))
        h0_all = jnp.concatenate([jnp.zeros((n_ctx_blk, lat_slots, 2, D_A), F32), h0_lat], 0)
        out_a, lru_last = _lru_call(xa, ya, conv_a_w[l], conv_a_b[l, None], lru_w, lru_b, lru_lam[l], h0_all,
                                    l_ctx, l_lat, n_ctx_blk)

        habc, hd0c = spectra[l_ctx]
        habl, hd0l = spectra[l_lat]
        out_b = _hy_call(l, hyb, conv_b_w[l], conv_b_b[l, None], fwd16, inv16, habc, hd0c, habl, hd0l,
                         hy_bias[l][:, None, :], l_ctx, l_lat, n_ctx_blk)

        n0 = state_mlstm_n[:, l].reshape(b_lat, 2 * H_C, DK)
        cx0 = jnp.concatenate([state_mlstm_C[:, l].reshape(b_lat, 2 * H_C, DK, DK), n0[..., None],
                               jnp.zeros((b_lat, 2 * H_C, DK, DK - 1), F32)], -1)
        m0 = _pad_to(state_mlstm_m[:, l], (b_lat, SUB, LANE))
        hcf, hcb, c_fin, n_fin, m_fin = _mlstm_call(q, v, kt, gates, cx0, n0, m0, n_ctx // CH, nc_lat)

        rt_w = _pad_to(jnp.concatenate([rt_wg[l], rt_we[l]], 1), (D, LANE))
        rt_b = _pad_to(jnp.concatenate([rt_bg[l], rt_be[l]])[None], (1, LANE))
        x1, h_ext, counts, info_t = _out_call(x, out_a, out_b, hcf, hcb, og, mod[l], mix_g[l, None],
                                              w_out[l].astype(BF16), ln1_g[l, None], ln1_b[l, None], rt_w, rt_b,
                                              group_of(TM_OUT))

        n_tiles_max = n // TM + N_CLASS
        plan = _routing_plan(info_t, counts, n_tiles_max)
        x_sorted = _scatter_call(plan, h_ext, n_tiles_max)
        y_sorted = _moe_call(l, plan, x_sorted, moe_w1, moe_w3, moe_w2)
        outs = _ln2_call(plan, x1, y_sorted, mod[l], ln2_g[l, None], ln2_b[l, None], group_of_tile,
                         n_ctx_tiles, split=(l == DEPTH - 1))
        x = outs[0]

        st_lru.append(lru_last[:n_ctx_blk].reshape(B, 2, D_A))
        st_c.append(c_fin.reshape(B, 2, H_C, DK, DK))
        st_n.append(n_fin.reshape(B, 2, H_C, DK))
        st_m.append(m_fin[:, :2, :H_C])

    return (outs[0].reshape(B, l_ctx, D), outs[1].reshape(b_lat, l_lat, D),
            jnp.stack(st_lru, 1), jnp.stack(st_c, 1), jnp.stack(st_n, 1), jnp.stack(st_m, 1))
```

```python
import functools
import math

import numpy as np
import jax
import jax.numpy as jnp
from jax import lax
from jax.experimental import pallas as pl
from jax.experimental.pallas import tpu as pltpu

F32 = jnp.float32
BF16 = jnp.bfloat16

D_MODEL = 1024
DEPTH = 2
GRID_W = 64
D_A = 256
H_A = 4
BA = D_A // H_A
LRU_C = 8.0
D_B = 256
HY_ORDER = 2
HY_BANDS = 16
HY_EMB = 1 + 2 * HY_BANDS
HY_FH = 64
HY_DECAY_TARGET = 1e-2
D_C = 512
H_C = 4
DK = D_C // H_C
N_GROUPS = 4
E_PER_GROUP = 4
N_EXP = N_GROUPS * E_PER_GROUP
N_PAIRS = 6
N_CLASS = N_GROUPS * N_PAIRS
D_E = 512
ALPHA = (2 * DEPTH) ** 0.25
EPS = 1e-6
D_MAIN = 2 * D_A + 3 * D_B + 4 * D_C
N_GATE = 4 * H_C

LANE = 128
SUB = 8
VMEM_LIMIT = 56 * 1024 * 1024

CH = 256
SEG = 2048
TM = 256
TM_IN = 512
TM_OUT = 512
D_EXT = D_MODEL + LANE
FS = 64
DMA_UNROLL = 8
RING = 3

I_CLS, I_ELO, I_EHI, I_WLO, I_WHI, I_RANK = range(6)


def _cp(sem, vmem=VMEM_LIMIT):
    return pltpu.CompilerParams(dimension_semantics=sem, vmem_limit_bytes=vmem)


def _dot(a, b):
    return jnp.dot(a, b, preferred_element_type=F32)


def _split2(x):
    hi = x.astype(BF16)
    lo = (x - hi.astype(F32)).astype(BF16)
    return hi, lo


def _dot3(a, b):
    ah, al = _split2(a)
    bh, bl = _split2(b)
    return _dot(ah, bh) + (_dot(ah, bl) + _dot(al, bh))


def _split3(x):
    hi = x.astype(BF16)
    r1 = x - hi.astype(F32)
    mid = r1.astype(BF16)
    lo = (r1 - mid.astype(F32)).astype(BF16)
    return hi, mid, lo


def _sigmoid(x):
    return 1.0 / (1.0 + jnp.exp(-x))


def _log_sigmoid(x):
    return jnp.minimum(x, 0.0) - jnp.log1p(jnp.exp(-jnp.abs(x)))


def _gelu_tanh(x):
    return 0.5 * x * (1.0 + jnp.tanh(math.sqrt(2.0 / math.pi) * (x + 0.044715 * (x * x * x))))


def _ln_plain(x):
    mu = jnp.mean(x, -1, keepdims=True)
    xc = x - mu
    var = jnp.mean(xc * xc, -1, keepdims=True)
    return xc * lax.rsqrt(var + EPS)


def _rms(x):
    return x * lax.rsqrt(jnp.mean(x * x, -1, keepdims=True) + EPS)


def _halo_rows(ref, start, rows):
    total = ref.shape[0]
    prev = ref[pl.ds(pl.multiple_of(jnp.maximum(start - SUB, 0), SUB), SUB), :]
    main = ref[pl.ds(start, rows), :]
    nxt = ref[pl.ds(pl.multiple_of(jnp.minimum(start + rows, total - SUB), SUB), SUB), :]
    return jnp.concatenate([prev, main, nxt], axis=0), main


def _mod_kernel(c_ref, w_ref, b_ref, o_ref):
    c = c_ref[...]
    o_ref[0] = _dot3(c * _sigmoid(c), w_ref[0]) + b_ref[0]


def _mod_call(cond, w_ada, b_ada):
    tn = 1536
    n6 = w_ada.shape[-1]
    return pl.pallas_call(
        _mod_kernel,
        grid=(DEPTH, n6 // tn),
        in_specs=[pl.BlockSpec((SUB, D_MODEL), lambda l, j: (0, 0)),
                  pl.BlockSpec((1, D_MODEL, tn), lambda l, j: (l, 0, j)),
                  pl.BlockSpec((1, 1, tn), lambda l, j: (l, 0, j))],
        out_specs=pl.BlockSpec((1, SUB, tn), lambda l, j: (l, 0, j)),
        out_shape=jax.ShapeDtypeStruct((DEPTH, SUB, n6), F32),
        compiler_params=_cp(("parallel", "parallel")),
        name="adaln_mod",
    )(cond, w_ada, b_ada.reshape(DEPTH, 1, n6))


def _entry_kernel(xc_ref, xl_ref, pos_ref, o_ref, *, n_ctx_tiles):
    i = pl.program_id(0)

    @pl.when(i < n_ctx_tiles)
    def _():
        o_ref[...] = _ln_plain(xc_ref[...])

    @pl.when(i >= n_ctx_tiles)
    def _():
        o_ref[...] = _ln_plain(xl_ref[...] + pos_ref[...])


def _entry_call(xc, xl, pos):
    tm = 512
    n_ctx, n_lat, l_lat = xc.shape[0], xl.shape[0], pos.shape[0]
    nct = n_ctx // tm
    per_seq = l_lat // tm
    return pl.pallas_call(
        functools.partial(_entry_kernel, n_ctx_tiles=nct),
        grid=((n_ctx + n_lat) // tm,),
        in_specs=[pl.BlockSpec((tm, D_MODEL), lambda i: (jnp.minimum(i, nct - 1), 0)),
                  pl.BlockSpec((tm, D_MODEL), lambda i: (jnp.maximum(i - nct, 0), 0)),
                  pl.BlockSpec((tm, D_MODEL), lambda i: (jnp.maximum(i - nct, 0) % per_seq, 0))],
        out_specs=pl.BlockSpec((tm, D_MODEL), lambda i: (i, 0)),
        out_shape=jax.ShapeDtypeStruct((n_ctx + n_lat, D_MODEL), F32),
        compiler_params=_cp(("parallel",)),
        name="entry_ln",
    )(xc, xl, pos)


_Z_CUTS = (0, D_A, 2 * D_A, 2 * D_A + 3 * D_B, 2 * D_A + 3 * D_B + D_C, 2 * D_A + 3 * D_B + 2 * D_C,
           2 * D_A + 3 * D_B + 3 * D_C, D_MAIN)


_K_CUT = 4
_ROW_CUTS = tuple(c for i, c in enumerate(zip(_Z_CUTS[:-1], _Z_CUTS[1:])) if i != _K_CUT)
_NT = (((1,), (1,)), ((), ()))


W_CHUNK = D_MAIN // 8
_N_IN_OUT = len(_ROW_CUTS) + 2


def _stage_weight(wt_hbm, layer, wt16, stage, sem):
    def chunk(c, carry):
        r0 = pl.multiple_of(c * W_CHUNK, 2 * SUB)
        cp = pltpu.make_async_copy(wt_hbm.at[layer, pl.ds(r0, W_CHUNK), :], stage, sem)
        cp.start()
        cp.wait()
        wt16[pl.ds(r0, W_CHUNK), :] = stage[...].astype(BF16)
        return carry

    lax.fori_loop(0, D_MAIN // W_CHUNK, chunk, 0)


def _in_project(x, mod_ref, wt16, b_ref, bkt_ref, wg_ref, bg_ref, out_refs):
    m = mod_ref[0]
    h = x * (1.0 + m[1:2]) + m[0:1]
    hb = h.astype(BF16)
    dg = lambda a, b: lax.dot_general(a, b, _NT, preferred_element_type=F32)
    for ref, (a, b) in zip(out_refs[:-2], _ROW_CUTS):
        ref[...] = dg(hb, wt16[a:b, :]) + b_ref[:, a:b]
    g_ref, kt_ref = out_refs[-2:]
    g_ref[...] = _dot3(h, wg_ref[...]) + bg_ref[...]
    kt_ref[...] = dg(wt16[_Z_CUTS[_K_CUT]:_Z_CUTS[_K_CUT + 1], :], hb) + bkt_ref[...]


def _in_kernel(x_ref, mod_ref, wt_hbm, b_ref, bkt_ref, wg_ref, bg_ref, *refs, layer):
    out_refs, (wt16, stage, wsem) = refs[:_N_IN_OUT], refs[_N_IN_OUT:]

    @pl.when(pl.program_id(0) == 0)
    def _():
        _stage_weight(wt_hbm, layer, wt16, stage, wsem)

    _in_project(x_ref[...], mod_ref, wt16, b_ref, bkt_ref, wg_ref, bg_ref, out_refs)


def _in_gather_kernel(cls_ref, rank_ref, start_ref, x1_ref, modp_ref, g2_ref, b2_ref, y_hbm,
                      mod_ref, wt_hbm, b_ref, bkt_ref, wg_ref, bg_ref, *refs, layer):
    x_out_ref, out_refs = refs[0], refs[1:1 + _N_IN_OUT]
    wt16, stage, wsem, buf_a, buf_b, sems = refs[1 + _N_IN_OUT:]
    i = pl.program_id(0)
    n_steps = pl.num_programs(0)

    def start_row(step, r, buf, sem):
        src = _sorted_row(cls_ref, rank_ref, start_ref, step * TM_IN + r)
        _row_copy(y_hbm, src, buf, r, sem).start()

    def wait_tile(buf, sem):
        pltpu.make_async_copy(y_hbm.at[pl.ds(0, TM_IN), :], buf, sem).wait()

    @pl.when(i == 0)
    def _():
        _stage_weight(wt_hbm, layer, wt16, stage, wsem)

        def first(r, carry):
            start_row(0, r, buf_a, sems.at[0])
            return carry

        lax.fori_loop(0, TM_IN, first, 0, unroll=DMA_UNROLL)

    nxt = jnp.minimum(i + 1, n_steps - 1)

    def step(own, other, own_sem, other_sem):
        wait_tile(own, own_sem)
        for r in range(TM_IN):
            start_row(nxt, r, other, other_sem)
        mp = modp_ref[0]
        x = _ln_plain(ALPHA * x1_ref[...] + mp[5:6] * own[...]) * g2_ref[...] + b2_ref[...]
        x_out_ref[...] = x
        _in_project(x, mod_ref, wt16, b_ref, bkt_ref, wg_ref, bg_ref, out_refs)

        @pl.when(i == n_steps - 1)
        def _():
            wait_tile(other, other_sem)

    @pl.when(i % 2 == 0)
    def _():
        step(buf_a, buf_b, sems.at[0], sems.at[1])

    @pl.when(i % 2 == 1)
    def _():
        step(buf_b, buf_a, sems.at[1], sems.at[0])


def _in_specs(layer, mod_map, fixed):
    return [pl.BlockSpec((1, 6, D_MODEL), mod_map),
            pl.BlockSpec(memory_space=pl.ANY),
            pl.BlockSpec((1, D_MAIN), fixed),
            pl.BlockSpec((D_C, TM_IN), fixed),
            pl.BlockSpec((D_MODEL, LANE), fixed), pl.BlockSpec((1, LANE), fixed)]


def _in_outs(n, row, col):
    widths = [b - a for a, b in _ROW_CUTS] + [LANE]
    return ([pl.BlockSpec((TM_IN, w), row) for w in widths] + [pl.BlockSpec((D_C, TM_IN), col)],
            [jax.ShapeDtypeStruct((n, w), F32) for w in widths] + [jax.ShapeDtypeStruct((D_C, n), F32)])


_IN_SCRATCH = [pltpu.VMEM((D_MAIN, D_MODEL), BF16), pltpu.VMEM((W_CHUNK, D_MODEL), F32),
               pltpu.SemaphoreType.DMA(())]


def _in_call(layer, x, mod_l, w_in_t, b_main, b_kt, w_gate, b_gate, group_of_tile):
    n = x.shape[0]
    row = lambda i: (i, 0)
    fixed = lambda i: (0, 0)
    out_specs, out_shape = _in_outs(n, row, lambda i: (0, i))
    return pl.pallas_call(
        functools.partial(_in_kernel, layer=layer),
        grid=(n // TM_IN,),
        in_specs=[pl.BlockSpec((TM_IN, D_MODEL), row)]
        + _in_specs(layer, lambda i: (group_of_tile(i), 0, 0), fixed),
        out_specs=out_specs,
        out_shape=out_shape,
        scratch_shapes=_IN_SCRATCH,
        compiler_params=_cp(("arbitrary",)),
        name="in_proj",
    )(x, mod_l, w_in_t, b_main, b_kt, w_gate, b_gate)


def _in_gather_call(layer, plan, x1, y_sorted, mod_prev, ln_g, ln_b, mod_l, w_in_t, b_main, b_kt, w_gate, b_gate,
                    group_of_tile):
    n = x1.shape[0]
    row = lambda i, *_: (i, 0)
    fixed = lambda i, *_: (0, 0)
    mod_map = lambda i, *_: (group_of_tile(i), 0, 0)
    out_specs, out_shape = _in_outs(n, row, lambda i, *_: (0, i))
    outs = pl.pallas_call(
        functools.partial(_in_gather_kernel, layer=layer),
        grid_spec=pltpu.PrefetchScalarGridSpec(
            num_scalar_prefetch=3, grid=(n // TM_IN,),
            in_specs=[pl.BlockSpec((TM_IN, D_MODEL), row), pl.BlockSpec((1, 6, D_MODEL), mod_map),
                      pl.BlockSpec((1, D_MODEL), fixed), pl.BlockSpec((1, D_MODEL), fixed),
                      pl.BlockSpec(memory_space=pl.ANY)] + _in_specs(layer, mod_map, fixed),
            out_specs=[pl.BlockSpec((TM_IN, D_MODEL), row)] + out_specs,
            scratch_shapes=_IN_SCRATCH + [pltpu.VMEM((TM_IN, D_MODEL), F32), pltpu.VMEM((TM_IN, D_MODEL), F32),
                                          pltpu.SemaphoreType.DMA((2,))]),
        out_shape=[jax.ShapeDtypeStruct((n, D_MODEL), F32)] + out_shape,
        compiler_params=_cp(("arbitrary",)),
        name="gather_ln2_in_proj",
    )(plan["cls"], plan["rank"], plan["row_start"], x1, mod_prev, ln_g, ln_b, y_sorted,
      mod_l, w_in_t, b_main, b_kt, w_gate, b_gate)
    return outs[0], outs[1:]


def _lru_variant(L, xa_ref, ya_ref, cw_ref, cb_ref, wg_ref, bg_ref, lam_ref, h0_ref, o_ref, st_ref,
                 s_af, s_bf, s_ab, s_bb):
    nch, nseq, ntile = SEG // CH, SEG // L, L // SUB
    lam = lam_ref[...]
    sp = jnp.maximum(-lam, 0.0) + jnp.log1p(jnp.exp(-jnp.abs(lam)))
    cw = cw_ref[...]
    cb = cb_ref[...]
    row = lax.broadcasted_iota(jnp.int32, (CH, 1), 0)
    sub3 = lax.broadcasted_iota(jnp.int32, (1, SUB, 1), 1)

    def gates_and_tile_scan(c, carry):
        start = pl.multiple_of(c * CH, CH)
        xcat, main = _halo_rows(xa_ref, start, CH)
        tpos = (start + row) & (L - 1)
        xm2 = jnp.where(tpos >= 2, xcat[SUB - 2:SUB - 2 + CH], 0.0)
        xm1 = jnp.where(tpos >= 1, xcat[SUB - 1:SUB - 1 + CH], 0.0)
        xp1 = jnp.where(tpos <= L - 2, xcat[SUB + 1:SUB + 1 + CH], 0.0)
        xc = cw[0:1] * xm2 + cw[1:2] * xm1 + cw[2:3] * main + cw[3:4] * xp1 + cb
        g = _dot(xc.astype(BF16), wg_ref[...]) + bg_ref[...]
        for d, (sa, sb) in enumerate(((s_af, s_bf), (s_ab, s_bb))):
            r = _sigmoid(g[:, 2 * d * D_A:(2 * d + 1) * D_A])
            ig = _sigmoid(g[:, (2 * d + 1) * D_A:(2 * d + 2) * D_A])
            a = jnp.exp(-LRU_C * r * sp[d:d + 1])
            b = jnp.sqrt(1.0 - a * a) * (ig * xc)
            a3, b3 = a.reshape(CH // SUB, SUB, D_A), b.reshape(CH // SUB, SUB, D_A)
            for s in (1, 2, 4):
                shift, keep = (s, sub3 >= s) if d == 0 else (SUB - s, sub3 < SUB - s)
                b3 = a3 * jnp.where(keep, pltpu.roll(b3, shift, 1), 0.0) + b3
                a3 = a3 * jnp.where(keep, pltpu.roll(a3, shift, 1), 1.0)
            sa[pl.ds(start, CH), :] = a3.reshape(CH, D_A)
            sb[pl.ds(start, CH), :] = b3.reshape(CH, D_A)
        return carry

    lax.fori_loop(0, nch, gates_and_tile_scan, 0)

    def carry_tiles(k, carry):
        cf, cbk = carry
        nf, nb = [], []
        for s in range(nseq):
            rf = pl.multiple_of(s * L + k * SUB, SUB)
            hf = s_af[pl.ds(rf, SUB), :] * cf[s] + s_bf[pl.ds(rf, SUB), :]
            s_bf[pl.ds(rf, SUB), :] = hf
            nf.append(hf[SUB - 1:SUB, :])
            rb = pl.multiple_of(s * L + (ntile - 1 - k) * SUB, SUB)
            hb = s_ab[pl.ds(rb, SUB), :] * cbk[s] + s_bb[pl.ds(rb, SUB), :]
            s_bb[pl.ds(rb, SUB), :] = hb
            nb.append(hb[0:1, :])
        return tuple(nf), tuple(nb)

    cf0 = tuple(h0_ref[0, s, 0:1, :] for s in range(nseq))
    cb0 = tuple(h0_ref[0, s, 1:2, :] for s in range(nseq))
    cf, cbk = lax.fori_loop(0, ntile, carry_tiles, (cf0, cb0))

    st_ref[...] = jnp.zeros(st_ref.shape, F32)
    for s in range(nseq):
        st_ref[0, s] = jnp.concatenate([cf[s], cbk[s]], axis=0)

    def finish(c, carry):
        start = pl.multiple_of(c * CH, CH)
        h = s_bf[pl.ds(start, CH), :] + s_bb[pl.ds(start, CH), :]
        o_ref[pl.ds(start, CH), :] = _rms(_gelu_tanh(ya_ref[pl.ds(start, CH), :]) * h)
        return carry

    lax.fori_loop(0, nch, finish, 0)


def _lru_kernel(*refs, l_ctx, l_lat, n_ctx_blk):
    i = pl.program_id(0)

    @pl.when(i < n_ctx_blk)
    def _():
        _lru_variant(l_ctx, *refs)

    @pl.when(i >= n_ctx_blk)
    def _():
        _lru_variant(l_lat, *refs)


def _lru_call(xa, ya, conv_w, conv_b, w_gate, b_gate, lam, h0_all, l_ctx, l_lat, n_ctx_blk):
    n = xa.shape[0]
    nblk = n // SEG
    row = lambda i: (i, 0)
    fixed = lambda i: (0, 0)
    slots = SEG // l_ctx
    return pl.pallas_call(
        functools.partial(_lru_kernel, l_ctx=l_ctx, l_lat=l_lat, n_ctx_blk=n_ctx_blk),
        grid=(nblk,),
        in_specs=[pl.BlockSpec((SEG, D_A), row), pl.BlockSpec((SEG, D_A), row),
                  pl.BlockSpec((4, D_A), fixed), pl.BlockSpec((1, D_A), fixed),
                  pl.BlockSpec((D_A, 4 * D_A), fixed), pl.BlockSpec((1, 4 * D_A), fixed),
                  pl.BlockSpec((2, D_A), fixed),
                  pl.BlockSpec((1, slots, 2, D_A), lambda i: (i, 0, 0, 0))],
        out_specs=[pl.BlockSpec((SEG, D_A), row),
                   pl.BlockSpec((1, slots, 2, D_A), lambda i: (i, 0, 0, 0))],
        out_shape=[jax.ShapeDtypeStruct((n, D_A), F32),
                   jax.ShapeDtypeStruct((nblk, slots, 2, D_A), F32)],
        scratch_shapes=[pltpu.VMEM((SEG, D_A), F32) for _ in range(4)],
        compiler_params=_cp(("parallel",)),
        name="rglru",
    )(xa, ya, conv_w, conv_b, w_gate, b_gate, lam, h0_all)


def _filt_kernel(z_ref, dec_ref, w1_ref, b1_ref, w2_ref, b2_ref, fr_ref, w3_ref, fwd_ref,
                 oab_ref, od0_ref, s_k, s_kf, *, L):
    nblk = 2 * L // CH
    d_idx = pl.program_id(1)
    row = lax.broadcasted_iota(jnp.int32, (CH, 1), 0)

    @pl.when(d_idx == 0)
    def _():
        fr = fr_ref[0]

        def taps(c, carry):
            start = pl.multiple_of(c * CH, CH)
            h1 = jnp.sin(fr * (_dot3(z_ref[pl.ds(start, CH), :], w1_ref[0]) + b1_ref[0]))
            h2 = jnp.sin(fr * (_dot3(h1, w2_ref[0]) + b2_ref[0]))
            t = _dot3(h2, w3_ref[0])
            dec = dec_ref[pl.ds(start, CH), :]
            rg = start + row
            for o in range(HY_ORDER):
                fwd_t = t[:, (2 * o) * D_B:(2 * o + 1) * D_B]
                bwd_t = t[:, (2 * o + 1) * D_B:(2 * o + 2) * D_B]
                ko = jnp.where(rg < L, bwd_t, fwd_t) * dec
                s_k[pl.ds(start, CH), o * D_B:(o + 1) * D_B] = jnp.where(rg == 0, 0.0, ko)
            return carry

        lax.fori_loop(0, nblk, taps, 0)
        fwd = fwd_ref[...]

        def spectra(e, carry):
            start = pl.multiple_of(e * CH, CH)
            s_kf[e] = _dot3(fwd, s_k[pl.ds(start, CH), :])
            return carry

        lax.fori_loop(0, nblk, spectra, 0)

    kd = s_kf[d_idx + 1]
    km = s_kf[d_idx]
    k0 = s_k[pl.ds(pl.multiple_of(d_idx * CH, CH), 1), :]
    sgn = jnp.where((row & 1) == 0, 1.0, -1.0)
    a = kd[:CH] + sgn * (km[:CH] - k0)
    b = jnp.where(row == 0, 0.0, kd[CH:] + sgn * km[CH:])
    hn = kd[CH:CH + 1] + km[CH:CH + 1] - k0
    for o in range(HY_ORDER):
        oab_ref[0, o, 0, 0] = a[:, o * D_B:(o + 1) * D_B]
        oab_ref[0, o, 0, 1] = b[:, o * D_B:(o + 1) * D_B]
        od0_ref[0, o, 0] = jnp.broadcast_to(hn[:, o * D_B:(o + 1) * D_B], (SUB, D_B))


def _filt_call(L, z, dec, w1, b1, w2, b2, fr, w3, fwd32):
    nd = 2 * (L // CH) - 1
    fixed = lambda l, d: (0, 0)
    lay3 = lambda l, d: (l, 0, 0)
    return pl.pallas_call(
        functools.partial(_filt_kernel, L=L),
        grid=(DEPTH, nd),
        in_specs=[pl.BlockSpec((2 * L, LANE), fixed), pl.BlockSpec((2 * L, D_B), fixed),
                  pl.BlockSpec((1, LANE, LANE), lay3), pl.BlockSpec((1, 1, LANE), lay3),
                  pl.BlockSpec((1, LANE, LANE), lay3), pl.BlockSpec((1, 1, LANE), lay3),
                  pl.BlockSpec((1, 1, LANE), lay3),
                  pl.BlockSpec((1, LANE, HY_ORDER * 2 * D_B), lay3),
                  pl.BlockSpec((2 * CH, CH), fixed)],
        out_specs=[pl.BlockSpec((1, HY_ORDER, 1, 2, CH, D_B), lambda l, d: (l, 0, d, 0, 0, 0)),
                   pl.BlockSpec((1, HY_ORDER, 1, SUB, D_B), lambda l, d: (l, 0, d, 0, 0))],
        out_shape=[jax.ShapeDtypeStruct((DEPTH, HY_ORDER, nd, 2, CH, D_B), F32),
                   jax.ShapeDtypeStruct((DEPTH, HY_ORDER, nd, SUB, D_B), F32)],
        scratch_shapes=[pltpu.VMEM((2 * L, HY_ORDER * D_B), F32),
                        pltpu.VMEM((2 * L // CH, 2 * CH, HY_ORDER * D_B), F32)],
        compiler_params=_cp(("parallel", "arbitrary")),
        name=f"hyena_filter_{L}",
    )(z, dec, w1, b1, w2, b2, fr, w3, fwd32)


def _hy_variant(L, o_idx, hy_ref, cw_ref, cb_ref, fwd_ref, inv_ref, hab_ref, hd0_ref, bias_ref, o_ref,
                s_y, s_x, s_u, s_v):
    nch, nseq, P = SEG // CH, SEG // L, L // CH
    row = lax.broadcasted_iota(jnp.int32, (CH, 1), 0)
    frow = lax.broadcasted_iota(jnp.int32, (FS, 1), 0)

    @pl.when(o_idx == 0)
    def _():
        cw = cw_ref[...]
        cb = cb_ref[...]

        def short_conv(c, carry):
            start = pl.multiple_of(c * CH, CH)
            xcat, main = _halo_rows(hy_ref, start, CH)
            tpos = (start + row) & (L - 1)
            xm1 = jnp.where(tpos >= 1, xcat[SUB - 1:SUB - 1 + CH], 0.0)
            xp1 = jnp.where(tpos <= L - 2, xcat[SUB + 1:SUB + 1 + CH], 0.0)
            hc = cw[0:1] * xm1 + cw[1:2] * main + cw[2:3] * xp1 + cb
            s_y[pl.ds(start, CH), :] = hc[:, :D_B]
            s_x[0, pl.ds(start, CH), :] = hc[:, D_B:2 * D_B]
            s_x[1, pl.ds(start, CH), :] = hc[:, 2 * D_B:]
            return carry

        lax.fori_loop(0, nch, short_conv, 0)

    bias = bias_ref[0]

    def loop(n, body, init):
        return body(0, init) if n == 1 else lax.fori_loop(0, n, body, init)

    def one_sequence(s, slot):
        base = s * L
        u0 = slot * P

        def forward_dft(j, cc):
            r = pl.multiple_of(base + j * CH, CH)
            s_u[u0 + j] = _dot(fwd_ref[...], s_y[pl.ds(r, CH), :].astype(BF16))
            return cc

        loop(P, forward_dft, 0)

        def output_block(i, cc):
            for fs in range(CH // FS):
                lo = fs * FS

                def accumulate(j, acc):
                    yre, yim = acc
                    d = i - j + (P - 1)
                    ure = s_u[u0 + j, lo:lo + FS, :]
                    uim = s_u[u0 + j, CH + lo:CH + lo + FS, :]
                    a = hab_ref[0, 0, d, 0, lo:lo + FS, :]
                    b = hab_ref[0, 0, d, 1, lo:lo + FS, :]
                    dd = jnp.where(frow == 0, hd0_ref[0, 0, d, 0:1, :], a) if fs == 0 else a
                    return yre + ure * a - uim * b, yim + ure * b + uim * dd

                zero = jnp.zeros((FS, D_B), F32)
                yre, yim = loop(P, accumulate, (zero, zero))
                s_v[slot, lo:lo + FS, :] = yre.astype(BF16)
                s_v[slot, CH + lo:CH + lo + FS, :] = yim.astype(BF16)
            yc = _dot(inv_ref[...], s_v[slot])
            r = pl.multiple_of(base + i * CH, CH)
            s_y[pl.ds(r, CH), :] = s_x[o_idx, pl.ds(r, CH), :] * (yc + s_y[pl.ds(r, CH), :] * bias)
            return cc

        loop(P, output_block, 0)

    def single_block_pair(p, carry):
        rows = [pl.multiple_of((2 * p + k) * L, CH) for k in range(2)]
        ys = [s_y[pl.ds(r, CH), :] for r in rows]
        gates = [s_x[o_idx, pl.ds(r, CH), :] for r in rows]
        a = hab_ref[0, 0, 0, 0]
        b = hab_ref[0, 0, 0, 1]
        dd = jnp.where(row == 0, hd0_ref[0, 0, 0, 0:1, :], a)
        outs = []
        for y, gate in zip(ys, gates):
            u = _dot(fwd_ref[...], y.astype(BF16))
            ure, uim = u[:CH], u[CH:]
            v = jnp.concatenate([ure * a - uim * b, ure * b + uim * dd], axis=0).astype(BF16)
            outs.append(gate * (_dot(inv_ref[...], v) + y * bias))
        for r, out in zip(rows, outs):
            s_y[pl.ds(r, CH), :] = out
        return carry

    if P == 1 and nseq % 2 == 0:
        lax.fori_loop(0, nseq // 2, single_block_pair, 0)
    else:
        lax.fori_loop(0, nseq, lambda s, carry: (one_sequence(s, 0), carry)[1], 0)

    @pl.when(o_idx == HY_ORDER - 1)
    def _():
        def finish(c, carry):
            start = pl.multiple_of(c * CH, CH)
            o_ref[pl.ds(start, CH), :] = _rms(s_y[pl.ds(start, CH), :])
            return carry

        lax.fori_loop(0, nch, finish, 0)


def _hy_kernel(hy_ref, cw_ref, cb_ref, fwd_ref, inv_ref, habc_ref, hd0c_ref, habl_ref, hd0l_ref, bias_ref,
               o_ref, s_y, s_x, s_u, s_v, *, l_ctx, l_lat, n_ctx_blk):
    i = pl.program_id(0)
    o_idx = pl.program_id(1)
    scratch = (s_y, s_x, s_u, s_v)

    @pl.when(i < n_ctx_blk)
    def _():
        _hy_variant(l_ctx, o_idx, hy_ref, cw_ref, cb_ref, fwd_ref, inv_ref, habc_ref, hd0c_ref, bias_ref, o_ref,
                    *scratch)

    @pl.when(i >= n_ctx_blk)
    def _():
        _hy_variant(l_lat, o_idx, hy_ref, cw_ref, cb_ref, fwd_ref, inv_ref, habl_ref, hd0l_ref, bias_ref, o_ref,
                    *scratch)


def _hy_call(layer, hyb, conv_w, conv_b, fwd, inv, habc, hd0c, habl, hd0l, bias, l_ctx, l_lat, n_ctx_blk):
    n = hyb.shape[0]
    ndc, ndl = habc.shape[2], habl.shape[2]
    pmax = max(l_ctx, l_lat) // CH
    row = lambda i, o: (i, 0)
    fixed = lambda i, o: (0, 0)
    lat_o = lambda i, o: jnp.where(i >= n_ctx_blk, o, 0)
    ctx_o = lambda i, o: jnp.where(i < n_ctx_blk, o, 0)
    return pl.pallas_call(
        functools.partial(_hy_kernel, l_ctx=l_ctx, l_lat=l_lat, n_ctx_blk=n_ctx_blk),
        grid=(n // SEG, HY_ORDER),
        in_specs=[pl.BlockSpec((SEG, 3 * D_B), row),
                  pl.BlockSpec((3, 3 * D_B), fixed), pl.BlockSpec((1, 3 * D_B), fixed),
                  pl.BlockSpec((2 * CH, CH), fixed), pl.BlockSpec((CH, 2 * CH), fixed),
                  pl.BlockSpec((1, 1, ndc, 2, CH, D_B), lambda i, o: (layer, ctx_o(i, o), 0, 0, 0, 0)),
                  pl.BlockSpec((1, 1, ndc, SUB, D_B), lambda i, o: (layer, ctx_o(i, o), 0, 0, 0)),
                  pl.BlockSpec((1, 1, ndl, 2, CH, D_B), lambda i, o: (layer, lat_o(i, o), 0, 0, 0, 0)),
                  pl.BlockSpec((1, 1, ndl, SUB, D_B), lambda i, o: (layer, lat_o(i, o), 0, 0, 0)),
                  pl.BlockSpec((1, 1, D_B), lambda i, o: (o, 0, 0))],
        out_specs=pl.BlockSpec((SEG, D_B), row),
        out_shape=jax.ShapeDtypeStruct((n, D_B), F32),
        scratch_shapes=[pltpu.VMEM((SEG, D_B), F32), pltpu.VMEM((HY_ORDER, SEG, D_B), F32),
                        pltpu.VMEM((max(pmax, 2), 2 * CH, D_B), F32), pltpu.VMEM((2, 2 * CH, D_B), BF16)],
        compiler_params=_cp(("parallel", "arbitrary")),
        name="hyena",
    )(hyb, conv_w, conv_b, fwd, inv, habc, hd0c, habl, hd0l, bias)


def _row_scan(x, op, fill, reverse):
    t, width = x.shape
    n_tiles = t // SUB
    sub = lax.broadcasted_iota(jnp.int32, (1, SUB, 1), 1)
    x3 = x.reshape(n_tiles, SUB, width)
    for s in (1, 2, 4):
        shift, keep = (SUB - s, sub < SUB - s) if reverse else (s, sub >= s)
        x3 = op(x3, jnp.where(keep, pltpu.roll(x3, shift, 1), fill))
    x = x3.reshape(t, width)
    out = [None] * n_tiles
    carry = None
    for i in (reversed(range(n_tiles)) if reverse else range(n_tiles)):
        tile = x[i * SUB:(i + 1) * SUB]
        out[i] = tile if carry is None else op(tile, carry)
        carry = out[i][0:1] if reverse else out[i][SUB - 1:SUB]
    return jnp.concatenate(out, axis=0)


_STK_ONE = 3 * SUB


def _mlstm_prep(d, g_ref, m_old):
    T = CH
    reverse = d == 1
    g = g_ref[...]
    if d == 1:
        g = pltpu.roll(g, LANE - 2 * H_C, 1)
    lane = lax.broadcasted_iota(jnp.int32, (1, LANE), 1)
    head = lane < H_C
    b = pltpu.roll(_row_scan(_log_sigmoid(g), jnp.add, 0.0, reverse), LANE - H_C, 1)
    r = jnp.where(head, g - b, 0.0)
    big_m = jnp.maximum(m_old, _row_scan(r, jnp.maximum, -jnp.inf, reverse))
    last = 0 if reverse else T - 1
    m_last = big_m[last:last + 1, :]
    low = lane < SUB
    p0, p1, p2 = (jnp.where(low, p.astype(F32), 0.0) for p in _split3(-big_m))
    cols = (p0 + pltpu.roll(p1, SUB, 1) + pltpu.roll(p2, 2 * SUB, 1)
            + jnp.where(jnp.logical_and(lane >= _STK_ONE, lane < _STK_ONE + SUB), 1.0, 0.0))
    rowid = lax.broadcasted_iota(jnp.int32, (SUB, 1), 0)
    r8 = r.T[0:SUB, :]
    m_last8 = sum(jnp.where(rowid == h, m_last[:, h:h + 1], 0.0) for h in range(H_C))
    ws8 = jnp.where(rowid < H_C, jnp.exp(r8 - m_last8), 0.0)
    return {"r3": [p.astype(F32) for p in _split3(r8)], "ws8": ws8, "wc": jnp.exp(m_old - m_last),
            "m_new": jnp.where(head, b[last:last + 1, :] + m_last, 0.0),
            "wi": jnp.exp(m_old - big_m), "e": jnp.exp(-(b + big_m)), "cols_b": cols.astype(BF16)}


def _mlstm_variant(carry, qf, vf, ktf, gf, qb_, vb_, ktb_, gb_, hf_ref, hb_ref, co_ref, no_ref, mo_ref,
                   s_cx, s_n, s_m):
    T = CH
    ii = lax.broadcasted_iota(jnp.int32, (T, T), 0)
    jj = lax.broadcasted_iota(jnp.int32, (T, T), 1)
    rowid = lax.broadcasted_iota(jnp.int32, (SUB, 1), 0)
    one_col = jnp.where(lax.broadcasted_iota(jnp.int32, (T, DK), 1) == 0, 1.0, 0.0).astype(BF16)
    prep = []
    for d, g_ref in ((0, gf), (1, gb_ if carry else gf)):
        m_old = s_m[d:d + 1, :] if carry else jnp.zeros((1, LANE), F32)
        prep.append(_mlstm_prep(d, g_ref, m_old))
    refs = ((qf, vf, ktf, hf_ref), (qb_, vb_, ktb_, hb_ref))
    for h in range(H_C):
        sl = slice(h * DK, (h + 1) * DK)
        for d in range(2):
            p = prep[d]
            q_ref, v_ref, kt_ref, h_ref = refs[d]
            idx = d * H_C + h
            if carry or d == 0:
                qb = (q_ref[:, sl] * (DK ** -0.5)).astype(BF16)
                kt = kt_ref[sl, :]
                ktb = kt.astype(BF16)
                v_ext = jnp.concatenate([v_ref[:, sl].astype(BF16), one_col], axis=1)
                s_raw = _dot(qb, ktb)
            tri = (jj >= ii) if d == 1 else (jj <= ii)
            sel = jnp.broadcast_to(jnp.where(rowid == h, 1.0, 0.0), (SUB, T))
            rr = sum(jnp.where(rowid == i, piece[h:h + 1, :], 0.0) for i, piece in enumerate(p["r3"]))
            rmat = jnp.concatenate([sel, sel, sel, rr, jnp.zeros((LANE - 4 * SUB, T), F32)], axis=0)
            expo = _dot(p["cols_b"], rmat.astype(BF16))
            s = s_raw * jnp.exp(jnp.where(tri, expo, -jnp.inf))
            intra = _dot(s.astype(BF16), v_ext)
            num, den = intra[:, :DK], intra[:, DK:DK + 1]
            if carry:
                cx = s_cx[idx]
                inter = _dot(qb, cx.astype(BF16))
                wi = p["wi"][:, h:h + 1]
                num, den = num + wi * inter[:, :DK], den + wi * inter[:, DK:DK + 1]
            h_ref[:, sl] = num / jnp.maximum(jnp.abs(den), p["e"][:, h:h + 1])
            upd = _dot((kt * p["ws8"][h:h + 1, :]).astype(BF16), v_ext)
            n_upd = lax.dot_general(p["ws8"].astype(BF16), ktb, _NT, preferred_element_type=F32)[h:h + 1, :]
            if carry:
                wc = p["wc"][:, h:h + 1]
                s_cx[idx] = wc * cx + upd
                s_n[idx:idx + 1, :] = wc * s_n[idx:idx + 1, :] + n_upd
            else:
                co_ref[0, idx] = upd[:, :DK]
                no_ref[0, idx:idx + 1, :] = n_upd
    m_rows = jnp.concatenate([prep[0]["m_new"], prep[1]["m_new"], jnp.zeros((SUB - 2, LANE), F32)], axis=0)
    if carry:
        s_m[...] = m_rows
    else:
        mo_ref[0] = m_rows


def _mlstm_kernel(*refs, n_ctx_steps, nc_lat):
    cx0_ref, n0_ref, m0_ref = refs[8:11]
    s_cx, s_n, s_m = refs[-3:]
    data = refs[:8] + refs[11:]
    t = pl.program_id(0)
    is_ctx = t < n_ctx_steps

    @pl.when(is_ctx)
    def _():
        _mlstm_variant(False, *data)

    @pl.when(jnp.logical_not(is_ctx))
    def _():
        @pl.when((t - n_ctx_steps) % nc_lat == 0)
        def _():
            s_cx[...] = cx0_ref[0]
            s_n[...] = n0_ref[0]
            s_m[...] = m0_ref[0]

        _mlstm_variant(True, *data)


def _mlstm_call(q, v, kt, gates, cx0, n0, m0, n_ctx_steps, nc_lat):
    n = q.shape[0]
    steps = n // CH
    nst = 2 * H_C

    def bwd_blk(t):
        r = jnp.maximum(t - n_ctx_steps, 0)
        return n_ctx_steps + (r // nc_lat) * nc_lat + (nc_lat - 1 - r % nc_lat)

    out_bwd = lambda t: jnp.where(t < n_ctx_steps, t, bwd_blk(t))
    lat_b = lambda t: jnp.maximum(t - n_ctx_steps, 0) // nc_lat
    ctx_b = lambda t: jnp.minimum(t, n_ctx_steps - 1)
    rows = lambda w, blk: pl.BlockSpec((CH, w), lambda t: (blk(t), 0))
    cols = lambda h, blk: pl.BlockSpec((h, CH), lambda t: (0, blk(t)))
    ident = lambda t: t
    return pl.pallas_call(
        functools.partial(_mlstm_kernel, n_ctx_steps=n_ctx_steps, nc_lat=nc_lat),
        grid=(steps,),
        in_specs=[rows(D_C, ident), rows(D_C, ident), cols(D_C, ident), rows(LANE, ident),
                  rows(D_C, bwd_blk), rows(D_C, bwd_blk), cols(D_C, bwd_blk), rows(LANE, bwd_blk),
                  pl.BlockSpec((1, nst, DK, 2 * DK), lambda t: (lat_b(t), 0, 0, 0)),
                  pl.BlockSpec((1, nst, DK), lambda t: (lat_b(t), 0, 0)),
                  pl.BlockSpec((1, SUB, LANE), lambda t: (lat_b(t), 0, 0))],
        out_specs=[rows(D_C, ident), rows(D_C, out_bwd),
                   pl.BlockSpec((1, nst, DK, DK), lambda t: (ctx_b(t), 0, 0, 0)),
                   pl.BlockSpec((1, nst, DK), lambda t: (ctx_b(t), 0, 0)),
                   pl.BlockSpec((1, SUB, LANE), lambda t: (ctx_b(t), 0, 0))],
        out_shape=[jax.ShapeDtypeStruct((n, D_C), F32), jax.ShapeDtypeStruct((n, D_C), F32),
                   jax.ShapeDtypeStruct((n_ctx_steps, nst, DK, DK), F32),
                   jax.ShapeDtypeStruct((n_ctx_steps, nst, DK), F32),
                   jax.ShapeDtypeStruct((n_ctx_steps, SUB, LANE), F32)],
        scratch_shapes=[pltpu.VMEM((nst, DK, 2 * DK), F32), pltpu.VMEM((nst, DK), F32),
                        pltpu.VMEM((SUB, LANE), F32)],
        compiler_params=_cp(("arbitrary",)),
        name="mlstm",
    )(q, v, kt, gates, q, v, kt, gates, cx0, n0, m0)


def _out_kernel(x_ref, oa_ref, ob_ref, hf_ref, hb_ref, og_ref, mod_ref, mg_ref, w_ref, g_ref, b_ref,
                rw_ref, rb_ref, x1_ref, he_ref, cnt_ref, it_ref, s_cnt):
    i = pl.program_id(0)

    @pl.when(i == 0)
    def _():
        s_cnt[...] = jnp.zeros(s_cnt.shape, F32)

    m = mod_ref[0]
    mg = mg_ref[...]
    acc = _dot((oa_ref[...] * mg[:, :D_A]).astype(BF16), w_ref[0:D_A, :])
    acc += _dot((ob_ref[...] * mg[:, D_A:D_A + D_B]).astype(BF16), w_ref[D_A:D_A + D_B, :])
    hc = hf_ref[...] + hb_ref[...]
    og = og_ref[...]
    off = D_A + D_B
    for h in range(H_C):
        sl = slice(h * DK, (h + 1) * DK)
        oc = _sigmoid(og[:, sl]) * _rms(hc[:, sl]) * mg[:, off + h * DK:off + (h + 1) * DK]
        acc += _dot(oc.astype(BF16), w_ref[off + h * DK:off + (h + 1) * DK, :])
    x1 = _ln_plain(ALPHA * x_ref[...] + m[2:3] * acc) * g_ref[...] + b_ref[...]
    x1_ref[...] = x1
    h2 = x1 * (1.0 + m[4:5]) + m[3:4]
    he_ref[:, :D_MODEL] = h2

    lg = _dot3(h2, rw_ref[...]) + rb_ref[...]
    col = lax.broadcasted_iota(jnp.int32, lg.shape, 1)
    ninf = -jnp.inf
    lgm = jnp.where(col < N_GROUPS, lg, ninf)
    mx = jnp.max(lgm, -1, keepdims=True)
    gi = jnp.min(jnp.where(lgm == mx, col, LANE), -1, keepdims=True)
    pg_top = 1.0 / jnp.sum(jnp.where(col < N_GROUPS, jnp.exp(lg - mx), 0.0), -1, keepdims=True)
    lo4 = N_GROUPS + E_PER_GROUP * gi
    lem = jnp.where(jnp.logical_and(col >= lo4, col < lo4 + E_PER_GROUP), lg, ninf)
    v1 = jnp.max(lem, -1, keepdims=True)
    i1 = jnp.min(jnp.where(lem == v1, col, LANE), -1, keepdims=True)
    lem2 = jnp.where(col == i1, ninf, lem)
    v2 = jnp.max(lem2, -1, keepdims=True)
    i2 = jnp.min(jnp.where(lem2 == v2, col, LANE), -1, keepdims=True)
    e21 = jnp.exp(v2 - v1)
    w1 = pg_top / (1.0 + e21)
    w2 = pg_top * e21 / (1.0 + e21)
    e1, e2 = i1 - N_GROUPS, i2 - N_GROUPS
    first_lo = e1 < e2
    elo, ehi = jnp.minimum(e1, e2), jnp.maximum(e1, e2)
    wlo, whi = jnp.where(first_lo, w1, w2), jnp.where(first_lo, w2, w1)
    llo, lhi = elo - E_PER_GROUP * gi, ehi - E_PER_GROUP * gi
    cls = gi * N_PAIRS + ((llo * (7 - llo)) >> 1) + lhi - llo - 1

    oh = jnp.where(col == cls, 1.0, 0.0)
    ii = lax.broadcasted_iota(jnp.int32, (TM_OUT, TM_OUT), 0)
    jj = lax.broadcasted_iota(jnp.int32, (TM_OUT, TM_OUT), 1)
    before = jnp.where(jj < ii, 1.0, 0.0).astype(BF16)
    cnt = s_cnt[0:1, :]
    rank = jnp.sum(oh * (_dot(before, oh.astype(BF16)) + cnt), -1, keepdims=True)
    cnt = cnt + jnp.sum(oh, 0, keepdims=True)
    s_cnt[...] = jnp.broadcast_to(cnt, s_cnt.shape)
    cnt_ref[...] = jnp.broadcast_to(cnt, cnt_ref.shape)

    info = jnp.zeros(lg.shape, F32)
    for c, val in ((I_CLS, cls.astype(F32)), (I_ELO, elo.astype(F32)), (I_EHI, ehi.astype(F32)),
                   (I_WLO, wlo), (I_WHI, whi), (I_RANK, rank)):
        info = jnp.where(col == c, val, info)
    he_ref[:, D_MODEL:] = info
    it_ref[...] = info.T[:SUB, :]


def _out_call(x, out_a, out_b, hcf, hcb, ogate, mod_l, mix_g, w_out, ln_g, ln_b, rt_w, rt_b, group_of_tile):
    n = x.shape[0]
    row = lambda i: (i, 0)
    fixed = lambda i: (0, 0)
    return pl.pallas_call(
        _out_kernel,
        grid=(n // TM_OUT,),
        in_specs=[pl.BlockSpec((TM_OUT, D_MODEL), row), pl.BlockSpec((TM_OUT, D_A), row),
                  pl.BlockSpec((TM_OUT, D_B), row),
                  pl.BlockSpec((TM_OUT, D_C), row), pl.BlockSpec((TM_OUT, D_C), row), pl.BlockSpec((TM_OUT, D_C), row),
                  pl.BlockSpec((1, 6, D_MODEL), lambda i: (group_of_tile(i), 0, 0)),
                  pl.BlockSpec((1, D_MODEL), fixed), pl.BlockSpec((D_MODEL, D_MODEL), fixed),
                  pl.BlockSpec((1, D_MODEL), fixed), pl.BlockSpec((1, D_MODEL), fixed),
                  pl.BlockSpec((D_MODEL, LANE), fixed), pl.BlockSpec((1, LANE), fixed)],
        out_specs=[pl.BlockSpec((TM_OUT, D_MODEL), row), pl.BlockSpec((TM_OUT, D_EXT), row),
                   pl.BlockSpec((SUB, LANE), fixed), pl.BlockSpec((SUB, TM_OUT), lambda i: (0, i))],
        out_shape=[jax.ShapeDtypeStruct((n, D_MODEL), F32), jax.ShapeDtypeStruct((n, D_EXT), F32),
                   jax.ShapeDtypeStruct((SUB, LANE), F32), jax.ShapeDtypeStruct((SUB, n), F32)],
        scratch_shapes=[pltpu.VMEM((SUB, LANE), F32)],
        compiler_params=_cp(("arbitrary",)),
        name="out_proj_router",
    )(x, out_a, out_b, hcf, hcb, ogate, mod_l, mix_g, w_out, ln_g, ln_b, rt_w, rt_b)


def _row_copy(src_ref, src_row, dst_ref, dst_row, sem):
    return pltpu.make_async_copy(src_ref.at[pl.ds(src_row, 1), :], dst_ref.at[pl.ds(dst_row, 1), :], sem)


def _sorted_row(cls_ref, rank_ref, start_ref, t):
    return start_ref[cls_ref[t]] + rank_ref[t]


def _scatter_kernel(cls_ref, rank_ref, start_ref, pad_ref, na_ref, x_ref, o_ref, ring, z_ref, sem_z, load_sems,
                    row_sems):
    n_tiles = o_ref.shape[0] // TM

    def zero_tile(row):
        return pltpu.make_async_copy(z_ref, o_ref.at[pl.ds(pl.multiple_of(row, TM), TM), :], sem_z)

    def for_zero_tiles(fn):
        def per_class(c, carry):
            row = pad_ref[c]

            @pl.when(row >= 0)
            def _():
                fn(zero_tile(row))

            return carry

        lax.fori_loop(0, N_CLASS, per_class, 0)

        def per_idle(t, carry):
            fn(zero_tile(t * TM))
            return carry

        lax.fori_loop(na_ref[0], n_tiles, per_idle, 0)

    i = pl.program_id(0)
    n_steps = x_ref.shape[0] // TM

    def load(step):
        return pltpu.make_async_copy(x_ref.at[pl.ds(pl.multiple_of(step * TM, TM), TM), :],
                                     ring.at[step % RING], load_sems.at[step % RING])

    def start_rows(step):
        slot = step % RING

        def body(r, carry):
            dst = _sorted_row(cls_ref, rank_ref, start_ref, step * TM + r)
            _row_copy(ring.at[slot], r, o_ref, dst, row_sems.at[slot]).start()
            return carry

        lax.fori_loop(0, TM, body, 0, unroll=DMA_UNROLL)

    def wait_rows(step):
        slot = step % RING
        pltpu.make_async_copy(ring.at[slot], o_ref.at[pl.ds(0, TM), :], row_sems.at[slot]).wait()

    @pl.when(i == 0)
    def _():
        z_ref[...] = jnp.zeros(z_ref.shape, F32)
        for_zero_tiles(lambda cp: cp.start())
        for_zero_tiles(lambda cp: cp.wait())
        load(0).start()

    @pl.when(i >= RING - 1)
    def _():
        wait_rows(i - (RING - 1))

    @pl.when(i + 1 < n_steps)
    def _():
        load(i + 1).start()

    load(i).wait()
    start_rows(i)

    @pl.when(i == n_steps - 1)
    def _():
        for back in range(min(RING - 1, n_steps) - 1, -1, -1):
            wait_rows(i - back)


def _scatter_call(plan, h_ext, n_tiles_max):
    any_spec = pl.BlockSpec(memory_space=pl.ANY)
    return pl.pallas_call(
        _scatter_kernel,
        grid_spec=pltpu.PrefetchScalarGridSpec(
            num_scalar_prefetch=5, grid=(h_ext.shape[0] // TM,),
            in_specs=[any_spec],
            out_specs=any_spec,
            scratch_shapes=[pltpu.VMEM((RING, TM, D_EXT), F32), pltpu.VMEM((TM, D_EXT), F32),
                            pltpu.SemaphoreType.DMA(()),
                            pltpu.SemaphoreType.DMA((RING,)), pltpu.SemaphoreType.DMA((RING,))]),
        out_shape=jax.ShapeDtypeStruct((n_tiles_max * TM, D_EXT), F32),
        compiler_params=_cp(("arbitrary",)),
        name="moe_scatter",
    )(plan["cls"], plan["rank"], plan["row_start"], plan["pad_rows"], plan["n_act"], h_ext)


def _moe_kernel(tg_ref, lo_ref, hi_ref, cg_ref, nv_ref, na_ref, x_ref, w1_ref, w3_ref, w2_ref, o_ref, s1, s3, s2):
    del tg_ref
    t = pl.program_id(0)
    active = t < na_ref[0]
    half = TM // 2

    @pl.when(jnp.logical_not(active))
    def _():
        o_ref[...] = jnp.zeros(o_ref.shape, F32)

    @pl.when(jnp.logical_and(active, cg_ref[t] == 1))
    def _():
        for e in range(E_PER_GROUP):
            s1[e] = w1_ref[e].astype(BF16)
            s3[e] = w3_ref[e].astype(BF16)
            s2[e] = w2_ref[e].astype(BF16)

    def run(rows):
        xe = x_ref[0:rows, :]
        x = xe[:, :D_MODEL].astype(BF16)

        def expert(e, gate):
            a = _dot(x, s1[e])
            hm = a * _sigmoid(a) * _dot(x, s3[e]) * gate
            return _dot(hm.astype(BF16), s2[e])

        o_ref[0:rows, :] = (expert(lo_ref[t], xe[:, D_MODEL + I_WLO:D_MODEL + I_WLO + 1])
                            + expert(hi_ref[t], xe[:, D_MODEL + I_WHI:D_MODEL + I_WHI + 1]))

    @pl.when(jnp.logical_and(active, nv_ref[t] > half))
    def _():
        run(TM)

    @pl.when(jnp.logical_and(active, nv_ref[t] <= half))
    def _():
        run(half)
        o_ref[half:, :] = jnp.zeros((TM - half, D_MODEL), F32)


def _moe_call(layer, plan, x_sorted, w1, w3, w2):
    r = x_sorted.shape[0]
    act = lambda t, tg, lo, hi, cg, nv, na: (jnp.minimum(t, na[0] - 1), 0)
    grp = lambda t, tg, lo, hi, cg, nv, na: (layer, tg[t], 0, 0)
    up = pl.BlockSpec((None, E_PER_GROUP, D_MODEL, D_E), grp, pipeline_mode=pl.Buffered(1))
    down = pl.BlockSpec((None, E_PER_GROUP, D_E, D_MODEL), grp, pipeline_mode=pl.Buffered(1))
    return pl.pallas_call(
        _moe_kernel,
        grid_spec=pltpu.PrefetchScalarGridSpec(
            num_scalar_prefetch=6, grid=(r // TM,),
            in_specs=[pl.BlockSpec((TM, D_EXT), act), up, up, down],
            out_specs=pl.BlockSpec((TM, D_MODEL), lambda t, *_: (t, 0)),
            scratch_shapes=[pltpu.VMEM((E_PER_GROUP, D_MODEL, D_E), BF16),
                            pltpu.VMEM((E_PER_GROUP, D_MODEL, D_E), BF16),
                            pltpu.VMEM((E_PER_GROUP, D_E, D_MODEL), BF16)]),
        out_shape=jax.ShapeDtypeStruct((r, D_MODEL), F32),
        compiler_params=_cp(("arbitrary",)),
        name="moe_experts",
    )(plan["tile_grp"], plan["tile_lo"], plan["tile_hi"], plan["chg_grp"], plan["valid"], plan["n_act"],
      x_sorted, w1, w3, w2)


def _ln2_kernel(cls_ref, rank_ref, start_ref, x1_ref, mod_ref, g_ref, b_ref, y_ref, *rest, n_ctx_tiles):
    o_refs, (buf, sems) = rest[:-2], rest[-2:]
    i = pl.program_id(0)
    n_steps = pl.num_programs(0)

    def start_rows(step):
        slot = step % 2

        def body(r, carry):
            src = _sorted_row(cls_ref, rank_ref, start_ref, step * TM + r)
            _row_copy(y_ref, src, buf.at[slot], r, sems.at[slot]).start()
            return carry

        lax.fori_loop(0, TM, body, 0, unroll=DMA_UNROLL)

    @pl.when(i == 0)
    def _():
        start_rows(0)

    @pl.when(i + 1 < n_steps)
    def _():
        start_rows(i + 1)

    slot = i % 2
    pltpu.make_async_copy(y_ref.at[pl.ds(0, TM), :], buf.at[slot], sems.at[slot]).wait()
    m = mod_ref[0]
    y = _ln_plain(ALPHA * x1_ref[...] + m[5:6] * buf[slot]) * g_ref[...] + b_ref[...]
    if len(o_refs) == 1:
        o_refs[0][...] = y
    else:
        @pl.when(i < n_ctx_tiles)
        def _():
            o_refs[0][...] = y

        @pl.when(i >= n_ctx_tiles)
        def _():
            o_refs[1][...] = y


def _ln2_call(plan, x1, y_sorted, mod_l, ln_g, ln_b, group_of_tile, n_ctx_tiles, split):
    n = x1.shape[0]
    row = lambda i, *_: (i, 0)
    fixed = lambda i, *_: (0, 0)
    if split:
        n_ctx = n_ctx_tiles * TM
        out_specs = [pl.BlockSpec((TM, D_MODEL), lambda i, *_: (jnp.minimum(i, n_ctx_tiles - 1), 0)),
                     pl.BlockSpec((TM, D_MODEL), lambda i, *_: (jnp.maximum(i - n_ctx_tiles, 0), 0))]
        out_shape = [jax.ShapeDtypeStruct((n_ctx, D_MODEL), F32), jax.ShapeDtypeStruct((n - n_ctx, D_MODEL), F32)]
    else:
        out_specs = [pl.BlockSpec((TM, D_MODEL), row)]
        out_shape = [jax.ShapeDtypeStruct((n, D_MODEL), F32)]
    return pl.pallas_call(
        functools.partial(_ln2_kernel, n_ctx_tiles=n_ctx_tiles),
        grid_spec=pltpu.PrefetchScalarGridSpec(
            num_scalar_prefetch=3, grid=(n // TM,),
            in_specs=[pl.BlockSpec((TM, D_MODEL), row),
                      pl.BlockSpec((1, 6, D_MODEL), lambda i, *_: (group_of_tile(i), 0, 0)),
                      pl.BlockSpec((1, D_MODEL), fixed), pl.BlockSpec((1, D_MODEL), fixed),
                      pl.BlockSpec(memory_space=pl.ANY)],
            out_specs=out_specs,
            scratch_shapes=[pltpu.VMEM((2, TM, D_MODEL), F32), pltpu.SemaphoreType.DMA((2,))]),
        out_shape=out_shape,
        compiler_params=_cp(("arbitrary",)),
        name="moe_gather_ln2",
    )(plan["cls"], plan["rank"], plan["row_start"], x1, mod_l, ln_g, ln_b, y_sorted)


def _dft_matrices():
    n2 = 2 * CH
    f = np.arange(CH, dtype=np.float64)[:, None]
    t = np.arange(CH, dtype=np.float64)[None, :]
    ang = 2.0 * np.pi * f * t / n2
    re, im = np.cos(ang), -np.sin(ang)
    im[0, :] = np.cos(np.pi * t[0])
    fwd = np.concatenate([re, im], axis=0)
    scale = np.full((CH, 1), 2.0 / n2)
    scale[0, 0] = 1.0 / n2
    inv = np.concatenate([(re * scale).T, (im * scale).T], axis=1)
    return fwd.astype(np.float32), inv.astype(np.float32)


def _filter_features(L):
    lag = np.arange(-L, L)
    m = np.minimum(np.abs(lag), L - 1)
    t = (np.arange(L, dtype=np.float32) / np.float32(max(L - 1, 1)))[m]
    w = (np.float32(2.0 * math.pi) * np.arange(L, dtype=np.float32) / np.float32(L))[m]
    bands = np.linspace(1e-4, HY_BANDS - 1, HY_BANDS, dtype=np.float32)
    z = np.zeros((2 * L, LANE), np.float32)
    z[:, 0] = t
    z[:, 1:1 + HY_BANDS] = np.cos(w[:, None] * bands)
    z[:, 1 + HY_BANDS:HY_EMB] = -np.sin(w[:, None] * bands)
    lo, hi = math.log(HY_DECAY_TARGET) / 1.5, math.log(HY_DECAY_TARGET) / 0.3
    deltas = np.abs(np.linspace(lo, hi, D_B, dtype=np.float32))
    dec = np.exp(-t[:, None] * deltas)
    return z, dec.astype(np.float32)


def _sincos_2d(rows, cols):
    quarter = D_MODEL // 4
    omega = 1.0 / (10000.0 ** (jnp.arange(quarter, dtype=F32) / quarter))

    def emb(n):
        ang = jnp.arange(n, dtype=F32)[:, None] * omega[None]
        return jnp.concatenate([jnp.sin(ang), jnp.cos(ang)], -1)

    er, ec = emb(rows), emb(cols)
    half = D_MODEL // 2
    pos = jnp.concatenate([jnp.broadcast_to(er[:, None], (rows, cols, half)),
                           jnp.broadcast_to(ec[None], (rows, cols, half))], -1)
    return pos.reshape(rows * cols, D_MODEL)


def _pad_to(x, shape):
    return jnp.pad(x, [(0, s - d) for d, s in zip(x.shape, shape)])


def _block_diag(w):
    eye = jnp.eye(H_A, dtype=w.dtype)
    return jnp.einsum("hij,hg->higj", w, eye).reshape(D_A, D_A)


_PAIR_LO = np.array([0, 0, 0, 1, 1, 2], np.int32)
_PAIR_HI = np.array([1, 2, 3, 2, 3, 3], np.int32)


def _routing_plan(info_t, counts, n_tiles_max):
    cnt = counts[0, :N_CLASS].astype(jnp.int32)
    tiles = (cnt + TM - 1) // TM
    tile_end = jnp.cumsum(tiles)
    n_act = tile_end[-1]
    t = jnp.minimum(jnp.arange(n_tiles_max, dtype=jnp.int32), n_act - 1)
    tcls = jnp.minimum(jnp.sum((tile_end[None, :] <= t[:, None]).astype(jnp.int32), 1), N_CLASS - 1)
    grp, pair = (tcls // N_PAIRS).astype(jnp.int32), tcls % N_PAIRS
    valid = jnp.clip(cnt[tcls] - (t - (tile_end - tiles)[tcls]) * TM, 0, TM).astype(jnp.int32)
    first = jnp.ones((1,), jnp.int32)
    changed = lambda e: jnp.concatenate([first, (e[1:] != e[:-1]).astype(jnp.int32)])
    return {"cls": info_t[I_CLS].astype(jnp.int32), "rank": info_t[I_RANK].astype(jnp.int32),
            "row_start": ((tile_end - tiles) * TM).astype(jnp.int32), "tile_grp": grp, "chg_grp": changed(grp),
            "tile_lo": jnp.asarray(_PAIR_LO)[pair], "tile_hi": jnp.asarray(_PAIR_HI)[pair], "valid": valid,
            "n_act": n_act.reshape(1).astype(jnp.int32),
            "pad_rows": jnp.where(tiles > 0, (tile_end - 1) * TM, -1).astype(jnp.int32)}


def kernel(x_prompt, x_sample, c, state_lru, state_mlstm_C, state_mlstm_n, state_mlstm_m, c_ctx, w_ada, b_ada, w_in, b_in, conv_a_w, conv_a_b, lru_wa, lru_ba, lru_wx, lru_bx, lru_lam, conv_b_w, conv_b_b, hy_w1, hy_b1, hy_w2, hy_b2, hy_freq, hy_w3, hy_bias, mix_g, w_out, ln1_g, ln1_b, rt_wg, rt_bg, rt_we, rt_be, moe_w1, moe_w3, moe_w2, ln2_g, ln2_b):
    B, l_ctx, D = x_prompt.shape
    b_lat, l_lat, _ = x_sample.shape
    n_ctx, n_lat = B * l_ctx, b_lat * l_lat
    n = n_ctx + n_lat
    assert D == D_MODEL and w_in.shape[-1] == D_MAIN + N_GATE
    assert SEG % l_ctx == 0 and l_lat == SEG and l_ctx % CH == 0 and n_ctx % SEG == 0
    assert l_ctx == CH, "the mLSTM step schedule assumes one chunk per context sequence"
    assert 1 + b_lat <= SUB
    n_ctx_blk = n_ctx // SEG
    n_ctx_tiles = n_ctx // TM
    tiles_per_lat = l_lat // TM
    nc_lat = l_lat // CH

    def group_of(tile_rows):
        first_lat, per_seq = n_ctx // tile_rows, l_lat // tile_rows
        return lambda i: jnp.where(i < first_lat, 0, 1 + (i - first_lat) // per_seq)

    group_of_tile = group_of(TM)

    cond = jnp.concatenate([c_ctx[None], c, jnp.zeros((SUB - 1 - b_lat, D), F32)], 0)
    mod = _mod_call(cond, w_ada, b_ada).reshape(DEPTH, SUB, 6, D)
    pos = _sincos_2d(l_lat // GRID_W, GRID_W)
    x = _entry_call(x_prompt.reshape(n_ctx, D), x_sample.reshape(n_lat, D), pos)

    fwd_np, inv_np = _dft_matrices()
    fwd32 = jnp.asarray(fwd_np)
    fwd16, inv16 = fwd32.astype(BF16), jnp.asarray(inv_np).astype(BF16)
    fw1 = _pad_to(hy_w1, (DEPTH, LANE, LANE))
    fb1 = _pad_to(hy_b1[:, None, :], (DEPTH, 1, LANE))
    fw2 = _pad_to(hy_w2, (DEPTH, LANE, LANE))
    fb2 = _pad_to(hy_b2[:, None, :], (DEPTH, 1, LANE))
    ffr = _pad_to(hy_freq[:, None, :], (DEPTH, 1, LANE))
    fw3 = _pad_to(hy_w3, (DEPTH, LANE, HY_ORDER * 2 * D_B))
    spectra = {}
    for L in (l_ctx, l_lat):
        z_np, dec_np = _filter_features(L)
        spectra[L] = _filt_call(L, jnp.asarray(z_np), jnp.asarray(dec_np), fw1, fb1, fw2, fb2, ffr, fw3, fwd32)

    w_in_t = jnp.swapaxes(w_in, 1, 2)
    lat_slots = SEG // l_ctx
    st_lru, st_c, st_n, st_m = [], [], [], []
    pending = None
    for l in range(DEPTH):
        b_main = b_in[l, None, :D_MAIN]
        k_lo, k_hi = _Z_CUTS[_K_CUT], _Z_CUTS[_K_CUT + 1]
        b_kt = jnp.broadcast_to(b_in[l, k_lo:k_hi, None], (D_C, TM_IN))
        w_gate = _pad_to(w_in[l, :, D_MAIN:], (D, LANE))
        b_gate = _pad_to(b_in[l, None, D_MAIN:], (1, LANE))
        if pending is None:
            proj = _in_call(l, x, mod[l], w_in_t, b_main, b_kt, w_gate, b_gate, group_of(TM_IN))
        else:
            x, proj = _in_gather_call(l, *pending, mod[l], w_in_t, b_main, b_kt, w_gate, b_gate, group_of(TM_IN))
        xa, ya, hyb, q, v, og, gates, kt = proj

        lru_w = jnp.concatenate([_block_diag(lru_wa[l, 0]), _block_diag(lru_wx[l, 0]),
                                 _block_diag(lru_wa[l, 1]), _block_diag(lru_wx[l, 1])], 1).astype(BF16)
        lru_b = jnp.concatenate([lru_ba[l, 0], lru_bx[l, 0], lru_ba[l, 1], lru_bx[l, 1]])[None]
        h0_lat = _pad_to(state_lru[:, l][:, None], (b_lat, lat_slots, 2, D_A))
        h0_all = jnp.concatenate([jnp.zeros((n_ctx_blk, lat_slots, 2, D_A), F32), h0_lat], 0)
        out_a, lru_last = _lru_call(xa, ya, conv_a_w[l], conv_a_b[l, None], lru_w, lru_b, lru_lam[l], h0_all,
                                    l_ctx, l_lat, n_ctx_blk)

        habc, hd0c = spectra[l_ctx]
        habl, hd0l = spectra[l_lat]
        out_b = _hy_call(l, hyb, conv_b_w[l], conv_b_b[l, None], fwd16, inv16, habc, hd0c, habl, hd0l,
                         hy_bias[l][:, None, :], l_ctx, l_lat, n_ctx_blk)

        n0 = state_mlstm_n[:, l].reshape(b_lat, 2 * H_C, DK)
        cx0 = jnp.concatenate([state_mlstm_C[:, l].reshape(b_lat, 2 * H_C, DK, DK), n0[..., None],
                               jnp.zeros((b_lat, 2 * H_C, DK, DK - 1), F32)], -1)
        m0 = _pad_to(state_mlstm_m[:, l], (b_lat, SUB, LANE))
        hcf, hcb, c_fin, n_fin, m_fin = _mlstm_call(q, v, kt, gates, cx0, n0, m0, n_ctx // CH, nc_lat)

        rt_w = _pad_to(jnp.concatenate([rt_wg[l], rt_we[l]], 1), (D, LANE))
        rt_b = _pad_to(jnp.concatenate([rt_bg[l], rt_be[l]])[None], (1, LANE))
        x1, h_ext, counts, info_t = _out_call(x, out_a, out_b, hcf, hcb, og, mod[l], mix_g[l, None],
                                              w_out[l].astype(BF16), ln1_g[l, None], ln1_b[l, None], rt_w, rt_b,
                                              group_of(TM_OUT))

        n_tiles_max = n // TM + N_CLASS
        plan = _routing_plan(info_t, counts, n_tiles_max)
        x_sorted = _scatter_call(plan, h_ext, n_tiles_max)
        y_sorted = _moe_call(l, plan, x_sorted, moe_w1, moe_w3, moe_w2)
        if l == DEPTH - 1:
            outs = _ln2_call(plan, x1, y_sorted, mod[l], ln2_g[l, None], ln2_b[l, None], group_of_tile,
                             n_ctx_tiles, split=True)
        else:
            pending = (plan, x1, y_sorted, mod[l], ln2_g[l, None], ln2_b[l, None])

        st_lru.append(lru_last[:n_ctx_blk].reshape(B, 2, D_A))
        st_c.append(c_fin.reshape(B, 2, H_C, DK, DK))
        st_n.append(n_fin.reshape(B, 2, H_C, DK))
        st_m.append(m_fin[:, :2, :H_C])

    return (outs[0].reshape(B, l_ctx, D), outs[1].reshape(b_lat, l_lat, D),
            jnp.stack(st_lru, 1), jnp.stack(st_c, 1), jnp.stack(st_n, 1), jnp.stack(st_m, 1))
```

```python
import functools
import math

import numpy as np
import jax
import jax.numpy as jnp
from jax import lax
from jax.experimental import pallas as pl
from jax.experimental.pallas import tpu as pltpu

F32 = jnp.float32
BF16 = jnp.bfloat16

D_MODEL = 1024
DEPTH = 2
GRID_W = 64
D_A = 256
H_A = 4
BA = D_A // H_A
LRU_C = 8.0
D_B = 256
HY_ORDER = 2
HY_BANDS = 16
HY_EMB = 1 + 2 * HY_BANDS
HY_FH = 64
HY_DECAY_TARGET = 1e-2
D_C = 512
H_C = 4
DK = D_C // H_C
N_GROUPS = 4
E_PER_GROUP = 4
N_EXP = N_GROUPS * E_PER_GROUP
N_PAIRS = 6
N_CLASS = N_GROUPS * N_PAIRS
D_E = 512
ALPHA = (2 * DEPTH) ** 0.25
EPS = 1e-6
D_MAIN = 2 * D_A + 3 * D_B + 4 * D_C
N_GATE = 4 * H_C

LANE = 128
SUB = 8
VMEM_LIMIT = 56 * 1024 * 1024

CH = 256
SEG = 2048
TM = 256
TM_IN = 512
TM_OUT = 512
D_EXT = D_MODEL + LANE
FS = 64
DMA_UNROLL = 8
RING = 3

I_CLS, I_ELO, I_EHI, I_WLO, I_WHI, I_RANK = range(6)


def _cp(sem, vmem=VMEM_LIMIT):
    return pltpu.CompilerParams(dimension_semantics=sem, vmem_limit_bytes=vmem)


def _dot(a, b):
    return jnp.dot(a, b, preferred_element_type=F32)


def _split2(x):
    hi = x.astype(BF16)
    lo = (x - hi.astype(F32)).astype(BF16)
    return hi, lo


def _dot3(a, b):
    ah, al = _split2(a)
    bh, bl = _split2(b)
    return _dot(ah, bh) + (_dot(ah, bl) + _dot(al, bh))


def _split3(x):
    hi = x.astype(BF16)
    r1 = x - hi.astype(F32)
    mid = r1.astype(BF16)
    lo = (r1 - mid.astype(F32)).astype(BF16)
    return hi, mid, lo


def _sigmoid(x):
    return 1.0 / (1.0 + jnp.exp(-x))


def _log_sigmoid(x):
    return jnp.minimum(x, 0.0) - jnp.log1p(jnp.exp(-jnp.abs(x)))


def _gelu_tanh(x):
    return 0.5 * x * (1.0 + jnp.tanh(math.sqrt(2.0 / math.pi) * (x + 0.044715 * (x * x * x))))


def _ln_plain(x):
    mu = jnp.mean(x, -1, keepdims=True)
    xc = x - mu
    var = jnp.mean(xc * xc, -1, keepdims=True)
    return xc * lax.rsqrt(var + EPS)


def _rms(x):
    return x * lax.rsqrt(jnp.mean(x * x, -1, keepdims=True) + EPS)


def _halo_rows(ref, start, rows):
    total = ref.shape[0]
    prev = ref[pl.ds(pl.multiple_of(jnp.maximum(start - SUB, 0), SUB), SUB), :]
    main = ref[pl.ds(start, rows), :]
    nxt = ref[pl.ds(pl.multiple_of(jnp.minimum(start + rows, total - SUB), SUB), SUB), :]
    return jnp.concatenate([prev, main, nxt], axis=0), main


def _mod_kernel(c_ref, w_ref, b_ref, o_ref):
    c = c_ref[...]
    o_ref[0] = _dot3(c * _sigmoid(c), w_ref[0]) + b_ref[0]


def _mod_call(cond, w_ada, b_ada):
    tn = 1536
    n6 = w_ada.shape[-1]
    return pl.pallas_call(
        _mod_kernel,
        grid=(DEPTH, n6 // tn),
        in_specs=[pl.BlockSpec((SUB, D_MODEL), lambda l, j: (0, 0)),
                  pl.BlockSpec((1, D_MODEL, tn), lambda l, j: (l, 0, j)),
                  pl.BlockSpec((1, 1, tn), lambda l, j: (l, 0, j))],
        out_specs=pl.BlockSpec((1, SUB, tn), lambda l, j: (l, 0, j)),
        out_shape=jax.ShapeDtypeStruct((DEPTH, SUB, n6), F32),
        compiler_params=_cp(("parallel", "parallel")),
        name="adaln_mod",
    )(cond, w_ada, b_ada.reshape(DEPTH, 1, n6))


def _entry_kernel(xc_ref, xl_ref, pos_ref, o_ref, *, n_ctx_tiles):
    i = pl.program_id(0)

    @pl.when(i < n_ctx_tiles)
    def _():
        o_ref[...] = _ln_plain(xc_ref[...])

    @pl.when(i >= n_ctx_tiles)
    def _():
        o_ref[...] = _ln_plain(xl_ref[...] + pos_ref[...])


def _entry_call(xc, xl, pos):
    tm = 512
    n_ctx, n_lat, l_lat = xc.shape[0], xl.shape[0], pos.shape[0]
    nct = n_ctx // tm
    per_seq = l_lat // tm
    return pl.pallas_call(
        functools.partial(_entry_kernel, n_ctx_tiles=nct),
        grid=((n_ctx + n_lat) // tm,),
        in_specs=[pl.BlockSpec((tm, D_MODEL), lambda i: (jnp.minimum(i, nct - 1), 0)),
                  pl.BlockSpec((tm, D_MODEL), lambda i: (jnp.maximum(i - nct, 0), 0)),
                  pl.BlockSpec((tm, D_MODEL), lambda i: (jnp.maximum(i - nct, 0) % per_seq, 0))],
        out_specs=pl.BlockSpec((tm, D_MODEL), lambda i: (i, 0)),
        out_shape=jax.ShapeDtypeStruct((n_ctx + n_lat, D_MODEL), F32),
        compiler_params=_cp(("parallel",)),
        name="entry_ln",
    )(xc, xl, pos)


_Z_CUTS = (0, D_A, 2 * D_A, 2 * D_A + 3 * D_B, 2 * D_A + 3 * D_B + D_C, 2 * D_A + 3 * D_B + 2 * D_C,
           2 * D_A + 3 * D_B + 3 * D_C, D_MAIN)


_K_CUT = 4
_ROW_CUTS = tuple(c for i, c in enumerate(zip(_Z_CUTS[:-1], _Z_CUTS[1:])) if i != _K_CUT)
_NT = (((1,), (1,)), ((), ()))


def _in_kernel(x_ref, mod_ref, wt_ref, b_ref, bkt_ref, wg_ref, bg_ref, *refs):
    out_refs, wt16 = refs[:-1], refs[-1]

    @pl.when(pl.program_id(0) == 0)
    def _():
        wt16[...] = wt_ref[0, :D_MAIN, :].astype(BF16)

    m = mod_ref[0]
    h = x_ref[...] * (1.0 + m[1:2]) + m[0:1]
    hb = h.astype(BF16)
    dg = lambda a, b: lax.dot_general(a, b, _NT, preferred_element_type=F32)
    for ref, (a, b) in zip(out_refs[:-2], _ROW_CUTS):
        ref[...] = dg(hb, wt16[a:b, :]) + b_ref[:, a:b]
    g_ref, kt_ref = out_refs[-2:]
    g_ref[...] = _dot3(h, wg_ref[...]) + bg_ref[...]
    kt_ref[...] = dg(wt16[_Z_CUTS[_K_CUT]:_Z_CUTS[_K_CUT + 1], :], hb) + bkt_ref[...]


def _in_call(layer, x, mod_l, w_in_t, b_main, b_kt, w_gate, b_gate, group_of_tile):
    n = x.shape[0]
    widths = [b - a for a, b in _ROW_CUTS] + [LANE]
    row = lambda i: (i, 0)
    fixed = lambda i: (0, 0)
    return pl.pallas_call(
        _in_kernel,
        grid=(n // TM_IN,),
        in_specs=[pl.BlockSpec((TM_IN, D_MODEL), row),
                  pl.BlockSpec((1, 6, D_MODEL), lambda i: (group_of_tile(i), 0, 0)),
                  pl.BlockSpec((1, w_in_t.shape[1], D_MODEL), lambda i: (layer, 0, 0), pipeline_mode=pl.Buffered(1)),
                  pl.BlockSpec((1, D_MAIN), fixed),
                  pl.BlockSpec((D_C, TM_IN), fixed),
                  pl.BlockSpec((D_MODEL, LANE), fixed), pl.BlockSpec((1, LANE), fixed)],
        out_specs=[pl.BlockSpec((TM_IN, w), row) for w in widths] + [pl.BlockSpec((D_C, TM_IN), lambda i: (0, i))],
        out_shape=[jax.ShapeDtypeStruct((n, w), F32) for w in widths] + [jax.ShapeDtypeStruct((D_C, n), F32)],
        scratch_shapes=[pltpu.VMEM((D_MAIN, D_MODEL), BF16)],
        compiler_params=_cp(("arbitrary",)),
        name="in_proj",
    )(x, mod_l, w_in_t, b_main, b_kt, w_gate, b_gate)


def _lru_variant(L, xa_ref, ya_ref, cw_ref, cb_ref, wg_ref, bg_ref, lam_ref, h0_ref, o_ref, st_ref,
                 s_af, s_bf, s_ab, s_bb):
    nch, nseq, ntile = SEG // CH, SEG // L, L // SUB
    lam = lam_ref[...]
    sp = jnp.maximum(-lam, 0.0) + jnp.log1p(jnp.exp(-jnp.abs(lam)))
    cw = cw_ref[...]
    cb = cb_ref[...]
    row = lax.broadcasted_iota(jnp.int32, (CH, 1), 0)
    sub3 = lax.broadcasted_iota(jnp.int32, (1, SUB, 1), 1)

    def gates_and_tile_scan(c, carry):
        start = pl.multiple_of(c * CH, CH)
        xcat, main = _halo_rows(xa_ref, start, CH)
        tpos = (start + row) & (L - 1)
        xm2 = jnp.where(tpos >= 2, xcat[SUB - 2:SUB - 2 + CH], 0.0)
        xm1 = jnp.where(tpos >= 1, xcat[SUB - 1:SUB - 1 + CH], 0.0)
        xp1 = jnp.where(tpos <= L - 2, xcat[SUB + 1:SUB + 1 + CH], 0.0)
        xc = cw[0:1] * xm2 + cw[1:2] * xm1 + cw[2:3] * main + cw[3:4] * xp1 + cb
        g = _dot(xc.astype(BF16), wg_ref[...]) + bg_ref[...]
        for d, (sa, sb) in enumerate(((s_af, s_bf), (s_ab, s_bb))):
            r = _sigmoid(g[:, 2 * d * D_A:(2 * d + 1) * D_A])
            ig = _sigmoid(g[:, (2 * d + 1) * D_A:(2 * d + 2) * D_A])
            a = jnp.exp(-LRU_C * r * sp[d:d + 1])
            b = jnp.sqrt(1.0 - a * a) * (ig * xc)
            a3, b3 = a.reshape(CH // SUB, SUB, D_A), b.reshape(CH // SUB, SUB, D_A)
            for s in (1, 2, 4):
                shift, keep = (s, sub3 >= s) if d == 0 else (SUB - s, sub3 < SUB - s)
                b3 = a3 * jnp.where(keep, pltpu.roll(b3, shift, 1), 0.0) + b3
                a3 = a3 * jnp.where(keep, pltpu.roll(a3, shift, 1), 1.0)
            sa[pl.ds(start, CH), :] = a3.reshape(CH, D_A)
            sb[pl.ds(start, CH), :] = b3.reshape(CH, D_A)
        return carry

    lax.fori_loop(0, nch, gates_and_tile_scan, 0)

    def carry_tiles(k, carry):
        cf, cbk = carry
        nf, nb = [], []
        for s in range(nseq):
            rf = pl.multiple_of(s * L + k * SUB, SUB)
            hf = s_af[pl.ds(rf, SUB), :] * cf[s] + s_bf[pl.ds(rf, SUB), :]
            s_bf[pl.ds(rf, SUB), :] = hf
            nf.append(hf[SUB - 1:SUB, :])
            rb = pl.multiple_of(s * L + (ntile - 1 - k) * SUB, SUB)
            hb = s_ab[pl.ds(rb, SUB), :] * cbk[s] + s_bb[pl.ds(rb, SUB), :]
            s_bb[pl.ds(rb, SUB), :] = hb
            nb.append(hb[0:1, :])
        return tuple(nf), tuple(nb)

    cf0 = tuple(h0_ref[0, s, 0:1, :] for s in range(nseq))
    cb0 = tuple(h0_ref[0, s, 1:2, :] for s in range(nseq))
    cf, cbk = lax.fori_loop(0, ntile, carry_tiles, (cf0, cb0))

    st_ref[...] = jnp.zeros(st_ref.shape, F32)
    for s in range(nseq):
        st_ref[0, s] = jnp.concatenate([cf[s], cbk[s]], axis=0)

    def finish(c, carry):
        start = pl.multiple_of(c * CH, CH)
        h = s_bf[pl.ds(start, CH), :] + s_bb[pl.ds(start, CH), :]
        o_ref[pl.ds(start, CH), :] = _rms(_gelu_tanh(ya_ref[pl.ds(start, CH), :]) * h)
        return carry

    lax.fori_loop(0, nch, finish, 0)


def _lru_kernel(*refs, l_ctx, l_lat, n_ctx_blk):
    i = pl.program_id(0)

    @pl.when(i < n_ctx_blk)
    def _():
        _lru_variant(l_ctx, *refs)

    @pl.when(i >= n_ctx_blk)
    def _():
        _lru_variant(l_lat, *refs)


def _lru_call(xa, ya, conv_w, conv_b, w_gate, b_gate, lam, h0_all, l_ctx, l_lat, n_ctx_blk):
    n = xa.shape[0]
    nblk = n // SEG
    row = lambda i: (i, 0)
    fixed = lambda i: (0, 0)
    slots = SEG // l_ctx
    return pl.pallas_call(
        functools.partial(_lru_kernel, l_ctx=l_ctx, l_lat=l_lat, n_ctx_blk=n_ctx_blk),
        grid=(nblk,),
        in_specs=[pl.BlockSpec((SEG, D_A), row), pl.BlockSpec((SEG, D_A), row),
                  pl.BlockSpec((4, D_A), fixed), pl.BlockSpec((1, D_A), fixed),
                  pl.BlockSpec((D_A, 4 * D_A), fixed), pl.BlockSpec((1, 4 * D_A), fixed),
                  pl.BlockSpec((2, D_A), fixed),
                  pl.BlockSpec((1, slots, 2, D_A), lambda i: (i, 0, 0, 0))],
        out_specs=[pl.BlockSpec((SEG, D_A), row),
                   pl.BlockSpec((1, slots, 2, D_A), lambda i: (i, 0, 0, 0))],
        out_shape=[jax.ShapeDtypeStruct((n, D_A), F32),
                   jax.ShapeDtypeStruct((nblk, slots, 2, D_A), F32)],
        scratch_shapes=[pltpu.VMEM((SEG, D_A), F32) for _ in range(4)],
        compiler_params=_cp(("parallel",)),
        name="rglru",
    )(xa, ya, conv_w, conv_b, w_gate, b_gate, lam, h0_all)


def _filt_kernel(z_ref, dec_ref, w1_ref, b1_ref, w2_ref, b2_ref, fr_ref, w3_ref, fwd_ref,
                 oab_ref, od0_ref, s_k, s_kf, *, L):
    nblk = 2 * L // CH
    d_idx = pl.program_id(1)
    row = lax.broadcasted_iota(jnp.int32, (CH, 1), 0)

    @pl.when(d_idx == 0)
    def _():
        fr = fr_ref[0]

        def taps(c, carry):
            start = pl.multiple_of(c * CH, CH)
            h1 = jnp.sin(fr * (_dot3(z_ref[pl.ds(start, CH), :], w1_ref[0]) + b1_ref[0]))
            h2 = jnp.sin(fr * (_dot3(h1, w2_ref[0]) + b2_ref[0]))
            t = _dot3(h2, w3_ref[0])
            dec = dec_ref[pl.ds(start, CH), :]
            rg = start + row
            for o in range(HY_ORDER):
                fwd_t = t[:, (2 * o) * D_B:(2 * o + 1) * D_B]
                bwd_t = t[:, (2 * o + 1) * D_B:(2 * o + 2) * D_B]
                ko = jnp.where(rg < L, bwd_t, fwd_t) * dec
                s_k[pl.ds(start, CH), o * D_B:(o + 1) * D_B] = jnp.where(rg == 0, 0.0, ko)
            return carry

        lax.fori_loop(0, nblk, taps, 0)
        fwd = fwd_ref[...]

        def spectra(e, carry):
            start = pl.multiple_of(e * CH, CH)
            s_kf[e] = _dot3(fwd, s_k[pl.ds(start, CH), :])
            return carry

        lax.fori_loop(0, nblk, spectra, 0)

    kd = s_kf[d_idx + 1]
    km = s_kf[d_idx]
    k0 = s_k[pl.ds(pl.multiple_of(d_idx * CH, CH), 1), :]
    sgn = jnp.where((row & 1) == 0, 1.0, -1.0)
    a = kd[:CH] + sgn * (km[:CH] - k0)
    b = jnp.where(row == 0, 0.0, kd[CH:] + sgn * km[CH:])
    hn = kd[CH:CH + 1] + km[CH:CH + 1] - k0
    for o in range(HY_ORDER):
        oab_ref[0, o, 0, 0] = a[:, o * D_B:(o + 1) * D_B]
        oab_ref[0, o, 0, 1] = b[:, o * D_B:(o + 1) * D_B]
        od0_ref[0, o, 0] = jnp.broadcast_to(hn[:, o * D_B:(o + 1) * D_B], (SUB, D_B))


def _filt_call(L, z, dec, w1, b1, w2, b2, fr, w3, fwd32):
    nd = 2 * (L // CH) - 1
    fixed = lambda l, d: (0, 0)
    lay3 = lambda l, d: (l, 0, 0)
    return pl.pallas_call(
        functools.partial(_filt_kernel, L=L),
        grid=(DEPTH, nd),
        in_specs=[pl.BlockSpec((2 * L, LANE), fixed), pl.BlockSpec((2 * L, D_B), fixed),
                  pl.BlockSpec((1, LANE, LANE), lay3), pl.BlockSpec((1, 1, LANE), lay3),
                  pl.BlockSpec((1, LANE, LANE), lay3), pl.BlockSpec((1, 1, LANE), lay3),
                  pl.BlockSpec((1, 1, LANE), lay3),
                  pl.BlockSpec((1, LANE, HY_ORDER * 2 * D_B), lay3),
                  pl.BlockSpec((2 * CH, CH), fixed)],
        out_specs=[pl.BlockSpec((1, HY_ORDER, 1, 2, CH, D_B), lambda l, d: (l, 0, d, 0, 0, 0)),
                   pl.BlockSpec((1, HY_ORDER, 1, SUB, D_B), lambda l, d: (l, 0, d, 0, 0))],
        out_shape=[jax.ShapeDtypeStruct((DEPTH, HY_ORDER, nd, 2, CH, D_B), F32),
                   jax.ShapeDtypeStruct((DEPTH, HY_ORDER, nd, SUB, D_B), F32)],
        scratch_shapes=[pltpu.VMEM((2 * L, HY_ORDER * D_B), F32),
                        pltpu.VMEM((2 * L // CH, 2 * CH, HY_ORDER * D_B), F32)],
        compiler_params=_cp(("parallel", "arbitrary")),
        name=f"hyena_filter_{L}",
    )(z, dec, w1, b1, w2, b2, fr, w3, fwd32)


def _hy_variant(L, o_idx, hy_ref, cw_ref, cb_ref, fwd_ref, inv_ref, hab_ref, hd0_ref, bias_ref, o_ref,
                s_y, s_x, s_u, s_v):
    nch, nseq, P = SEG // CH, SEG // L, L // CH
    row = lax.broadcasted_iota(jnp.int32, (CH, 1), 0)
    frow = lax.broadcasted_iota(jnp.int32, (FS, 1), 0)

    @pl.when(o_idx == 0)
    def _():
        cw = cw_ref[...]
        cb = cb_ref[...]

        def short_conv(c, carry):
            start = pl.multiple_of(c * CH, CH)
            xcat, main = _halo_rows(hy_ref, start, CH)
            tpos = (start + row) & (L - 1)
            xm1 = jnp.where(tpos >= 1, xcat[SUB - 1:SUB - 1 + CH], 0.0)
            xp1 = jnp.where(tpos <= L - 2, xcat[SUB + 1:SUB + 1 + CH], 0.0)
            hc = cw[0:1] * xm1 + cw[1:2] * main + cw[2:3] * xp1 + cb
            s_y[pl.ds(start, CH), :] = hc[:, :D_B]
            s_x[0, pl.ds(start, CH), :] = hc[:, D_B:2 * D_B]
            s_x[1, pl.ds(start, CH), :] = hc[:, 2 * D_B:]
            return carry

        lax.fori_loop(0, nch, short_conv, 0)

    bias = bias_ref[0]

    def loop(n, body, init):
        return body(0, init) if n == 1 else lax.fori_loop(0, n, body, init)

    def one_sequence(s, slot):
        base = s * L
        u0 = slot * P

        def forward_dft(j, cc):
            r = pl.multiple_of(base + j * CH, CH)
            s_u[u0 + j] = _dot(fwd_ref[...], s_y[pl.ds(r, CH), :].astype(BF16))
            return cc

        loop(P, forward_dft, 0)

        def output_block(i, cc):
            for fs in range(CH // FS):
                lo = fs * FS

                def accumulate(j, acc):
                    yre, yim = acc
                    d = i - j + (P - 1)
                    ure = s_u[u0 + j, lo:lo + FS, :]
                    uim = s_u[u0 + j, CH + lo:CH + lo + FS, :]
                    a = hab_ref[0, 0, d, 0, lo:lo + FS, :]
                    b = hab_ref[0, 0, d, 1, lo:lo + FS, :]
                    dd = jnp.where(frow == 0, hd0_ref[0, 0, d, 0:1, :], a) if fs == 0 else a
                    return yre + ure * a - uim * b, yim + ure * b + uim * dd

                zero = jnp.zeros((FS, D_B), F32)
                yre, yim = loop(P, accumulate, (zero, zero))
                s_v[slot, lo:lo + FS, :] = yre.astype(BF16)
                s_v[slot, CH + lo:CH + lo + FS, :] = yim.astype(BF16)
            yc = _dot(inv_ref[...], s_v[slot])
            r = pl.multiple_of(base + i * CH, CH)
            s_y[pl.ds(r, CH), :] = s_x[o_idx, pl.ds(r, CH), :] * (yc + s_y[pl.ds(r, CH), :] * bias)
            return cc

        loop(P, output_block, 0)

    def single_block_pair(p, carry):
        rows = [pl.multiple_of((2 * p + k) * L, CH) for k in range(2)]
        ys = [s_y[pl.ds(r, CH), :] for r in rows]
        gates = [s_x[o_idx, pl.ds(r, CH), :] for r in rows]
        a = hab_ref[0, 0, 0, 0]
        b = hab_ref[0, 0, 0, 1]
        dd = jnp.where(row == 0, hd0_ref[0, 0, 0, 0:1, :], a)
        outs = []
        for y, gate in zip(ys, gates):
            u = _dot(fwd_ref[...], y.astype(BF16))
            ure, uim = u[:CH], u[CH:]
            v = jnp.concatenate([ure * a - uim * b, ure * b + uim * dd], axis=0).astype(BF16)
            outs.append(gate * (_dot(inv_ref[...], v) + y * bias))
        for r, out in zip(rows, outs):
            s_y[pl.ds(r, CH), :] = out
        return carry

    if P == 1 and nseq % 2 == 0:
        lax.fori_loop(0, nseq // 2, single_block_pair, 0)
    else:
        lax.fori_loop(0, nseq, lambda s, carry: (one_sequence(s, 0), carry)[1], 0)

    @pl.when(o_idx == HY_ORDER - 1)
    def _():
        def finish(c, carry):
            start = pl.multiple_of(c * CH, CH)
            o_ref[pl.ds(start, CH), :] = _rms(s_y[pl.ds(start, CH), :])
            return carry

        lax.fori_loop(0, nch, finish, 0)


def _hy_kernel(hy_ref, cw_ref, cb_ref, fwd_ref, inv_ref, habc_ref, hd0c_ref, habl_ref, hd0l_ref, bias_ref,
               o_ref, s_y, s_x, s_u, s_v, *, l_ctx, l_lat, n_ctx_blk):
    i = pl.program_id(0)
    o_idx = pl.program_id(1)
    scratch = (s_y, s_x, s_u, s_v)

    @pl.when(i < n_ctx_blk)
    def _():
        _hy_variant(l_ctx, o_idx, hy_ref, cw_ref, cb_ref, fwd_ref, inv_ref, habc_ref, hd0c_ref, bias_ref, o_ref,
                    *scratch)

    @pl.when(i >= n_ctx_blk)
    def _():
        _hy_variant(l_lat, o_idx, hy_ref, cw_ref, cb_ref, fwd_ref, inv_ref, habl_ref, hd0l_ref, bias_ref, o_ref,
                    *scratch)


def _hy_call(layer, hyb, conv_w, conv_b, fwd, inv, habc, hd0c, habl, hd0l, bias, l_ctx, l_lat, n_ctx_blk):
    n = hyb.shape[0]
    ndc, ndl = habc.shape[2], habl.shape[2]
    pmax = max(l_ctx, l_lat) // CH
    row = lambda i, o: (i, 0)
    fixed = lambda i, o: (0, 0)
    lat_o = lambda i, o: jnp.where(i >= n_ctx_blk, o, 0)
    ctx_o = lambda i, o: jnp.where(i < n_ctx_blk, o, 0)
    return pl.pallas_call(
        functools.partial(_hy_kernel, l_ctx=l_ctx, l_lat=l_lat, n_ctx_blk=n_ctx_blk),
        grid=(n // SEG, HY_ORDER),
        in_specs=[pl.BlockSpec((SEG, 3 * D_B), row),
                  pl.BlockSpec((3, 3 * D_B), fixed), pl.BlockSpec((1, 3 * D_B), fixed),
                  pl.BlockSpec((2 * CH, CH), fixed), pl.BlockSpec((CH, 2 * CH), fixed),
                  pl.BlockSpec((1, 1, ndc, 2, CH, D_B), lambda i, o: (layer, ctx_o(i, o), 0, 0, 0, 0)),
                  pl.BlockSpec((1, 1, ndc, SUB, D_B), lambda i, o: (layer, ctx_o(i, o), 0, 0, 0)),
                  pl.BlockSpec((1, 1, ndl, 2, CH, D_B), lambda i, o: (layer, lat_o(i, o), 0, 0, 0, 0)),
                  pl.BlockSpec((1, 1, ndl, SUB, D_B), lambda i, o: (layer, lat_o(i, o), 0, 0, 0)),
                  pl.BlockSpec((1, 1, D_B), lambda i, o: (o, 0, 0))],
        out_specs=pl.BlockSpec((SEG, D_B), row),
        out_shape=jax.ShapeDtypeStruct((n, D_B), F32),
        scratch_shapes=[pltpu.VMEM((SEG, D_B), F32), pltpu.VMEM((HY_ORDER, SEG, D_B), F32),
                        pltpu.VMEM((max(pmax, 2), 2 * CH, D_B), F32), pltpu.VMEM((2, 2 * CH, D_B), BF16)],
        compiler_params=_cp(("parallel", "arbitrary")),
        name="hyena",
    )(hyb, conv_w, conv_b, fwd, inv, habc, hd0c, habl, hd0l, bias)


def _row_scan(x, op, fill, reverse):
    t, width = x.shape
    n_tiles = t // SUB
    sub = lax.broadcasted_iota(jnp.int32, (1, SUB, 1), 1)
    x3 = x.reshape(n_tiles, SUB, width)
    for s in (1, 2, 4):
        shift, keep = (SUB - s, sub < SUB - s) if reverse else (s, sub >= s)
        x3 = op(x3, jnp.where(keep, pltpu.roll(x3, shift, 1), fill))
    x = x3.reshape(t, width)
    out = [None] * n_tiles
    carry = None
    for i in (reversed(range(n_tiles)) if reverse else range(n_tiles)):
        tile = x[i * SUB:(i + 1) * SUB]
        out[i] = tile if carry is None else op(tile, carry)
        carry = out[i][0:1] if reverse else out[i][SUB - 1:SUB]
    return jnp.concatenate(out, axis=0)


_STK_ONE = 3 * SUB


def _mlstm_prep(d, g_ref, m_old):
    T = CH
    reverse = d == 1
    g = g_ref[...]
    if d == 1:
        g = pltpu.roll(g, LANE - 2 * H_C, 1)
    lane = lax.broadcasted_iota(jnp.int32, (1, LANE), 1)
    head = lane < H_C
    b = pltpu.roll(_row_scan(_log_sigmoid(g), jnp.add, 0.0, reverse), LANE - H_C, 1)
    r = jnp.where(head, g - b, 0.0)
    big_m = jnp.maximum(m_old, _row_scan(r, jnp.maximum, -jnp.inf, reverse))
    last = 0 if reverse else T - 1
    m_last = big_m[last:last + 1, :]
    low = lane < SUB
    p0, p1, p2 = (jnp.where(low, p.astype(F32), 0.0) for p in _split3(-big_m))
    cols = (p0 + pltpu.roll(p1, SUB, 1) + pltpu.roll(p2, 2 * SUB, 1)
            + jnp.where(jnp.logical_and(lane >= _STK_ONE, lane < _STK_ONE + SUB), 1.0, 0.0))
    rowid = lax.broadcasted_iota(jnp.int32, (SUB, 1), 0)
    r8 = r.T[0:SUB, :]
    m_last8 = sum(jnp.where(rowid == h, m_last[:, h:h + 1], 0.0) for h in range(H_C))
    ws8 = jnp.where(rowid < H_C, jnp.exp(r8 - m_last8), 0.0)
    return {"r3": [p.astype(F32) for p in _split3(r8)], "ws8": ws8, "wc": jnp.exp(m_old - m_last),
            "m_new": jnp.where(head, b[last:last + 1, :] + m_last, 0.0),
            "wi": jnp.exp(m_old - big_m), "e": jnp.exp(-(b + big_m)), "cols_b": cols.astype(BF16)}


def _mlstm_variant(carry, qf, vf, ktf, gf, qb_, vb_, ktb_, gb_, hf_ref, hb_ref, co_ref, no_ref, mo_ref,
                   s_cx, s_n, s_m):
    T = CH
    ii = lax.broadcasted_iota(jnp.int32, (T, T), 0)
    jj = lax.broadcasted_iota(jnp.int32, (T, T), 1)
    rowid = lax.broadcasted_iota(jnp.int32, (SUB, 1), 0)
    one_col = jnp.where(lax.broadcasted_iota(jnp.int32, (T, DK), 1) == 0, 1.0, 0.0).astype(BF16)
    prep = []
    for d, g_ref in ((0, gf), (1, gb_ if carry else gf)):
        m_old = s_m[d:d + 1, :] if carry else jnp.zeros((1, LANE), F32)
        prep.append(_mlstm_prep(d, g_ref, m_old))
    refs = ((qf, vf, ktf, hf_ref), (qb_, vb_, ktb_, hb_ref))
    for h in range(H_C):
        sl = slice(h * DK, (h + 1) * DK)
        for d in range(2):
            p = prep[d]
            q_ref, v_ref, kt_ref, h_ref = refs[d]
            idx = d * H_C + h
            if carry or d == 0:
                qb = (q_ref[:, sl] * (DK ** -0.5)).astype(BF16)
                kt = kt_ref[sl, :]
                ktb = kt.astype(BF16)
                v_ext = jnp.concatenate([v_ref[:, sl].astype(BF16), one_col], axis=1)
                s_raw = _dot(qb, ktb)
            tri = (jj >= ii) if d == 1 else (jj <= ii)
            sel = jnp.broadcast_to(jnp.where(rowid == h, 1.0, 0.0), (SUB, T))
            rr = sum(jnp.where(rowid == i, piece[h:h + 1, :], 0.0) for i, piece in enumerate(p["r3"]))
            rmat = jnp.concatenate([sel, sel, sel, rr, jnp.zeros((LANE - 4 * SUB, T), F32)], axis=0)
            expo = _dot(p["cols_b"], rmat.astype(BF16))
            s = s_raw * jnp.exp(jnp.where(tri, expo, -jnp.inf))
            intra = _dot(s.astype(BF16), v_ext)
            num, den = intra[:, :DK], intra[:, DK:DK + 1]
            if carry:
                cx = s_cx[idx]
                inter = _dot(qb, cx.astype(BF16))
                wi = p["wi"][:, h:h + 1]
                num, den = num + wi * inter[:, :DK], den + wi * inter[:, DK:DK + 1]
            h_ref[:, sl] = num / jnp.maximum(jnp.abs(den), p["e"][:, h:h + 1])
            upd = _dot((kt * p["ws8"][h:h + 1, :]).astype(BF16), v_ext)
            n_upd = lax.dot_general(p["ws8"].astype(BF16), ktb, _NT, preferred_element_type=F32)[h:h + 1, :]
            if carry:
                wc = p["wc"][:, h:h + 1]
                s_cx[idx] = wc * cx + upd
                s_n[idx:idx + 1, :] = wc * s_n[idx:idx + 1, :] + n_upd
            else:
                co_ref[0, idx] = upd[:, :DK]
                no_ref[0, idx:idx + 1, :] = n_upd
    m_rows = jnp.concatenate([prep[0]["m_new"], prep[1]["m_new"], jnp.zeros((SUB - 2, LANE), F32)], axis=0)
    if carry:
        s_m[...] = m_rows
    else:
        mo_ref[0] = m_rows


def _mlstm_kernel(*refs, n_ctx_steps, nc_lat):
    cx0_ref, n0_ref, m0_ref = refs[8:11]
    s_cx, s_n, s_m = refs[-3:]
    data = refs[:8] + refs[11:]
    t = pl.program_id(0)
    is_ctx = t < n_ctx_steps

    @pl.when(is_ctx)
    def _():
        _mlstm_variant(False, *data)

    @pl.when(jnp.logical_not(is_ctx))
    def _():
        @pl.when((t - n_ctx_steps) % nc_lat == 0)
        def _():
            s_cx[...] = cx0_ref[0]
            s_n[...] = n0_ref[0]
            s_m[...] = m0_ref[0]

        _mlstm_variant(True, *data)


def _mlstm_call(q, v, kt, gates, cx0, n0, m0, n_ctx_steps, nc_lat):
    n = q.shape[0]
    steps = n // CH
    nst = 2 * H_C

    def bwd_blk(t):
        r = jnp.maximum(t - n_ctx_steps, 0)
        return n_ctx_steps + (r // nc_lat) * nc_lat + (nc_lat - 1 - r % nc_lat)

    out_bwd = lambda t: jnp.where(t < n_ctx_steps, t, bwd_blk(t))
    lat_b = lambda t: jnp.maximum(t - n_ctx_steps, 0) // nc_lat
    ctx_b = lambda t: jnp.minimum(t, n_ctx_steps - 1)
    rows = lambda w, blk: pl.BlockSpec((CH, w), lambda t: (blk(t), 0))
    cols = lambda h, blk: pl.BlockSpec((h, CH), lambda t: (0, blk(t)))
    ident = lambda t: t
    return pl.pallas_call(
        functools.partial(_mlstm_kernel, n_ctx_steps=n_ctx_steps, nc_lat=nc_lat),
        grid=(steps,),
        in_specs=[rows(D_C, ident), rows(D_C, ident), cols(D_C, ident), rows(LANE, ident),
                  rows(D_C, bwd_blk), rows(D_C, bwd_blk), cols(D_C, bwd_blk), rows(LANE, bwd_blk),
                  pl.BlockSpec((1, nst, DK, 2 * DK), lambda t: (lat_b(t), 0, 0, 0)),
                  pl.BlockSpec((1, nst, DK), lambda t: (lat_b(t), 0, 0)),
                  pl.BlockSpec((1, SUB, LANE), lambda t: (lat_b(t), 0, 0))],
        out_specs=[rows(D_C, ident), rows(D_C, out_bwd),
                   pl.BlockSpec((1, nst, DK, DK), lambda t: (ctx_b(t), 0, 0, 0)),
                   pl.BlockSpec((1, nst, DK), lambda t: (ctx_b(t), 0, 0)),
                   pl.BlockSpec((1, SUB, LANE), lambda t: (ctx_b(t), 0, 0))],
        out_shape=[jax.ShapeDtypeStruct((n, D_C), F32), jax.ShapeDtypeStruct((n, D_C), F32),
                   jax.ShapeDtypeStruct((n_ctx_steps, nst, DK, DK), F32),
                   jax.ShapeDtypeStruct((n_ctx_steps, nst, DK), F32),
                   jax.ShapeDtypeStruct((n_ctx_steps, SUB, LANE), F32)],
        scratch_shapes=[pltpu.VMEM((nst, DK, 2 * DK), F32), pltpu.VMEM((nst, DK), F32),
                        pltpu.VMEM((SUB, LANE), F32)],
        compiler_params=_cp(("arbitrary",)),
        name="mlstm",
    )(q, v, kt, gates, q, v, kt, gates, cx0, n0, m0)


def _out_kernel(x_ref, oa_ref, ob_ref, hf_ref, hb_ref, og_ref, mod_ref, mg_ref, w_ref, g_ref, b_ref,
                rw_ref, rb_ref, x1_ref, he_ref, cnt_ref, it_ref, s_cnt):
    i = pl.program_id(0)

    @pl.when(i == 0)
    def _():
        s_cnt[...] = jnp.zeros(s_cnt.shape, F32)

    m = mod_ref[0]
    mg = mg_ref[...]
    acc = _dot((oa_ref[...] * mg[:, :D_A]).astype(BF16), w_ref[0:D_A, :])
    acc += _dot((ob_ref[...] * mg[:, D_A:D_A + D_B]).astype(BF16), w_ref[D_A:D_A + D_B, :])
    hc = hf_ref[...] + hb_ref[...]
    og = og_ref[...]
    off = D_A + D_B
    for h in range(H_C):
        sl = slice(h * DK, (h + 1) * DK)
        oc = _sigmoid(og[:, sl]) * _rms(hc[:, sl]) * mg[:, off + h * DK:off + (h + 1) * DK]
        acc += _dot(oc.astype(BF16), w_ref[off + h * DK:off + (h + 1) * DK, :])
    x1 = _ln_plain(ALPHA * x_ref[...] + m[2:3] * acc) * g_ref[...] + b_ref[...]
    x1_ref[...] = x1
    h2 = x1 * (1.0 + m[4:5]) + m[3:4]
    he_ref[:, :D_MODEL] = h2

    lg = _dot3(h2, rw_ref[...]) + rb_ref[...]
    col = lax.broadcasted_iota(jnp.int32, lg.shape, 1)
    ninf = -jnp.inf
    lgm = jnp.where(col < N_GROUPS, lg, ninf)
    mx = jnp.max(lgm, -1, keepdims=True)
    gi = jnp.min(jnp.where(lgm == mx, col, LANE), -1, keepdims=True)
    pg_top = 1.0 / jnp.sum(jnp.where(col < N_GROUPS, jnp.exp(lg - mx), 0.0), -1, keepdims=True)
    lo4 = N_GROUPS + E_PER_GROUP * gi
    lem = jnp.where(jnp.logical_and(col >= lo4, col < lo4 + E_PER_GROUP), lg, ninf)
    v1 = jnp.max(lem, -1, keepdims=True)
    i1 = jnp.min(jnp.where(lem == v1, col, LANE), -1, keepdims=True)
    lem2 = jnp.where(col == i1, ninf, lem)
    v2 = jnp.max(lem2, -1, keepdims=True)
    i2 = jnp.min(jnp.where(lem2 == v2, col, LANE), -1, keepdims=True)
    e21 = jnp.exp(v2 - v1)
    w1 = pg_top / (1.0 + e21)
    w2 = pg_top * e21 / (1.0 + e21)
    e1, e2 = i1 - N_GROUPS, i2 - N_GROUPS
    first_lo = e1 < e2
    elo, ehi = jnp.minimum(e1, e2), jnp.maximum(e1, e2)
    wlo, whi = jnp.where(first_lo, w1, w2), jnp.where(first_lo, w2, w1)
    llo, lhi = elo - E_PER_GROUP * gi, ehi - E_PER_GROUP * gi
    cls = gi * N_PAIRS + ((llo * (7 - llo)) >> 1) + lhi - llo - 1

    oh = jnp.where(col == cls, 1.0, 0.0)
    ii = lax.broadcasted_iota(jnp.int32, (TM_OUT, TM_OUT), 0)
    jj = lax.broadcasted_iota(jnp.int32, (TM_OUT, TM_OUT), 1)
    before = jnp.where(jj < ii, 1.0, 0.0).astype(BF16)
    cnt = s_cnt[0:1, :]
    rank = jnp.sum(oh * (_dot(before, oh.astype(BF16)) + cnt), -1, keepdims=True)
    cnt = cnt + jnp.sum(oh, 0, keepdims=True)
    s_cnt[...] = jnp.broadcast_to(cnt, s_cnt.shape)
    cnt_ref[...] = jnp.broadcast_to(cnt, cnt_ref.shape)

    info = jnp.zeros(lg.shape, F32)
    for c, val in ((I_CLS, cls.astype(F32)), (I_ELO, elo.astype(F32)), (I_EHI, ehi.astype(F32)),
                   (I_WLO, wlo), (I_WHI, whi), (I_RANK, rank)):
        info = jnp.where(col == c, val, info)
    he_ref[:, D_MODEL:] = info
    it_ref[...] = info.T[:SUB, :]


def _out_call(x, out_a, out_b, hcf, hcb, ogate, mod_l, mix_g, w_out, ln_g, ln_b, rt_w, rt_b, group_of_tile):
    n = x.shape[0]
    row = lambda i: (i, 0)
    fixed = lambda i: (0, 0)
    return pl.pallas_call(
        _out_kernel,
        grid=(n // TM_OUT,),
        in_specs=[pl.BlockSpec((TM_OUT, D_MODEL), row), pl.BlockSpec((TM_OUT, D_A), row),
                  pl.BlockSpec((TM_OUT, D_B), row),
                  pl.BlockSpec((TM_OUT, D_C), row), pl.BlockSpec((TM_OUT, D_C), row), pl.BlockSpec((TM_OUT, D_C), row),
                  pl.BlockSpec((1, 6, D_MODEL), lambda i: (group_of_tile(i), 0, 0)),
                  pl.BlockSpec((1, D_MODEL), fixed), pl.BlockSpec((D_MODEL, D_MODEL), fixed),
                  pl.BlockSpec((1, D_MODEL), fixed), pl.BlockSpec((1, D_MODEL), fixed),
                  pl.BlockSpec((D_MODEL, LANE), fixed), pl.BlockSpec((1, LANE), fixed)],
        out_specs=[pl.BlockSpec((TM_OUT, D_MODEL), row), pl.BlockSpec((TM_OUT, D_EXT), row),
                   pl.BlockSpec((SUB, LANE), fixed), pl.BlockSpec((SUB, TM_OUT), lambda i: (0, i))],
        out_shape=[jax.ShapeDtypeStruct((n, D_MODEL), F32), jax.ShapeDtypeStruct((n, D_EXT), F32),
                   jax.ShapeDtypeStruct((SUB, LANE), F32), jax.ShapeDtypeStruct((SUB, n), F32)],
        scratch_shapes=[pltpu.VMEM((SUB, LANE), F32)],
        compiler_params=_cp(("arbitrary",)),
        name="out_proj_router",
    )(x, out_a, out_b, hcf, hcb, ogate, mod_l, mix_g, w_out, ln_g, ln_b, rt_w, rt_b)


def _row_copy(src_ref, src_row, dst_ref, dst_row, sem):
    return pltpu.make_async_copy(src_ref.at[pl.ds(src_row, 1), :], dst_ref.at[pl.ds(dst_row, 1), :], sem)


def _sorted_row(cls_ref, rank_ref, start_ref, t):
    return start_ref[cls_ref[t]] + rank_ref[t]


def _scatter_kernel(cls_ref, rank_ref, start_ref, pad_ref, na_ref, x_ref, o_ref, ring, z_ref, sem_z, load_sems,
                    row_sems):
    n_tiles = o_ref.shape[0] // TM

    def zero_tile(row):
        return pltpu.make_async_copy(z_ref, o_ref.at[pl.ds(pl.multiple_of(row, TM), TM), :], sem_z)

    def for_zero_tiles(fn):
        def per_class(c, carry):
            row = pad_ref[c]

            @pl.when(row >= 0)
            def _():
                fn(zero_tile(row))

            return carry

        lax.fori_loop(0, N_CLASS, per_class, 0)

        def per_idle(t, carry):
            fn(zero_tile(t * TM))
            return carry

        lax.fori_loop(na_ref[0], n_tiles, per_idle, 0)

    i = pl.program_id(0)
    n_steps = x_ref.shape[0] // TM

    def load(step):
        return pltpu.make_async_copy(x_ref.at[pl.ds(pl.multiple_of(step * TM, TM), TM), :],
                                     ring.at[step % RING], load_sems.at[step % RING])

    def start_rows(step):
        slot = step % RING

        def body(r, carry):
            dst = _sorted_row(cls_ref, rank_ref, start_ref, step * TM + r)
            _row_copy(ring.at[slot], r, o_ref, dst, row_sems.at[slot]).start()
            return carry

        lax.fori_loop(0, TM, body, 0, unroll=DMA_UNROLL)

    def wait_rows(step):
        slot = step % RING
        pltpu.make_async_copy(ring.at[slot], o_ref.at[pl.ds(0, TM), :], row_sems.at[slot]).wait()

    @pl.when(i == 0)
    def _():
        z_ref[...] = jnp.zeros(z_ref.shape, F32)
        for_zero_tiles(lambda cp: cp.start())
        for_zero_tiles(lambda cp: cp.wait())
        load(0).start()

    @pl.when(i >= RING - 1)
    def _():
        wait_rows(i - (RING - 1))

    @pl.when(i + 1 < n_steps)
    def _():
        load(i + 1).start()

    load(i).wait()
    start_rows(i)

    @pl.when(i == n_steps - 1)
    def _():
        for back in range(min(RING - 1, n_steps) - 1, -1, -1):
            wait_rows(i - back)


def _scatter_call(plan, h_ext, n_tiles_max):
    any_spec = pl.BlockSpec(memory_space=pl.ANY)
    return pl.pallas_call(
        _scatter_kernel,
        grid_spec=pltpu.PrefetchScalarGridSpec(
            num_scalar_prefetch=5, grid=(h_ext.shape[0] // TM,),
            in_specs=[any_spec],
            out_specs=any_spec,
            scratch_shapes=[pltpu.VMEM((RING, TM, D_EXT), F32), pltpu.VMEM((TM, D_EXT), F32),
                            pltpu.SemaphoreType.DMA(()),
                            pltpu.SemaphoreType.DMA((RING,)), pltpu.SemaphoreType.DMA((RING,))]),
        out_shape=jax.ShapeDtypeStruct((n_tiles_max * TM, D_EXT), F32),
        compiler_params=_cp(("arbitrary",)),
        name="moe_scatter",
    )(plan["cls"], plan["rank"], plan["row_start"], plan["pad_rows"], plan["n_act"], h_ext)


def _moe_kernel(tg_ref, ng_ref, lo_ref, hi_ref, cg_ref, nv_ref, na_ref, x_ref, w1_hbm, w3_hbm, w2_hbm, o_ref,
                w1_ref, w3_ref, w2_ref, s1, s3, s2, wsem, *, layer):
    t = pl.program_id(0)
    active = t < na_ref[0]
    half = TM // 2

    def group_copies(g):
        rows = pl.ds(g * E_PER_GROUP, E_PER_GROUP)
        return [pltpu.make_async_copy(src.at[layer, rows], dst, wsem.at[k])
                for k, (src, dst) in enumerate(((w1_hbm, w1_ref), (w3_hbm, w3_ref), (w2_hbm, w2_ref)))]

    @pl.when(t == 0)
    def _():
        for cp in group_copies(tg_ref[0]):
            cp.start()

    @pl.when(jnp.logical_not(active))
    def _():
        o_ref[...] = jnp.zeros(o_ref.shape, F32)

    @pl.when(jnp.logical_and(active, cg_ref[t] == 1))
    def _():
        for cp in group_copies(tg_ref[t]):
            cp.wait()
        for e in range(E_PER_GROUP):
            s1[e] = w1_ref[e].astype(BF16)
            s3[e] = w3_ref[e].astype(BF16)
            s2[e] = w2_ref[e].astype(BF16)

        @pl.when(ng_ref[t] >= 0)
        def _():
            for cp in group_copies(ng_ref[t]):
                cp.start()

    def run(rows):
        xe = x_ref[0:rows, :]
        x = xe[:, :D_MODEL].astype(BF16)

        def expert(e, gate):
            a = _dot(x, s1[e])
            hm = a * _sigmoid(a) * _dot(x, s3[e]) * gate
            return _dot(hm.astype(BF16), s2[e])

        o_ref[0:rows, :] = (expert(lo_ref[t], xe[:, D_MODEL + I_WLO:D_MODEL + I_WLO + 1])
                            + expert(hi_ref[t], xe[:, D_MODEL + I_WHI:D_MODEL + I_WHI + 1]))

    @pl.when(jnp.logical_and(active, nv_ref[t] > half))
    def _():
        run(TM)

    @pl.when(jnp.logical_and(active, nv_ref[t] <= half))
    def _():
        run(half)
        o_ref[half:, :] = jnp.zeros((TM - half, D_MODEL), F32)


def _moe_call(layer, plan, x_sorted, w1, w3, w2):
    r = x_sorted.shape[0]
    act = lambda t, *scalars: (jnp.minimum(t, scalars[-1][0] - 1), 0)
    any_spec = pl.BlockSpec(memory_space=pl.ANY)
    up, down = (E_PER_GROUP, D_MODEL, D_E), (E_PER_GROUP, D_E, D_MODEL)
    return pl.pallas_call(
        functools.partial(_moe_kernel, layer=layer),
        grid_spec=pltpu.PrefetchScalarGridSpec(
            num_scalar_prefetch=7, grid=(r // TM,),
            in_specs=[pl.BlockSpec((TM, D_EXT), act), any_spec, any_spec, any_spec],
            out_specs=pl.BlockSpec((TM, D_MODEL), lambda t, *_: (t, 0)),
            scratch_shapes=[pltpu.VMEM(up, F32), pltpu.VMEM(up, F32), pltpu.VMEM(down, F32),
                            pltpu.VMEM(up, BF16), pltpu.VMEM(up, BF16), pltpu.VMEM(down, BF16),
                            pltpu.SemaphoreType.DMA((3,))]),
        out_shape=jax.ShapeDtypeStruct((r, D_MODEL), F32),
        compiler_params=_cp(("arbitrary",)),
        name="moe_experts",
    )(plan["tile_grp"], plan["next_grp"], plan["tile_lo"], plan["tile_hi"], plan["chg_grp"], plan["valid"],
      plan["n_act"], x_sorted, w1, w3, w2)


def _ln2_kernel(cls_ref, rank_ref, start_ref, x1_ref, mod_ref, g_ref, b_ref, y_ref, *rest, n_ctx_tiles):
    o_refs, (buf, sems) = rest[:-2], rest[-2:]
    i = pl.program_id(0)
    n_steps = pl.num_programs(0)

    def start_rows(step):
        slot = step % 2

        def body(r, carry):
            src = _sorted_row(cls_ref, rank_ref, start_ref, step * TM + r)
            _row_copy(y_ref, src, buf.at[slot], r, sems.at[slot]).start()
            return carry

        lax.fori_loop(0, TM, body, 0, unroll=DMA_UNROLL)

    @pl.when(i == 0)
    def _():
        start_rows(0)

    @pl.when(i + 1 < n_steps)
    def _():
        start_rows(i + 1)

    slot = i % 2
    pltpu.make_async_copy(y_ref.at[pl.ds(0, TM), :], buf.at[slot], sems.at[slot]).wait()
    m = mod_ref[0]
    y = _ln_plain(ALPHA * x1_ref[...] + m[5:6] * buf[slot]) * g_ref[...] + b_ref[...]
    if len(o_refs) == 1:
        o_refs[0][...] = y
    else:
        @pl.when(i < n_ctx_tiles)
        def _():
            o_refs[0][...] = y

        @pl.when(i >= n_ctx_tiles)
        def _():
            o_refs[1][...] = y


def _ln2_call(plan, x1, y_sorted, mod_l, ln_g, ln_b, group_of_tile, n_ctx_tiles, split):
    n = x1.shape[0]
    row = lambda i, *_: (i, 0)
    fixed = lambda i, *_: (0, 0)
    if split:
        n_ctx = n_ctx_tiles * TM
        out_specs = [pl.BlockSpec((TM, D_MODEL), lambda i, *_: (jnp.minimum(i, n_ctx_tiles - 1), 0)),
                     pl.BlockSpec((TM, D_MODEL), lambda i, *_: (jnp.maximum(i - n_ctx_tiles, 0), 0))]
        out_shape = [jax.ShapeDtypeStruct((n_ctx, D_MODEL), F32), jax.ShapeDtypeStruct((n - n_ctx, D_MODEL), F32)]
    else:
        out_specs = [pl.BlockSpec((TM, D_MODEL), row)]
        out_shape = [jax.ShapeDtypeStruct((n, D_MODEL), F32)]
    return pl.pallas_call(
        functools.partial(_ln2_kernel, n_ctx_tiles=n_ctx_tiles),
        grid_spec=pltpu.PrefetchScalarGridSpec(
            num_scalar_prefetch=3, grid=(n // TM,),
            in_specs=[pl.BlockSpec((TM, D_MODEL), row),
                      pl.BlockSpec((1, 6, D_MODEL), lambda i, *_: (group_of_tile(i), 0, 0)),
                      pl.BlockSpec((1, D_MODEL), fixed), pl.BlockSpec((1, D_MODEL), fixed),
                      pl.BlockSpec(memory_space=pl.ANY)],
            out_specs=out_specs,
            scratch_shapes=[pltpu.VMEM((2, TM, D_MODEL), F32), pltpu.SemaphoreType.DMA((2,))]),
        out_shape=out_shape,
        compiler_params=_cp(("arbitrary",)),
        name="moe_gather_ln2",
    )(plan["cls"], plan["rank"], plan["row_start"], x1, mod_l, ln_g, ln_b, y_sorted)


def _dft_matrices():
    n2 = 2 * CH
    f = np.arange(CH, dtype=np.float64)[:, None]
    t = np.arange(CH, dtype=np.float64)[None, :]
    ang = 2.0 * np.pi * f * t / n2
    re, im = np.cos(ang), -np.sin(ang)
    im[0, :] = np.cos(np.pi * t[0])
    fwd = np.concatenate([re, im], axis=0)
    scale = np.full((CH, 1), 2.0 / n2)
    scale[0, 0] = 1.0 / n2
    inv = np.concatenate([(re * scale).T, (im * scale).T], axis=1)
    return fwd.astype(np.float32), inv.astype(np.float32)


def _filter_features(L):
    lag = np.arange(-L, L)
    m = np.minimum(np.abs(lag), L - 1)
    t = (np.arange(L, dtype=np.float32) / np.float32(max(L - 1, 1)))[m]
    w = (np.float32(2.0 * math.pi) * np.arange(L, dtype=np.float32) / np.float32(L))[m]
    bands = np.linspace(1e-4, HY_BANDS - 1, HY_BANDS, dtype=np.float32)
    z = np.zeros((2 * L, LANE), np.float32)
    z[:, 0] = t
    z[:, 1:1 + HY_BANDS] = np.cos(w[:, None] * bands)
    z[:, 1 + HY_BANDS:HY_EMB] = -np.sin(w[:, None] * bands)
    lo, hi = math.log(HY_DECAY_TARGET) / 1.5, math.log(HY_DECAY_TARGET) / 0.3
    deltas = np.abs(np.linspace(lo, hi, D_B, dtype=np.float32))
    dec = np.exp(-t[:, None] * deltas)
    return z, dec.astype(np.float32)


def _sincos_2d(rows, cols):
    quarter = D_MODEL // 4
    omega = 1.0 / (10000.0 ** (jnp.arange(quarter, dtype=F32) / quarter))

    def emb(n):
        ang = jnp.arange(n, dtype=F32)[:, None] * omega[None]
        return jnp.concatenate([jnp.sin(ang), jnp.cos(ang)], -1)

    er, ec = emb(rows), emb(cols)
    half = D_MODEL // 2
    pos = jnp.concatenate([jnp.broadcast_to(er[:, None], (rows, cols, half)),
                           jnp.broadcast_to(ec[None], (rows, cols, half))], -1)
    return pos.reshape(rows * cols, D_MODEL)


def _pad_to(x, shape):
    return jnp.pad(x, [(0, s - d) for d, s in zip(x.shape, shape)])


def _block_diag(w):
    eye = jnp.eye(H_A, dtype=w.dtype)
    return jnp.einsum("hij,hg->higj", w, eye).reshape(D_A, D_A)


_PAIR_LO = np.array([0, 0, 0, 1, 1, 2], np.int32)
_PAIR_HI = np.array([1, 2, 3, 2, 3, 3], np.int32)


def _routing_plan(info_t, counts, n_tiles_max):
    cnt = counts[0, :N_CLASS].astype(jnp.int32)
    tiles = (cnt + TM - 1) // TM
    tile_end = jnp.cumsum(tiles)
    n_act = tile_end[-1]
    t = jnp.minimum(jnp.arange(n_tiles_max, dtype=jnp.int32), n_act - 1)
    tcls = jnp.minimum(jnp.sum((tile_end[None, :] <= t[:, None]).astype(jnp.int32), 1), N_CLASS - 1)
    grp, pair = (tcls // N_PAIRS).astype(jnp.int32), tcls % N_PAIRS
    valid = jnp.clip(cnt[tcls] - (t - (tile_end - tiles)[tcls]) * TM, 0, TM).astype(jnp.int32)
    first = jnp.ones((1,), jnp.int32)
    changed = lambda e: jnp.concatenate([first, (e[1:] != e[:-1]).astype(jnp.int32)])
    chg_grp = changed(grp)
    idx = jnp.arange(n_tiles_max, dtype=jnp.int32)
    later_change = jnp.logical_and(idx[None, :] > idx[:, None], chg_grp[None, :] == 1)
    next_pos = jnp.min(jnp.where(later_change, idx[None, :], n_tiles_max), axis=1)
    next_grp = jnp.where(next_pos < n_tiles_max, grp[jnp.minimum(next_pos, n_tiles_max - 1)], -1).astype(jnp.int32)
    return {"cls": info_t[I_CLS].astype(jnp.int32), "rank": info_t[I_RANK].astype(jnp.int32),
            "row_start": ((tile_end - tiles) * TM).astype(jnp.int32), "tile_grp": grp, "chg_grp": chg_grp, "next_grp": next_grp,
            "tile_lo": jnp.asarray(_PAIR_LO)[pair], "tile_hi": jnp.asarray(_PAIR_HI)[pair], "valid": valid,
            "n_act": n_act.reshape(1).astype(jnp.int32),
            "pad_rows": jnp.where(tiles > 0, (tile_end - 1) * TM, -1).astype(jnp.int32)}


def kernel(x_prompt, x_sample, c, state_lru, state_mlstm_C, state_mlstm_n, state_mlstm_m, c_ctx, w_ada, b_ada, w_in, b_in, conv_a_w, conv_a_b, lru_wa, lru_ba, lru_wx, lru_bx, lru_lam, conv_b_w, conv_b_b, hy_w1, hy_b1, hy_w2, hy_b2, hy_freq, hy_w3, hy_bias, mix_g, w_out, ln1_g, ln1_b, rt_wg, rt_bg, rt_we, rt_be, moe_w1, moe_w3, moe_w2, ln2_g, ln2_b):
    B, l_ctx, D = x_prompt.shape
    b_lat, l_lat, _ = x_sample.shape
    n_ctx, n_lat = B * l_ctx, b_lat * l_lat
    n = n_ctx + n_lat
    assert D == D_MODEL and w_in.shape[-1] == D_MAIN + N_GATE
    assert SEG % l_ctx == 0 and l_lat == SEG and l_ctx % CH == 0 and n_ctx % SEG == 0
    assert l_ctx == CH, "the mLSTM step schedule assumes one chunk per context sequence"
    assert 1 + b_lat <= SUB
    n_ctx_blk = n_ctx // SEG
    n_ctx_tiles = n_ctx // TM
    tiles_per_lat = l_lat // TM
    nc_lat = l_lat // CH

    def group_of(tile_rows):
        first_lat, per_seq = n_ctx // tile_rows, l_lat // tile_rows
        return lambda i: jnp.where(i < first_lat, 0, 1 + (i - first_lat) // per_seq)

    group_of_tile = group_of(TM)

    cond = jnp.concatenate([c_ctx[None], c, jnp.zeros((SUB - 1 - b_lat, D), F32)], 0)
    mod = _mod_call(cond, w_ada, b_ada).reshape(DEPTH, SUB, 6, D)
    pos = _sincos_2d(l_lat // GRID_W, GRID_W)
    x = _entry_call(x_prompt.reshape(n_ctx, D), x_sample.reshape(n_lat, D), pos)

    fwd_np, inv_np = _dft_matrices()
    fwd32 = jnp.asarray(fwd_np)
    fwd16, inv16 = fwd32.astype(BF16), jnp.asarray(inv_np).astype(BF16)
    fw1 = _pad_to(hy_w1, (DEPTH, LANE, LANE))
    fb1 = _pad_to(hy_b1[:, None, :], (DEPTH, 1, LANE))
    fw2 = _pad_to(hy_w2, (DEPTH, LANE, LANE))
    fb2 = _pad_to(hy_b2[:, None, :], (DEPTH, 1, LANE))
    ffr = _pad_to(hy_freq[:, None, :], (DEPTH, 1, LANE))
    fw3 = _pad_to(hy_w3, (DEPTH, LANE, HY_ORDER * 2 * D_B))
    spectra = {}
    for L in (l_ctx, l_lat):
        z_np, dec_np = _filter_features(L)
        spectra[L] = _filt_call(L, jnp.asarray(z_np), jnp.asarray(dec_np), fw1, fb1, fw2, fb2, ffr, fw3, fwd32)

    w_in_t = jnp.swapaxes(w_in, 1, 2)
    lat_slots = SEG // l_ctx
    st_lru, st_c, st_n, st_m = [], [], [], []
    for l in range(DEPTH):
        b_main = b_in[l, None, :D_MAIN]
        k_lo, k_hi = _Z_CUTS[_K_CUT], _Z_CUTS[_K_CUT + 1]
        b_kt = jnp.broadcast_to(b_in[l, k_lo:k_hi, None], (D_C, TM_IN))
        w_gate = _pad_to(w_in[l, :, D_MAIN:], (D, LANE))
        b_gate = _pad_to(b_in[l, None, D_MAIN:], (1, LANE))
        xa, ya, hyb, q, v, og, gates, kt = _in_call(l, x, mod[l], w_in_t, b_main, b_kt, w_gate, b_gate,
                                                    group_of(TM_IN))

        lru_w = jnp.concatenate([_block_diag(lru_wa[l, 0]), _block_diag(lru_wx[l, 0]),
                                 _block_diag(lru_wa[l, 1]), _block_diag(lru_wx[l, 1])], 1).astype(BF16)
        lru_b = jnp.concatenate([lru_ba[l, 0], lru_bx[l, 0], lru_ba[l, 1], lru_bx[l, 1]])[None]
        h0_lat = _pad_to(state_lru[:, l][:, None], (b_lat, lat_slots, 2, D_A))
        h0_all = jnp.concatenate([jnp.zeros((n_ctx_blk, lat_slots, 2, D_A), F32), h0_lat], 0)
        out_a, lru_last = _lru_call(xa, ya, conv_a_w[l], conv_a_b[l, None], lru_w, lru_b, lru_lam[l], h0_all,
                                    l_ctx, l_lat, n_ctx_blk)

        habc, hd0c = spectra[l_ctx]
        habl, hd0l = spectra[l_lat]
        out_b = _hy_call(l, hyb, conv_b_w[l], conv_b_b[l, None], fwd16, inv16, habc, hd0c, habl, hd0l,
                         hy_bias[l][:, None, :], l_ctx, l_lat, n_ctx_blk)

        n0 = state_mlstm_n[:, l].reshape(b_lat, 2 * H_C, DK)
        cx0 = jnp.concatenate([state_mlstm_C[:, l].reshape(b_lat, 2 * H_C, DK, DK), n0[..., None],
                               jnp.zeros((b_lat, 2 * H_C, DK, DK - 1), F32)], -1)
        m0 = _pad_to(state_mlstm_m[:, l], (b_lat, SUB, LANE))
        hcf, hcb, c_fin, n_fin, m_fin = _mlstm_call(q, v, kt, gates, cx0, n0, m0, n_ctx // CH, nc_lat)

        rt_w = _pad_to(jnp.concatenate([rt_wg[l], rt_we[l]], 1), (D, LANE))
        rt_b = _pad_to(jnp.concatenate([rt_bg[l], rt_be[l]])[None], (1, LANE))
        x1, h_ext, counts, info_t = _out_call(x, out_a, out_b, hcf, hcb, og, mod[l], mix_g[l, None],
                                              w_out[l].astype(BF16), ln1_g[l, None], ln1_b[l, None], rt_w, rt_b,
                                              group_of(TM_OUT))

        n_tiles_max = n // TM + N_CLASS
        plan = _routing_plan(info_t, counts, n_tiles_max)
        x_sorted = _scatter_call(plan, h_ext, n_tiles_max)
        y_sorted = _moe_call(l, plan, x_sorted, moe_w1, moe_w3, moe_w2)
        outs = _ln2_call(plan, x1, y_sorted, mod[l], ln2_g[l, None], ln2_b[l, None], group_of_tile,
                         n_ctx_tiles, split=(l == DEPTH - 1))
        x = outs[0]

        st_lru.append(lru_last[:n_ctx_blk].reshape(B, 2, D_A))
        st_c.append(c_fin.reshape(B, 2, H_C, DK, DK))
        st_n.append(n_fin.reshape(B, 2, H_C, DK))
        st_m.append(m_fin[:, :2, :H_C])

    return (outs[0].reshape(B, l_ctx, D), outs[1].reshape(b_lat, l_lat, D),
            jnp.stack(st_lru, 1), jnp.stack(st_c, 1), jnp.stack(st_n, 1), jnp.stack(st_m, 1))
```

```python
import functools
import math

import numpy as np
import jax
import jax.numpy as jnp
from jax import lax
from jax.experimental import pallas as pl
from jax.experimental.pallas import tpu as pltpu

F32 = jnp.float32
BF16 = jnp.bfloat16

D_MODEL = 1024
DEPTH = 2
GRID_W = 64
D_A = 256
H_A = 4
BA = D_A // H_A
LRU_C = 8.0
D_B = 256
HY_ORDER = 2
HY_BANDS = 16
HY_EMB = 1 + 2 * HY_BANDS
HY_FH = 64
HY_DECAY_TARGET = 1e-2
D_C = 512
H_C = 4
DK = D_C // H_C
N_GROUPS = 4
E_PER_GROUP = 4
N_EXP = N_GROUPS * E_PER_GROUP
N_PAIRS = 6
N_CLASS = N_GROUPS * N_PAIRS
D_E = 512
ALPHA = (2 * DEPTH) ** 0.25
EPS = 1e-6
D_MAIN = 2 * D_A + 3 * D_B + 4 * D_C
N_GATE = 4 * H_C

LANE = 128
SUB = 8
VMEM_LIMIT = 56 * 1024 * 1024

CH = 256
SEG = 2048
TM = 256
TM_IN = 512
TM_OUT = 512
TM_PERM = 512
D_EXT = D_MODEL + LANE
FS = 64
DMA_UNROLL = 8
RING = 3

I_CLS, I_ELO, I_EHI, I_WLO, I_WHI, I_RANK = range(6)


def _cp(sem, vmem=VMEM_LIMIT):
    return pltpu.CompilerParams(dimension_semantics=sem, vmem_limit_bytes=vmem)


def _dot(a, b):
    return jnp.dot(a, b, preferred_element_type=F32)


def _split2(x):
    hi = x.astype(BF16)
    lo = (x - hi.astype(F32)).astype(BF16)
    return hi, lo


def _dot3(a, b):
    ah, al = _split2(a)
    bh, bl = _split2(b)
    return _dot(ah, bh) + (_dot(ah, bl) + _dot(al, bh))


def _split3(x):
    hi = x.astype(BF16)
    r1 = x - hi.astype(F32)
    mid = r1.astype(BF16)
    lo = (r1 - mid.astype(F32)).astype(BF16)
    return hi, mid, lo


def _sigmoid(x):
    return 1.0 / (1.0 + jnp.exp(-x))


def _log_sigmoid(x):
    return jnp.minimum(x, 0.0) - jnp.log1p(jnp.exp(-jnp.abs(x)))


def _gelu_tanh(x):
    return 0.5 * x * (1.0 + jnp.tanh(math.sqrt(2.0 / math.pi) * (x + 0.044715 * (x * x * x))))


def _ln_plain(x):
    mu = jnp.mean(x, -1, keepdims=True)
    xc = x - mu
    var = jnp.mean(xc * xc, -1, keepdims=True)
    return xc * lax.rsqrt(var + EPS)


def _rms(x):
    return x * lax.rsqrt(jnp.mean(x * x, -1, keepdims=True) + EPS)


def _halo_rows(ref, start, rows):
    total = ref.shape[0]
    prev = ref[pl.ds(pl.multiple_of(jnp.maximum(start - SUB, 0), SUB), SUB), :]
    main = ref[pl.ds(start, rows), :]
    nxt = ref[pl.ds(pl.multiple_of(jnp.minimum(start + rows, total - SUB), SUB), SUB), :]
    return jnp.concatenate([prev, main, nxt], axis=0), main


def _mod_kernel(c_ref, w_ref, b_ref, o_ref):
    c = c_ref[...]
    o_ref[0] = _dot3(c * _sigmoid(c), w_ref[0]) + b_ref[0]


def _mod_call(cond, w_ada, b_ada):
    tn = 1536
    n6 = w_ada.shape[-1]
    return pl.pallas_call(
        _mod_kernel,
        grid=(DEPTH, n6 // tn),
        in_specs=[pl.BlockSpec((SUB, D_MODEL), lambda l, j: (0, 0)),
                  pl.BlockSpec((1, D_MODEL, tn), lambda l, j: (l, 0, j)),
                  pl.BlockSpec((1, 1, tn), lambda l, j: (l, 0, j))],
        out_specs=pl.BlockSpec((1, SUB, tn), lambda l, j: (l, 0, j)),
        out_shape=jax.ShapeDtypeStruct((DEPTH, SUB, n6), F32),
        compiler_params=_cp(("parallel", "parallel")),
        name="adaln_mod",
    )(cond, w_ada, b_ada.reshape(DEPTH, 1, n6))


def _entry_kernel(xc_ref, xl_ref, pos_ref, o_ref, *, n_ctx_tiles):
    i = pl.program_id(0)

    @pl.when(i < n_ctx_tiles)
    def _():
        o_ref[...] = _ln_plain(xc_ref[...])

    @pl.when(i >= n_ctx_tiles)
    def _():
        o_ref[...] = _ln_plain(xl_ref[...] + pos_ref[...])


def _entry_call(xc, xl, pos):
    tm = 512
    n_ctx, n_lat, l_lat = xc.shape[0], xl.shape[0], pos.shape[0]
    nct = n_ctx // tm
    per_seq = l_lat // tm
    return pl.pallas_call(
        functools.partial(_entry_kernel, n_ctx_tiles=nct),
        grid=((n_ctx + n_lat) // tm,),
        in_specs=[pl.BlockSpec((tm, D_MODEL), lambda i: (jnp.minimum(i, nct - 1), 0)),
                  pl.BlockSpec((tm, D_MODEL), lambda i: (jnp.maximum(i - nct, 0), 0)),
                  pl.BlockSpec((tm, D_MODEL), lambda i: (jnp.maximum(i - nct, 0) % per_seq, 0))],
        out_specs=pl.BlockSpec((tm, D_MODEL), lambda i: (i, 0)),
        out_shape=jax.ShapeDtypeStruct((n_ctx + n_lat, D_MODEL), F32),
        compiler_params=_cp(("parallel",)),
        name="entry_ln",
    )(xc, xl, pos)


_Z_CUTS = (0, D_A, 2 * D_A, 2 * D_A + 3 * D_B, 2 * D_A + 3 * D_B + D_C, 2 * D_A + 3 * D_B + 2 * D_C,
           2 * D_A + 3 * D_B + 3 * D_C, D_MAIN)


_K_CUT = 4
_ROW_CUTS = tuple(c for i, c in enumerate(zip(_Z_CUTS[:-1], _Z_CUTS[1:])) if i != _K_CUT)
_NT = (((1,), (1,)), ((), ()))


def _in_kernel(x_ref, mod_ref, wt_ref, b_ref, bkt_ref, wg_ref, bg_ref, *refs):
    out_refs, wt16 = refs[:-1], refs[-1]

    @pl.when(pl.program_id(0) == 0)
    def _():
        wt16[...] = wt_ref[0, :D_MAIN, :].astype(BF16)

    m = mod_ref[0]
    h = x_ref[...] * (1.0 + m[1:2]) + m[0:1]
    hb = h.astype(BF16)
    dg = lambda a, b: lax.dot_general(a, b, _NT, preferred_element_type=F32)
    for ref, (a, b) in zip(out_refs[:-2], _ROW_CUTS):
        ref[...] = dg(hb, wt16[a:b, :]) + b_ref[:, a:b]
    g_ref, kt_ref = out_refs[-2:]
    g_ref[...] = _dot3(h, wg_ref[...]) + bg_ref[...]
    kt_ref[...] = dg(wt16[_Z_CUTS[_K_CUT]:_Z_CUTS[_K_CUT + 1], :], hb) + bkt_ref[...]


def _in_call(layer, x, mod_l, w_in_t, b_main, b_kt, w_gate, b_gate, group_of_tile):
    n = x.shape[0]
    widths = [b - a for a, b in _ROW_CUTS] + [LANE]
    row = lambda i: (i, 0)
    fixed = lambda i: (0, 0)
    return pl.pallas_call(
        _in_kernel,
        grid=(n // TM_IN,),
        in_specs=[pl.BlockSpec((TM_IN, D_MODEL), row),
                  pl.BlockSpec((1, 6, D_MODEL), lambda i: (group_of_tile(i), 0, 0)),
                  pl.BlockSpec((1, w_in_t.shape[1], D_MODEL), lambda i: (layer, 0, 0), pipeline_mode=pl.Buffered(1)),
                  pl.BlockSpec((1, D_MAIN), fixed),
                  pl.BlockSpec((D_C, TM_IN), fixed),
                  pl.BlockSpec((D_MODEL, LANE), fixed), pl.BlockSpec((1, LANE), fixed)],
        out_specs=[pl.BlockSpec((TM_IN, w), row) for w in widths] + [pl.BlockSpec((D_C, TM_IN), lambda i: (0, i))],
        out_shape=[jax.ShapeDtypeStruct((n, w), F32) for w in widths] + [jax.ShapeDtypeStruct((D_C, n), F32)],
        scratch_shapes=[pltpu.VMEM((D_MAIN, D_MODEL), BF16)],
        compiler_params=_cp(("arbitrary",)),
        name="in_proj",
    )(x, mod_l, w_in_t, b_main, b_kt, w_gate, b_gate)


def _lru_variant(L, xa_ref, ya_ref, cw_ref, cb_ref, wg_ref, bg_ref, lam_ref, h0_ref, o_ref, st_ref,
                 s_af, s_bf, s_ab, s_bb):
    nch, nseq, ntile = SEG // CH, SEG // L, L // SUB
    lam = lam_ref[...]
    sp = jnp.maximum(-lam, 0.0) + jnp.log1p(jnp.exp(-jnp.abs(lam)))
    cw = cw_ref[...]
    cb = cb_ref[...]
    row = lax.broadcasted_iota(jnp.int32, (CH, 1), 0)
    sub3 = lax.broadcasted_iota(jnp.int32, (1, SUB, 1), 1)

    def gates_and_tile_scan(c, carry):
        start = pl.multiple_of(c * CH, CH)
        xcat, main = _halo_rows(xa_ref, start, CH)
        tpos = (start + row) & (L - 1)
        xm2 = jnp.where(tpos >= 2, xcat[SUB - 2:SUB - 2 + CH], 0.0)
        xm1 = jnp.where(tpos >= 1, xcat[SUB - 1:SUB - 1 + CH], 0.0)
        xp1 = jnp.where(tpos <= L - 2, xcat[SUB + 1:SUB + 1 + CH], 0.0)
        xc = cw[0:1] * xm2 + cw[1:2] * xm1 + cw[2:3] * main + cw[3:4] * xp1 + cb
        g = _dot(xc.astype(BF16), wg_ref[...]) + bg_ref[...]
        for d, (sa, sb) in enumerate(((s_af, s_bf), (s_ab, s_bb))):
            r = _sigmoid(g[:, 2 * d * D_A:(2 * d + 1) * D_A])
            ig = _sigmoid(g[:, (2 * d + 1) * D_A:(2 * d + 2) * D_A])
            a = jnp.exp(-LRU_C * r * sp[d:d + 1])
            y = 1.0 - a * a
            b = jnp.where(y > 0.0, y * lax.rsqrt(y), 0.0) * (ig * xc)
            a3, b3 = a.reshape(CH // SUB, SUB, D_A), b.reshape(CH // SUB, SUB, D_A)
            for s in (1, 2, 4):
                shift, keep = (s, sub3 >= s) if d == 0 else (SUB - s, sub3 < SUB - s)
                b3 = a3 * jnp.where(keep, pltpu.roll(b3, shift, 1), 0.0) + b3
                a3 = a3 * jnp.where(keep, pltpu.roll(a3, shift, 1), 1.0)
            sa[pl.ds(start, CH), :] = a3.reshape(CH, D_A)
            sb[pl.ds(start, CH), :] = b3.reshape(CH, D_A)
        return carry

    lax.fori_loop(0, nch, gates_and_tile_scan, 0)

    def carry_tiles(k, carry):
        cf, cbk = carry
        nf, nb = [], []
        for s in range(nseq):
            rf = pl.multiple_of(s * L + k * SUB, SUB)
            hf = s_af[pl.ds(rf, SUB), :] * cf[s] + s_bf[pl.ds(rf, SUB), :]
            s_bf[pl.ds(rf, SUB), :] = hf
            nf.append(hf[SUB - 1:SUB, :])
            rb = pl.multiple_of(s * L + (ntile - 1 - k) * SUB, SUB)
            hb = s_ab[pl.ds(rb, SUB), :] * cbk[s] + s_bb[pl.ds(rb, SUB), :]
            s_bb[pl.ds(rb, SUB), :] = hb
            nb.append(hb[0:1, :])
        return tuple(nf), tuple(nb)

    cf0 = tuple(h0_ref[0, s, 0:1, :] for s in range(nseq))
    cb0 = tuple(h0_ref[0, s, 1:2, :] for s in range(nseq))
    cf, cbk = lax.fori_loop(0, ntile, carry_tiles, (cf0, cb0))

    st_ref[...] = jnp.zeros(st_ref.shape, F32)
    for s in range(nseq):
        st_ref[0, s] = jnp.concatenate([cf[s], cbk[s]], axis=0)

    def finish(c, carry):
        start = pl.multiple_of(c * CH, CH)
        h = s_bf[pl.ds(start, CH), :] + s_bb[pl.ds(start, CH), :]
        o_ref[pl.ds(start, CH), :] = _rms(_gelu_tanh(ya_ref[pl.ds(start, CH), :]) * h)
        return carry

    lax.fori_loop(0, nch, finish, 0)


def _lru_kernel(*refs, l_ctx, l_lat, n_ctx_blk):
    i = pl.program_id(0)

    @pl.when(i < n_ctx_blk)
    def _():
        _lru_variant(l_ctx, *refs)

    @pl.when(i >= n_ctx_blk)
    def _():
        _lru_variant(l_lat, *refs)


def _lru_call(xa, ya, conv_w, conv_b, w_gate, b_gate, lam, h0_all, l_ctx, l_lat, n_ctx_blk):
    n = xa.shape[0]
    nblk = n // SEG
    row = lambda i: (i, 0)
    fixed = lambda i: (0, 0)
    slots = SEG // l_ctx
    return pl.pallas_call(
        functools.partial(_lru_kernel, l_ctx=l_ctx, l_lat=l_lat, n_ctx_blk=n_ctx_blk),
        grid=(nblk,),
        in_specs=[pl.BlockSpec((SEG, D_A), row), pl.BlockSpec((SEG, D_A), row),
                  pl.BlockSpec((4, D_A), fixed), pl.BlockSpec((1, D_A), fixed),
                  pl.BlockSpec((D_A, 4 * D_A), fixed), pl.BlockSpec((1, 4 * D_A), fixed),
                  pl.BlockSpec((2, D_A), fixed),
                  pl.BlockSpec((1, slots, 2, D_A), lambda i: (i, 0, 0, 0))],
        out_specs=[pl.BlockSpec((SEG, D_A), row),
                   pl.BlockSpec((1, slots, 2, D_A), lambda i: (i, 0, 0, 0))],
        out_shape=[jax.ShapeDtypeStruct((n, D_A), F32),
                   jax.ShapeDtypeStruct((nblk, slots, 2, D_A), F32)],
        scratch_shapes=[pltpu.VMEM((SEG, D_A), F32) for _ in range(4)],
        compiler_params=_cp(("parallel",)),
        name="rglru",
    )(xa, ya, conv_w, conv_b, w_gate, b_gate, lam, h0_all)


def _filt_kernel(z_ref, dec_ref, w1_ref, b1_ref, w2_ref, b2_ref, fr_ref, w3_ref, fwd_ref,
                 oab_ref, od0_ref, s_k, s_kf, *, L):
    nblk = 2 * L // CH
    d_idx = pl.program_id(1)
    row = lax.broadcasted_iota(jnp.int32, (CH, 1), 0)

    @pl.when(d_idx == 0)
    def _():
        fr = fr_ref[0]

        def taps(c, carry):
            start = pl.multiple_of(c * CH, CH)
            h1 = jnp.sin(fr * (_dot3(z_ref[pl.ds(start, CH), :], w1_ref[0]) + b1_ref[0]))
            h2 = jnp.sin(fr * (_dot3(h1, w2_ref[0]) + b2_ref[0]))
            t = _dot3(h2, w3_ref[0])
            dec = dec_ref[pl.ds(start, CH), :]
            rg = start + row
            for o in range(HY_ORDER):
                fwd_t = t[:, (2 * o) * D_B:(2 * o + 1) * D_B]
                bwd_t = t[:, (2 * o + 1) * D_B:(2 * o + 2) * D_B]
                ko = jnp.where(rg < L, bwd_t, fwd_t) * dec
                s_k[pl.ds(start, CH), o * D_B:(o + 1) * D_B] = jnp.where(rg == 0, 0.0, ko)
            return carry

        lax.fori_loop(0, nblk, taps, 0)
        fwd = fwd_ref[...]

        def spectra(e, carry):
            start = pl.multiple_of(e * CH, CH)
            s_kf[e] = _dot3(fwd, s_k[pl.ds(start, CH), :])
            return carry

        lax.fori_loop(0, nblk, spectra, 0)

    kd = s_kf[d_idx + 1]
    km = s_kf[d_idx]
    k0 = s_k[pl.ds(pl.multiple_of(d_idx * CH, CH), 1), :]
    sgn = jnp.where((row & 1) == 0, 1.0, -1.0)
    a = kd[:CH] + sgn * (km[:CH] - k0)
    b = jnp.where(row == 0, 0.0, kd[CH:] + sgn * km[CH:])
    hn = kd[CH:CH + 1] + km[CH:CH + 1] - k0
    for o in range(HY_ORDER):
        oab_ref[0, o, 0, 0] = a[:, o * D_B:(o + 1) * D_B]
        oab_ref[0, o, 0, 1] = b[:, o * D_B:(o + 1) * D_B]
        od0_ref[0, o, 0] = jnp.broadcast_to(hn[:, o * D_B:(o + 1) * D_B], (SUB, D_B))


def _filt_call(L, z, dec, w1, b1, w2, b2, fr, w3, fwd32):
    nd = 2 * (L // CH) - 1
    fixed = lambda l, d: (0, 0)
    lay3 = lambda l, d: (l, 0, 0)
    return pl.pallas_call(
        functools.partial(_filt_kernel, L=L),
        grid=(DEPTH, nd),
        in_specs=[pl.BlockSpec((2 * L, LANE), fixed), pl.BlockSpec((2 * L, D_B), fixed),
                  pl.BlockSpec((1, LANE, LANE), lay3), pl.BlockSpec((1, 1, LANE), lay3),
                  pl.BlockSpec((1, LANE, LANE), lay3), pl.BlockSpec((1, 1, LANE), lay3),
                  pl.BlockSpec((1, 1, LANE), lay3),
                  pl.BlockSpec((1, LANE, HY_ORDER * 2 * D_B), lay3),
                  pl.BlockSpec((2 * CH, CH), fixed)],
        out_specs=[pl.BlockSpec((1, HY_ORDER, 1, 2, CH, D_B), lambda l, d: (l, 0, d, 0, 0, 0)),
                   pl.BlockSpec((1, HY_ORDER, 1, SUB, D_B), lambda l, d: (l, 0, d, 0, 0))],
        out_shape=[jax.ShapeDtypeStruct((DEPTH, HY_ORDER, nd, 2, CH, D_B), F32),
                   jax.ShapeDtypeStruct((DEPTH, HY_ORDER, nd, SUB, D_B), F32)],
        scratch_shapes=[pltpu.VMEM((2 * L, HY_ORDER * D_B), F32),
                        pltpu.VMEM((2 * L // CH, 2 * CH, HY_ORDER * D_B), F32)],
        compiler_params=_cp(("parallel", "arbitrary")),
        name=f"hyena_filter_{L}",
    )(z, dec, w1, b1, w2, b2, fr, w3, fwd32)


def _hy_variant(L, o_idx, hy_ref, cw_ref, cb_ref, fwd_ref, inv_ref, hab_ref, hd0_ref, bias_ref, o_ref,
                s_y, s_x, s_u, s_v):
    nch, nseq, P = SEG // CH, SEG // L, L // CH
    row = lax.broadcasted_iota(jnp.int32, (CH, 1), 0)
    frow = lax.broadcasted_iota(jnp.int32, (FS, 1), 0)

    @pl.when(o_idx == 0)
    def _():
        cw = cw_ref[...]
        cb = cb_ref[...]

        def short_conv(c, carry):
            start = pl.multiple_of(c * CH, CH)
            xcat, main = _halo_rows(hy_ref, start, CH)
            tpos = (start + row) & (L - 1)
            xm1 = jnp.where(tpos >= 1, xcat[SUB - 1:SUB - 1 + CH], 0.0)
            xp1 = jnp.where(tpos <= L - 2, xcat[SUB + 1:SUB + 1 + CH], 0.0)
            hc = cw[0:1] * xm1 + cw[1:2] * main + cw[2:3] * xp1 + cb
            s_y[pl.ds(start, CH), :] = hc[:, :D_B]
            s_x[0, pl.ds(start, CH), :] = hc[:, D_B:2 * D_B]
            s_x[1, pl.ds(start, CH), :] = hc[:, 2 * D_B:]
            return carry

        lax.fori_loop(0, nch, short_conv, 0)

    bias = bias_ref[0]

    def loop(n, body, init):
        return body(0, init) if n == 1 else lax.fori_loop(0, n, body, init)

    def one_sequence(s, slot):
        base = s * L
        u0 = slot * P

        def forward_dft(j, cc):
            r = pl.multiple_of(base + j * CH, CH)
            s_u[u0 + j] = _dot(fwd_ref[...], s_y[pl.ds(r, CH), :].astype(BF16))
            return cc

        loop(P, forward_dft, 0)

        def output_block(i, cc):
            for fs in range(CH // FS):
                lo = fs * FS

                def accumulate(j, acc):
                    yre, yim = acc
                    d = i - j + (P - 1)
                    ure = s_u[u0 + j, lo:lo + FS, :]
                    uim = s_u[u0 + j, CH + lo:CH + lo + FS, :]
                    a = hab_ref[0, 0, d, 0, lo:lo + FS, :]
                    b = hab_ref[0, 0, d, 1, lo:lo + FS, :]
                    dd = jnp.where(frow == 0, hd0_ref[0, 0, d, 0:1, :], a) if fs == 0 else a
                    return yre + ure * a - uim * b, yim + ure * b + uim * dd

                zero = jnp.zeros((FS, D_B), F32)
                yre, yim = loop(P, accumulate, (zero, zero))
                s_v[slot, lo:lo + FS, :] = yre.astype(BF16)
                s_v[slot, CH + lo:CH + lo + FS, :] = yim.astype(BF16)
            yc = _dot(inv_ref[...], s_v[slot])
            r = pl.multiple_of(base + i * CH, CH)
            s_y[pl.ds(r, CH), :] = s_x[o_idx, pl.ds(r, CH), :] * (yc + s_y[pl.ds(r, CH), :] * bias)
            return cc

        loop(P, output_block, 0)

    def single_block_pair(p, carry):
        rows = [pl.multiple_of((2 * p + k) * L, CH) for k in range(2)]
        ys = [s_y[pl.ds(r, CH), :] for r in rows]
        gates = [s_x[o_idx, pl.ds(r, CH), :] for r in rows]
        a = hab_ref[0, 0, 0, 0]
        b = hab_ref[0, 0, 0, 1]
        dd = jnp.where(row == 0, hd0_ref[0, 0, 0, 0:1, :], a)
        outs = []
        for y, gate in zip(ys, gates):
            u = _dot(fwd_ref[...], y.astype(BF16))
            ure, uim = u[:CH], u[CH:]
            v = jnp.concatenate([ure * a - uim * b, ure * b + uim * dd], axis=0).astype(BF16)
            outs.append(gate * (_dot(inv_ref[...], v) + y * bias))
        for r, out in zip(rows, outs):
            s_y[pl.ds(r, CH), :] = out
        return carry

    if P == 1 and nseq % 2 == 0:
        lax.fori_loop(0, nseq // 2, single_block_pair, 0)
    else:
        lax.fori_loop(0, nseq, lambda s, carry: (one_sequence(s, 0), carry)[1], 0)

    @pl.when(o_idx == HY_ORDER - 1)
    def _():
        def finish(c, carry):
            start = pl.multiple_of(c * CH, CH)
            o_ref[pl.ds(start, CH), :] = _rms(s_y[pl.ds(start, CH), :])
            return carry

        lax.fori_loop(0, nch, finish, 0)


def _hy_kernel(hy_ref, cw_ref, cb_ref, fwd_ref, inv_ref, habc_ref, hd0c_ref, habl_ref, hd0l_ref, bias_ref,
               o_ref, s_y, s_x, s_u, s_v, *, l_ctx, l_lat, n_ctx_blk):
    i = pl.program_id(0)
    o_idx = pl.program_id(1)
    scratch = (s_y, s_x, s_u, s_v)

    @pl.when(i < n_ctx_blk)
    def _():
        _hy_variant(l_ctx, o_idx, hy_ref, cw_ref, cb_ref, fwd_ref, inv_ref, habc_ref, hd0c_ref, bias_ref, o_ref,
                    *scratch)

    @pl.when(i >= n_ctx_blk)
    def _():
        _hy_variant(l_lat, o_idx, hy_ref, cw_ref, cb_ref, fwd_ref, inv_ref, habl_ref, hd0l_ref, bias_ref, o_ref,
                    *scratch)


def _hy_call(layer, hyb, conv_w, conv_b, fwd, inv, habc, hd0c, habl, hd0l, bias, l_ctx, l_lat, n_ctx_blk):
    n = hyb.shape[0]
    ndc, ndl = habc.shape[2], habl.shape[2]
    pmax = max(l_ctx, l_lat) // CH
    row = lambda i, o: (i, 0)
    fixed = lambda i, o: (0, 0)
    lat_o = lambda i, o: jnp.where(i >= n_ctx_blk, o, 0)
    ctx_o = lambda i, o: jnp.where(i < n_ctx_blk, o, 0)
    return pl.pallas_call(
        functools.partial(_hy_kernel, l_ctx=l_ctx, l_lat=l_lat, n_ctx_blk=n_ctx_blk),
        grid=(n // SEG, HY_ORDER),
        in_specs=[pl.BlockSpec((SEG, 3 * D_B), row),
                  pl.BlockSpec((3, 3 * D_B), fixed), pl.BlockSpec((1, 3 * D_B), fixed),
                  pl.BlockSpec((2 * CH, CH), fixed), pl.BlockSpec((CH, 2 * CH), fixed),
                  pl.BlockSpec((1, 1, ndc, 2, CH, D_B), lambda i, o: (layer, ctx_o(i, o), 0, 0, 0, 0)),
                  pl.BlockSpec((1, 1, ndc, SUB, D_B), lambda i, o: (layer, ctx_o(i, o), 0, 0, 0)),
                  pl.BlockSpec((1, 1, ndl, 2, CH, D_B), lambda i, o: (layer, lat_o(i, o), 0, 0, 0, 0)),
                  pl.BlockSpec((1, 1, ndl, SUB, D_B), lambda i, o: (layer, lat_o(i, o), 0, 0, 0)),
                  pl.BlockSpec((1, 1, D_B), lambda i, o: (o, 0, 0))],
        out_specs=pl.BlockSpec((SEG, D_B), row),
        out_shape=jax.ShapeDtypeStruct((n, D_B), F32),
        scratch_shapes=[pltpu.VMEM((SEG, D_B), F32), pltpu.VMEM((HY_ORDER, SEG, D_B), F32),
                        pltpu.VMEM((max(pmax, 2), 2 * CH, D_B), F32), pltpu.VMEM((2, 2 * CH, D_B), BF16)],
        compiler_params=_cp(("parallel", "arbitrary")),
        name="hyena",
    )(hyb, conv_w, conv_b, fwd, inv, habc, hd0c, habl, hd0l, bias)


def _row_scan(x, op, fill, reverse):
    t, width = x.shape
    n_tiles = t // SUB
    sub = lax.broadcasted_iota(jnp.int32, (1, SUB, 1), 1)
    x3 = x.reshape(n_tiles, SUB, width)
    for s in (1, 2, 4):
        shift, keep = (SUB - s, sub < SUB - s) if reverse else (s, sub >= s)
        x3 = op(x3, jnp.where(keep, pltpu.roll(x3, shift, 1), fill))
    x = x3.reshape(t, width)
    out = [None] * n_tiles
    carry = None
    for i in (reversed(range(n_tiles)) if reverse else range(n_tiles)):
        tile = x[i * SUB:(i + 1) * SUB]
        out[i] = tile if carry is None else op(tile, carry)
        carry = out[i][0:1] if reverse else out[i][SUB - 1:SUB]
    return jnp.concatenate(out, axis=0)


_STK_ONE = 3 * SUB


def _mlstm_prep(d, g_ref, m_old):
    T = CH
    reverse = d == 1
    g = g_ref[...]
    if d == 1:
        g = pltpu.roll(g, LANE - 2 * H_C, 1)
    lane = lax.broadcasted_iota(jnp.int32, (1, LANE), 1)
    head = lane < H_C
    b = pltpu.roll(_row_scan(_log_sigmoid(g), jnp.add, 0.0, reverse), LANE - H_C, 1)
    r = jnp.where(head, g - b, 0.0)
    big_m = jnp.maximum(m_old, _row_scan(r, jnp.maximum, -jnp.inf, reverse))
    last = 0 if reverse else T - 1
    m_last = big_m[last:last + 1, :]
    low = lane < SUB
    p0, p1, p2 = (jnp.where(low, p.astype(F32), 0.0) for p in _split3(-big_m))
    cols = (p0 + pltpu.roll(p1, SUB, 1) + pltpu.roll(p2, 2 * SUB, 1)
            + jnp.where(jnp.logical_and(lane >= _STK_ONE, lane < _STK_ONE + SUB), 1.0, 0.0))
    rowid = lax.broadcasted_iota(jnp.int32, (SUB, 1), 0)
    r8 = r.T[0:SUB, :]
    m_last8 = sum(jnp.where(rowid == h, m_last[:, h:h + 1], 0.0) for h in range(H_C))
    ws8 = jnp.where(rowid < H_C, jnp.exp(r8 - m_last8), 0.0)
    return {"r3": [p.astype(F32) for p in _split3(r8)], "ws8": ws8, "wc": jnp.exp(m_old - m_last),
            "m_new": jnp.where(head, b[last:last + 1, :] + m_last, 0.0),
            "wi": jnp.exp(m_old - big_m), "e": jnp.exp(-(b + big_m)), "cols_b": cols.astype(BF16)}


def _mlstm_variant(carry, qf, vf, ktf, gf, qb_, vb_, ktb_, gb_, hf_ref, hb_ref, co_ref, no_ref, mo_ref,
                   s_cx, s_n, s_m):
    T = CH
    ii = lax.broadcasted_iota(jnp.int32, (T, T), 0)
    jj = lax.broadcasted_iota(jnp.int32, (T, T), 1)
    rowid = lax.broadcasted_iota(jnp.int32, (SUB, 1), 0)
    one_col = jnp.where(lax.broadcasted_iota(jnp.int32, (T, DK), 1) == 0, 1.0, 0.0).astype(BF16)
    prep = []
    for d, g_ref in ((0, gf), (1, gb_ if carry else gf)):
        m_old = s_m[d:d + 1, :] if carry else jnp.zeros((1, LANE), F32)
        prep.append(_mlstm_prep(d, g_ref, m_old))
    refs = ((qf, vf, ktf, hf_ref), (qb_, vb_, ktb_, hb_ref))
    for h in range(H_C):
        sl = slice(h * DK, (h + 1) * DK)
        for d in range(2):
            p = prep[d]
            q_ref, v_ref, kt_ref, h_ref = refs[d]
            idx = d * H_C + h
            if carry or d == 0:
                qb = (q_ref[:, sl] * (DK ** -0.5)).astype(BF16)
                kt = kt_ref[sl, :]
                ktb = kt.astype(BF16)
                v_ext = jnp.concatenate([v_ref[:, sl].astype(BF16), one_col], axis=1)
                s_raw = _dot(qb, ktb)
            tri = (jj >= ii) if d == 1 else (jj <= ii)
            sel = jnp.broadcast_to(jnp.where(rowid == h, 1.0, 0.0), (SUB, T))
            rr = sum(jnp.where(rowid == i, piece[h:h + 1, :], 0.0) for i, piece in enumerate(p["r3"]))
            rmat = jnp.concatenate([sel, sel, sel, rr, jnp.zeros((LANE - 4 * SUB, T), F32)], axis=0)
            expo = _dot(p["cols_b"], rmat.astype(BF16))
            s = s_raw * jnp.exp(jnp.where(tri, expo, -jnp.inf))
            intra = _dot(s.astype(BF16), v_ext)
            num, den = intra[:, :DK], intra[:, DK:DK + 1]
            if carry:
                cx = s_cx[idx]
                inter = _dot(qb, cx.astype(BF16))
                wi = p["wi"][:, h:h + 1]
                num, den = num + wi * inter[:, :DK], den + wi * inter[:, DK:DK + 1]
            h_ref[:, sl] = num / jnp.maximum(jnp.abs(den), p["e"][:, h:h + 1])
            upd = _dot((kt * p["ws8"][h:h + 1, :]).astype(BF16), v_ext)
            n_upd = lax.dot_general(p["ws8"].astype(BF16), ktb, _NT, preferred_element_type=F32)[h:h + 1, :]
            if carry:
                wc = p["wc"][:, h:h + 1]
                s_cx[idx] = wc * cx + upd
                s_n[idx:idx + 1, :] = wc * s_n[idx:idx + 1, :] + n_upd
            else:
                co_ref[0, idx] = upd[:, :DK]
                no_ref[0, idx:idx + 1, :] = n_upd
    m_rows = jnp.concatenate([prep[0]["m_new"], prep[1]["m_new"], jnp.zeros((SUB - 2, LANE), F32)], axis=0)
    if carry:
        s_m[...] = m_rows
    else:
        mo_ref[0] = m_rows


def _mlstm_kernel(*refs, n_ctx_steps, nc_lat):
    cx0_ref, n0_ref, m0_ref = refs[8:11]
    s_cx, s_n, s_m = refs[-3:]
    data = refs[:8] + refs[11:]
    t = pl.program_id(0)
    is_ctx = t < n_ctx_steps

    @pl.when(is_ctx)
    def _():
        _mlstm_variant(False, *data)

    @pl.when(jnp.logical_not(is_ctx))
    def _():
        @pl.when((t - n_ctx_steps) % nc_lat == 0)
        def _():
            s_cx[...] = cx0_ref[0]
            s_n[...] = n0_ref[0]
            s_m[...] = m0_ref[0]

        _mlstm_variant(True, *data)


def _mlstm_call(q, v, kt, gates, cx0, n0, m0, n_ctx_steps, nc_lat):
    n = q.shape[0]
    steps = n // CH
    nst = 2 * H_C

    def bwd_blk(t):
        r = jnp.maximum(t - n_ctx_steps, 0)
        return n_ctx_steps + (r // nc_lat) * nc_lat + (nc_lat - 1 - r % nc_lat)

    out_bwd = lambda t: jnp.where(t < n_ctx_steps, t, bwd_blk(t))
    lat_b = lambda t: jnp.maximum(t - n_ctx_steps, 0) // nc_lat
    ctx_b = lambda t: jnp.minimum(t, n_ctx_steps - 1)
    rows = lambda w, blk: pl.BlockSpec((CH, w), lambda t: (blk(t), 0))
    cols = lambda h, blk: pl.BlockSpec((h, CH), lambda t: (0, blk(t)))
    ident = lambda t: t
    return pl.pallas_call(
        functools.partial(_mlstm_kernel, n_ctx_steps=n_ctx_steps, nc_lat=nc_lat),
        grid=(steps,),
        in_specs=[rows(D_C, ident), rows(D_C, ident), cols(D_C, ident), rows(LANE, ident),
                  rows(D_C, bwd_blk), rows(D_C, bwd_blk), cols(D_C, bwd_blk), rows(LANE, bwd_blk),
                  pl.BlockSpec((1, nst, DK, 2 * DK), lambda t: (lat_b(t), 0, 0, 0)),
                  pl.BlockSpec((1, nst, DK), lambda t: (lat_b(t), 0, 0)),
                  pl.BlockSpec((1, SUB, LANE), lambda t: (lat_b(t), 0, 0))],
        out_specs=[rows(D_C, ident), rows(D_C, out_bwd),
                   pl.BlockSpec((1, nst, DK, DK), lambda t: (ctx_b(t), 0, 0, 0)),
                   pl.BlockSpec((1, nst, DK), lambda t: (ctx_b(t), 0, 0)),
                   pl.BlockSpec((1, SUB, LANE), lambda t: (ctx_b(t), 0, 0))],
        out_shape=[jax.ShapeDtypeStruct((n, D_C), F32), jax.ShapeDtypeStruct((n, D_C), F32),
                   jax.ShapeDtypeStruct((n_ctx_steps, nst, DK, DK), F32),
                   jax.ShapeDtypeStruct((n_ctx_steps, nst, DK), F32),
                   jax.ShapeDtypeStruct((n_ctx_steps, SUB, LANE), F32)],
        scratch_shapes=[pltpu.VMEM((nst, DK, 2 * DK), F32), pltpu.VMEM((nst, DK), F32),
                        pltpu.VMEM((SUB, LANE), F32)],
        compiler_params=_cp(("arbitrary",)),
        name="mlstm",
    )(q, v, kt, gates, q, v, kt, gates, cx0, n0, m0)


def _out_kernel(x_ref, oa_ref, ob_ref, hf_ref, hb_ref, og_ref, mod_ref, mg_ref, w_ref, g_ref, b_ref,
                rw_ref, rb_ref, x1_ref, he_ref, cnt_ref, it_ref, s_cnt):
    i = pl.program_id(0)

    @pl.when(i == 0)
    def _():
        s_cnt[...] = jnp.zeros(s_cnt.shape, F32)

    m = mod_ref[0]
    mg = mg_ref[...]
    acc = _dot((oa_ref[...] * mg[:, :D_A]).astype(BF16), w_ref[0:D_A, :])
    acc += _dot((ob_ref[...] * mg[:, D_A:D_A + D_B]).astype(BF16), w_ref[D_A:D_A + D_B, :])
    hc = hf_ref[...] + hb_ref[...]
    og = og_ref[...]
    off = D_A + D_B
    for h in range(H_C):
        sl = slice(h * DK, (h + 1) * DK)
        oc = _sigmoid(og[:, sl]) * _rms(hc[:, sl]) * mg[:, off + h * DK:off + (h + 1) * DK]
        acc += _dot(oc.astype(BF16), w_ref[off + h * DK:off + (h + 1) * DK, :])
    x1 = _ln_plain(ALPHA * x_ref[...] + m[2:3] * acc) * g_ref[...] + b_ref[...]
    x1_ref[...] = x1
    h2 = x1 * (1.0 + m[4:5]) + m[3:4]
    he_ref[:, :D_MODEL] = h2

    lg = _dot3(h2, rw_ref[...]) + rb_ref[...]
    col = lax.broadcasted_iota(jnp.int32, lg.shape, 1)
    ninf = -jnp.inf
    lgm = jnp.where(col < N_GROUPS, lg, ninf)
    mx = jnp.max(lgm, -1, keepdims=True)
    gi = jnp.min(jnp.where(lgm == mx, col, LANE), -1, keepdims=True)
    pg_top = 1.0 / jnp.sum(jnp.where(col < N_GROUPS, jnp.exp(lg - mx), 0.0), -1, keepdims=True)
    lo4 = N_GROUPS + E_PER_GROUP * gi
    lem = jnp.where(jnp.logical_and(col >= lo4, col < lo4 + E_PER_GROUP), lg, ninf)
    v1 = jnp.max(lem, -1, keepdims=True)
    i1 = jnp.min(jnp.where(lem == v1, col, LANE), -1, keepdims=True)
    lem2 = jnp.where(col == i1, ninf, lem)
    v2 = jnp.max(lem2, -1, keepdims=True)
    i2 = jnp.min(jnp.where(lem2 == v2, col, LANE), -1, keepdims=True)
    e21 = jnp.exp(v2 - v1)
    w1 = pg_top / (1.0 + e21)
    w2 = pg_top * e21 / (1.0 + e21)
    e1, e2 = i1 - N_GROUPS, i2 - N_GROUPS
    first_lo = e1 < e2
    elo, ehi = jnp.minimum(e1, e2), jnp.maximum(e1, e2)
    wlo, whi = jnp.where(first_lo, w1, w2), jnp.where(first_lo, w2, w1)
    llo, lhi = elo - E_PER_GROUP * gi, ehi - E_PER_GROUP * gi
    cls = gi * N_PAIRS + ((llo * (7 - llo)) >> 1) + lhi - llo - 1

    oh = jnp.where(col == cls, 1.0, 0.0)
    ii = lax.broadcasted_iota(jnp.int32, (TM_OUT, TM_OUT), 0)
    jj = lax.broadcasted_iota(jnp.int32, (TM_OUT, TM_OUT), 1)
    before = jnp.where(jj < ii, 1.0, 0.0).astype(BF16)
    cnt = s_cnt[0:1, :]
    rank = jnp.sum(oh * (_dot(before, oh.astype(BF16)) + cnt), -1, keepdims=True)
    cnt = cnt + jnp.sum(oh, 0, keepdims=True)
    s_cnt[...] = jnp.broadcast_to(cnt, s_cnt.shape)
    cnt_ref[...] = jnp.broadcast_to(cnt, cnt_ref.shape)

    info = jnp.zeros(lg.shape, F32)
    for c, val in ((I_CLS, cls.astype(F32)), (I_ELO, elo.astype(F32)), (I_EHI, ehi.astype(F32)),
                   (I_WLO, wlo), (I_WHI, whi), (I_RANK, rank)):
        info = jnp.where(col == c, val, info)
    he_ref[:, D_MODEL:] = info
    it_ref[...] = info.T[:SUB, :]


def _out_call(x, out_a, out_b, hcf, hcb, ogate, mod_l, mix_g, w_out, ln_g, ln_b, rt_w, rt_b, group_of_tile):
    n = x.shape[0]
    row = lambda i: (i, 0)
    fixed = lambda i: (0, 0)
    return pl.pallas_call(
        _out_kernel,
        grid=(n // TM_OUT,),
        in_specs=[pl.BlockSpec((TM_OUT, D_MODEL), row), pl.BlockSpec((TM_OUT, D_A), row),
                  pl.BlockSpec((TM_OUT, D_B), row),
                  pl.BlockSpec((TM_OUT, D_C), row), pl.BlockSpec((TM_OUT, D_C), row), pl.BlockSpec((TM_OUT, D_C), row),
                  pl.BlockSpec((1, 6, D_MODEL), lambda i: (group_of_tile(i), 0, 0)),
                  pl.BlockSpec((1, D_MODEL), fixed), pl.BlockSpec((D_MODEL, D_MODEL), fixed),
                  pl.BlockSpec((1, D_MODEL), fixed), pl.BlockSpec((1, D_MODEL), fixed),
                  pl.BlockSpec((D_MODEL, LANE), fixed), pl.BlockSpec((1, LANE), fixed)],
        out_specs=[pl.BlockSpec((TM_OUT, D_MODEL), row), pl.BlockSpec((TM_OUT, D_EXT), row),
                   pl.BlockSpec((SUB, LANE), fixed), pl.BlockSpec((SUB, TM_OUT), lambda i: (0, i))],
        out_shape=[jax.ShapeDtypeStruct((n, D_MODEL), F32), jax.ShapeDtypeStruct((n, D_EXT), F32),
                   jax.ShapeDtypeStruct((SUB, LANE), F32), jax.ShapeDtypeStruct((SUB, n), F32)],
        scratch_shapes=[pltpu.VMEM((SUB, LANE), F32)],
        compiler_params=_cp(("arbitrary",)),
        name="out_proj_router",
    )(x, out_a, out_b, hcf, hcb, ogate, mod_l, mix_g, w_out, ln_g, ln_b, rt_w, rt_b)


def _row_copy(src_ref, src_row, dst_ref, dst_row, sem):
    return pltpu.make_async_copy(src_ref.at[pl.ds(src_row, 1), :], dst_ref.at[pl.ds(dst_row, 1), :], sem)


def _sorted_row(cls_ref, rank_ref, start_ref, t):
    return start_ref[cls_ref[t]] + rank_ref[t]


def _scatter_kernel(cls_ref, rank_ref, start_ref, pad_ref, na_ref, x_ref, o_ref, ring, z_ref, sem_z, load_sems,
                    row_sems):
    n_tiles = o_ref.shape[0] // TM

    def zero_tile(row):
        return pltpu.make_async_copy(z_ref, o_ref.at[pl.ds(pl.multiple_of(row, TM), TM), :], sem_z)

    def for_zero_tiles(fn):
        def per_class(c, carry):
            row = pad_ref[c]

            @pl.when(row >= 0)
            def _():
                fn(zero_tile(row))

            return carry

        lax.fori_loop(0, N_CLASS, per_class, 0)

        def per_idle(t, carry):
            fn(zero_tile(t * TM))
            return carry

        lax.fori_loop(na_ref[0], n_tiles, per_idle, 0)

    i = pl.program_id(0)
    n_steps = x_ref.shape[0] // TM_PERM

    def load(step):
        return pltpu.make_async_copy(x_ref.at[pl.ds(pl.multiple_of(step * TM_PERM, TM_PERM), TM_PERM), :],
                                     ring.at[step % RING], load_sems.at[step % RING])

    def start_rows(step):
        slot = step % RING

        def body(r, carry):
            dst = _sorted_row(cls_ref, rank_ref, start_ref, step * TM_PERM + r)
            _row_copy(ring.at[slot], r, o_ref, dst, row_sems.at[slot]).start()
            return carry

        lax.fori_loop(0, TM_PERM, body, 0, unroll=DMA_UNROLL)

    def wait_rows(step):
        slot = step % RING
        pltpu.make_async_copy(ring.at[slot], o_ref.at[pl.ds(0, TM_PERM), :], row_sems.at[slot]).wait()

    @pl.when(i == 0)
    def _():
        z_ref[...] = jnp.zeros(z_ref.shape, F32)
        for_zero_tiles(lambda cp: cp.start())
        for_zero_tiles(lambda cp: cp.wait())
        load(0).start()

    @pl.when(i >= RING - 1)
    def _():
        wait_rows(i - (RING - 1))

    @pl.when(i + 1 < n_steps)
    def _():
        load(i + 1).start()

    load(i).wait()
    start_rows(i)

    @pl.when(i == n_steps - 1)
    def _():
        for back in range(min(RING - 1, n_steps) - 1, -1, -1):
            wait_rows(i - back)


def _scatter_call(plan, h_ext, n_tiles_max):
    any_spec = pl.BlockSpec(memory_space=pl.ANY)
    return pl.pallas_call(
        _scatter_kernel,
        grid_spec=pltpu.PrefetchScalarGridSpec(
            num_scalar_prefetch=5, grid=(h_ext.shape[0] // TM_PERM,),
            in_specs=[any_spec],
            out_specs=any_spec,
            scratch_shapes=[pltpu.VMEM((RING, TM_PERM, D_EXT), F32), pltpu.VMEM((TM, D_EXT), F32),
                            pltpu.SemaphoreType.DMA(()),
                            pltpu.SemaphoreType.DMA((RING,)), pltpu.SemaphoreType.DMA((RING,))]),
        out_shape=jax.ShapeDtypeStruct((n_tiles_max * TM, D_EXT), F32),
        compiler_params=_cp(("arbitrary",)),
        name="moe_scatter",
    )(plan["cls"], plan["rank"], plan["row_start"], plan["pad_rows"], plan["n_act"], h_ext)


def _moe_kernel(tg_ref, ng_ref, lo_ref, hi_ref, cg_ref, nv_ref, na_ref, x_ref, w1_hbm, w3_hbm, w2_hbm, o_ref,
                w1_ref, w3_ref, w2_ref, s1, s3, s2, wsem, *, layer):
    t = pl.program_id(0)
    active = t < na_ref[0]
    half = TM // 2

    def group_copies(g):
        rows = pl.ds(g * E_PER_GROUP, E_PER_GROUP)
        return [pltpu.make_async_copy(src.at[layer, rows], dst, wsem.at[k])
                for k, (src, dst) in enumerate(((w1_hbm, w1_ref), (w3_hbm, w3_ref), (w2_hbm, w2_ref)))]

    @pl.when(t == 0)
    def _():
        for cp in group_copies(tg_ref[0]):
            cp.start()

    @pl.when(jnp.logical_not(active))
    def _():
        o_ref[...] = jnp.zeros(o_ref.shape, F32)

    @pl.when(jnp.logical_and(active, cg_ref[t] == 1))
    def _():
        for cp in group_copies(tg_ref[t]):
            cp.wait()
        for e in range(E_PER_GROUP):
            s1[e] = w1_ref[e].astype(BF16)
            s3[e] = w3_ref[e].astype(BF16)
            s2[e] = w2_ref[e].astype(BF16)

        @pl.when(ng_ref[t] >= 0)
        def _():
            for cp in group_copies(ng_ref[t]):
                cp.start()

    def run(rows):
        xe = x_ref[0:rows, :]
        x = xe[:, :D_MODEL].astype(BF16)

        def expert(e, gate):
            a = _dot(x, s1[e])
            hm = a * _sigmoid(a) * _dot(x, s3[e]) * gate
            return _dot(hm.astype(BF16), s2[e])

        o_ref[0:rows, :] = (expert(lo_ref[t], xe[:, D_MODEL + I_WLO:D_MODEL + I_WLO + 1])
                            + expert(hi_ref[t], xe[:, D_MODEL + I_WHI:D_MODEL + I_WHI + 1]))

    @pl.when(jnp.logical_and(active, nv_ref[t] > half))
    def _():
        run(TM)

    @pl.when(jnp.logical_and(active, nv_ref[t] <= half))
    def _():
        run(half)
        o_ref[half:, :] = jnp.zeros((TM - half, D_MODEL), F32)


def _moe_call(layer, plan, x_sorted, w1, w3, w2):
    r = x_sorted.shape[0]
    act = lambda t, *scalars: (jnp.minimum(t, scalars[-1][0] - 1), 0)
    any_spec = pl.BlockSpec(memory_space=pl.ANY)
    up, down = (E_PER_GROUP, D_MODEL, D_E), (E_PER_GROUP, D_E, D_MODEL)
    return pl.pallas_call(
        functools.partial(_moe_kernel, layer=layer),
        grid_spec=pltpu.PrefetchScalarGridSpec(
            num_scalar_prefetch=7, grid=(r // TM,),
            in_specs=[pl.BlockSpec((TM, D_EXT), act), any_spec, any_spec, any_spec],
            out_specs=pl.BlockSpec((TM, D_MODEL), lambda t, *_: (t, 0)),
            scratch_shapes=[pltpu.VMEM(up, F32), pltpu.VMEM(up, F32), pltpu.VMEM(down, F32),
                            pltpu.VMEM(up, BF16), pltpu.VMEM(up, BF16), pltpu.VMEM(down, BF16),
                            pltpu.SemaphoreType.DMA((3,))]),
        out_shape=jax.ShapeDtypeStruct((r, D_MODEL), F32),
        compiler_params=_cp(("arbitrary",)),
        name="moe_experts",
    )(plan["tile_grp"], plan["next_grp"], plan["tile_lo"], plan["tile_hi"], plan["chg_grp"], plan["valid"],
      plan["n_act"], x_sorted, w1, w3, w2)


def _ln2_kernel(cls_ref, rank_ref, start_ref, x1_ref, mod_ref, g_ref, b_ref, y_ref, *rest, n_ctx_tiles):
    o_refs, (buf, sems) = rest[:-2], rest[-2:]
    i = pl.program_id(0)
    n_steps = pl.num_programs(0)

    def start_rows(step):
        slot = step % 2

        def body(r, carry):
            src = _sorted_row(cls_ref, rank_ref, start_ref, step * TM_PERM + r)
            _row_copy(y_ref, src, buf.at[slot], r, sems.at[slot]).start()
            return carry

        lax.fori_loop(0, TM_PERM, body, 0, unroll=DMA_UNROLL)

    @pl.when(i == 0)
    def _():
        start_rows(0)

    @pl.when(i + 1 < n_steps)
    def _():
        start_rows(i + 1)

    slot = i % 2
    pltpu.make_async_copy(y_ref.at[pl.ds(0, TM_PERM), :], buf.at[slot], sems.at[slot]).wait()
    m = mod_ref[0]
    y = _ln_plain(ALPHA * x1_ref[...] + m[5:6] * buf[slot]) * g_ref[...] + b_ref[...]
    if len(o_refs) == 1:
        o_refs[0][...] = y
    else:
        @pl.when(i < n_ctx_tiles)
        def _():
            o_refs[0][...] = y

        @pl.when(i >= n_ctx_tiles)
        def _():
            o_refs[1][...] = y


def _ln2_call(plan, x1, y_sorted, mod_l, ln_g, ln_b, group_of_tile, n_ctx_tiles, split):
    n = x1.shape[0]
    row = lambda i, *_: (i, 0)
    fixed = lambda i, *_: (0, 0)
    if split:
        n_ctx = n_ctx_tiles * TM_PERM
        out_specs = [pl.BlockSpec((TM_PERM, D_MODEL), lambda i, *_: (jnp.minimum(i, n_ctx_tiles - 1), 0)),
                     pl.BlockSpec((TM_PERM, D_MODEL), lambda i, *_: (jnp.maximum(i - n_ctx_tiles, 0), 0))]
        out_shape = [jax.ShapeDtypeStruct((n_ctx, D_MODEL), F32), jax.ShapeDtypeStruct((n - n_ctx, D_MODEL), F32)]
    else:
        out_specs = [pl.BlockSpec((TM_PERM, D_MODEL), row)]
        out_shape = [jax.ShapeDtypeStruct((n, D_MODEL), F32)]
    return pl.pallas_call(
        functools.partial(_ln2_kernel, n_ctx_tiles=n_ctx_tiles),
        grid_spec=pltpu.PrefetchScalarGridSpec(
            num_scalar_prefetch=3, grid=(n // TM_PERM,),
            in_specs=[pl.BlockSpec((TM_PERM, D_MODEL), row),
                      pl.BlockSpec((1, 6, D_MODEL), lambda i, *_: (group_of_tile(i), 0, 0)),
                      pl.BlockSpec((1, D_MODEL), fixed), pl.BlockSpec((1, D_MODEL), fixed),
                      pl.BlockSpec(memory_space=pl.ANY)],
            out_specs=out_specs,
            scratch_shapes=[pltpu.VMEM((2, TM_PERM, D_MODEL), F32), pltpu.SemaphoreType.DMA((2,))]),
        out_shape=out_shape,
        compiler_params=_cp(("arbitrary",)),
        name="moe_gather_ln2",
    )(plan["cls"], plan["rank"], plan["row_start"], x1, mod_l, ln_g, ln_b, y_sorted)


def _dft_matrices():
    n2 = 2 * CH
    f = np.arange(CH, dtype=np.float64)[:, None]
    t = np.arange(CH, dtype=np.float64)[None, :]
    ang = 2.0 * np.pi * f * t / n2
    re, im = np.cos(ang), -np.sin(ang)
    im[0, :] = np.cos(np.pi * t[0])
    fwd = np.concatenate([re, im], axis=0)
    scale = np.full((CH, 1), 2.0 / n2)
    scale[0, 0] = 1.0 / n2
    inv = np.concatenate([(re * scale).T, (im * scale).T], axis=1)
    return fwd.astype(np.float32), inv.astype(np.float32)


def _filter_features(L):
    lag = np.arange(-L, L)
    m = np.minimum(np.abs(lag), L - 1)
    t = (np.arange(L, dtype=np.float32) / np.float32(max(L - 1, 1)))[m]
    w = (np.float32(2.0 * math.pi) * np.arange(L, dtype=np.float32) / np.float32(L))[m]
    bands = np.linspace(1e-4, HY_BANDS - 1, HY_BANDS, dtype=np.float32)
    z = np.zeros((2 * L, LANE), np.float32)
    z[:, 0] = t
    z[:, 1:1 + HY_BANDS] = np.cos(w[:, None] * bands)
    z[:, 1 + HY_BANDS:HY_EMB] = -np.sin(w[:, None] * bands)
    lo, hi = math.log(HY_DECAY_TARGET) / 1.5, math.log(HY_DECAY_TARGET) / 0.3
    deltas = np.abs(np.linspace(lo, hi, D_B, dtype=np.float32))
    dec = np.exp(-t[:, None] * deltas)
    return z, dec.astype(np.float32)


def _sincos_2d(rows, cols):
    quarter = D_MODEL // 4
    omega = 1.0 / (10000.0 ** (jnp.arange(quarter, dtype=F32) / quarter))

    def emb(n):
        ang = jnp.arange(n, dtype=F32)[:, None] * omega[None]
        return jnp.concatenate([jnp.sin(ang), jnp.cos(ang)], -1)

    er, ec = emb(rows), emb(cols)
    half = D_MODEL // 2
    pos = jnp.concatenate([jnp.broadcast_to(er[:, None], (rows, cols, half)),
                           jnp.broadcast_to(ec[None], (rows, cols, half))], -1)
    return pos.reshape(rows * cols, D_MODEL)


def _pad_to(x, shape):
    return jnp.pad(x, [(0, s - d) for d, s in zip(x.shape, shape)])


def _block_diag(w):
    eye = jnp.eye(H_A, dtype=w.dtype)
    return jnp.einsum("hij,hg->higj", w, eye).reshape(D_A, D_A)


_PAIR_LO = np.array([0, 0, 0, 1, 1, 2], np.int32)
_PAIR_HI = np.array([1, 2, 3, 2, 3, 3], np.int32)


def _routing_plan(info_t, counts, n_tiles_max):
    cnt = counts[0, :N_CLASS].astype(jnp.int32)
    tiles = (cnt + TM - 1) // TM
    tile_end = jnp.cumsum(tiles)
    n_act = tile_end[-1]
    t = jnp.minimum(jnp.arange(n_tiles_max, dtype=jnp.int32), n_act - 1)
    tcls = jnp.minimum(jnp.sum((tile_end[None, :] <= t[:, None]).astype(jnp.int32), 1), N_CLASS - 1)
    grp, pair = (tcls // N_PAIRS).astype(jnp.int32), tcls % N_PAIRS
    valid = jnp.clip(cnt[tcls] - (t - (tile_end - tiles)[tcls]) * TM, 0, TM).astype(jnp.int32)
    first = jnp.ones((1,), jnp.int32)
    changed = lambda e: jnp.concatenate([first, (e[1:] != e[:-1]).astype(jnp.int32)])
    chg_grp = changed(grp)
    idx = jnp.arange(n_tiles_max, dtype=jnp.int32)
    later_change = jnp.logical_and(idx[None, :] > idx[:, None], chg_grp[None, :] == 1)
    next_pos = jnp.min(jnp.where(later_change, idx[None, :], n_tiles_max), axis=1)
    next_grp = jnp.where(next_pos < n_tiles_max, grp[jnp.minimum(next_pos, n_tiles_max - 1)], -1).astype(jnp.int32)
    return {"cls": info_t[I_CLS].astype(jnp.int32), "rank": info_t[I_RANK].astype(jnp.int32),
            "row_start": ((tile_end - tiles) * TM).astype(jnp.int32), "tile_grp": grp, "chg_grp": chg_grp, "next_grp": next_grp,
            "tile_lo": jnp.asarray(_PAIR_LO)[pair], "tile_hi": jnp.asarray(_PAIR_HI)[pair], "valid": valid,
            "n_act": n_act.reshape(1).astype(jnp.int32),
            "pad_rows": jnp.where(tiles > 0, (tile_end - 1) * TM, -1).astype(jnp.int32)}


def kernel(x_prompt, x_sample, c, state_lru, state_mlstm_C, state_mlstm_n, state_mlstm_m, c_ctx, w_ada, b_ada, w_in, b_in, conv_a_w, conv_a_b, lru_wa, lru_ba, lru_wx, lru_bx, lru_lam, conv_b_w, conv_b_b, hy_w1, hy_b1, hy_w2, hy_b2, hy_freq, hy_w3, hy_bias, mix_g, w_out, ln1_g, ln1_b, rt_wg, rt_bg, rt_we, rt_be, moe_w1, moe_w3, moe_w2, ln2_g, ln2_b):
    B, l_ctx, D = x_prompt.shape
    b_lat, l_lat, _ = x_sample.shape
    n_ctx, n_lat = B * l_ctx, b_lat * l_lat
    n = n_ctx + n_lat
    assert D == D_MODEL and w_in.shape[-1] == D_MAIN + N_GATE
    assert SEG % l_ctx == 0 and l_lat == SEG and l_ctx % CH == 0 and n_ctx % SEG == 0
    assert l_ctx == CH, "the mLSTM step schedule assumes one chunk per context sequence"
    assert 1 + b_lat <= SUB
    n_ctx_blk = n_ctx // SEG
    nc_lat = l_lat // CH

    def group_of(tile_rows):
        first_lat, per_seq = n_ctx // tile_rows, l_lat // tile_rows
        return lambda i: jnp.where(i < first_lat, 0, 1 + (i - first_lat) // per_seq)

    cond = jnp.concatenate([c_ctx[None], c, jnp.zeros((SUB - 1 - b_lat, D), F32)], 0)
    mod = _mod_call(cond, w_ada, b_ada).reshape(DEPTH, SUB, 6, D)
    pos = _sincos_2d(l_lat // GRID_W, GRID_W)
    x = _entry_call(x_prompt.reshape(n_ctx, D), x_sample.reshape(n_lat, D), pos)

    fwd_np, inv_np = _dft_matrices()
    fwd32 = jnp.asarray(fwd_np)
    fwd16, inv16 = fwd32.astype(BF16), jnp.asarray(inv_np).astype(BF16)
    fw1 = _pad_to(hy_w1, (DEPTH, LANE, LANE))
    fb1 = _pad_to(hy_b1[:, None, :], (DEPTH, 1, LANE))
    fw2 = _pad_to(hy_w2, (DEPTH, LANE, LANE))
    fb2 = _pad_to(hy_b2[:, None, :], (DEPTH, 1, LANE))
    ffr = _pad_to(hy_freq[:, None, :], (DEPTH, 1, LANE))
    fw3 = _pad_to(hy_w3, (DEPTH, LANE, HY_ORDER * 2 * D_B))
    spectra = {}
    for L in (l_ctx, l_lat):
        z_np, dec_np = _filter_features(L)
        spectra[L] = _filt_call(L, jnp.asarray(z_np), jnp.asarray(dec_np), fw1, fb1, fw2, fb2, ffr, fw3, fwd32)

    w_in_t = jnp.swapaxes(w_in, 1, 2)
    lat_slots = SEG // l_ctx
    st_lru, st_c, st_n, st_m = [], [], [], []
    for l in range(DEPTH):
        b_main = b_in[l, None, :D_MAIN]
        k_lo, k_hi = _Z_CUTS[_K_CUT], _Z_CUTS[_K_CUT + 1]
        b_kt = jnp.broadcast_to(b_in[l, k_lo:k_hi, None], (D_C, TM_IN))
        w_gate = _pad_to(w_in[l, :, D_MAIN:], (D, LANE))
        b_gate = _pad_to(b_in[l, None, D_MAIN:], (1, LANE))
        xa, ya, hyb, q, v, og, gates, kt = _in_call(l, x, mod[l], w_in_t, b_main, b_kt, w_gate, b_gate,
                                                    group_of(TM_IN))

        lru_w = jnp.concatenate([_block_diag(lru_wa[l, 0]), _block_diag(lru_wx[l, 0]),
                                 _block_diag(lru_wa[l, 1]), _block_diag(lru_wx[l, 1])], 1).astype(BF16)
        lru_b = jnp.concatenate([lru_ba[l, 0], lru_bx[l, 0], lru_ba[l, 1], lru_bx[l, 1]])[None]
        h0_lat = _pad_to(state_lru[:, l][:, None], (b_lat, lat_slots, 2, D_A))
        h0_all = jnp.concatenate([jnp.zeros((n_ctx_blk, lat_slots, 2, D_A), F32), h0_lat], 0)
        out_a, lru_last = _lru_call(xa, ya, conv_a_w[l], conv_a_b[l, None], lru_w, lru_b, lru_lam[l], h0_all,
                                    l_ctx, l_lat, n_ctx_blk)

        habc, hd0c = spectra[l_ctx]
        habl, hd0l = spectra[l_lat]
        out_b = _hy_call(l, hyb, conv_b_w[l], conv_b_b[l, None], fwd16, inv16, habc, hd0c, habl, hd0l,
                         hy_bias[l][:, None, :], l_ctx, l_lat, n_ctx_blk)

        n0 = state_mlstm_n[:, l].reshape(b_lat, 2 * H_C, DK)
        cx0 = jnp.concatenate([state_mlstm_C[:, l].reshape(b_lat, 2 * H_C, DK, DK), n0[..., None],
                               jnp.zeros((b_lat, 2 * H_C, DK, DK - 1), F32)], -1)
        m0 = _pad_to(state_mlstm_m[:, l], (b_lat, SUB, LANE))
        hcf, hcb, c_fin, n_fin, m_fin = _mlstm_call(q, v, kt, gates, cx0, n0, m0, n_ctx // CH, nc_lat)

        rt_w = _pad_to(jnp.concatenate([rt_wg[l], rt_we[l]], 1), (D, LANE))
        rt_b = _pad_to(jnp.concatenate([rt_bg[l], rt_be[l]])[None], (1, LANE))
        x1, h_ext, counts, info_t = _out_call(x, out_a, out_b, hcf, hcb, og, mod[l], mix_g[l, None],
                                              w_out[l].astype(BF16), ln1_g[l, None], ln1_b[l, None], rt_w, rt_b,
                                              group_of(TM_OUT))

        n_tiles_max = n // TM + N_CLASS
        plan = _routing_plan(info_t, counts, n_tiles_max)
        x_sorted = _scatter_call(plan, h_ext, n_tiles_max)
        y_sorted = _moe_call(l, plan, x_sorted, moe_w1, moe_w3, moe_w2)
        outs = _ln2_call(plan, x1, y_sorted, mod[l], ln2_g[l, None], ln2_b[l, None], group_of(TM_PERM),
                         n_ctx // TM_PERM, split=(l == DEPTH - 1))
        x = outs[0]

        st_lru.append(lru_last[:n_ctx_blk].reshape(B, 2, D_A))
        st_c.append(c_fin.reshape(B, 2, H_C, DK, DK))
        st_n.append(n_fin.reshape(B, 2, H_C, DK))
        st_m.append(m_fin[:, :2, :H_C])

    return (outs[0].reshape(B, l_ctx, D), outs[1].reshape(b_lat, l_lat, D),
            jnp.stack(st_lru, 1), jnp.stack(st_c, 1), jnp.stack(st_n, 1), jnp.stack(st_m, 1))
```

```python
import functools
import math

import numpy as np
import jax
import jax.numpy as jnp
from jax import lax
from jax.experimental import pallas as pl
from jax.experimental.pallas import tpu as pltpu

F32 = jnp.float32
BF16 = jnp.bfloat16

D_MODEL = 1024
DEPTH = 2
GRID_W = 64
D_A = 256
H_A = 4
BA = D_A // H_A
LRU_C = 8.0
D_B = 256
HY_ORDER = 2
HY_BANDS = 16
HY_EMB = 1 + 2 * HY_BANDS
HY_FH = 64
HY_DECAY_TARGET = 1e-2
D_C = 512
H_C = 4
DK = D_C // H_C
N_GROUPS = 4
E_PER_GROUP = 4
N_EXP = N_GROUPS * E_PER_GROUP
N_PAIRS = 6
N_CLASS = N_GROUPS * N_PAIRS
D_E = 512
ALPHA = (2 * DEPTH) ** 0.25
EPS = 1e-6
D_MAIN = 2 * D_A + 3 * D_B + 4 * D_C
N_GATE = 4 * H_C

LANE = 128
SUB = 8
VMEM_LIMIT = 56 * 1024 * 1024

CH = 256
SEG = 2048
TM = 256
TM_IN = 512
TM_OUT = 512
TM_PERM = 512
D_EXT = D_MODEL + LANE
FS = 64
DMA_UNROLL = 8
RING = 3

I_CLS, I_ELO, I_EHI, I_WLO, I_WHI, I_RANK = range(6)


def _cp(sem, vmem=VMEM_LIMIT):
    return pltpu.CompilerParams(dimension_semantics=sem, vmem_limit_bytes=vmem)


def _dot(a, b):
    return jnp.dot(a, b, preferred_element_type=F32)


def _split2(x):
    hi = x.astype(BF16)
    lo = (x - hi.astype(F32)).astype(BF16)
    return hi, lo


def _dot3(a, b):
    ah, al = _split2(a)
    bh, bl = _split2(b)
    return _dot(ah, bh) + (_dot(ah, bl) + _dot(al, bh))


def _split3(x):
    hi = x.astype(BF16)
    r1 = x - hi.astype(F32)
    mid = r1.astype(BF16)
    lo = (r1 - mid.astype(F32)).astype(BF16)
    return hi, mid, lo


def _sigmoid(x):
    return 1.0 / (1.0 + jnp.exp(-x))


def _log_sigmoid(x):
    return jnp.minimum(x, 0.0) - jnp.log1p(jnp.exp(-jnp.abs(x)))


def _gelu_tanh(x):
    return 0.5 * x * (1.0 + jnp.tanh(math.sqrt(2.0 / math.pi) * (x + 0.044715 * (x * x * x))))


def _ln_plain(x):
    mu = jnp.mean(x, -1, keepdims=True)
    xc = x - mu
    var = jnp.mean(xc * xc, -1, keepdims=True)
    return xc * lax.rsqrt(var + EPS)


def _rms(x):
    return x * lax.rsqrt(jnp.mean(x * x, -1, keepdims=True) + EPS)


def _halo_rows(ref, start, rows):
    total = ref.shape[0]
    prev = ref[pl.ds(pl.multiple_of(jnp.maximum(start - SUB, 0), SUB), SUB), :]
    main = ref[pl.ds(start, rows), :]
    nxt = ref[pl.ds(pl.multiple_of(jnp.minimum(start + rows, total - SUB), SUB), SUB), :]
    return jnp.concatenate([prev, main, nxt], axis=0), main


def _mod_kernel(c_ref, w_ref, b_ref, o_ref):
    c = c_ref[...]
    o_ref[0] = _dot3(c * _sigmoid(c), w_ref[0]) + b_ref[0]


def _mod_call(cond, w_ada, b_ada):
    tn = 1536
    n6 = w_ada.shape[-1]
    return pl.pallas_call(
        _mod_kernel,
        grid=(DEPTH, n6 // tn),
        in_specs=[pl.BlockSpec((SUB, D_MODEL), lambda l, j: (0, 0)),
                  pl.BlockSpec((1, D_MODEL, tn), lambda l, j: (l, 0, j)),
                  pl.BlockSpec((1, 1, tn), lambda l, j: (l, 0, j))],
        out_specs=pl.BlockSpec((1, SUB, tn), lambda l, j: (l, 0, j)),
        out_shape=jax.ShapeDtypeStruct((DEPTH, SUB, n6), F32),
        compiler_params=_cp(("parallel", "parallel")),
        name="adaln_mod",
    )(cond, w_ada, b_ada.reshape(DEPTH, 1, n6))


def _entry_kernel(xc_ref, xl_ref, pos_ref, o_ref, *, n_ctx_tiles):
    i = pl.program_id(0)

    @pl.when(i < n_ctx_tiles)
    def _():
        o_ref[...] = _ln_plain(xc_ref[...])

    @pl.when(i >= n_ctx_tiles)
    def _():
        o_ref[...] = _ln_plain(xl_ref[...] + pos_ref[...])


def _entry_call(xc, xl, pos):
    tm = 512
    n_ctx, n_lat, l_lat = xc.shape[0], xl.shape[0], pos.shape[0]
    nct = n_ctx // tm
    per_seq = l_lat // tm
    return pl.pallas_call(
        functools.partial(_entry_kernel, n_ctx_tiles=nct),
        grid=((n_ctx + n_lat) // tm,),
        in_specs=[pl.BlockSpec((tm, D_MODEL), lambda i: (jnp.minimum(i, nct - 1), 0)),
                  pl.BlockSpec((tm, D_MODEL), lambda i: (jnp.maximum(i - nct, 0), 0)),
                  pl.BlockSpec((tm, D_MODEL), lambda i: (jnp.maximum(i - nct, 0) % per_seq, 0))],
        out_specs=pl.BlockSpec((tm, D_MODEL), lambda i: (i, 0)),
        out_shape=jax.ShapeDtypeStruct((n_ctx + n_lat, D_MODEL), F32),
        compiler_params=_cp(("parallel",)),
        name="entry_ln",
    )(xc, xl, pos)


_Z_CUTS = (0, D_A, 2 * D_A, 2 * D_A + 3 * D_B, 2 * D_A + 3 * D_B + D_C, 2 * D_A + 3 * D_B + 2 * D_C,
           2 * D_A + 3 * D_B + 3 * D_C, D_MAIN)


_K_CUT = 4
_ROW_CUTS = tuple(c for i, c in enumerate(zip(_Z_CUTS[:-1], _Z_CUTS[1:])) if i != _K_CUT)
_NT = (((1,), (1,)), ((), ()))


def _in_kernel(x_ref, mod_ref, wt_ref, b_ref, bkt_ref, wg_ref, bg_ref, *refs):
    out_refs, wt16 = refs[:-1], refs[-1]

    @pl.when(pl.program_id(0) == 0)
    def _():
        wt16[...] = wt_ref[0, :D_MAIN, :].astype(BF16)

    m = mod_ref[0]
    h = x_ref[...] * (1.0 + m[1:2]) + m[0:1]
    hb = h.astype(BF16)
    dg = lambda a, b: lax.dot_general(a, b, _NT, preferred_element_type=F32)
    for ref, (a, b) in zip(out_refs[:-2], _ROW_CUTS):
        ref[...] = dg(hb, wt16[a:b, :]) + b_ref[:, a:b]
    g_ref, kt_ref = out_refs[-2:]
    g_ref[...] = _dot3(h, wg_ref[...]) + bg_ref[...]
    kt_ref[...] = dg(wt16[_Z_CUTS[_K_CUT]:_Z_CUTS[_K_CUT + 1], :], hb) + bkt_ref[...]


def _in_call(layer, x, mod_l, w_in_t, b_main, b_kt, w_gate, b_gate, group_of_tile):
    n = x.shape[0]
    widths = [b - a for a, b in _ROW_CUTS] + [LANE]
    row = lambda i: (i, 0)
    fixed = lambda i: (0, 0)
    return pl.pallas_call(
        _in_kernel,
        grid=(n // TM_IN,),
        in_specs=[pl.BlockSpec((TM_IN, D_MODEL), row),
                  pl.BlockSpec((1, 6, D_MODEL), lambda i: (group_of_tile(i), 0, 0)),
                  pl.BlockSpec((1, w_in_t.shape[1], D_MODEL), lambda i: (layer, 0, 0), pipeline_mode=pl.Buffered(1)),
                  pl.BlockSpec((1, D_MAIN), fixed),
                  pl.BlockSpec((D_C, TM_IN), fixed),
                  pl.BlockSpec((D_MODEL, LANE), fixed), pl.BlockSpec((1, LANE), fixed)],
        out_specs=[pl.BlockSpec((TM_IN, w), row) for w in widths] + [pl.BlockSpec((D_C, TM_IN), lambda i: (0, i))],
        out_shape=[jax.ShapeDtypeStruct((n, w), F32) for w in widths] + [jax.ShapeDtypeStruct((D_C, n), F32)],
        scratch_shapes=[pltpu.VMEM((D_MAIN, D_MODEL), BF16)],
        compiler_params=_cp(("arbitrary",)),
        name="in_proj",
    )(x, mod_l, w_in_t, b_main, b_kt, w_gate, b_gate)


def _lru_variant(L, xa_ref, ya_ref, cw_ref, cb_ref, wg_ref, bg_ref, lam_ref, h0_ref, o_ref, st_ref,
                 s_af, s_bf, s_ab, s_bb):
    nch, nseq, ntile = SEG // CH, SEG // L, L // SUB
    lam = lam_ref[...]
    sp = jnp.maximum(-lam, 0.0) + jnp.log1p(jnp.exp(-jnp.abs(lam)))
    cw = cw_ref[...]
    cb = cb_ref[...]
    row = lax.broadcasted_iota(jnp.int32, (CH, 1), 0)
    sub3 = lax.broadcasted_iota(jnp.int32, (1, SUB, 1), 1)

    def gates_and_tile_scan(c, carry):
        start = pl.multiple_of(c * CH, CH)
        xcat, main = _halo_rows(xa_ref, start, CH)
        tpos = (start + row) & (L - 1)
        xm2 = jnp.where(tpos >= 2, xcat[SUB - 2:SUB - 2 + CH], 0.0)
        xm1 = jnp.where(tpos >= 1, xcat[SUB - 1:SUB - 1 + CH], 0.0)
        xp1 = jnp.where(tpos <= L - 2, xcat[SUB + 1:SUB + 1 + CH], 0.0)
        xc = cw[0:1] * xm2 + cw[1:2] * xm1 + cw[2:3] * main + cw[3:4] * xp1 + cb
        g = _dot(xc.astype(BF16), wg_ref[...]) + bg_ref[...]
        for d, (sa, sb) in enumerate(((s_af, s_bf), (s_ab, s_bb))):
            r = _sigmoid(g[:, 2 * d * D_A:(2 * d + 1) * D_A])
            ig = _sigmoid(g[:, (2 * d + 1) * D_A:(2 * d + 2) * D_A])
            a = jnp.exp(-LRU_C * r * sp[d:d + 1])
            y = 1.0 - a * a
            b = jnp.where(y > 0.0, y * lax.rsqrt(y), 0.0) * (ig * xc)
            a3, b3 = a.reshape(CH // SUB, SUB, D_A), b.reshape(CH // SUB, SUB, D_A)
            for s in (1, 2, 4):
                shift, keep = (s, sub3 >= s) if d == 0 else (SUB - s, sub3 < SUB - s)
                b3 = a3 * jnp.where(keep, pltpu.roll(b3, shift, 1), 0.0) + b3
                a3 = a3 * jnp.where(keep, pltpu.roll(a3, shift, 1), 1.0)
            sa[pl.ds(start, CH), :] = a3.reshape(CH, D_A)
            sb[pl.ds(start, CH), :] = b3.reshape(CH, D_A)
        return carry

    lax.fori_loop(0, nch, gates_and_tile_scan, 0)

    def carry_tiles(k, carry):
        cf, cbk = carry
        nf, nb = [], []
        for s in range(nseq):
            rf = pl.multiple_of(s * L + k * SUB, SUB)
            hf = s_af[pl.ds(rf, SUB), :] * cf[s] + s_bf[pl.ds(rf, SUB), :]
            s_bf[pl.ds(rf, SUB), :] = hf
            nf.append(hf[SUB - 1:SUB, :])
            rb = pl.multiple_of(s * L + (ntile - 1 - k) * SUB, SUB)
            hb = s_ab[pl.ds(rb, SUB), :] * cbk[s] + s_bb[pl.ds(rb, SUB), :]
            s_bb[pl.ds(rb, SUB), :] = hb
            nb.append(hb[0:1, :])
        return tuple(nf), tuple(nb)

    cf0 = tuple(h0_ref[0, s, 0:1, :] for s in range(nseq))
    cb0 = tuple(h0_ref[0, s, 1:2, :] for s in range(nseq))
    cf, cbk = lax.fori_loop(0, ntile, carry_tiles, (cf0, cb0))

    st_ref[...] = jnp.zeros(st_ref.shape, F32)
    for s in range(nseq):
        st_ref[0, s] = jnp.concatenate([cf[s], cbk[s]], axis=0)

    def finish(c, carry):
        start = pl.multiple_of(c * CH, CH)
        h = s_bf[pl.ds(start, CH), :] + s_bb[pl.ds(start, CH), :]
        o_ref[pl.ds(start, CH), :] = _rms(_gelu_tanh(ya_ref[pl.ds(start, CH), :]) * h)
        return carry

    lax.fori_loop(0, nch, finish, 0)


def _lru_kernel(*refs, l_ctx, l_lat, n_ctx_blk):
    i = pl.program_id(0)

    @pl.when(i < n_ctx_blk)
    def _():
        _lru_variant(l_ctx, *refs)

    @pl.when(i >= n_ctx_blk)
    def _():
        _lru_variant(l_lat, *refs)


def _lru_call(xa, ya, conv_w, conv_b, w_gate, b_gate, lam, h0_all, l_ctx, l_lat, n_ctx_blk):
    n = xa.shape[0]
    nblk = n // SEG
    row = lambda i: (i, 0)
    fixed = lambda i: (0, 0)
    slots = SEG // l_ctx
    return pl.pallas_call(
        functools.partial(_lru_kernel, l_ctx=l_ctx, l_lat=l_lat, n_ctx_blk=n_ctx_blk),
        grid=(nblk,),
        in_specs=[pl.BlockSpec((SEG, D_A), row), pl.BlockSpec((SEG, D_A), row),
                  pl.BlockSpec((4, D_A), fixed), pl.BlockSpec((1, D_A), fixed),
                  pl.BlockSpec((D_A, 4 * D_A), fixed), pl.BlockSpec((1, 4 * D_A), fixed),
                  pl.BlockSpec((2, D_A), fixed),
                  pl.BlockSpec((1, slots, 2, D_A), lambda i: (i, 0, 0, 0))],
        out_specs=[pl.BlockSpec((SEG, D_A), row),
                   pl.BlockSpec((1, slots, 2, D_A), lambda i: (i, 0, 0, 0))],
        out_shape=[jax.ShapeDtypeStruct((n, D_A), F32),
                   jax.ShapeDtypeStruct((nblk, slots, 2, D_A), F32)],
        scratch_shapes=[pltpu.VMEM((SEG, D_A), F32) for _ in range(4)],
        compiler_params=_cp(("parallel",)),
        name="rglru",
    )(xa, ya, conv_w, conv_b, w_gate, b_gate, lam, h0_all)


def _filt_kernel(z_ref, dec_ref, w1_ref, b1_ref, w2_ref, b2_ref, fr_ref, w3_ref, fwd_ref,
                 oab_ref, od0_ref, s_k, s_kf, *, L):
    nblk = 2 * L // CH
    d_idx = pl.program_id(1)
    row = lax.broadcasted_iota(jnp.int32, (CH, 1), 0)

    @pl.when(d_idx == 0)
    def _():
        fr = fr_ref[0]

        def taps(c, carry):
            start = pl.multiple_of(c * CH, CH)
            h1 = jnp.sin(fr * (_dot3(z_ref[pl.ds(start, CH), :], w1_ref[0]) + b1_ref[0]))
            h2 = jnp.sin(fr * (_dot3(h1, w2_ref[0]) + b2_ref[0]))
            t = _dot3(h2, w3_ref[0])
            dec = dec_ref[pl.ds(start, CH), :]
            rg = start + row
            for o in range(HY_ORDER):
                fwd_t = t[:, (2 * o) * D_B:(2 * o + 1) * D_B]
                bwd_t = t[:, (2 * o + 1) * D_B:(2 * o + 2) * D_B]
                ko = jnp.where(rg < L, bwd_t, fwd_t) * dec
                s_k[pl.ds(start, CH), o * D_B:(o + 1) * D_B] = jnp.where(rg == 0, 0.0, ko)
            return carry

        lax.fori_loop(0, nblk, taps, 0)
        fwd = fwd_ref[...]

        def spectra(e, carry):
            start = pl.multiple_of(e * CH, CH)
            s_kf[e] = _dot3(fwd, s_k[pl.ds(start, CH), :])
            return carry

        lax.fori_loop(0, nblk, spectra, 0)

    kd = s_kf[d_idx + 1]
    km = s_kf[d_idx]
    k0 = s_k[pl.ds(pl.multiple_of(d_idx * CH, CH), 1), :]
    sgn = jnp.where((row & 1) == 0, 1.0, -1.0)
    a = kd[:CH] + sgn * (km[:CH] - k0)
    b = jnp.where(row == 0, 0.0, kd[CH:] + sgn * km[CH:])
    hn = kd[CH:CH + 1] + km[CH:CH + 1] - k0
    for o in range(HY_ORDER):
        oab_ref[0, o, 0, 0] = a[:, o * D_B:(o + 1) * D_B]
        oab_ref[0, o, 0, 1] = b[:, o * D_B:(o + 1) * D_B]
        od0_ref[0, o, 0] = jnp.broadcast_to(hn[:, o * D_B:(o + 1) * D_B], (SUB, D_B))


def _filt_call(L, z, dec, w1, b1, w2, b2, fr, w3, fwd32):
    nd = 2 * (L // CH) - 1
    fixed = lambda l, d: (0, 0)
    lay3 = lambda l, d: (l, 0, 0)
    return pl.pallas_call(
        functools.partial(_filt_kernel, L=L),
        grid=(DEPTH, nd),
        in_specs=[pl.BlockSpec((2 * L, LANE), fixed), pl.BlockSpec((2 * L, D_B), fixed),
                  pl.BlockSpec((1, LANE, LANE), lay3), pl.BlockSpec((1, 1, LANE), lay3),
                  pl.BlockSpec((1, LANE, LANE), lay3), pl.BlockSpec((1, 1, LANE), lay3),
                  pl.BlockSpec((1, 1, LANE), lay3),
                  pl.BlockSpec((1, LANE, HY_ORDER * 2 * D_B), lay3),
                  pl.BlockSpec((2 * CH, CH), fixed)],
        out_specs=[pl.BlockSpec((1, HY_ORDER, 1, 2, CH, D_B), lambda l, d: (l, 0, d, 0, 0, 0)),
                   pl.BlockSpec((1, HY_ORDER, 1, SUB, D_B), lambda l, d: (l, 0, d, 0, 0))],
        out_shape=[jax.ShapeDtypeStruct((DEPTH, HY_ORDER, nd, 2, CH, D_B), F32),
                   jax.ShapeDtypeStruct((DEPTH, HY_ORDER, nd, SUB, D_B), F32)],
        scratch_shapes=[pltpu.VMEM((2 * L, HY_ORDER * D_B), F32),
                        pltpu.VMEM((2 * L // CH, 2 * CH, HY_ORDER * D_B), F32)],
        compiler_params=_cp(("parallel", "arbitrary")),
        name=f"hyena_filter_{L}",
    )(z, dec, w1, b1, w2, b2, fr, w3, fwd32)


def _hy_variant(L, o_idx, hy_ref, cw_ref, cb_ref, fwd_ref, inv_ref, hab_ref, hd0_ref, bias_ref, o_ref,
                s_y, s_x, s_u, s_v):
    nch, nseq, P = SEG // CH, SEG // L, L // CH
    row = lax.broadcasted_iota(jnp.int32, (CH, 1), 0)
    frow = lax.broadcasted_iota(jnp.int32, (FS, 1), 0)

    @pl.when(o_idx == 0)
    def _():
        cw = cw_ref[...]
        cb = cb_ref[...]

        def short_conv(c, carry):
            start = pl.multiple_of(c * CH, CH)
            xcat, main = _halo_rows(hy_ref, start, CH)
            tpos = (start + row) & (L - 1)
            xm1 = jnp.where(tpos >= 1, xcat[SUB - 1:SUB - 1 + CH], 0.0)
            xp1 = jnp.where(tpos <= L - 2, xcat[SUB + 1:SUB + 1 + CH], 0.0)
            hc = cw[0:1] * xm1 + cw[1:2] * main + cw[2:3] * xp1 + cb
            s_y[pl.ds(start, CH), :] = hc[:, :D_B]
            s_x[0, pl.ds(start, CH), :] = hc[:, D_B:2 * D_B]
            s_x[1, pl.ds(start, CH), :] = hc[:, 2 * D_B:]
            return carry

        lax.fori_loop(0, nch, short_conv, 0)

    bias = bias_ref[0]

    def loop(n, body, init):
        return body(0, init) if n == 1 else lax.fori_loop(0, n, body, init)

    def one_sequence(s, slot):
        base = s * L
        u0 = slot * P

        def forward_dft(j, cc):
            r = pl.multiple_of(base + j * CH, CH)
            s_u[u0 + j] = _dot(fwd_ref[...], s_y[pl.ds(r, CH), :].astype(BF16))
            return cc

        loop(P, forward_dft, 0)

        def output_block(i, cc):
            for fs in range(CH // FS):
                lo = fs * FS

                def accumulate(j, acc):
                    yre, yim = acc
                    d = i - j + (P - 1)
                    ure = s_u[u0 + j, lo:lo + FS, :]
                    uim = s_u[u0 + j, CH + lo:CH + lo + FS, :]
                    a = hab_ref[0, 0, d, 0, lo:lo + FS, :]
                    b = hab_ref[0, 0, d, 1, lo:lo + FS, :]
                    dd = jnp.where(frow == 0, hd0_ref[0, 0, d, 0:1, :], a) if fs == 0 else a
                    return yre + ure * a - uim * b, yim + ure * b + uim * dd

                zero = jnp.zeros((FS, D_B), F32)
                yre, yim = loop(P, accumulate, (zero, zero))
                s_v[slot, lo:lo + FS, :] = yre.astype(BF16)
                s_v[slot, CH + lo:CH + lo + FS, :] = yim.astype(BF16)
            yc = _dot(inv_ref[...], s_v[slot])
            r = pl.multiple_of(base + i * CH, CH)
            s_y[pl.ds(r, CH), :] = s_x[o_idx, pl.ds(r, CH), :] * (yc + s_y[pl.ds(r, CH), :] * bias)
            return cc

        loop(P, output_block, 0)

    def single_block_pair(p, carry):
        rows = [pl.multiple_of((2 * p + k) * L, CH) for k in range(2)]
        ys = [s_y[pl.ds(r, CH), :] for r in rows]
        gates = [s_x[o_idx, pl.ds(r, CH), :] for r in rows]
        a = hab_ref[0, 0, 0, 0]
        b = hab_ref[0, 0, 0, 1]
        dd = jnp.where(row == 0, hd0_ref[0, 0, 0, 0:1, :], a)
        outs = []
        for y, gate in zip(ys, gates):
            u = _dot(fwd_ref[...], y.astype(BF16))
            ure, uim = u[:CH], u[CH:]
            v = jnp.concatenate([ure * a - uim * b, ure * b + uim * dd], axis=0).astype(BF16)
            outs.append(gate * (_dot(inv_ref[...], v) + y * bias))
        for r, out in zip(rows, outs):
            s_y[pl.ds(r, CH), :] = out
        return carry

    if P == 1 and nseq % 2 == 0:
        lax.fori_loop(0, nseq // 2, single_block_pair, 0)
    else:
        lax.fori_loop(0, nseq, lambda s, carry: (one_sequence(s, 0), carry)[1], 0)

    @pl.when(o_idx == HY_ORDER - 1)
    def _():
        def finish(c, carry):
            start = pl.multiple_of(c * CH, CH)
            o_ref[pl.ds(start, CH), :] = _rms(s_y[pl.ds(start, CH), :])
            return carry

        lax.fori_loop(0, nch, finish, 0)


def _hy_kernel(hy_ref, cw_ref, cb_ref, fwd_ref, inv_ref, habc_ref, hd0c_ref, habl_ref, hd0l_ref, bias_ref,
               o_ref, s_y, s_x, s_u, s_v, *, l_ctx, l_lat, n_ctx_blk):
    i = pl.program_id(0)
    o_idx = pl.program_id(1)
    scratch = (s_y, s_x, s_u, s_v)

    @pl.when(i < n_ctx_blk)
    def _():
        _hy_variant(l_ctx, o_idx, hy_ref, cw_ref, cb_ref, fwd_ref, inv_ref, habc_ref, hd0c_ref, bias_ref, o_ref,
                    *scratch)

    @pl.when(i >= n_ctx_blk)
    def _():
        _hy_variant(l_lat, o_idx, hy_ref, cw_ref, cb_ref, fwd_ref, inv_ref, habl_ref, hd0l_ref, bias_ref, o_ref,
                    *scratch)


def _hy_call(layer, hyb, conv_w, conv_b, fwd, inv, habc, hd0c, habl, hd0l, bias, l_ctx, l_lat, n_ctx_blk):
    n = hyb.shape[0]
    ndc, ndl = habc.shape[2], habl.shape[2]
    pmax = max(l_ctx, l_lat) // CH
    row = lambda i, o: (i, 0)
    fixed = lambda i, o: (0, 0)
    lat_o = lambda i, o: jnp.where(i >= n_ctx_blk, o, 0)
    ctx_o = lambda i, o: jnp.where(i < n_ctx_blk, o, 0)
    return pl.pallas_call(
        functools.partial(_hy_kernel, l_ctx=l_ctx, l_lat=l_lat, n_ctx_blk=n_ctx_blk),
        grid=(n // SEG, HY_ORDER),
        in_specs=[pl.BlockSpec((SEG, 3 * D_B), row),
                  pl.BlockSpec((3, 3 * D_B), fixed), pl.BlockSpec((1, 3 * D_B), fixed),
                  pl.BlockSpec((2 * CH, CH), fixed), pl.BlockSpec((CH, 2 * CH), fixed),
                  pl.BlockSpec((1, 1, ndc, 2, CH, D_B), lambda i, o: (layer, ctx_o(i, o), 0, 0, 0, 0)),
                  pl.BlockSpec((1, 1, ndc, SUB, D_B), lambda i, o: (layer, ctx_o(i, o), 0, 0, 0)),
                  pl.BlockSpec((1, 1, ndl, 2, CH, D_B), lambda i, o: (layer, lat_o(i, o), 0, 0, 0, 0)),
                  pl.BlockSpec((1, 1, ndl, SUB, D_B), lambda i, o: (layer, lat_o(i, o), 0, 0, 0)),
                  pl.BlockSpec((1, 1, D_B), lambda i, o: (o, 0, 0))],
        out_specs=pl.BlockSpec((SEG, D_B), row),
        out_shape=jax.ShapeDtypeStruct((n, D_B), F32),
        scratch_shapes=[pltpu.VMEM((SEG, D_B), F32), pltpu.VMEM((HY_ORDER, SEG, D_B), F32),
                        pltpu.VMEM((max(pmax, 2), 2 * CH, D_B), F32), pltpu.VMEM((2, 2 * CH, D_B), BF16)],
        compiler_params=_cp(("parallel", "arbitrary")),
        name="hyena",
    )(hyb, conv_w, conv_b, fwd, inv, habc, hd0c, habl, hd0l, bias)


def _row_scan(x, op, fill, reverse):
    t, width = x.shape
    n_tiles = t // SUB
    sub = lax.broadcasted_iota(jnp.int32, (1, SUB, 1), 1)
    x3 = x.reshape(n_tiles, SUB, width)
    for s in (1, 2, 4):
        shift, keep = (SUB - s, sub < SUB - s) if reverse else (s, sub >= s)
        x3 = op(x3, jnp.where(keep, pltpu.roll(x3, shift, 1), fill))
    x = x3.reshape(t, width)
    out = [None] * n_tiles
    carry = None
    for i in (reversed(range(n_tiles)) if reverse else range(n_tiles)):
        tile = x[i * SUB:(i + 1) * SUB]
        out[i] = tile if carry is None else op(tile, carry)
        carry = out[i][0:1] if reverse else out[i][SUB - 1:SUB]
    return jnp.concatenate(out, axis=0)


_STK_ONE = 3 * SUB


def _mlstm_prep(d, g_ref, m_old):
    T = CH
    reverse = d == 1
    g = g_ref[...]
    if d == 1:
        g = pltpu.roll(g, LANE - 2 * H_C, 1)
    lane = lax.broadcasted_iota(jnp.int32, (1, LANE), 1)
    head = lane < H_C
    b = pltpu.roll(_row_scan(_log_sigmoid(g), jnp.add, 0.0, reverse), LANE - H_C, 1)
    r = jnp.where(head, g - b, 0.0)
    big_m = jnp.maximum(m_old, _row_scan(r, jnp.maximum, -jnp.inf, reverse))
    last = 0 if reverse else T - 1
    m_last = big_m[last:last + 1, :]
    low = lane < SUB
    p0, p1, p2 = (jnp.where(low, p.astype(F32), 0.0) for p in _split3(-big_m))
    cols = (p0 + pltpu.roll(p1, SUB, 1) + pltpu.roll(p2, 2 * SUB, 1)
            + jnp.where(jnp.logical_and(lane >= _STK_ONE, lane < _STK_ONE + SUB), 1.0, 0.0))
    rowid = lax.broadcasted_iota(jnp.int32, (SUB, 1), 0)
    r8 = r.T[0:SUB, :]
    m_last8 = sum(jnp.where(rowid == h, m_last[:, h:h + 1], 0.0) for h in range(H_C))
    ws8 = jnp.where(rowid < H_C, jnp.exp(r8 - m_last8), 0.0)
    return {"r3": [p.astype(F32) for p in _split3(r8)], "ws8": ws8, "wc": jnp.exp(m_old - m_last),
            "m_new": jnp.where(head, b[last:last + 1, :] + m_last, 0.0),
            "wi": jnp.exp(m_old - big_m), "e": jnp.exp(-(b + big_m)), "cols_b": cols.astype(BF16)}


def _mlstm_variant(carry, qf, vf, ktf, gf, qb_, vb_, ktb_, gb_, hf_ref, hb_ref, co_ref, no_ref, mo_ref,
                   s_cx, s_n, s_m):
    T = CH
    ii = lax.broadcasted_iota(jnp.int32, (T, T), 0)
    jj = lax.broadcasted_iota(jnp.int32, (T, T), 1)
    rowid = lax.broadcasted_iota(jnp.int32, (SUB, 1), 0)
    one_col = jnp.where(lax.broadcasted_iota(jnp.int32, (T, DK), 1) == 0, 1.0, 0.0).astype(BF16)
    prep = []
    for d, g_ref in ((0, gf), (1, gb_ if carry else gf)):
        m_old = s_m[d:d + 1, :] if carry else jnp.zeros((1, LANE), F32)
        prep.append(_mlstm_prep(d, g_ref, m_old))
    refs = ((qf, vf, ktf, hf_ref), (qb_, vb_, ktb_, hb_ref))
    for h in range(H_C):
        sl = slice(h * DK, (h + 1) * DK)
        for d in range(2):
            p = prep[d]
            q_ref, v_ref, kt_ref, h_ref = refs[d]
            idx = d * H_C + h
            if carry or d == 0:
                qb = (q_ref[:, sl] * (DK ** -0.5)).astype(BF16)
                kt = kt_ref[sl, :]
                ktb = kt.astype(BF16)
                v_ext = jnp.concatenate([v_ref[:, sl].astype(BF16), one_col], axis=1)
                s_raw = _dot(qb, ktb)
            tri = (jj >= ii) if d == 1 else (jj <= ii)
            sel = jnp.broadcast_to(jnp.where(rowid == h, 1.0, 0.0), (SUB, T))
            rr = sum(jnp.where(rowid == i, piece[h:h + 1, :], 0.0) for i, piece in enumerate(p["r3"]))
            rmat = jnp.concatenate([sel, sel, sel, rr, jnp.zeros((LANE - 4 * SUB, T), F32)], axis=0)
            expo = _dot(p["cols_b"], rmat.astype(BF16))
            s = s_raw * jnp.exp(jnp.where(tri, expo, -jnp.inf))
            intra = _dot(s.astype(BF16), v_ext)
            num, den = intra[:, :DK], intra[:, DK:DK + 1]
            if carry:
                cx = s_cx[idx]
                inter = _dot(qb, cx.astype(BF16))
                wi = p["wi"][:, h:h + 1]
                num, den = num + wi * inter[:, :DK], den + wi * inter[:, DK:DK + 1]
            h_ref[:, sl] = num / jnp.maximum(jnp.abs(den), p["e"][:, h:h + 1])
            upd = _dot((kt * p["ws8"][h:h + 1, :]).astype(BF16), v_ext)
            n_upd = lax.dot_general(p["ws8"].astype(BF16), ktb, _NT, preferred_element_type=F32)[h:h + 1, :]
            if carry:
                wc = p["wc"][:, h:h + 1]
                s_cx[idx] = wc * cx + upd
                s_n[idx:idx + 1, :] = wc * s_n[idx:idx + 1, :] + n_upd
            else:
                co_ref[0, idx] = upd[:, :DK]
                no_ref[0, idx:idx + 1, :] = n_upd
    m_rows = jnp.concatenate([prep[0]["m_new"], prep[1]["m_new"], jnp.zeros((SUB - 2, LANE), F32)], axis=0)
    if carry:
        s_m[...] = m_rows
    else:
        mo_ref[0] = m_rows


def _mlstm_kernel(*refs, n_ctx_steps, nc_lat):
    cx0_ref, n0_ref, m0_ref = refs[8:11]
    s_cx, s_n, s_m = refs[-3:]
    data = refs[:8] + refs[11:]
    t = pl.program_id(0)
    is_ctx = t < n_ctx_steps

    @pl.when(is_ctx)
    def _():
        _mlstm_variant(False, *data)

    @pl.when(jnp.logical_not(is_ctx))
    def _():
        @pl.when((t - n_ctx_steps) % nc_lat == 0)
        def _():
            s_cx[...] = cx0_ref[0]
            s_n[...] = n0_ref[0]
            s_m[...] = m0_ref[0]

        _mlstm_variant(True, *data)


def _mlstm_call(q, v, kt, gates, cx0, n0, m0, n_ctx_steps, nc_lat):
    n = q.shape[0]
    steps = n // CH
    nst = 2 * H_C

    def bwd_blk(t):
        r = jnp.maximum(t - n_ctx_steps, 0)
        return n_ctx_steps + (r // nc_lat) * nc_lat + (nc_lat - 1 - r % nc_lat)

    out_bwd = lambda t: jnp.where(t < n_ctx_steps, t, bwd_blk(t))
    lat_b = lambda t: jnp.maximum(t - n_ctx_steps, 0) // nc_lat
    ctx_b = lambda t: jnp.minimum(t, n_ctx_steps - 1)
    rows = lambda w, blk: pl.BlockSpec((CH, w), lambda t: (blk(t), 0))
    cols = lambda h, blk: pl.BlockSpec((h, CH), lambda t: (0, blk(t)))
    ident = lambda t: t
    return pl.pallas_call(
        functools.partial(_mlstm_kernel, n_ctx_steps=n_ctx_steps, nc_lat=nc_lat),
        grid=(steps,),
        in_specs=[rows(D_C, ident), rows(D_C, ident), cols(D_C, ident), rows(LANE, ident),
                  rows(D_C, bwd_blk), rows(D_C, bwd_blk), cols(D_C, bwd_blk), rows(LANE, bwd_blk),
                  pl.BlockSpec((1, nst, DK, 2 * DK), lambda t: (lat_b(t), 0, 0, 0)),
                  pl.BlockSpec((1, nst, DK), lambda t: (lat_b(t), 0, 0)),
                  pl.BlockSpec((1, SUB, LANE), lambda t: (lat_b(t), 0, 0))],
        out_specs=[rows(D_C, ident), rows(D_C, out_bwd),
                   pl.BlockSpec((1, nst, DK, DK), lambda t: (ctx_b(t), 0, 0, 0)),
                   pl.BlockSpec((1, nst, DK), lambda t: (ctx_b(t), 0, 0)),
                   pl.BlockSpec((1, SUB, LANE), lambda t: (ctx_b(t), 0, 0))],
        out_shape=[jax.ShapeDtypeStruct((n, D_C), F32), jax.ShapeDtypeStruct((n, D_C), F32),
                   jax.ShapeDtypeStruct((n_ctx_steps, nst, DK, DK), F32),
                   jax.ShapeDtypeStruct((n_ctx_steps, nst, DK), F32),
                   jax.ShapeDtypeStruct((n_ctx_steps, SUB, LANE), F32)],
        scratch_shapes=[pltpu.VMEM((nst, DK, 2 * DK), F32), pltpu.VMEM((nst, DK), F32),
                        pltpu.VMEM((SUB, LANE), F32)],
        compiler_params=_cp(("arbitrary",)),
        name="mlstm",
    )(q, v, kt, gates, q, v, kt, gates, cx0, n0, m0)


def _out_kernel(x_ref, oa_ref, ob_ref, hf_ref, hb_ref, og_ref, mod_ref, mg_ref, w_ref, g_ref, b_ref,
                rw_ref, rb_ref, x1_ref, he_ref, cnt_ref, it_ref, s_cnt):
    i = pl.program_id(0)

    @pl.when(i == 0)
    def _():
        s_cnt[...] = jnp.zeros(s_cnt.shape, F32)

    m = mod_ref[0]
    mg = mg_ref[...]
    hc = hf_ref[...] + hb_ref[...]
    og = og_ref[...]
    out_c = [_sigmoid(og[:, h * DK:(h + 1) * DK]) * _rms(hc[:, h * DK:(h + 1) * DK]) for h in range(H_C)]
    mix = jnp.concatenate([oa_ref[...], ob_ref[...]] + out_c, axis=1) * mg
    acc = _dot(mix.astype(BF16), w_ref[...])
    x1 = _ln_plain(ALPHA * x_ref[...] + m[2:3] * acc) * g_ref[...] + b_ref[...]
    x1_ref[...] = x1
    h2 = x1 * (1.0 + m[4:5]) + m[3:4]
    he_ref[:, :D_MODEL] = h2

    lg = _dot3(h2, rw_ref[...]) + rb_ref[...]
    col = lax.broadcasted_iota(jnp.int32, lg.shape, 1)
    colf = col.astype(F32)
    first_at = lambda hit: jnp.min(jnp.where(hit, colf, float(LANE)), -1, keepdims=True).astype(jnp.int32)
    ninf = -jnp.inf
    lgm = jnp.where(col < N_GROUPS, lg, ninf)
    mx = jnp.max(lgm, -1, keepdims=True)
    gi = first_at(lgm == mx)
    pg_top = 1.0 / jnp.sum(jnp.where(col < N_GROUPS, jnp.exp(lg - mx), 0.0), -1, keepdims=True)
    lo4 = N_GROUPS + E_PER_GROUP * gi
    lem = jnp.where(jnp.logical_and(col >= lo4, col < lo4 + E_PER_GROUP), lg, ninf)
    v1 = jnp.max(lem, -1, keepdims=True)
    i1 = first_at(lem == v1)
    lem2 = jnp.where(col == i1, ninf, lem)
    v2 = jnp.max(lem2, -1, keepdims=True)
    i2 = first_at(lem2 == v2)
    e21 = jnp.exp(v2 - v1)
    w1 = pg_top / (1.0 + e21)
    w2 = pg_top * e21 / (1.0 + e21)
    e1, e2 = i1 - N_GROUPS, i2 - N_GROUPS
    first_lo = e1 < e2
    elo, ehi = jnp.minimum(e1, e2), jnp.maximum(e1, e2)
    wlo, whi = jnp.where(first_lo, w1, w2), jnp.where(first_lo, w2, w1)
    llo, lhi = elo - E_PER_GROUP * gi, ehi - E_PER_GROUP * gi
    cls = gi * N_PAIRS + ((llo * (7 - llo)) >> 1) + lhi - llo - 1

    oh = jnp.where(col == cls, 1.0, 0.0)
    ii = lax.broadcasted_iota(jnp.int32, (TM_OUT, TM_OUT), 0)
    jj = lax.broadcasted_iota(jnp.int32, (TM_OUT, TM_OUT), 1)
    before = jnp.where(jj < ii, 1.0, 0.0).astype(BF16)
    cnt = s_cnt[0:1, :]
    rank = jnp.sum(oh * (_dot(before, oh.astype(BF16)) + cnt), -1, keepdims=True)
    cnt = cnt + jnp.sum(oh, 0, keepdims=True)
    s_cnt[...] = jnp.broadcast_to(cnt, s_cnt.shape)
    cnt_ref[...] = jnp.broadcast_to(cnt, cnt_ref.shape)

    info = jnp.zeros(lg.shape, F32)
    for c, val in ((I_CLS, cls.astype(F32)), (I_ELO, elo.astype(F32)), (I_EHI, ehi.astype(F32)),
                   (I_WLO, wlo), (I_WHI, whi), (I_RANK, rank)):
        info = jnp.where(col == c, val, info)
    he_ref[:, D_MODEL:] = info
    it_ref[...] = info.T[:SUB, :]


def _out_call(x, out_a, out_b, hcf, hcb, ogate, mod_l, mix_g, w_out, ln_g, ln_b, rt_w, rt_b, group_of_tile):
    n = x.shape[0]
    row = lambda i: (i, 0)
    fixed = lambda i: (0, 0)
    return pl.pallas_call(
        _out_kernel,
        grid=(n // TM_OUT,),
        in_specs=[pl.BlockSpec((TM_OUT, D_MODEL), row), pl.BlockSpec((TM_OUT, D_A), row),
                  pl.BlockSpec((TM_OUT, D_B), row),
                  pl.BlockSpec((TM_OUT, D_C), row), pl.BlockSpec((TM_OUT, D_C), row), pl.BlockSpec((TM_OUT, D_C), row),
                  pl.BlockSpec((1, 6, D_MODEL), lambda i: (group_of_tile(i), 0, 0)),
                  pl.BlockSpec((1, D_MODEL), fixed), pl.BlockSpec((D_MODEL, D_MODEL), fixed),
                  pl.BlockSpec((1, D_MODEL), fixed), pl.BlockSpec((1, D_MODEL), fixed),
                  pl.BlockSpec((D_MODEL, LANE), fixed), pl.BlockSpec((1, LANE), fixed)],
        out_specs=[pl.BlockSpec((TM_OUT, D_MODEL), row), pl.BlockSpec((TM_OUT, D_EXT), row),
                   pl.BlockSpec((SUB, LANE), fixed), pl.BlockSpec((SUB, TM_OUT), lambda i: (0, i))],
        out_shape=[jax.ShapeDtypeStruct((n, D_MODEL), F32), jax.ShapeDtypeStruct((n, D_EXT), F32),
                   jax.ShapeDtypeStruct((SUB, LANE), F32), jax.ShapeDtypeStruct((SUB, n), F32)],
        scratch_shapes=[pltpu.VMEM((SUB, LANE), F32)],
        compiler_params=_cp(("arbitrary",)),
        name="out_proj_router",
    )(x, out_a, out_b, hcf, hcb, ogate, mod_l, mix_g, w_out, ln_g, ln_b, rt_w, rt_b)


def _row_copy(src_ref, src_row, dst_ref, dst_row, sem):
    return pltpu.make_async_copy(src_ref.at[pl.ds(src_row, 1), :], dst_ref.at[pl.ds(dst_row, 1), :], sem)


def _sorted_row(cls_ref, rank_ref, start_ref, t):
    return start_ref[cls_ref[t]] + rank_ref[t]


def _scatter_kernel(cls_ref, rank_ref, start_ref, pad_ref, na_ref, x_ref, o_ref, ring, z_ref, sem_z, load_sems,
                    row_sems):
    n_tiles = o_ref.shape[0] // TM

    def zero_tile(row):
        return pltpu.make_async_copy(z_ref, o_ref.at[pl.ds(pl.multiple_of(row, TM), TM), :], sem_z)

    def for_zero_tiles(fn):
        def per_class(c, carry):
            row = pad_ref[c]

            @pl.when(row >= 0)
            def _():
                fn(zero_tile(row))

            return carry

        lax.fori_loop(0, N_CLASS, per_class, 0)

        def per_idle(t, carry):
            fn(zero_tile(t * TM))
            return carry

        lax.fori_loop(na_ref[0], n_tiles, per_idle, 0)

    i = pl.program_id(0)
    n_steps = x_ref.shape[0] // TM_PERM

    def load(step):
        return pltpu.make_async_copy(x_ref.at[pl.ds(pl.multiple_of(step * TM_PERM, TM_PERM), TM_PERM), :],
                                     ring.at[step % RING], load_sems.at[step % RING])

    def start_rows(step):
        slot = step % RING

        def body(r, carry):
            dst = _sorted_row(cls_ref, rank_ref, start_ref, step * TM_PERM + r)
            _row_copy(ring.at[slot], r, o_ref, dst, row_sems.at[slot]).start()
            return carry

        lax.fori_loop(0, TM_PERM, body, 0, unroll=DMA_UNROLL)

    def wait_rows(step):
        slot = step % RING
        pltpu.make_async_copy(ring.at[slot], o_ref.at[pl.ds(0, TM_PERM), :], row_sems.at[slot]).wait()

    @pl.when(i == 0)
    def _():
        z_ref[...] = jnp.zeros(z_ref.shape, F32)
        for_zero_tiles(lambda cp: cp.start())
        for_zero_tiles(lambda cp: cp.wait())
        load(0).start()

    @pl.when(i >= RING - 1)
    def _():
        wait_rows(i - (RING - 1))

    @pl.when(i + 1 < n_steps)
    def _():
        load(i + 1).start()

    load(i).wait()
    start_rows(i)

    @pl.when(i == n_steps - 1)
    def _():
        for back in range(min(RING - 1, n_steps) - 1, -1, -1):
            wait_rows(i - back)


def _scatter_call(plan, h_ext, n_tiles_max):
    any_spec = pl.BlockSpec(memory_space=pl.ANY)
    return pl.pallas_call(
        _scatter_kernel,
        grid_spec=pltpu.PrefetchScalarGridSpec(
            num_scalar_prefetch=5, grid=(h_ext.shape[0] // TM_PERM,),
            in_specs=[any_spec],
            out_specs=any_spec,
            scratch_shapes=[pltpu.VMEM((RING, TM_PERM, D_EXT), F32), pltpu.VMEM((TM, D_EXT), F32),
                            pltpu.SemaphoreType.DMA(()),
                            pltpu.SemaphoreType.DMA((RING,)), pltpu.SemaphoreType.DMA((RING,))]),
        out_shape=jax.ShapeDtypeStruct((n_tiles_max * TM, D_EXT), F32),
        compiler_params=_cp(("arbitrary",)),
        name="moe_scatter",
    )(plan["cls"], plan["rank"], plan["row_start"], plan["pad_rows"], plan["n_act"], h_ext)


def _moe_kernel(tg_ref, ng_ref, lo_ref, hi_ref, cg_ref, nv_ref, na_ref, x_ref, w1_hbm, w3_hbm, w2_hbm, o_ref,
                w1_ref, w3_ref, w2_ref, s1, s3, s2, wsem, *, layer):
    t = pl.program_id(0)
    active = t < na_ref[0]
    half = TM // 2

    def group_copies(g):
        rows = pl.ds(g * E_PER_GROUP, E_PER_GROUP)
        return [pltpu.make_async_copy(src.at[layer, rows], dst, wsem.at[k])
                for k, (src, dst) in enumerate(((w1_hbm, w1_ref), (w3_hbm, w3_ref), (w2_hbm, w2_ref)))]

    @pl.when(t == 0)
    def _():
        for cp in group_copies(tg_ref[0]):
            cp.start()

    @pl.when(jnp.logical_not(active))
    def _():
        o_ref[...] = jnp.zeros(o_ref.shape, F32)

    @pl.when(jnp.logical_and(active, cg_ref[t] == 1))
    def _():
        for cp in group_copies(tg_ref[t]):
            cp.wait()
        for e in range(E_PER_GROUP):
            s1[e] = w1_ref[e].astype(BF16)
            s3[e] = w3_ref[e].astype(BF16)
            s2[e] = w2_ref[e].astype(BF16)

        @pl.when(ng_ref[t] >= 0)
        def _():
            for cp in group_copies(ng_ref[t]):
                cp.start()

    def run(rows):
        xe = x_ref[0:rows, :]
        x = xe[:, :D_MODEL].astype(BF16)

        def expert(e, gate):
            a = _dot(x, s1[e])
            hm = a * _sigmoid(a) * _dot(x, s3[e]) * gate
            return _dot(hm.astype(BF16), s2[e])

        o_ref[0:rows, :] = (expert(lo_ref[t], xe[:, D_MODEL + I_WLO:D_MODEL + I_WLO + 1])
                            + expert(hi_ref[t], xe[:, D_MODEL + I_WHI:D_MODEL + I_WHI + 1]))

    @pl.when(jnp.logical_and(active, nv_ref[t] > half))
    def _():
        run(TM)

    @pl.when(jnp.logical_and(active, nv_ref[t] <= half))
    def _():
        run(half)
        o_ref[half:, :] = jnp.zeros((TM - half, D_MODEL), F32)


def _moe_call(layer, plan, x_sorted, w1, w3, w2):
    r = x_sorted.shape[0]
    act = lambda t, *scalars: (jnp.minimum(t, scalars[-1][0] - 1), 0)
    any_spec = pl.BlockSpec(memory_space=pl.ANY)
    up, down = (E_PER_GROUP, D_MODEL, D_E), (E_PER_GROUP, D_E, D_MODEL)
    return pl.pallas_call(
        functools.partial(_moe_kernel, layer=layer),
        grid_spec=pltpu.PrefetchScalarGridSpec(
            num_scalar_prefetch=7, grid=(r // TM,),
            in_specs=[pl.BlockSpec((TM, D_EXT), act), any_spec, any_spec, any_spec],
            out_specs=pl.BlockSpec((TM, D_MODEL), lambda t, *_: (t, 0)),
            scratch_shapes=[pltpu.VMEM(up, F32), pltpu.VMEM(up, F32), pltpu.VMEM(down, F32),
                            pltpu.VMEM(up, BF16), pltpu.VMEM(up, BF16), pltpu.VMEM(down, BF16),
                            pltpu.SemaphoreType.DMA((3,))]),
        out_shape=jax.ShapeDtypeStruct((r, D_MODEL), F32),
        compiler_params=_cp(("arbitrary",)),
        name="moe_experts",
    )(plan["tile_grp"], plan["next_grp"], plan["tile_lo"], plan["tile_hi"], plan["chg_grp"], plan["valid"],
      plan["n_act"], x_sorted, w1, w3, w2)


def _ln2_kernel(cls_ref, rank_ref, start_ref, x1_ref, mod_ref, g_ref, b_ref, y_ref, *rest, n_ctx_tiles):
    o_refs, (buf, sems) = rest[:-2], rest[-2:]
    i = pl.program_id(0)
    n_steps = pl.num_programs(0)

    def start_rows(step):
        slot = step % 2

        def body(r, carry):
            src = _sorted_row(cls_ref, rank_ref, start_ref, step * TM_PERM + r)
            _row_copy(y_ref, src, buf.at[slot], r, sems.at[slot]).start()
            return carry

        lax.fori_loop(0, TM_PERM, body, 0, unroll=DMA_UNROLL)

    @pl.when(i == 0)
    def _():
        start_rows(0)

    @pl.when(i + 1 < n_steps)
    def _():
        start_rows(i + 1)

    slot = i % 2
    pltpu.make_async_copy(y_ref.at[pl.ds(0, TM_PERM), :], buf.at[slot], sems.at[slot]).wait()
    m = mod_ref[0]
    y = _ln_plain(ALPHA * x1_ref[...] + m[5:6] * buf[slot]) * g_ref[...] + b_ref[...]
    if len(o_refs) == 1:
        o_refs[0][...] = y
    else:
        @pl.when(i < n_ctx_tiles)
        def _():
            o_refs[0][...] = y

        @pl.when(i >= n_ctx_tiles)
        def _():
            o_refs[1][...] = y


def _ln2_call(plan, x1, y_sorted, mod_l, ln_g, ln_b, group_of_tile, n_ctx_tiles, split):
    n = x1.shape[0]
    row = lambda i, *_: (i, 0)
    fixed = lambda i, *_: (0, 0)
    if split:
        n_ctx = n_ctx_tiles * TM_PERM
        out_specs = [pl.BlockSpec((TM_PERM, D_MODEL), lambda i, *_: (jnp.minimum(i, n_ctx_tiles - 1), 0)),
                     pl.BlockSpec((TM_PERM, D_MODEL), lambda i, *_: (jnp.maximum(i - n_ctx_tiles, 0), 0))]
        out_shape = [jax.ShapeDtypeStruct((n_ctx, D_MODEL), F32), jax.ShapeDtypeStruct((n - n_ctx, D_MODEL), F32)]
    else:
        out_specs = [pl.BlockSpec((TM_PERM, D_MODEL), row)]
        out_shape = [jax.ShapeDtypeStruct((n, D_MODEL), F32)]
    return pl.pallas_call(
        functools.partial(_ln2_kernel, n_ctx_tiles=n_ctx_tiles),
        grid_spec=pltpu.PrefetchScalarGridSpec(
            num_scalar_prefetch=3, grid=(n // TM_PERM,),
            in_specs=[pl.BlockSpec((TM_PERM, D_MODEL), row),
                      pl.BlockSpec((1, 6, D_MODEL), lambda i, *_: (group_of_tile(i), 0, 0)),
                      pl.BlockSpec((1, D_MODEL), fixed), pl.BlockSpec((1, D_MODEL), fixed),
                      pl.BlockSpec(memory_space=pl.ANY)],
            out_specs=out_specs,
            scratch_shapes=[pltpu.VMEM((2, TM_PERM, D_MODEL), F32), pltpu.SemaphoreType.DMA((2,))]),
        out_shape=out_shape,
        compiler_params=_cp(("arbitrary",)),
        name="moe_gather_ln2",
    )(plan["cls"], plan["rank"], plan["row_start"], x1, mod_l, ln_g, ln_b, y_sorted)


def _dft_matrices():
    n2 = 2 * CH
    f = np.arange(CH, dtype=np.float64)[:, None]
    t = np.arange(CH, dtype=np.float64)[None, :]
    ang = 2.0 * np.pi * f * t / n2
    re, im = np.cos(ang), -np.sin(ang)
    im[0, :] = np.cos(np.pi * t[0])
    fwd = np.concatenate([re, im], axis=0)
    scale = np.full((CH, 1), 2.0 / n2)
    scale[0, 0] = 1.0 / n2
    inv = np.concatenate([(re * scale).T, (im * scale).T], axis=1)
    return fwd.astype(np.float32), inv.astype(np.float32)


def _filter_features(L):
    lag = np.arange(-L, L)
    m = np.minimum(np.abs(lag), L - 1)
    t = (np.arange(L, dtype=np.float32) / np.float32(max(L - 1, 1)))[m]
    w = (np.float32(2.0 * math.pi) * np.arange(L, dtype=np.float32) / np.float32(L))[m]
    bands = np.linspace(1e-4, HY_BANDS - 1, HY_BANDS, dtype=np.float32)
    z = np.zeros((2 * L, LANE), np.float32)
    z[:, 0] = t
    z[:, 1:1 + HY_BANDS] = np.cos(w[:, None] * bands)
    z[:, 1 + HY_BANDS:HY_EMB] = -np.sin(w[:, None] * bands)
    lo, hi = math.log(HY_DECAY_TARGET) / 1.5, math.log(HY_DECAY_TARGET) / 0.3
    deltas = np.abs(np.linspace(lo, hi, D_B, dtype=np.float32))
    dec = np.exp(-t[:, None] * deltas)
    return z, dec.astype(np.float32)


def _sincos_2d(rows, cols):
    quarter = D_MODEL // 4
    omega = 1.0 / (10000.0 ** (jnp.arange(quarter, dtype=F32) / quarter))

    def emb(n):
        ang = jnp.arange(n, dtype=F32)[:, None] * omega[None]
        return jnp.concatenate([jnp.sin(ang), jnp.cos(ang)], -1)

    er, ec = emb(rows), emb(cols)
    half = D_MODEL // 2
    pos = jnp.concatenate([jnp.broadcast_to(er[:, None], (rows, cols, half)),
                           jnp.broadcast_to(ec[None], (rows, cols, half))], -1)
    return pos.reshape(rows * cols, D_MODEL)


def _pad_to(x, shape):
    return jnp.pad(x, [(0, s - d) for d, s in zip(x.shape, shape)])


def _block_diag(w):
    eye = jnp.eye(H_A, dtype=w.dtype)
    return jnp.einsum("hij,hg->higj", w, eye).reshape(D_A, D_A)


_PAIR_LO = np.array([0, 0, 0, 1, 1, 2], np.int32)
_PAIR_HI = np.array([1, 2, 3, 2, 3, 3], np.int32)


def _routing_plan(info_t, counts, n_tiles_max):
    cnt = counts[0, :N_CLASS].astype(jnp.int32)
    tiles = (cnt + TM - 1) // TM
    tile_end = jnp.cumsum(tiles)
    n_act = tile_end[-1]
    t = jnp.minimum(jnp.arange(n_tiles_max, dtype=jnp.int32), n_act - 1)
    tcls = jnp.minimum(jnp.sum((tile_end[None, :] <= t[:, None]).astype(jnp.int32), 1), N_CLASS - 1)
    grp, pair = (tcls // N_PAIRS).astype(jnp.int32), tcls % N_PAIRS
    valid = jnp.clip(cnt[tcls] - (t - (tile_end - tiles)[tcls]) * TM, 0, TM).astype(jnp.int32)
    first = jnp.ones((1,), jnp.int32)
    changed = lambda e: jnp.concatenate([first, (e[1:] != e[:-1]).astype(jnp.int32)])
    chg_grp = changed(grp)
    idx = jnp.arange(n_tiles_max, dtype=jnp.int32)
    later_change = jnp.logical_and(idx[None, :] > idx[:, None], chg_grp[None, :] == 1)
    next_pos = jnp.min(jnp.where(later_change, idx[None, :], n_tiles_max), axis=1)
    next_grp = jnp.where(next_pos < n_tiles_max, grp[jnp.minimum(next_pos, n_tiles_max - 1)], -1).astype(jnp.int32)
    return {"cls": info_t[I_CLS].astype(jnp.int32), "rank": info_t[I_RANK].astype(jnp.int32),
            "row_start": ((tile_end - tiles) * TM).astype(jnp.int32), "tile_grp": grp, "chg_grp": chg_grp, "next_grp": next_grp,
            "tile_lo": jnp.asarray(_PAIR_LO)[pair], "tile_hi": jnp.asarray(_PAIR_HI)[pair], "valid": valid,
            "n_act": n_act.reshape(1).astype(jnp.int32),
            "pad_rows": jnp.where(tiles > 0, (tile_end - 1) * TM, -1).astype(jnp.int32)}


def kernel(x_prompt, x_sample, c, state_lru, state_mlstm_C, state_mlstm_n, state_mlstm_m, c_ctx, w_ada, b_ada, w_in, b_in, conv_a_w, conv_a_b, lru_wa, lru_ba, lru_wx, lru_bx, lru_lam, conv_b_w, conv_b_b, hy_w1, hy_b1, hy_w2, hy_b2, hy_freq, hy_w3, hy_bias, mix_g, w_out, ln1_g, ln1_b, rt_wg, rt_bg, rt_we, rt_be, moe_w1, moe_w3, moe_w2, ln2_g, ln2_b):
    B, l_ctx, D = x_prompt.shape
    b_lat, l_lat, _ = x_sample.shape
    n_ctx, n_lat = B * l_ctx, b_lat * l_lat
    n = n_ctx + n_lat
    assert D == D_MODEL and w_in.shape[-1] == D_MAIN + N_GATE
    assert SEG % l_ctx == 0 and l_lat == SEG and l_ctx % CH == 0 and n_ctx % SEG == 0
    assert l_ctx == CH, "the mLSTM step schedule assumes one chunk per context sequence"
    assert 1 + b_lat <= SUB
    n_ctx_blk = n_ctx // SEG
    nc_lat = l_lat // CH

    def group_of(tile_rows):
        first_lat, per_seq = n_ctx // tile_rows, l_lat // tile_rows
        return lambda i: jnp.where(i < first_lat, 0, 1 + (i - first_lat) // per_seq)

    cond = jnp.concatenate([c_ctx[None], c, jnp.zeros((SUB - 1 - b_lat, D), F32)], 0)
    mod = _mod_call(cond, w_ada, b_ada).reshape(DEPTH, SUB, 6, D)
    pos = _sincos_2d(l_lat // GRID_W, GRID_W)
    x = _entry_call(x_prompt.reshape(n_ctx, D), x_sample.reshape(n_lat, D), pos)

    fwd_np, inv_np = _dft_matrices()
    fwd32 = jnp.asarray(fwd_np)
    fwd16, inv16 = fwd32.astype(BF16), jnp.asarray(inv_np).astype(BF16)
    fw1 = _pad_to(hy_w1, (DEPTH, LANE, LANE))
    fb1 = _pad_to(hy_b1[:, None, :], (DEPTH, 1, LANE))
    fw2 = _pad_to(hy_w2, (DEPTH, LANE, LANE))
    fb2 = _pad_to(hy_b2[:, None, :], (DEPTH, 1, LANE))
    ffr = _pad_to(hy_freq[:, None, :], (DEPTH, 1, LANE))
    fw3 = _pad_to(hy_w3, (DEPTH, LANE, HY_ORDER * 2 * D_B))
    spectra = {}
    for L in (l_ctx, l_lat):
        z_np, dec_np = _filter_features(L)
        spectra[L] = _filt_call(L, jnp.asarray(z_np), jnp.asarray(dec_np), fw1, fb1, fw2, fb2, ffr, fw3, fwd32)

    w_in_t = jnp.swapaxes(w_in, 1, 2)
    lat_slots = SEG // l_ctx
    st_lru, st_c, st_n, st_m = [], [], [], []
    for l in range(DEPTH):
        b_main = b_in[l, None, :D_MAIN]
        k_lo, k_hi = _Z_CUTS[_K_CUT], _Z_CUTS[_K_CUT + 1]
        b_kt = jnp.broadcast_to(b_in[l, k_lo:k_hi, None], (D_C, TM_IN))
        w_gate = _pad_to(w_in[l, :, D_MAIN:], (D, LANE))
        b_gate = _pad_to(b_in[l, None, D_MAIN:], (1, LANE))
        xa, ya, hyb, q, v, og, gates, kt = _in_call(l, x, mod[l], w_in_t, b_main, b_kt, w_gate, b_gate,
                                                    group_of(TM_IN))

        lru_w = jnp.concatenate([_block_diag(lru_wa[l, 0]), _block_diag(lru_wx[l, 0]),
                                 _block_diag(lru_wa[l, 1]), _block_diag(lru_wx[l, 1])], 1).astype(BF16)
        lru_b = jnp.concatenate([lru_ba[l, 0], lru_bx[l, 0], lru_ba[l, 1], lru_bx[l, 1]])[None]
        h0_lat = _pad_to(state_lru[:, l][:, None], (b_lat, lat_slots, 2, D_A))
        h0_all = jnp.concatenate([jnp.zeros((n_ctx_blk, lat_slots, 2, D_A), F32), h0_lat], 0)
        out_a, lru_last = _lru_call(xa, ya, conv_a_w[l], conv_a_b[l, None], lru_w, lru_b, lru_lam[l], h0_all,
                                    l_ctx, l_lat, n_ctx_blk)

        habc, hd0c = spectra[l_ctx]
        habl, hd0l = spectra[l_lat]
        out_b = _hy_call(l, hyb, conv_b_w[l], conv_b_b[l, None], fwd16, inv16, habc, hd0c, habl, hd0l,
                         hy_bias[l][:, None, :], l_ctx, l_lat, n_ctx_blk)

        n0 = state_mlstm_n[:, l].reshape(b_lat, 2 * H_C, DK)
        cx0 = jnp.concatenate([state_mlstm_C[:, l].reshape(b_lat, 2 * H_C, DK, DK), n0[..., None],
                               jnp.zeros((b_lat, 2 * H_C, DK, DK - 1), F32)], -1)
        m0 = _pad_to(state_mlstm_m[:, l], (b_lat, SUB, LANE))
        hcf, hcb, c_fin, n_fin, m_fin = _mlstm_call(q, v, kt, gates, cx0, n0, m0, n_ctx // CH, nc_lat)

        rt_w = _pad_to(jnp.concatenate([rt_wg[l], rt_we[l]], 1), (D, LANE))
        rt_b = _pad_to(jnp.concatenate([rt_bg[l], rt_be[l]])[None], (1, LANE))
        x1, h_ext, counts, info_t = _out_call(x, out_a, out_b, hcf, hcb, og, mod[l], mix_g[l, None],
                                              w_out[l].astype(BF16), ln1_g[l, None], ln1_b[l, None], rt_w, rt_b,
                                              group_of(TM_OUT))

        n_tiles_max = n // TM + N_CLASS
        plan = _routing_plan(info_t, counts, n_tiles_max)
        x_sorted = _scatter_call(plan, h_ext, n_tiles_max)
        y_sorted = _moe_call(l, plan, x_sorted, moe_w1, moe_w3, moe_w2)
        outs = _ln2_call(plan, x1, y_sorted, mod[l], ln2_g[l, None], ln2_b[l, None], group_of(TM_PERM),
                         n_ctx // TM_PERM, split=(l == DEPTH - 1))
        x = outs[0]

        st_lru.append(lru_last[:n_ctx_blk].reshape(B, 2, D_A))
        st_c.append(c_fin.reshape(B, 2, H_C, DK, DK))
        st_n.append(n_fin.reshape(B, 2, H_C, DK))
        st_m.append(m_fin[:, :2, :H_C])

    return (outs[0].reshape(B, l_ctx, D), outs[1].reshape(b_lat, l_lat, D),
            jnp.stack(st_lru, 1), jnp.stack(st_c, 1), jnp.stack(st_n, 1), jnp.stack(st_m, 1))
```

```python
import functools
import math

import numpy as np
import jax
import jax.numpy as jnp
from jax import lax
from jax.experimental import pallas as pl
from jax.experimental.pallas import tpu as pltpu

F32 = jnp.float32
BF16 = jnp.bfloat16

D_MODEL = 1024
DEPTH = 2
GRID_W = 64
D_A = 256
H_A = 4
BA = D_A // H_A
LRU_C = 8.0
D_B = 256
HY_ORDER = 2
HY_BANDS = 16
HY_EMB = 1 + 2 * HY_BANDS
HY_FH = 64
HY_DECAY_TARGET = 1e-2
D_C = 512
H_C = 4
DK = D_C // H_C
N_GROUPS = 4
E_PER_GROUP = 4
N_EXP = N_GROUPS * E_PER_GROUP
N_PAIRS = 6
N_CLASS = N_GROUPS * N_PAIRS
D_E = 512
ALPHA = (2 * DEPTH) ** 0.25
EPS = 1e-6
D_MAIN = 2 * D_A + 3 * D_B + 4 * D_C
N_GATE = 4 * H_C

LANE = 128
SUB = 8
VMEM_LIMIT = 56 * 1024 * 1024

CH = 256
SEG = 2048
TM = 256
TM_IN = 512
TM_OUT = 512
TM_PERM = 512
D_EXT = D_MODEL + LANE
FS = 64
DMA_UNROLL = 8
RING = 3

I_CLS, I_ELO, I_EHI, I_WLO, I_WHI, I_RANK = range(6)


def _cp(sem, vmem=VMEM_LIMIT):
    return pltpu.CompilerParams(dimension_semantics=sem, vmem_limit_bytes=vmem)


def _dot(a, b):
    return jnp.dot(a, b, preferred_element_type=F32)


def _split2(x):
    hi = x.astype(BF16)
    lo = (x - hi.astype(F32)).astype(BF16)
    return hi, lo


def _dot3(a, b):
    ah, al = _split2(a)
    bh, bl = _split2(b)
    return _dot(ah, bh) + (_dot(ah, bl) + _dot(al, bh))


def _split3(x):
    hi = x.astype(BF16)
    r1 = x - hi.astype(F32)
    mid = r1.astype(BF16)
    lo = (r1 - mid.astype(F32)).astype(BF16)
    return hi, mid, lo


def _sigmoid(x):
    return 1.0 / (1.0 + jnp.exp(-x))


def _log_sigmoid(x):
    return jnp.minimum(x, 0.0) - jnp.log1p(jnp.exp(-jnp.abs(x)))


def _gelu_tanh(x):
    return 0.5 * x * (1.0 + jnp.tanh(math.sqrt(2.0 / math.pi) * (x + 0.044715 * (x * x * x))))


def _ln_plain(x):
    mu = jnp.mean(x, -1, keepdims=True)
    xc = x - mu
    var = jnp.mean(xc * xc, -1, keepdims=True)
    return xc * lax.rsqrt(var + EPS)


def _rms(x):
    return x * lax.rsqrt(jnp.mean(x * x, -1, keepdims=True) + EPS)


def _halo_rows(ref, start, rows):
    total = ref.shape[0]
    prev = ref[pl.ds(pl.multiple_of(jnp.maximum(start - SUB, 0), SUB), SUB), :]
    main = ref[pl.ds(start, rows), :]
    nxt = ref[pl.ds(pl.multiple_of(jnp.minimum(start + rows, total - SUB), SUB), SUB), :]
    return jnp.concatenate([prev, main, nxt], axis=0), main


def _mod_kernel(c_ref, w_ref, b_ref, o_ref):
    c = c_ref[...]
    o_ref[0] = _dot3(c * _sigmoid(c), w_ref[0]) + b_ref[0]


def _mod_call(cond, w_ada, b_ada):
    tn = 1536
    n6 = w_ada.shape[-1]
    return pl.pallas_call(
        _mod_kernel,
        grid=(DEPTH, n6 // tn),
        in_specs=[pl.BlockSpec((SUB, D_MODEL), lambda l, j: (0, 0)),
                  pl.BlockSpec((1, D_MODEL, tn), lambda l, j: (l, 0, j)),
                  pl.BlockSpec((1, 1, tn), lambda l, j: (l, 0, j))],
        out_specs=pl.BlockSpec((1, SUB, tn), lambda l, j: (l, 0, j)),
        out_shape=jax.ShapeDtypeStruct((DEPTH, SUB, n6), F32),
        compiler_params=_cp(("parallel", "parallel")),
        name="adaln_mod",
    )(cond, w_ada, b_ada.reshape(DEPTH, 1, n6))


def _entry_kernel(xc_ref, xl_ref, pos_ref, o_ref, *, n_ctx_tiles):
    i = pl.program_id(0)

    @pl.when(i < n_ctx_tiles)
    def _():
        o_ref[...] = _ln_plain(xc_ref[...])

    @pl.when(i >= n_ctx_tiles)
    def _():
        o_ref[...] = _ln_plain(xl_ref[...] + pos_ref[...])


def _entry_call(xc, xl, pos):
    tm = 512
    n_ctx, n_lat, l_lat = xc.shape[0], xl.shape[0], pos.shape[0]
    nct = n_ctx // tm
    per_seq = l_lat // tm
    return pl.pallas_call(
        functools.partial(_entry_kernel, n_ctx_tiles=nct),
        grid=((n_ctx + n_lat) // tm,),
        in_specs=[pl.BlockSpec((tm, D_MODEL), lambda i: (jnp.minimum(i, nct - 1), 0)),
                  pl.BlockSpec((tm, D_MODEL), lambda i: (jnp.maximum(i - nct, 0), 0)),
                  pl.BlockSpec((tm, D_MODEL), lambda i: (jnp.maximum(i - nct, 0) % per_seq, 0))],
        out_specs=pl.BlockSpec((tm, D_MODEL), lambda i: (i, 0)),
        out_shape=jax.ShapeDtypeStruct((n_ctx + n_lat, D_MODEL), F32),
        compiler_params=_cp(("parallel",)),
        name="entry_ln",
    )(xc, xl, pos)


_Z_CUTS = (0, D_A, 2 * D_A, 2 * D_A + 3 * D_B, 2 * D_A + 3 * D_B + D_C, 2 * D_A + 3 * D_B + 2 * D_C,
           2 * D_A + 3 * D_B + 3 * D_C, D_MAIN)


_K_CUT = 4
_ROW_CUTS = tuple(c for i, c in enumerate(zip(_Z_CUTS[:-1], _Z_CUTS[1:])) if i != _K_CUT)
_NT = (((1,), (1,)), ((), ()))


def _in_kernel(x_ref, mod_ref, wt_ref, b_ref, bkt_ref, wg_ref, bg_ref, *refs):
    out_refs, wt16 = refs[:-1], refs[-1]

    @pl.when(pl.program_id(0) == 0)
    def _():
        wt16[...] = wt_ref[0, :D_MAIN, :].astype(BF16)

    m = mod_ref[0]
    h = x_ref[...] * (1.0 + m[1:2]) + m[0:1]
    hb = h.astype(BF16)
    dg = lambda a, b: lax.dot_general(a, b, _NT, preferred_element_type=F32)
    for ref, (a, b) in zip(out_refs[:-2], _ROW_CUTS):
        ref[...] = dg(hb, wt16[a:b, :]) + b_ref[:, a:b]
    g_ref, kt_ref = out_refs[-2:]
    g_ref[...] = _dot3(h, wg_ref[...]) + bg_ref[...]
    kt_ref[...] = dg(wt16[_Z_CUTS[_K_CUT]:_Z_CUTS[_K_CUT + 1], :], hb) + bkt_ref[...]


def _in_call(layer, x, mod_l, w_in_t, b_main, b_kt, w_gate, b_gate, group_of_tile):
    n = x.shape[0]
    widths = [b - a for a, b in _ROW_CUTS] + [LANE]
    row = lambda i: (i, 0)
    fixed = lambda i: (0, 0)
    return pl.pallas_call(
        _in_kernel,
        grid=(n // TM_IN,),
        in_specs=[pl.BlockSpec((TM_IN, D_MODEL), row),
                  pl.BlockSpec((1, 6, D_MODEL), lambda i: (group_of_tile(i), 0, 0)),
                  pl.BlockSpec((1, w_in_t.shape[1], D_MODEL), lambda i: (layer, 0, 0), pipeline_mode=pl.Buffered(1)),
                  pl.BlockSpec((1, D_MAIN), fixed),
                  pl.BlockSpec((D_C, TM_IN), fixed),
                  pl.BlockSpec((D_MODEL, LANE), fixed), pl.BlockSpec((1, LANE), fixed)],
        out_specs=[pl.BlockSpec((TM_IN, w), row) for w in widths] + [pl.BlockSpec((D_C, TM_IN), lambda i: (0, i))],
        out_shape=[jax.ShapeDtypeStruct((n, w), F32) for w in widths] + [jax.ShapeDtypeStruct((D_C, n), F32)],
        scratch_shapes=[pltpu.VMEM((D_MAIN, D_MODEL), BF16)],
        compiler_params=_cp(("arbitrary",)),
        name="in_proj",
    )(x, mod_l, w_in_t, b_main, b_kt, w_gate, b_gate)


def _lru_variant(L, xa_ref, ya_ref, cw_ref, cb_ref, wg_ref, bg_ref, lam_ref, h0_ref, o_ref, st_ref,
                 s_af, s_bf, s_ab, s_bb):
    nch, nseq, ntile = SEG // CH, SEG // L, L // SUB
    lam = lam_ref[...]
    sp = jnp.maximum(-lam, 0.0) + jnp.log1p(jnp.exp(-jnp.abs(lam)))
    cw = cw_ref[...]
    cb = cb_ref[...]
    row = lax.broadcasted_iota(jnp.int32, (CH, 1), 0)
    sub3 = lax.broadcasted_iota(jnp.int32, (1, SUB, 1), 1)

    def gates_and_tile_scan(c, carry):
        start = pl.multiple_of(c * CH, CH)
        xcat, main = _halo_rows(xa_ref, start, CH)
        tpos = (start + row) & (L - 1)
        xm2 = jnp.where(tpos >= 2, xcat[SUB - 2:SUB - 2 + CH], 0.0)
        xm1 = jnp.where(tpos >= 1, xcat[SUB - 1:SUB - 1 + CH], 0.0)
        xp1 = jnp.where(tpos <= L - 2, xcat[SUB + 1:SUB + 1 + CH], 0.0)
        xc = cw[0:1] * xm2 + cw[1:2] * xm1 + cw[2:3] * main + cw[3:4] * xp1 + cb
        g = _dot(xc.astype(BF16), wg_ref[...]) + bg_ref[...]
        for d, (sa, sb) in enumerate(((s_af, s_bf), (s_ab, s_bb))):
            r = _sigmoid(g[:, 2 * d * D_A:(2 * d + 1) * D_A])
            ig = _sigmoid(g[:, (2 * d + 1) * D_A:(2 * d + 2) * D_A])
            a = jnp.exp(-LRU_C * r * sp[d:d + 1])
            y = 1.0 - a * a
            b = jnp.where(y > 0.0, y * lax.rsqrt(y), 0.0) * (ig * xc)
            a3, b3 = a.reshape(CH // SUB, SUB, D_A), b.reshape(CH // SUB, SUB, D_A)
            for s in (1, 2, 4):
                shift, keep = (s, sub3 >= s) if d == 0 else (SUB - s, sub3 < SUB - s)
                b3 = a3 * jnp.where(keep, pltpu.roll(b3, shift, 1), 0.0) + b3
                a3 = a3 * jnp.where(keep, pltpu.roll(a3, shift, 1), 1.0)
            sa[pl.ds(start, CH), :] = a3.reshape(CH, D_A)
            sb[pl.ds(start, CH), :] = b3.reshape(CH, D_A)
        return carry

    lax.fori_loop(0, nch, gates_and_tile_scan, 0)

    def carry_tiles(k, carry):
        cf, cbk = carry
        nf, nb = [], []
        for s in range(nseq):
            rf = pl.multiple_of(s * L + k * SUB, SUB)
            hf = s_af[pl.ds(rf, SUB), :] * cf[s] + s_bf[pl.ds(rf, SUB), :]
            s_bf[pl.ds(rf, SUB), :] = hf
            nf.append(hf[SUB - 1:SUB, :])
            rb = pl.multiple_of(s * L + (ntile - 1 - k) * SUB, SUB)
            hb = s_ab[pl.ds(rb, SUB), :] * cbk[s] + s_bb[pl.ds(rb, SUB), :]
            s_bb[pl.ds(rb, SUB), :] = hb
            nb.append(hb[0:1, :])
        return tuple(nf), tuple(nb)

    cf0 = tuple(h0_ref[0, s, 0:1, :] for s in range(nseq))
    cb0 = tuple(h0_ref[0, s, 1:2, :] for s in range(nseq))
    cf, cbk = lax.fori_loop(0, ntile, carry_tiles, (cf0, cb0))

    st_ref[...] = jnp.zeros(st_ref.shape, F32)
    for s in range(nseq):
        st_ref[0, s] = jnp.concatenate([cf[s], cbk[s]], axis=0)

    def finish(c, carry):
        start = pl.multiple_of(c * CH, CH)
        h = s_bf[pl.ds(start, CH), :] + s_bb[pl.ds(start, CH), :]
        o_ref[pl.ds(start, CH), :] = _rms(_gelu_tanh(ya_ref[pl.ds(start, CH), :]) * h)
        return carry

    lax.fori_loop(0, nch, finish, 0)


def _lru_kernel(*refs, l_ctx, l_lat, n_ctx_blk):
    i = pl.program_id(0)

    @pl.when(i < n_ctx_blk)
    def _():
        _lru_variant(l_ctx, *refs)

    @pl.when(i >= n_ctx_blk)
    def _():
        _lru_variant(l_lat, *refs)


def _lru_call(xa, ya, conv_w, conv_b, w_gate, b_gate, lam, h0_all, l_ctx, l_lat, n_ctx_blk):
    n = xa.shape[0]
    nblk = n // SEG
    row = lambda i: (i, 0)
    fixed = lambda i: (0, 0)
    slots = SEG // l_ctx
    return pl.pallas_call(
        functools.partial(_lru_kernel, l_ctx=l_ctx, l_lat=l_lat, n_ctx_blk=n_ctx_blk),
        grid=(nblk,),
        in_specs=[pl.BlockSpec((SEG, D_A), row), pl.BlockSpec((SEG, D_A), row),
                  pl.BlockSpec((4, D_A), fixed), pl.BlockSpec((1, D_A), fixed),
                  pl.BlockSpec((D_A, 4 * D_A), fixed), pl.BlockSpec((1, 4 * D_A), fixed),
                  pl.BlockSpec((2, D_A), fixed),
                  pl.BlockSpec((1, slots, 2, D_A), lambda i: (i, 0, 0, 0))],
        out_specs=[pl.BlockSpec((SEG, D_A), row),
                   pl.BlockSpec((1, slots, 2, D_A), lambda i: (i, 0, 0, 0))],
        out_shape=[jax.ShapeDtypeStruct((n, D_A), F32),
                   jax.ShapeDtypeStruct((nblk, slots, 2, D_A), F32)],
        scratch_shapes=[pltpu.VMEM((SEG, D_A), F32) for _ in range(4)],
        compiler_params=_cp(("parallel",)),
        name="rglru",
    )(xa, ya, conv_w, conv_b, w_gate, b_gate, lam, h0_all)


def _filt_kernel(z_ref, dec_ref, w1_ref, b1_ref, w2_ref, b2_ref, fr_ref, w3_ref, fwd_ref,
                 oab_ref, od0_ref, s_k, s_kf, *, L):
    nblk = 2 * L // CH
    d_idx = pl.program_id(1)
    row = lax.broadcasted_iota(jnp.int32, (CH, 1), 0)

    @pl.when(d_idx == 0)
    def _():
        fr = fr_ref[0]

        def taps(c, carry):
            start = pl.multiple_of(c * CH, CH)
            h1 = jnp.sin(fr * (_dot3(z_ref[pl.ds(start, CH), :], w1_ref[0]) + b1_ref[0]))
            h2 = jnp.sin(fr * (_dot3(h1, w2_ref[0]) + b2_ref[0]))
            t = _dot3(h2, w3_ref[0])
            dec = dec_ref[pl.ds(start, CH), :]
            rg = start + row
            for o in range(HY_ORDER):
                fwd_t = t[:, (2 * o) * D_B:(2 * o + 1) * D_B]
                bwd_t = t[:, (2 * o + 1) * D_B:(2 * o + 2) * D_B]
                ko = jnp.where(rg < L, bwd_t, fwd_t) * dec
                s_k[pl.ds(start, CH), o * D_B:(o + 1) * D_B] = jnp.where(rg == 0, 0.0, ko)
            return carry

        lax.fori_loop(0, nblk, taps, 0)
        fwd = fwd_ref[...]

        def spectra(e, carry):
            start = pl.multiple_of(e * CH, CH)
            s_kf[e] = _dot3(fwd, s_k[pl.ds(start, CH), :])
            return carry

        lax.fori_loop(0, nblk, spectra, 0)

    kd = s_kf[d_idx + 1]
    km = s_kf[d_idx]
    k0 = s_k[pl.ds(pl.multiple_of(d_idx * CH, CH), 1), :]
    sgn = jnp.where((row & 1) == 0, 1.0, -1.0)
    a = kd[:CH] + sgn * (km[:CH] - k0)
    b = jnp.where(row == 0, 0.0, kd[CH:] + sgn * km[CH:])
    hn = kd[CH:CH + 1] + km[CH:CH + 1] - k0
    for o in range(HY_ORDER):
        oab_ref[0, o, 0, 0] = a[:, o * D_B:(o + 1) * D_B]
        oab_ref[0, o, 0, 1] = b[:, o * D_B:(o + 1) * D_B]
        od0_ref[0, o, 0] = jnp.broadcast_to(hn[:, o * D_B:(o + 1) * D_B], (SUB, D_B))


def _filt_call(L, z, dec, w1, b1, w2, b2, fr, w3, fwd32):
    nd = 2 * (L // CH) - 1
    fixed = lambda l, d: (0, 0)
    lay3 = lambda l, d: (l, 0, 0)
    return pl.pallas_call(
        functools.partial(_filt_kernel, L=L),
        grid=(DEPTH, nd),
        in_specs=[pl.BlockSpec((2 * L, LANE), fixed), pl.BlockSpec((2 * L, D_B), fixed),
                  pl.BlockSpec((1, LANE, LANE), lay3), pl.BlockSpec((1, 1, LANE), lay3),
                  pl.BlockSpec((1, LANE, LANE), lay3), pl.BlockSpec((1, 1, LANE), lay3),
                  pl.BlockSpec((1, 1, LANE), lay3),
                  pl.BlockSpec((1, LANE, HY_ORDER * 2 * D_B), lay3),
                  pl.BlockSpec((2 * CH, CH), fixed)],
        out_specs=[pl.BlockSpec((1, HY_ORDER, 1, 2, CH, D_B), lambda l, d: (l, 0, d, 0, 0, 0)),
                   pl.BlockSpec((1, HY_ORDER, 1, SUB, D_B), lambda l, d: (l, 0, d, 0, 0))],
        out_shape=[jax.ShapeDtypeStruct((DEPTH, HY_ORDER, nd, 2, CH, D_B), F32),
                   jax.ShapeDtypeStruct((DEPTH, HY_ORDER, nd, SUB, D_B), F32)],
        scratch_shapes=[pltpu.VMEM((2 * L, HY_ORDER * D_B), F32),
                        pltpu.VMEM((2 * L // CH, 2 * CH, HY_ORDER * D_B), F32)],
        compiler_params=_cp(("parallel", "arbitrary")),
        name=f"hyena_filter_{L}",
    )(z, dec, w1, b1, w2, b2, fr, w3, fwd32)


def _hy_variant(L, o_idx, hy_ref, cw_ref, cb_ref, fwd_ref, inv_ref, hab_ref, hd0_ref, bias_ref, o_ref,
                s_y, s_x, s_u, s_v):
    nch, nseq, P = SEG // CH, SEG // L, L // CH
    row = lax.broadcasted_iota(jnp.int32, (CH, 1), 0)
    frow = lax.broadcasted_iota(jnp.int32, (FS, 1), 0)

    @pl.when(o_idx == 0)
    def _():
        cw = cw_ref[...]
        cb = cb_ref[...]

        def short_conv(c, carry):
            start = pl.multiple_of(c * CH, CH)
            xcat, main = _halo_rows(hy_ref, start, CH)
            tpos = (start + row) & (L - 1)
            xm1 = jnp.where(tpos >= 1, xcat[SUB - 1:SUB - 1 + CH], 0.0)
            xp1 = jnp.where(tpos <= L - 2, xcat[SUB + 1:SUB + 1 + CH], 0.0)
            hc = cw[0:1] * xm1 + cw[1:2] * main + cw[2:3] * xp1 + cb
            s_y[pl.ds(start, CH), :] = hc[:, :D_B]
            s_x[0, pl.ds(start, CH), :] = hc[:, D_B:2 * D_B]
            s_x[1, pl.ds(start, CH), :] = hc[:, 2 * D_B:]
            return carry

        lax.fori_loop(0, nch, short_conv, 0)

    bias = bias_ref[0]

    def loop(n, body, init):
        return body(0, init) if n == 1 else lax.fori_loop(0, n, body, init)

    def one_sequence(s, slot):
        base = s * L
        u0 = slot * P

        def forward_dft(j, cc):
            r = pl.multiple_of(base + j * CH, CH)
            s_u[u0 + j] = _dot(fwd_ref[...], s_y[pl.ds(r, CH), :].astype(BF16))
            return cc

        loop(P, forward_dft, 0)

        def output_block(i, cc):
            for fs in range(CH // FS):
                lo = fs * FS

                def accumulate(j, acc):
                    yre, yim = acc
                    d = i - j + (P - 1)
                    ure = s_u[u0 + j, lo:lo + FS, :]
                    uim = s_u[u0 + j, CH + lo:CH + lo + FS, :]
                    a = hab_ref[0, 0, d, 0, lo:lo + FS, :]
                    b = hab_ref[0, 0, d, 1, lo:lo + FS, :]
                    dd = jnp.where(frow == 0, hd0_ref[0, 0, d, 0:1, :], a) if fs == 0 else a
                    return yre + ure * a - uim * b, yim + ure * b + uim * dd

                zero = jnp.zeros((FS, D_B), F32)
                yre, yim = loop(P, accumulate, (zero, zero))
                s_v[slot, lo:lo + FS, :] = yre.astype(BF16)
                s_v[slot, CH + lo:CH + lo + FS, :] = yim.astype(BF16)
            yc = _dot(inv_ref[...], s_v[slot])
            r = pl.multiple_of(base + i * CH, CH)
            s_y[pl.ds(r, CH), :] = s_x[o_idx, pl.ds(r, CH), :] * (yc + s_y[pl.ds(r, CH), :] * bias)
            return cc

        loop(P, output_block, 0)

    def single_block_pair(p, carry):
        rows = [pl.multiple_of((2 * p + k) * L, CH) for k in range(2)]
        ys = [s_y[pl.ds(r, CH), :] for r in rows]
        gates = [s_x[o_idx, pl.ds(r, CH), :] for r in rows]
        a = hab_ref[0, 0, 0, 0]
        b = hab_ref[0, 0, 0, 1]
        dd = jnp.where(row == 0, hd0_ref[0, 0, 0, 0:1, :], a)
        outs = []
        for y, gate in zip(ys, gates):
            u = _dot(fwd_ref[...], y.astype(BF16))
            ure, uim = u[:CH], u[CH:]
            v = jnp.concatenate([ure * a - uim * b, ure * b + uim * dd], axis=0).astype(BF16)
            outs.append(gate * (_dot(inv_ref[...], v) + y * bias))
        for r, out in zip(rows, outs):
            s_y[pl.ds(r, CH), :] = out
        return carry

    if P == 1 and nseq % 2 == 0:
        lax.fori_loop(0, nseq // 2, single_block_pair, 0)
    else:
        lax.fori_loop(0, nseq, lambda s, carry: (one_sequence(s, 0), carry)[1], 0)

    @pl.when(o_idx == HY_ORDER - 1)
    def _():
        def finish(c, carry):
            start = pl.multiple_of(c * CH, CH)
            o_ref[pl.ds(start, CH), :] = _rms(s_y[pl.ds(start, CH), :])
            return carry

        lax.fori_loop(0, nch, finish, 0)


def _hy_kernel(hy_ref, cw_ref, cb_ref, fwd_ref, inv_ref, habc_ref, hd0c_ref, habl_ref, hd0l_ref, bias_ref,
               o_ref, s_y, s_x, s_u, s_v, *, l_ctx, l_lat, n_ctx_blk):
    i = pl.program_id(0)
    o_idx = pl.program_id(1)
    scratch = (s_y, s_x, s_u, s_v)

    @pl.when(i < n_ctx_blk)
    def _():
        _hy_variant(l_ctx, o_idx, hy_ref, cw_ref, cb_ref, fwd_ref, inv_ref, habc_ref, hd0c_ref, bias_ref, o_ref,
                    *scratch)

    @pl.when(i >= n_ctx_blk)
    def _():
        _hy_variant(l_lat, o_idx, hy_ref, cw_ref, cb_ref, fwd_ref, inv_ref, habl_ref, hd0l_ref, bias_ref, o_ref,
                    *scratch)


def _hy_call(layer, hyb, conv_w, conv_b, fwd, inv, habc, hd0c, habl, hd0l, bias, l_ctx, l_lat, n_ctx_blk):
    n = hyb.shape[0]
    ndc, ndl = habc.shape[2], habl.shape[2]
    pmax = max(l_ctx, l_lat) // CH
    row = lambda i, o: (i, 0)
    fixed = lambda i, o: (0, 0)
    lat_o = lambda i, o: jnp.where(i >= n_ctx_blk, o, 0)
    ctx_o = lambda i, o: jnp.where(i < n_ctx_blk, o, 0)
    return pl.pallas_call(
        functools.partial(_hy_kernel, l_ctx=l_ctx, l_lat=l_lat, n_ctx_blk=n_ctx_blk),
        grid=(n // SEG, HY_ORDER),
        in_specs=[pl.BlockSpec((SEG, 3 * D_B), row),
                  pl.BlockSpec((3, 3 * D_B), fixed), pl.BlockSpec((1, 3 * D_B), fixed),
                  pl.BlockSpec((2 * CH, CH), fixed), pl.BlockSpec((CH, 2 * CH), fixed),
                  pl.BlockSpec((1, 1, ndc, 2, CH, D_B), lambda i, o: (layer, ctx_o(i, o), 0, 0, 0, 0)),
                  pl.BlockSpec((1, 1, ndc, SUB, D_B), lambda i, o: (layer, ctx_o(i, o), 0, 0, 0)),
                  pl.BlockSpec((1, 1, ndl, 2, CH, D_B), lambda i, o: (layer, lat_o(i, o), 0, 0, 0, 0)),
                  pl.BlockSpec((1, 1, ndl, SUB, D_B), lambda i, o: (layer, lat_o(i, o), 0, 0, 0)),
                  pl.BlockSpec((1, 1, D_B), lambda i, o: (o, 0, 0))],
        out_specs=pl.BlockSpec((SEG, D_B), row),
        out_shape=jax.ShapeDtypeStruct((n, D_B), F32),
        scratch_shapes=[pltpu.VMEM((SEG, D_B), F32), pltpu.VMEM((HY_ORDER, SEG, D_B), F32),
                        pltpu.VMEM((max(pmax, 2), 2 * CH, D_B), F32), pltpu.VMEM((2, 2 * CH, D_B), BF16)],
        compiler_params=_cp(("parallel", "arbitrary")),
        name="hyena",
    )(hyb, conv_w, conv_b, fwd, inv, habc, hd0c, habl, hd0l, bias)


def _row_scan(x, op, fill, reverse):
    t, width = x.shape
    n_tiles = t // SUB
    sub = lax.broadcasted_iota(jnp.int32, (1, SUB, 1), 1)
    x3 = x.reshape(n_tiles, SUB, width)
    for s in (1, 2, 4):
        shift, keep = (SUB - s, sub < SUB - s) if reverse else (s, sub >= s)
        x3 = op(x3, jnp.where(keep, pltpu.roll(x3, shift, 1), fill))
    x = x3.reshape(t, width)
    out = [None] * n_tiles
    carry = None
    for i in (reversed(range(n_tiles)) if reverse else range(n_tiles)):
        tile = x[i * SUB:(i + 1) * SUB]
        out[i] = tile if carry is None else op(tile, carry)
        carry = out[i][0:1] if reverse else out[i][SUB - 1:SUB]
    return jnp.concatenate(out, axis=0)


_STK_ONE = 3 * SUB


def _mlstm_prep(d, g_ref, m_old):
    T = CH
    reverse = d == 1
    g = g_ref[...]
    if d == 1:
        g = pltpu.roll(g, LANE - 2 * H_C, 1)
    lane = lax.broadcasted_iota(jnp.int32, (1, LANE), 1)
    head = lane < H_C
    b = pltpu.roll(_row_scan(_log_sigmoid(g), jnp.add, 0.0, reverse), LANE - H_C, 1)
    r = jnp.where(head, g - b, 0.0)
    big_m = jnp.maximum(m_old, _row_scan(r, jnp.maximum, -jnp.inf, reverse))
    last = 0 if reverse else T - 1
    m_last = big_m[last:last + 1, :]
    low = lane < SUB
    p0, p1, p2 = (jnp.where(low, p.astype(F32), 0.0) for p in _split3(-big_m))
    cols = (p0 + pltpu.roll(p1, SUB, 1) + pltpu.roll(p2, 2 * SUB, 1)
            + jnp.where(jnp.logical_and(lane >= _STK_ONE, lane < _STK_ONE + SUB), 1.0, 0.0))
    rowid = lax.broadcasted_iota(jnp.int32, (SUB, 1), 0)
    r8 = r.T[0:SUB, :]
    m_last8 = sum(jnp.where(rowid == h, m_last[:, h:h + 1], 0.0) for h in range(H_C))
    ws8 = jnp.where(rowid < H_C, jnp.exp(r8 - m_last8), 0.0)
    return {"r3": [p.astype(F32) for p in _split3(r8)], "ws8": ws8, "wc": jnp.exp(m_old - m_last),
            "m_new": jnp.where(head, b[last:last + 1, :] + m_last, 0.0),
            "wi": jnp.exp(m_old - big_m), "e": jnp.exp(-(b + big_m)), "cols_b": cols.astype(BF16)}


def _mlstm_variant(carry, qf, vf, ktf, gf, qb_, vb_, ktb_, gb_, hf_ref, hb_ref, co_ref, no_ref, mo_ref,
                   s_cx, s_n, s_m):
    T = CH
    ii = lax.broadcasted_iota(jnp.int32, (T, T), 0)
    jj = lax.broadcasted_iota(jnp.int32, (T, T), 1)
    rowid = lax.broadcasted_iota(jnp.int32, (SUB, 1), 0)
    one_col = jnp.where(lax.broadcasted_iota(jnp.int32, (T, DK), 1) == 0, 1.0, 0.0).astype(BF16)
    prep = []
    for d, g_ref in ((0, gf), (1, gb_ if carry else gf)):
        m_old = s_m[d:d + 1, :] if carry else jnp.zeros((1, LANE), F32)
        prep.append(_mlstm_prep(d, g_ref, m_old))
    refs = ((qf, vf, ktf, hf_ref), (qb_, vb_, ktb_, hb_ref))
    for h in range(H_C):
        sl = slice(h * DK, (h + 1) * DK)
        for d in range(2):
            p = prep[d]
            q_ref, v_ref, kt_ref, h_ref = refs[d]
            idx = d * H_C + h
            if carry or d == 0:
                qb = (q_ref[:, sl] * (DK ** -0.5)).astype(BF16)
                kt = kt_ref[sl, :]
                ktb = kt.astype(BF16)
                v_ext = jnp.concatenate([v_ref[:, sl].astype(BF16), one_col], axis=1)
                s_raw = _dot(qb, ktb)
            tri = (jj >= ii) if d == 1 else (jj <= ii)
            sel = jnp.broadcast_to(jnp.where(rowid == h, 1.0, 0.0), (SUB, T))
            rr = sum(jnp.where(rowid == i, piece[h:h + 1, :], 0.0) for i, piece in enumerate(p["r3"]))
            rmat = jnp.concatenate([sel, sel, sel, rr, jnp.zeros((LANE - 4 * SUB, T), F32)], axis=0)
            expo = _dot(p["cols_b"], rmat.astype(BF16))
            s = s_raw * jnp.exp(jnp.where(tri, expo, -jnp.inf))
            intra = _dot(s.astype(BF16), v_ext)
            num, den = intra[:, :DK], intra[:, DK:DK + 1]
            if carry:
                cx = s_cx[idx]
                inter = _dot(qb, cx.astype(BF16))
                wi = p["wi"][:, h:h + 1]
                num, den = num + wi * inter[:, :DK], den + wi * inter[:, DK:DK + 1]
            h_ref[:, sl] = num / jnp.maximum(jnp.abs(den), p["e"][:, h:h + 1])
            upd = _dot((kt * p["ws8"][h:h + 1, :]).astype(BF16), v_ext)
            n_upd = lax.dot_general(p["ws8"].astype(BF16), ktb, _NT, preferred_element_type=F32)[h:h + 1, :]
            if carry:
                wc = p["wc"][:, h:h + 1]
                s_cx[idx] = wc * cx + upd
                s_n[idx:idx + 1, :] = wc * s_n[idx:idx + 1, :] + n_upd
            else:
                co_ref[0, co_ref.shape[1] - 1, idx] = upd[:, :DK]
                no_ref[0, idx:idx + 1, :] = n_upd
    m_rows = jnp.concatenate([prep[0]["m_new"], prep[1]["m_new"], jnp.zeros((SUB - 2, LANE), F32)], axis=0)
    if carry:
        s_m[...] = m_rows
    else:
        mo_ref[0] = m_rows


def _mlstm_kernel(*refs, n_ctx_steps, nc_lat, n_prev):
    cx0_ref, n0_ref, m0_ref = refs[8:11]
    s_cx, s_n, s_m = refs[-3:]
    n_in = 12 if n_prev else 11
    data = refs[:8] + refs[n_in:]
    co_ref = data[10]
    t = pl.program_id(0)
    is_ctx = t < n_ctx_steps

    @pl.when(is_ctx)
    def _():
        if n_prev:
            co_ref[0, 0:n_prev] = refs[11][0]
        _mlstm_variant(False, *data)

    @pl.when(jnp.logical_not(is_ctx))
    def _():
        @pl.when((t - n_ctx_steps) % nc_lat == 0)
        def _():
            s_cx[...] = cx0_ref[0]
            s_n[...] = n0_ref[0]
            s_m[...] = m0_ref[0]

        _mlstm_variant(True, *data)


def _mlstm_call(q, v, kt, gates, cx0, n0, m0, prev_c, n_ctx_steps, nc_lat):
    n = q.shape[0]
    steps = n // CH
    nst = 2 * H_C
    n_prev = 0 if prev_c is None else prev_c.shape[1]

    def bwd_blk(t):
        r = jnp.maximum(t - n_ctx_steps, 0)
        return n_ctx_steps + (r // nc_lat) * nc_lat + (nc_lat - 1 - r % nc_lat)

    out_bwd = lambda t: jnp.where(t < n_ctx_steps, t, bwd_blk(t))
    lat_b = lambda t: jnp.maximum(t - n_ctx_steps, 0) // nc_lat
    ctx_b = lambda t: jnp.minimum(t, n_ctx_steps - 1)
    rows = lambda w, blk: pl.BlockSpec((CH, w), lambda t: (blk(t), 0))
    cols = lambda h, blk: pl.BlockSpec((h, CH), lambda t: (0, blk(t)))
    ident = lambda t: t
    state = lambda layers: pl.BlockSpec((1, layers, nst, DK, DK), lambda t: (ctx_b(t), 0, 0, 0, 0))
    prev_specs, prev_args = ([state(n_prev)], [prev_c]) if n_prev else ([], [])
    return pl.pallas_call(
        functools.partial(_mlstm_kernel, n_ctx_steps=n_ctx_steps, nc_lat=nc_lat, n_prev=n_prev),
        grid=(steps,),
        in_specs=[rows(D_C, ident), rows(D_C, ident), cols(D_C, ident), rows(LANE, ident),
                  rows(D_C, bwd_blk), rows(D_C, bwd_blk), cols(D_C, bwd_blk), rows(LANE, bwd_blk),
                  pl.BlockSpec((1, nst, DK, 2 * DK), lambda t: (lat_b(t), 0, 0, 0)),
                  pl.BlockSpec((1, nst, DK), lambda t: (lat_b(t), 0, 0)),
                  pl.BlockSpec((1, SUB, LANE), lambda t: (lat_b(t), 0, 0))] + prev_specs,
        out_specs=[rows(D_C, ident), rows(D_C, out_bwd),
                   state(n_prev + 1),
                   pl.BlockSpec((1, nst, DK), lambda t: (ctx_b(t), 0, 0)),
                   pl.BlockSpec((1, SUB, LANE), lambda t: (ctx_b(t), 0, 0))],
        out_shape=[jax.ShapeDtypeStruct((n, D_C), F32), jax.ShapeDtypeStruct((n, D_C), F32),
                   jax.ShapeDtypeStruct((n_ctx_steps, n_prev + 1, nst, DK, DK), F32),
                   jax.ShapeDtypeStruct((n_ctx_steps, nst, DK), F32),
                   jax.ShapeDtypeStruct((n_ctx_steps, SUB, LANE), F32)],
        scratch_shapes=[pltpu.VMEM((nst, DK, 2 * DK), F32), pltpu.VMEM((nst, DK), F32),
                        pltpu.VMEM((SUB, LANE), F32)],
        compiler_params=_cp(("arbitrary",)),
        name="mlstm",
    )(q, v, kt, gates, q, v, kt, gates, cx0, n0, m0, *prev_args)


def _out_kernel(x_ref, oa_ref, ob_ref, hf_ref, hb_ref, og_ref, mod_ref, mg_ref, w_ref, g_ref, b_ref,
                rw_ref, rb_ref, x1_ref, he_ref, cnt_ref, it_ref, s_cnt):
    i = pl.program_id(0)

    @pl.when(i == 0)
    def _():
        s_cnt[...] = jnp.zeros(s_cnt.shape, F32)

    m = mod_ref[0]
    mg = mg_ref[...]
    hc = hf_ref[...] + hb_ref[...]
    og = og_ref[...]
    out_c = [_sigmoid(og[:, h * DK:(h + 1) * DK]) * _rms(hc[:, h * DK:(h + 1) * DK]) for h in range(H_C)]
    mix = jnp.concatenate([oa_ref[...], ob_ref[...]] + out_c, axis=1) * mg
    acc = _dot(mix.astype(BF16), w_ref[...])
    x1 = _ln_plain(ALPHA * x_ref[...] + m[2:3] * acc) * g_ref[...] + b_ref[...]
    x1_ref[...] = x1
    h2 = x1 * (1.0 + m[4:5]) + m[3:4]
    he_ref[:, :D_MODEL] = h2

    lg = _dot3(h2, rw_ref[...]) + rb_ref[...]
    col = lax.broadcasted_iota(jnp.int32, lg.shape, 1)
    colf = col.astype(F32)
    first_at = lambda hit: jnp.min(jnp.where(hit, colf, float(LANE)), -1, keepdims=True).astype(jnp.int32)
    ninf = -jnp.inf
    lgm = jnp.where(col < N_GROUPS, lg, ninf)
    mx = jnp.max(lgm, -1, keepdims=True)
    gi = first_at(lgm == mx)
    pg_top = 1.0 / jnp.sum(jnp.where(col < N_GROUPS, jnp.exp(lg - mx), 0.0), -1, keepdims=True)
    lo4 = N_GROUPS + E_PER_GROUP * gi
    lem = jnp.where(jnp.logical_and(col >= lo4, col < lo4 + E_PER_GROUP), lg, ninf)
    v1 = jnp.max(lem, -1, keepdims=True)
    i1 = first_at(lem == v1)
    lem2 = jnp.where(col == i1, ninf, lem)
    v2 = jnp.max(lem2, -1, keepdims=True)
    i2 = first_at(lem2 == v2)
    e21 = jnp.exp(v2 - v1)
    w1 = pg_top / (1.0 + e21)
    w2 = pg_top * e21 / (1.0 + e21)
    e1, e2 = i1 - N_GROUPS, i2 - N_GROUPS
    first_lo = e1 < e2
    elo, ehi = jnp.minimum(e1, e2), jnp.maximum(e1, e2)
    wlo, whi = jnp.where(first_lo, w1, w2), jnp.where(first_lo, w2, w1)
    llo, lhi = elo - E_PER_GROUP * gi, ehi - E_PER_GROUP * gi
    cls = gi * N_PAIRS + ((llo * (7 - llo)) >> 1) + lhi - llo - 1

    oh = jnp.where(col == cls, 1.0, 0.0)
    ii = lax.broadcasted_iota(jnp.int32, (TM_OUT, TM_OUT), 0)
    jj = lax.broadcasted_iota(jnp.int32, (TM_OUT, TM_OUT), 1)
    before = jnp.where(jj < ii, 1.0, 0.0).astype(BF16)
    cnt = s_cnt[0:1, :]
    rank = jnp.sum(oh * (_dot(before, oh.astype(BF16)) + cnt), -1, keepdims=True)
    cnt = cnt + jnp.sum(oh, 0, keepdims=True)
    s_cnt[...] = jnp.broadcast_to(cnt, s_cnt.shape)
    cnt_ref[...] = jnp.broadcast_to(cnt, cnt_ref.shape)

    info = jnp.zeros(lg.shape, F32)
    for c, val in ((I_CLS, cls.astype(F32)), (I_ELO, elo.astype(F32)), (I_EHI, ehi.astype(F32)),
                   (I_WLO, wlo), (I_WHI, whi), (I_RANK, rank)):
        info = jnp.where(col == c, val, info)
    he_ref[:, D_MODEL:] = info
    it_ref[...] = info.T[:SUB, :]


def _out_call(x, out_a, out_b, hcf, hcb, ogate, mod_l, mix_g, w_out, ln_g, ln_b, rt_w, rt_b, group_of_tile):
    n = x.shape[0]
    row = lambda i: (i, 0)
    fixed = lambda i: (0, 0)
    return pl.pallas_call(
        _out_kernel,
        grid=(n // TM_OUT,),
        in_specs=[pl.BlockSpec((TM_OUT, D_MODEL), row), pl.BlockSpec((TM_OUT, D_A), row),
                  pl.BlockSpec((TM_OUT, D_B), row),
                  pl.BlockSpec((TM_OUT, D_C), row), pl.BlockSpec((TM_OUT, D_C), row), pl.BlockSpec((TM_OUT, D_C), row),
                  pl.BlockSpec((1, 6, D_MODEL), lambda i: (group_of_tile(i), 0, 0)),
                  pl.BlockSpec((1, D_MODEL), fixed), pl.BlockSpec((D_MODEL, D_MODEL), fixed),
                  pl.BlockSpec((1, D_MODEL), fixed), pl.BlockSpec((1, D_MODEL), fixed),
                  pl.BlockSpec((D_MODEL, LANE), fixed), pl.BlockSpec((1, LANE), fixed)],
        out_specs=[pl.BlockSpec((TM_OUT, D_MODEL), row), pl.BlockSpec((TM_OUT, D_EXT), row),
                   pl.BlockSpec((SUB, LANE), fixed), pl.BlockSpec((SUB, TM_OUT), lambda i: (0, i))],
        out_shape=[jax.ShapeDtypeStruct((n, D_MODEL), F32), jax.ShapeDtypeStruct((n, D_EXT), F32),
                   jax.ShapeDtypeStruct((SUB, LANE), F32), jax.ShapeDtypeStruct((SUB, n), F32)],
        scratch_shapes=[pltpu.VMEM((SUB, LANE), F32)],
        compiler_params=_cp(("arbitrary",)),
        name="out_proj_router",
    )(x, out_a, out_b, hcf, hcb, ogate, mod_l, mix_g, w_out, ln_g, ln_b, rt_w, rt_b)


def _row_copy(src_ref, src_row, dst_ref, dst_row, sem):
    return pltpu.make_async_copy(src_ref.at[pl.ds(src_row, 1), :], dst_ref.at[pl.ds(dst_row, 1), :], sem)


def _sorted_row(cls_ref, rank_ref, start_ref, t):
    return start_ref[cls_ref[t]] + rank_ref[t]


def _scatter_kernel(cls_ref, rank_ref, start_ref, pad_ref, na_ref, x_ref, o_ref, ring, z_ref, sem_z, load_sems,
                    row_sems):
    n_tiles = o_ref.shape[0] // TM

    def zero_tile(row):
        return pltpu.make_async_copy(z_ref, o_ref.at[pl.ds(pl.multiple_of(row, TM), TM), :], sem_z)

    def for_zero_tiles(fn):
        def per_class(c, carry):
            row = pad_ref[c]

            @pl.when(row >= 0)
            def _():
                fn(zero_tile(row))

            return carry

        lax.fori_loop(0, N_CLASS, per_class, 0)

        def per_idle(t, carry):
            fn(zero_tile(t * TM))
            return carry

        lax.fori_loop(na_ref[0], n_tiles, per_idle, 0)

    i = pl.program_id(0)
    n_steps = x_ref.shape[0] // TM_PERM

    def load(step):
        return pltpu.make_async_copy(x_ref.at[pl.ds(pl.multiple_of(step * TM_PERM, TM_PERM), TM_PERM), :],
                                     ring.at[step % RING], load_sems.at[step % RING])

    def start_rows(step):
        slot = step % RING

        def body(r, carry):
            dst = _sorted_row(cls_ref, rank_ref, start_ref, step * TM_PERM + r)
            _row_copy(ring.at[slot], r, o_ref, dst, row_sems.at[slot]).start()
            return carry

        lax.fori_loop(0, TM_PERM, body, 0, unroll=DMA_UNROLL)

    def wait_rows(step):
        slot = step % RING
        pltpu.make_async_copy(ring.at[slot], o_ref.at[pl.ds(0, TM_PERM), :], row_sems.at[slot]).wait()

    @pl.when(i == 0)
    def _():
        z_ref[...] = jnp.zeros(z_ref.shape, F32)
        for_zero_tiles(lambda cp: cp.start())
        for_zero_tiles(lambda cp: cp.wait())
        load(0).start()

    @pl.when(i >= RING - 1)
    def _():
        wait_rows(i - (RING - 1))

    @pl.when(i + 1 < n_steps)
    def _():
        load(i + 1).start()

    load(i).wait()
    start_rows(i)

    @pl.when(i == n_steps - 1)
    def _():
        for back in range(min(RING - 1, n_steps) - 1, -1, -1):
            wait_rows(i - back)


def _scatter_call(plan, h_ext, n_tiles_max):
    any_spec = pl.BlockSpec(memory_space=pl.ANY)
    return pl.pallas_call(
        _scatter_kernel,
        grid_spec=pltpu.PrefetchScalarGridSpec(
            num_scalar_prefetch=5, grid=(h_ext.shape[0] // TM_PERM,),
            in_specs=[any_spec],
            out_specs=any_spec,
            scratch_shapes=[pltpu.VMEM((RING, TM_PERM, D_EXT), F32), pltpu.VMEM((TM, D_EXT), F32),
                            pltpu.SemaphoreType.DMA(()),
                            pltpu.SemaphoreType.DMA((RING,)), pltpu.SemaphoreType.DMA((RING,))]),
        out_shape=jax.ShapeDtypeStruct((n_tiles_max * TM, D_EXT), F32),
        compiler_params=_cp(("arbitrary",)),
        name="moe_scatter",
    )(plan["cls"], plan["rank"], plan["row_start"], plan["pad_rows"], plan["n_act"], h_ext)


def _moe_kernel(tg_ref, ng_ref, lo_ref, hi_ref, cg_ref, nv_ref, na_ref, x_ref, w1_hbm, w3_hbm, w2_hbm, o_ref,
                w1_ref, w3_ref, w2_ref, s1, s3, s2, wsem, *, layer):
    t = pl.program_id(0)
    active = t < na_ref[0]
    half = TM // 2

    def group_copies(g):
        rows = pl.ds(g * E_PER_GROUP, E_PER_GROUP)
        return [pltpu.make_async_copy(src.at[layer, rows], dst, wsem.at[k])
                for k, (src, dst) in enumerate(((w1_hbm, w1_ref), (w3_hbm, w3_ref), (w2_hbm, w2_ref)))]

    @pl.when(t == 0)
    def _():
        for cp in group_copies(tg_ref[0]):
            cp.start()

    @pl.when(jnp.logical_not(active))
    def _():
        o_ref[...] = jnp.zeros(o_ref.shape, F32)

    @pl.when(jnp.logical_and(active, cg_ref[t] == 1))
    def _():
        for cp in group_copies(tg_ref[t]):
            cp.wait()
        for e in range(E_PER_GROUP):
            s1[e] = w1_ref[e].astype(BF16)
            s3[e] = w3_ref[e].astype(BF16)
            s2[e] = w2_ref[e].astype(BF16)

        @pl.when(ng_ref[t] >= 0)
        def _():
            for cp in group_copies(ng_ref[t]):
                cp.start()

    def run(rows):
        xe = x_ref[0:rows, :]
        x = xe[:, :D_MODEL].astype(BF16)

        def expert(e, gate):
            a = _dot(x, s1[e])
            hm = a * _sigmoid(a) * _dot(x, s3[e]) * gate
            return _dot(hm.astype(BF16), s2[e])

        o_ref[0:rows, :] = (expert(lo_ref[t], xe[:, D_MODEL + I_WLO:D_MODEL + I_WLO + 1])
                            + expert(hi_ref[t], xe[:, D_MODEL + I_WHI:D_MODEL + I_WHI + 1]))

    @pl.when(jnp.logical_and(active, nv_ref[t] > half))
    def _():
        run(TM)

    @pl.when(jnp.logical_and(active, nv_ref[t] <= half))
    def _():
        run(half)
        o_ref[half:, :] = jnp.zeros((TM - half, D_MODEL), F32)


def _moe_call(layer, plan, x_sorted, w1, w3, w2):
    r = x_sorted.shape[0]
    act = lambda t, *scalars: (jnp.minimum(t, scalars[-1][0] - 1), 0)
    any_spec = pl.BlockSpec(memory_space=pl.ANY)
    up, down = (E_PER_GROUP, D_MODEL, D_E), (E_PER_GROUP, D_E, D_MODEL)
    return pl.pallas_call(
        functools.partial(_moe_kernel, layer=layer),
        grid_spec=pltpu.PrefetchScalarGridSpec(
            num_scalar_prefetch=7, grid=(r // TM,),
            in_specs=[pl.BlockSpec((TM, D_EXT), act), any_spec, any_spec, any_spec],
            out_specs=pl.BlockSpec((TM, D_MODEL), lambda t, *_: (t, 0)),
            scratch_shapes=[pltpu.VMEM(up, F32), pltpu.VMEM(up, F32), pltpu.VMEM(down, F32),
                            pltpu.VMEM(up, BF16), pltpu.VMEM(up, BF16), pltpu.VMEM(down, BF16),
                            pltpu.SemaphoreType.DMA((3,))]),
        out_shape=jax.ShapeDtypeStruct((r, D_MODEL), F32),
        compiler_params=_cp(("arbitrary",)),
        name="moe_experts",
    )(plan["tile_grp"], plan["next_grp"], plan["tile_lo"], plan["tile_hi"], plan["chg_grp"], plan["valid"],
      plan["n_act"], x_sorted, w1, w3, w2)


def _ln2_kernel(cls_ref, rank_ref, start_ref, x1_ref, mod_ref, g_ref, b_ref, y_ref, *rest, n_ctx_tiles):
    o_refs, (buf, sems) = rest[:-2], rest[-2:]
    i = pl.program_id(0)
    n_steps = pl.num_programs(0)

    def start_rows(step):
        slot = step % 2

        def body(r, carry):
            src = _sorted_row(cls_ref, rank_ref, start_ref, step * TM_PERM + r)
            _row_copy(y_ref, src, buf.at[slot], r, sems.at[slot]).start()
            return carry

        lax.fori_loop(0, TM_PERM, body, 0, unroll=DMA_UNROLL)

    @pl.when(i == 0)
    def _():
        start_rows(0)

    @pl.when(i + 1 < n_steps)
    def _():
        start_rows(i + 1)

    slot = i % 2
    pltpu.make_async_copy(y_ref.at[pl.ds(0, TM_PERM), :], buf.at[slot], sems.at[slot]).wait()
    m = mod_ref[0]
    y = _ln_plain(ALPHA * x1_ref[...] + m[5:6] * buf[slot]) * g_ref[...] + b_ref[...]
    if len(o_refs) == 1:
        o_refs[0][...] = y
    else:
        @pl.when(i < n_ctx_tiles)
        def _():
            o_refs[0][...] = y

        @pl.when(i >= n_ctx_tiles)
        def _():
            o_refs[1][...] = y


def _ln2_call(plan, x1, y_sorted, mod_l, ln_g, ln_b, group_of_tile, n_ctx_tiles, split):
    n = x1.shape[0]
    row = lambda i, *_: (i, 0)
    fixed = lambda i, *_: (0, 0)
    if split:
        n_ctx = n_ctx_tiles * TM_PERM
        out_specs = [pl.BlockSpec((TM_PERM, D_MODEL), lambda i, *_: (jnp.minimum(i, n_ctx_tiles - 1), 0)),
                     pl.BlockSpec((TM_PERM, D_MODEL), lambda i, *_: (jnp.maximum(i - n_ctx_tiles, 0), 0))]
        out_shape = [jax.ShapeDtypeStruct((n_ctx, D_MODEL), F32), jax.ShapeDtypeStruct((n - n_ctx, D_MODEL), F32)]
    else:
        out_specs = [pl.BlockSpec((TM_PERM, D_MODEL), row)]
        out_shape = [jax.ShapeDtypeStruct((n, D_MODEL), F32)]
    return pl.pallas_call(
        functools.partial(_ln2_kernel, n_ctx_tiles=n_ctx_tiles),
        grid_spec=pltpu.PrefetchScalarGridSpec(
            num_scalar_prefetch=3, grid=(n // TM_PERM,),
            in_specs=[pl.BlockSpec((TM_PERM, D_MODEL), row),
                      pl.BlockSpec((1, 6, D_MODEL), lambda i, *_: (group_of_tile(i), 0, 0)),
                      pl.BlockSpec((1, D_MODEL), fixed), pl.BlockSpec((1, D_MODEL), fixed),
                      pl.BlockSpec(memory_space=pl.ANY)],
            out_specs=out_specs,
            scratch_shapes=[pltpu.VMEM((2, TM_PERM, D_MODEL), F32), pltpu.SemaphoreType.DMA((2,))]),
        out_shape=out_shape,
        compiler_params=_cp(("arbitrary",)),
        name="moe_gather_ln2",
    )(plan["cls"], plan["rank"], plan["row_start"], x1, mod_l, ln_g, ln_b, y_sorted)


def _dft_matrices():
    n2 = 2 * CH
    f = np.arange(CH, dtype=np.float64)[:, None]
    t = np.arange(CH, dtype=np.float64)[None, :]
    ang = 2.0 * np.pi * f * t / n2
    re, im = np.cos(ang), -np.sin(ang)
    im[0, :] = np.cos(np.pi * t[0])
    fwd = np.concatenate([re, im], axis=0)
    scale = np.full((CH, 1), 2.0 / n2)
    scale[0, 0] = 1.0 / n2
    inv = np.concatenate([(re * scale).T, (im * scale).T], axis=1)
    return fwd.astype(np.float32), inv.astype(np.float32)


def _filter_features(L):
    lag = np.arange(-L, L)
    m = np.minimum(np.abs(lag), L - 1)
    t = (np.arange(L, dtype=np.float32) / np.float32(max(L - 1, 1)))[m]
    w = (np.float32(2.0 * math.pi) * np.arange(L, dtype=np.float32) / np.float32(L))[m]
    bands = np.linspace(1e-4, HY_BANDS - 1, HY_BANDS, dtype=np.float32)
    z = np.zeros((2 * L, LANE), np.float32)
    z[:, 0] = t
    z[:, 1:1 + HY_BANDS] = np.cos(w[:, None] * bands)
    z[:, 1 + HY_BANDS:HY_EMB] = -np.sin(w[:, None] * bands)
    lo, hi = math.log(HY_DECAY_TARGET) / 1.5, math.log(HY_DECAY_TARGET) / 0.3
    deltas = np.abs(np.linspace(lo, hi, D_B, dtype=np.float32))
    dec = np.exp(-t[:, None] * deltas)
    return z, dec.astype(np.float32)


def _sincos_2d(rows, cols):
    quarter = D_MODEL // 4
    omega = 1.0 / (10000.0 ** (jnp.arange(quarter, dtype=F32) / quarter))

    def emb(n):
        ang = jnp.arange(n, dtype=F32)[:, None] * omega[None]
        return jnp.concatenate([jnp.sin(ang), jnp.cos(ang)], -1)

    er, ec = emb(rows), emb(cols)
    half = D_MODEL // 2
    pos = jnp.concatenate([jnp.broadcast_to(er[:, None], (rows, cols, half)),
                           jnp.broadcast_to(ec[None], (rows, cols, half))], -1)
    return pos.reshape(rows * cols, D_MODEL)


def _pad_to(x, shape):
    return jnp.pad(x, [(0, s - d) for d, s in zip(x.shape, shape)])


def _block_diag(w):
    eye = jnp.eye(H_A, dtype=w.dtype)
    return jnp.einsum("hij,hg->higj", w, eye).reshape(D_A, D_A)


_PAIR_LO = np.array([0, 0, 0, 1, 1, 2], np.int32)
_PAIR_HI = np.array([1, 2, 3, 2, 3, 3], np.int32)


def _routing_plan(info_t, counts, n_tiles_max):
    cnt = counts[0, :N_CLASS].astype(jnp.int32)
    tiles = (cnt + TM - 1) // TM
    tile_end = jnp.cumsum(tiles)
    n_act = tile_end[-1]
    t = jnp.minimum(jnp.arange(n_tiles_max, dtype=jnp.int32), n_act - 1)
    tcls = jnp.minimum(jnp.sum((tile_end[None, :] <= t[:, None]).astype(jnp.int32), 1), N_CLASS - 1)
    grp, pair = (tcls // N_PAIRS).astype(jnp.int32), tcls % N_PAIRS
    valid = jnp.clip(cnt[tcls] - (t - (tile_end - tiles)[tcls]) * TM, 0, TM).astype(jnp.int32)
    first = jnp.ones((1,), jnp.int32)
    changed = lambda e: jnp.concatenate([first, (e[1:] != e[:-1]).astype(jnp.int32)])
    chg_grp = changed(grp)
    idx = jnp.arange(n_tiles_max, dtype=jnp.int32)
    later_change = jnp.logical_and(idx[None, :] > idx[:, None], chg_grp[None, :] == 1)
    next_pos = jnp.min(jnp.where(later_change, idx[None, :], n_tiles_max), axis=1)
    next_grp = jnp.where(next_pos < n_tiles_max, grp[jnp.minimum(next_pos, n_tiles_max - 1)], -1).astype(jnp.int32)
    return {"cls": info_t[I_CLS].astype(jnp.int32), "rank": info_t[I_RANK].astype(jnp.int32),
            "row_start": ((tile_end - tiles) * TM).astype(jnp.int32), "tile_grp": grp, "chg_grp": chg_grp, "next_grp": next_grp,
            "tile_lo": jnp.asarray(_PAIR_LO)[pair], "tile_hi": jnp.asarray(_PAIR_HI)[pair], "valid": valid,
            "n_act": n_act.reshape(1).astype(jnp.int32),
            "pad_rows": jnp.where(tiles > 0, (tile_end - 1) * TM, -1).astype(jnp.int32)}


def kernel(x_prompt, x_sample, c, state_lru, state_mlstm_C, state_mlstm_n, state_mlstm_m, c_ctx, w_ada, b_ada, w_in, b_in, conv_a_w, conv_a_b, lru_wa, lru_ba, lru_wx, lru_bx, lru_lam, conv_b_w, conv_b_b, hy_w1, hy_b1, hy_w2, hy_b2, hy_freq, hy_w3, hy_bias, mix_g, w_out, ln1_g, ln1_b, rt_wg, rt_bg, rt_we, rt_be, moe_w1, moe_w3, moe_w2, ln2_g, ln2_b):
    B, l_ctx, D = x_prompt.shape
    b_lat, l_lat, _ = x_sample.shape
    n_ctx, n_lat = B * l_ctx, b_lat * l_lat
    n = n_ctx + n_lat
    assert D == D_MODEL and w_in.shape[-1] == D_MAIN + N_GATE
    assert SEG % l_ctx == 0 and l_lat == SEG and l_ctx % CH == 0 and n_ctx % SEG == 0
    assert l_ctx == CH, "the mLSTM step schedule assumes one chunk per context sequence"
    assert 1 + b_lat <= SUB
    n_ctx_blk = n_ctx // SEG
    nc_lat = l_lat // CH

    def group_of(tile_rows):
        first_lat, per_seq = n_ctx // tile_rows, l_lat // tile_rows
        return lambda i: jnp.where(i < first_lat, 0, 1 + (i - first_lat) // per_seq)

    cond = jnp.concatenate([c_ctx[None], c, jnp.zeros((SUB - 1 - b_lat, D), F32)], 0)
    mod = _mod_call(cond, w_ada, b_ada).reshape(DEPTH, SUB, 6, D)
    pos = _sincos_2d(l_lat // GRID_W, GRID_W)
    x = _entry_call(x_prompt.reshape(n_ctx, D), x_sample.reshape(n_lat, D), pos)

    fwd_np, inv_np = _dft_matrices()
    fwd32 = jnp.asarray(fwd_np)
    fwd16, inv16 = fwd32.astype(BF16), jnp.asarray(inv_np).astype(BF16)
    fw1 = _pad_to(hy_w1, (DEPTH, LANE, LANE))
    fb1 = _pad_to(hy_b1[:, None, :], (DEPTH, 1, LANE))
    fw2 = _pad_to(hy_w2, (DEPTH, LANE, LANE))
    fb2 = _pad_to(hy_b2[:, None, :], (DEPTH, 1, LANE))
    ffr = _pad_to(hy_freq[:, None, :], (DEPTH, 1, LANE))
    fw3 = _pad_to(hy_w3, (DEPTH, LANE, HY_ORDER * 2 * D_B))
    spectra = {}
    for L in (l_ctx, l_lat):
        z_np, dec_np = _filter_features(L)
        spectra[L] = _filt_call(L, jnp.asarray(z_np), jnp.asarray(dec_np), fw1, fb1, fw2, fb2, ffr, fw3, fwd32)

    w_in_t = jnp.swapaxes(w_in, 1, 2)
    lat_slots = SEG // l_ctx
    st_lru, st_n, st_m = [], [], []
    c_all = None
    for l in range(DEPTH):
        b_main = b_in[l, None, :D_MAIN]
        k_lo, k_hi = _Z_CUTS[_K_CUT], _Z_CUTS[_K_CUT + 1]
        b_kt = jnp.broadcast_to(b_in[l, k_lo:k_hi, None], (D_C, TM_IN))
        w_gate = _pad_to(w_in[l, :, D_MAIN:], (D, LANE))
        b_gate = _pad_to(b_in[l, None, D_MAIN:], (1, LANE))
        xa, ya, hyb, q, v, og, gates, kt = _in_call(l, x, mod[l], w_in_t, b_main, b_kt, w_gate, b_gate,
                                                    group_of(TM_IN))

        lru_w = jnp.concatenate([_block_diag(lru_wa[l, 0]), _block_diag(lru_wx[l, 0]),
                                 _block_diag(lru_wa[l, 1]), _block_diag(lru_wx[l, 1])], 1).astype(BF16)
        lru_b = jnp.concatenate([lru_ba[l, 0], lru_bx[l, 0], lru_ba[l, 1], lru_bx[l, 1]])[None]
        h0_lat = _pad_to(state_lru[:, l][:, None], (b_lat, lat_slots, 2, D_A))
        h0_all = jnp.concatenate([jnp.zeros((n_ctx_blk, lat_slots, 2, D_A), F32), h0_lat], 0)
        out_a, lru_last = _lru_call(xa, ya, conv_a_w[l], conv_a_b[l, None], lru_w, lru_b, lru_lam[l], h0_all,
                                    l_ctx, l_lat, n_ctx_blk)

        habc, hd0c = spectra[l_ctx]
        habl, hd0l = spectra[l_lat]
        out_b = _hy_call(l, hyb, conv_b_w[l], conv_b_b[l, None], fwd16, inv16, habc, hd0c, habl, hd0l,
                         hy_bias[l][:, None, :], l_ctx, l_lat, n_ctx_blk)

        n0 = state_mlstm_n[:, l].reshape(b_lat, 2 * H_C, DK)
        cx0 = jnp.concatenate([state_mlstm_C[:, l].reshape(b_lat, 2 * H_C, DK, DK), n0[..., None],
                               jnp.zeros((b_lat, 2 * H_C, DK, DK - 1), F32)], -1)
        m0 = _pad_to(state_mlstm_m[:, l], (b_lat, SUB, LANE))
        hcf, hcb, c_all, n_fin, m_fin = _mlstm_call(q, v, kt, gates, cx0, n0, m0, c_all, n_ctx // CH, nc_lat)

        rt_w = _pad_to(jnp.concatenate([rt_wg[l], rt_we[l]], 1), (D, LANE))
        rt_b = _pad_to(jnp.concatenate([rt_bg[l], rt_be[l]])[None], (1, LANE))
        x1, h_ext, counts, info_t = _out_call(x, out_a, out_b, hcf, hcb, og, mod[l], mix_g[l, None],
                                              w_out[l].astype(BF16), ln1_g[l, None], ln1_b[l, None], rt_w, rt_b,
                                              group_of(TM_OUT))

        n_tiles_max = n // TM + N_CLASS
        plan = _routing_plan(info_t, counts, n_tiles_max)
        x_sorted = _scatter_call(plan, h_ext, n_tiles_max)
        y_sorted = _moe_call(l, plan, x_sorted, moe_w1, moe_w3, moe_w2)
        outs = _ln2_call(plan, x1, y_sorted, mod[l], ln2_g[l, None], ln2_b[l, None], group_of(TM_PERM),
                         n_ctx // TM_PERM, split=(l == DEPTH - 1))
        x = outs[0]

        st_lru.append(lru_last[:n_ctx_blk].reshape(B, 2, D_A))
        st_n.append(n_fin.reshape(B, 2, H_C, DK))
        st_m.append(m_fin[:, :2, :H_C])

    return (outs[0].reshape(B, l_ctx, D), outs[1].reshape(b_lat, l_lat, D),
            jnp.stack(st_lru, 1), c_all.reshape(B, DEPTH, 2, H_C, DK, DK), jnp.stack(st_n, 1), jnp.stack(st_m, 1))
```

```python
import functools
import math

import numpy as np
import jax
import jax.numpy as jnp
from jax import lax
from jax.experimental import pallas as pl
from jax.experimental.pallas import tpu as pltpu

F32 = jnp.float32
BF16 = jnp.bfloat16

D_MODEL = 1024
DEPTH = 2
GRID_W = 64
D_A = 256
H_A = 4
BA = D_A // H_A
LRU_C = 8.0
D_B = 256
HY_ORDER = 2
HY_BANDS = 16
HY_EMB = 1 + 2 * HY_BANDS
HY_FH = 64
HY_DECAY_TARGET = 1e-2
D_C = 512
H_C = 4
DK = D_C // H_C
N_GROUPS = 4
E_PER_GROUP = 4
N_EXP = N_GROUPS * E_PER_GROUP
N_PAIRS = 6
N_CLASS = N_GROUPS * N_PAIRS
D_E = 512
ALPHA = (2 * DEPTH) ** 0.25
EPS = 1e-6
D_MAIN = 2 * D_A + 3 * D_B + 4 * D_C
N_GATE = 4 * H_C

LANE = 128
SUB = 8
VMEM_LIMIT = 56 * 1024 * 1024

CH = 256
SEG = 2048
TM = 256
TM_IN = 512
TM_OUT = 512
TM_PERM = 512
D_EXT = D_MODEL + LANE
FS = 64
DMA_UNROLL = 8
RING = 3

I_CLS, I_ELO, I_EHI, I_WLO, I_WHI, I_RANK = range(6)


def _cp(sem, vmem=VMEM_LIMIT):
    return pltpu.CompilerParams(dimension_semantics=sem, vmem_limit_bytes=vmem)


def _dot(a, b):
    return jnp.dot(a, b, preferred_element_type=F32)


def _split2(x):
    hi = x.astype(BF16)
    lo = (x - hi.astype(F32)).astype(BF16)
    return hi, lo


def _dot3(a, b):
    ah, al = _split2(a)
    bh, bl = _split2(b)
    return _dot(ah, bh) + (_dot(ah, bl) + _dot(al, bh))


def _split3(x):
    hi = x.astype(BF16)
    r1 = x - hi.astype(F32)
    mid = r1.astype(BF16)
    lo = (r1 - mid.astype(F32)).astype(BF16)
    return hi, mid, lo


def _sigmoid(x):
    return 1.0 / (1.0 + jnp.exp(-x))


def _log_sigmoid(x):
    return jnp.minimum(x, 0.0) - jnp.log1p(jnp.exp(-jnp.abs(x)))


def _gelu_tanh(x):
    return 0.5 * x * (1.0 + jnp.tanh(math.sqrt(2.0 / math.pi) * (x + 0.044715 * (x * x * x))))


def _ln_plain(x):
    mu = jnp.mean(x, -1, keepdims=True)
    xc = x - mu
    var = jnp.mean(xc * xc, -1, keepdims=True)
    return xc * lax.rsqrt(var + EPS)


def _rms(x):
    return x * lax.rsqrt(jnp.mean(x * x, -1, keepdims=True) + EPS)


def _halo_rows(ref, start, rows):
    total = ref.shape[0]
    prev = ref[pl.ds(pl.multiple_of(jnp.maximum(start - SUB, 0), SUB), SUB), :]
    main = ref[pl.ds(start, rows), :]
    nxt = ref[pl.ds(pl.multiple_of(jnp.minimum(start + rows, total - SUB), SUB), SUB), :]
    return jnp.concatenate([prev, main, nxt], axis=0), main


def _mod_kernel(c_ref, w_ref, b_ref, o_ref):
    c = c_ref[...]
    o_ref[0] = _dot3(c * _sigmoid(c), w_ref[0]) + b_ref[0]


def _mod_call(cond, w_ada, b_ada):
    tn = 1536
    n6 = w_ada.shape[-1]
    return pl.pallas_call(
        _mod_kernel,
        grid=(DEPTH, n6 // tn),
        in_specs=[pl.BlockSpec((SUB, D_MODEL), lambda l, j: (0, 0)),
                  pl.BlockSpec((1, D_MODEL, tn), lambda l, j: (l, 0, j)),
                  pl.BlockSpec((1, 1, tn), lambda l, j: (l, 0, j))],
        out_specs=pl.BlockSpec((1, SUB, tn), lambda l, j: (l, 0, j)),
        out_shape=jax.ShapeDtypeStruct((DEPTH, SUB, n6), F32),
        compiler_params=_cp(("parallel", "parallel")),
        name="adaln_mod",
    )(cond, w_ada, b_ada.reshape(DEPTH, 1, n6))


def _entry_kernel(xc_ref, xl_ref, pos_ref, o_ref, *, n_ctx_tiles):
    i = pl.program_id(0)

    @pl.when(i < n_ctx_tiles)
    def _():
        o_ref[...] = _ln_plain(xc_ref[...])

    @pl.when(i >= n_ctx_tiles)
    def _():
        o_ref[...] = _ln_plain(xl_ref[...] + pos_ref[...])


def _entry_call(xc, xl, pos):
    tm = 512
    n_ctx, n_lat, l_lat = xc.shape[0], xl.shape[0], pos.shape[0]
    nct = n_ctx // tm
    per_seq = l_lat // tm
    return pl.pallas_call(
        functools.partial(_entry_kernel, n_ctx_tiles=nct),
        grid=((n_ctx + n_lat) // tm,),
        in_specs=[pl.BlockSpec((tm, D_MODEL), lambda i: (jnp.minimum(i, nct - 1), 0)),
                  pl.BlockSpec((tm, D_MODEL), lambda i: (jnp.maximum(i - nct, 0), 0)),
                  pl.BlockSpec((tm, D_MODEL), lambda i: (jnp.maximum(i - nct, 0) % per_seq, 0))],
        out_specs=pl.BlockSpec((tm, D_MODEL), lambda i: (i, 0)),
        out_shape=jax.ShapeDtypeStruct((n_ctx + n_lat, D_MODEL), F32),
        compiler_params=_cp(("parallel",)),
        name="entry_ln",
    )(xc, xl, pos)


_Z_CUTS = (0, D_A, 2 * D_A, 2 * D_A + 3 * D_B, 2 * D_A + 3 * D_B + D_C, 2 * D_A + 3 * D_B + 2 * D_C,
           2 * D_A + 3 * D_B + 3 * D_C, D_MAIN)


_K_CUT = 4
_ROW_CUTS = tuple(c for i, c in enumerate(zip(_Z_CUTS[:-1], _Z_CUTS[1:])) if i != _K_CUT)
_NT = (((1,), (1,)), ((), ()))


def _in_kernel(x_ref, mod_ref, wt_ref, b_ref, bkt_ref, wg_ref, bg_ref, *refs):
    out_refs, wt16 = refs[:-1], refs[-1]

    @pl.when(pl.program_id(0) == 0)
    def _():
        wt16[...] = wt_ref[0, :D_MAIN, :].astype(BF16)

    m = mod_ref[0]
    h = x_ref[...] * (1.0 + m[1:2]) + m[0:1]
    hb = h.astype(BF16)
    dg = lambda a, b: lax.dot_general(a, b, _NT, preferred_element_type=F32)
    for ref, (a, b) in zip(out_refs[:-2], _ROW_CUTS):
        ref[...] = dg(hb, wt16[a:b, :]) + b_ref[:, a:b]
    g_ref, kt_ref = out_refs[-2:]
    g_ref[...] = _dot3(h, wg_ref[...]) + bg_ref[...]
    kt_ref[...] = dg(wt16[_Z_CUTS[_K_CUT]:_Z_CUTS[_K_CUT + 1], :], hb) + bkt_ref[...]


def _in_call(layer, x, mod_l, w_in_t, b_main, b_kt, w_gate, b_gate, group_of_tile):
    n = x.shape[0]
    widths = [b - a for a, b in _ROW_CUTS] + [LANE]
    row = lambda i: (i, 0)
    fixed = lambda i: (0, 0)
    return pl.pallas_call(
        _in_kernel,
        grid=(n // TM_IN,),
        in_specs=[pl.BlockSpec((TM_IN, D_MODEL), row),
                  pl.BlockSpec((1, 6, D_MODEL), lambda i: (group_of_tile(i), 0, 0)),
                  pl.BlockSpec((1, w_in_t.shape[1], D_MODEL), lambda i: (layer, 0, 0), pipeline_mode=pl.Buffered(1)),
                  pl.BlockSpec((1, D_MAIN), fixed),
                  pl.BlockSpec((D_C, TM_IN), fixed),
                  pl.BlockSpec((D_MODEL, LANE), fixed), pl.BlockSpec((1, LANE), fixed)],
        out_specs=[pl.BlockSpec((TM_IN, w), row) for w in widths] + [pl.BlockSpec((D_C, TM_IN), lambda i: (0, i))],
        out_shape=[jax.ShapeDtypeStruct((n, w), F32) for w in widths] + [jax.ShapeDtypeStruct((D_C, n), F32)],
        scratch_shapes=[pltpu.VMEM((D_MAIN, D_MODEL), BF16)],
        compiler_params=_cp(("arbitrary",)),
        name="in_proj",
    )(x, mod_l, w_in_t, b_main, b_kt, w_gate, b_gate)


def _lru_variant(L, xa_ref, ya_ref, cw_ref, cb_ref, wg_ref, bg_ref, lam_ref, h0_ref, o_ref, st_ref,
                 s_af, s_bf, s_ab, s_bb):
    nch, nseq, ntile = SEG // CH, SEG // L, L // SUB
    lam = lam_ref[...]
    sp = jnp.maximum(-lam, 0.0) + jnp.log1p(jnp.exp(-jnp.abs(lam)))
    cw = cw_ref[...]
    cb = cb_ref[...]
    row = lax.broadcasted_iota(jnp.int32, (CH, 1), 0)
    sub3 = lax.broadcasted_iota(jnp.int32, (1, SUB, 1), 1)

    def gates_and_tile_scan(c, carry):
        start = pl.multiple_of(c * CH, CH)
        xcat, main = _halo_rows(xa_ref, start, CH)
        tpos = (start + row) & (L - 1)
        xm2 = jnp.where(tpos >= 2, xcat[SUB - 2:SUB - 2 + CH], 0.0)
        xm1 = jnp.where(tpos >= 1, xcat[SUB - 1:SUB - 1 + CH], 0.0)
        xp1 = jnp.where(tpos <= L - 2, xcat[SUB + 1:SUB + 1 + CH], 0.0)
        xc = cw[0:1] * xm2 + cw[1:2] * xm1 + cw[2:3] * main + cw[3:4] * xp1 + cb
        g = _dot(xc.astype(BF16), wg_ref[...]) + bg_ref[...]
        for d, (sa, sb) in enumerate(((s_af, s_bf), (s_ab, s_bb))):
            r = _sigmoid(g[:, 2 * d * D_A:(2 * d + 1) * D_A])
            ig = _sigmoid(g[:, (2 * d + 1) * D_A:(2 * d + 2) * D_A])
            a = jnp.exp(-LRU_C * r * sp[d:d + 1])
            y = 1.0 - a * a
            b = jnp.where(y > 0.0, y * lax.rsqrt(y), 0.0) * (ig * xc)
            a3, b3 = a.reshape(CH // SUB, SUB, D_A), b.reshape(CH // SUB, SUB, D_A)
            for s in (1, 2, 4):
                shift, keep = (s, sub3 >= s) if d == 0 else (SUB - s, sub3 < SUB - s)
                b3 = a3 * jnp.where(keep, pltpu.roll(b3, shift, 1), 0.0) + b3
                a3 = a3 * jnp.where(keep, pltpu.roll(a3, shift, 1), 1.0)
            sa[pl.ds(start, CH), :] = a3.reshape(CH, D_A)
            sb[pl.ds(start, CH), :] = b3.reshape(CH, D_A)
        return carry

    lax.fori_loop(0, nch, gates_and_tile_scan, 0)

    def carry_tiles(k, carry):
        cf, cbk = carry
        nf, nb = [], []
        for s in range(nseq):
            rf = pl.multiple_of(s * L + k * SUB, SUB)
            hf = s_af[pl.ds(rf, SUB), :] * cf[s] + s_bf[pl.ds(rf, SUB), :]
            s_bf[pl.ds(rf, SUB), :] = hf
            nf.append(hf[SUB - 1:SUB, :])
            rb = pl.multiple_of(s * L + (ntile - 1 - k) * SUB, SUB)
            hb = s_ab[pl.ds(rb, SUB), :] * cbk[s] + s_bb[pl.ds(rb, SUB), :]
            s_bb[pl.ds(rb, SUB), :] = hb
            nb.append(hb[0:1, :])
        return tuple(nf), tuple(nb)

    cf0 = tuple(h0_ref[0, s, 0:1, :] for s in range(nseq))
    cb0 = tuple(h0_ref[0, s, 1:2, :] for s in range(nseq))
    cf, cbk = lax.fori_loop(0, ntile, carry_tiles, (cf0, cb0))

    st_ref[...] = jnp.zeros(st_ref.shape, F32)
    for s in range(nseq):
        st_ref[0, s] = jnp.concatenate([cf[s], cbk[s]], axis=0)

    def finish(c, carry):
        start = pl.multiple_of(c * CH, CH)
        h = s_bf[pl.ds(start, CH), :] + s_bb[pl.ds(start, CH), :]
        o_ref[pl.ds(start, CH), :] = _rms(_gelu_tanh(ya_ref[pl.ds(start, CH), :]) * h)
        return carry

    lax.fori_loop(0, nch, finish, 0)


def _lru_kernel(*refs, l_ctx, l_lat, n_ctx_blk):
    i = pl.program_id(0)

    @pl.when(i < n_ctx_blk)
    def _():
        _lru_variant(l_ctx, *refs)

    @pl.when(i >= n_ctx_blk)
    def _():
        _lru_variant(l_lat, *refs)


def _lru_call(xa, ya, conv_w, conv_b, w_gate, b_gate, lam, h0_all, l_ctx, l_lat, n_ctx_blk):
    n = xa.shape[0]
    nblk = n // SEG
    row = lambda i: (i, 0)
    fixed = lambda i: (0, 0)
    slots = SEG // l_ctx
    return pl.pallas_call(
        functools.partial(_lru_kernel, l_ctx=l_ctx, l_lat=l_lat, n_ctx_blk=n_ctx_blk),
        grid=(nblk,),
        in_specs=[pl.BlockSpec((SEG, D_A), row), pl.BlockSpec((SEG, D_A), row),
                  pl.BlockSpec((4, D_A), fixed), pl.BlockSpec((1, D_A), fixed),
                  pl.BlockSpec((D_A, 4 * D_A), fixed), pl.BlockSpec((1, 4 * D_A), fixed),
                  pl.BlockSpec((2, D_A), fixed),
                  pl.BlockSpec((1, slots, 2, D_A), lambda i: (i, 0, 0, 0))],
        out_specs=[pl.BlockSpec((SEG, D_A), row),
                   pl.BlockSpec((1, slots, 2, D_A), lambda i: (i, 0, 0, 0))],
        out_shape=[jax.ShapeDtypeStruct((n, D_A), F32),
                   jax.ShapeDtypeStruct((nblk, slots, 2, D_A), F32)],
        scratch_shapes=[pltpu.VMEM((SEG, D_A), F32) for _ in range(4)],
        compiler_params=_cp(("parallel",)),
        name="rglru",
    )(xa, ya, conv_w, conv_b, w_gate, b_gate, lam, h0_all)


def _filt_kernel(z_ref, dec_ref, w1_ref, b1_ref, w2_ref, b2_ref, fr_ref, w3_ref, fwd_ref,
                 oab_ref, od0_ref, s_k, s_kf, *, L):
    nblk = 2 * L // CH
    d_idx = pl.program_id(1)
    row = lax.broadcasted_iota(jnp.int32, (CH, 1), 0)

    @pl.when(d_idx == 0)
    def _():
        fr = fr_ref[0]

        def taps(c, carry):
            start = pl.multiple_of(c * CH, CH)
            h1 = jnp.sin(fr * (_dot3(z_ref[pl.ds(start, CH), :], w1_ref[0]) + b1_ref[0]))
            h2 = jnp.sin(fr * (_dot3(h1, w2_ref[0]) + b2_ref[0]))
            t = _dot3(h2, w3_ref[0])
            dec = dec_ref[pl.ds(start, CH), :]
            rg = start + row
            for o in range(HY_ORDER):
                fwd_t = t[:, (2 * o) * D_B:(2 * o + 1) * D_B]
                bwd_t = t[:, (2 * o + 1) * D_B:(2 * o + 2) * D_B]
                ko = jnp.where(rg < L, bwd_t, fwd_t) * dec
                s_k[pl.ds(start, CH), o * D_B:(o + 1) * D_B] = jnp.where(rg == 0, 0.0, ko)
            return carry

        lax.fori_loop(0, nblk, taps, 0)
        fwd = fwd_ref[...]

        def spectra(e, carry):
            start = pl.multiple_of(e * CH, CH)
            s_kf[e] = _dot3(fwd, s_k[pl.ds(start, CH), :])
            return carry

        lax.fori_loop(0, nblk, spectra, 0)

    kd = s_kf[d_idx + 1]
    km = s_kf[d_idx]
    k0 = s_k[pl.ds(pl.multiple_of(d_idx * CH, CH), 1), :]
    sgn = jnp.where((row & 1) == 0, 1.0, -1.0)
    a = kd[:CH] + sgn * (km[:CH] - k0)
    b = jnp.where(row == 0, 0.0, kd[CH:] + sgn * km[CH:])
    hn = kd[CH:CH + 1] + km[CH:CH + 1] - k0
    for o in range(HY_ORDER):
        oab_ref[0, o, 0, 0] = a[:, o * D_B:(o + 1) * D_B]
        oab_ref[0, o, 0, 1] = b[:, o * D_B:(o + 1) * D_B]
        od0_ref[0, o, 0] = jnp.broadcast_to(hn[:, o * D_B:(o + 1) * D_B], (SUB, D_B))


def _filt_call(L, z, dec, w1, b1, w2, b2, fr, w3, fwd32):
    nd = 2 * (L // CH) - 1
    fixed = lambda l, d: (0, 0)
    lay3 = lambda l, d: (l, 0, 0)
    return pl.pallas_call(
        functools.partial(_filt_kernel, L=L),
        grid=(DEPTH, nd),
        in_specs=[pl.BlockSpec((2 * L, LANE), fixed), pl.BlockSpec((2 * L, D_B), fixed),
                  pl.BlockSpec((1, LANE, LANE), lay3), pl.BlockSpec((1, 1, LANE), lay3),
                  pl.BlockSpec((1, LANE, LANE), lay3), pl.BlockSpec((1, 1, LANE), lay3),
                  pl.BlockSpec((1, 1, LANE), lay3),
                  pl.BlockSpec((1, LANE, HY_ORDER * 2 * D_B), lay3),
                  pl.BlockSpec((2 * CH, CH), fixed)],
        out_specs=[pl.BlockSpec((1, HY_ORDER, 1, 2, CH, D_B), lambda l, d: (l, 0, d, 0, 0, 0)),
                   pl.BlockSpec((1, HY_ORDER, 1, SUB, D_B), lambda l, d: (l, 0, d, 0, 0))],
        out_shape=[jax.ShapeDtypeStruct((DEPTH, HY_ORDER, nd, 2, CH, D_B), F32),
                   jax.ShapeDtypeStruct((DEPTH, HY_ORDER, nd, SUB, D_B), F32)],
        scratch_shapes=[pltpu.VMEM((2 * L, HY_ORDER * D_B), F32),
                        pltpu.VMEM((2 * L // CH, 2 * CH, HY_ORDER * D_B), F32)],
        compiler_params=_cp(("parallel", "arbitrary")),
        name=f"hyena_filter_{L}",
    )(z, dec, w1, b1, w2, b2, fr, w3, fwd32)


def _hy_variant(L, o_idx, hy_ref, cw_ref, cb_ref, fwd_ref, inv_ref, hab_ref, hd0_ref, bias_ref, o_ref,
                s_y, s_x, s_u, s_v):
    nch, nseq, P = SEG // CH, SEG // L, L // CH
    row = lax.broadcasted_iota(jnp.int32, (CH, 1), 0)
    frow = lax.broadcasted_iota(jnp.int32, (FS, 1), 0)

    @pl.when(o_idx == 0)
    def _():
        cw = cw_ref[...]
        cb = cb_ref[...]

        def short_conv(c, carry):
            start = pl.multiple_of(c * CH, CH)
            xcat, main = _halo_rows(hy_ref, start, CH)
            tpos = (start + row) & (L - 1)
            xm1 = jnp.where(tpos >= 1, xcat[SUB - 1:SUB - 1 + CH], 0.0)
            xp1 = jnp.where(tpos <= L - 2, xcat[SUB + 1:SUB + 1 + CH], 0.0)
            hc = cw[0:1] * xm1 + cw[1:2] * main + cw[2:3] * xp1 + cb
            s_y[pl.ds(start, CH), :] = hc[:, :D_B]
            s_x[0, pl.ds(start, CH), :] = hc[:, D_B:2 * D_B]
            s_x[1, pl.ds(start, CH), :] = hc[:, 2 * D_B:]
            return carry

        lax.fori_loop(0, nch, short_conv, 0)

    bias = bias_ref[0]

    def loop(n, body, init):
        return body(0, init) if n == 1 else lax.fori_loop(0, n, body, init)

    def one_sequence(s, slot):
        base = s * L
        u0 = slot * P

        def forward_dft(j, cc):
            r = pl.multiple_of(base + j * CH, CH)
            s_u[u0 + j] = _dot(fwd_ref[...], s_y[pl.ds(r, CH), :].astype(BF16))
            return cc

        loop(P, forward_dft, 0)

        def output_block(i, cc):
            for fs in range(CH // FS):
                lo = fs * FS

                def accumulate(j, acc):
                    yre, yim = acc
                    d = i - j + (P - 1)
                    ure = s_u[u0 + j, lo:lo + FS, :]
                    uim = s_u[u0 + j, CH + lo:CH + lo + FS, :]
                    a = hab_ref[0, 0, d, 0, lo:lo + FS, :]
                    b = hab_ref[0, 0, d, 1, lo:lo + FS, :]
                    dd = jnp.where(frow == 0, hd0_ref[0, 0, d, 0:1, :], a) if fs == 0 else a
                    return yre + ure * a - uim * b, yim + ure * b + uim * dd

                zero = jnp.zeros((FS, D_B), F32)
                yre, yim = loop(P, accumulate, (zero, zero))
                s_v[slot, lo:lo + FS, :] = yre.astype(BF16)
                s_v[slot, CH + lo:CH + lo + FS, :] = yim.astype(BF16)
            yc = _dot(inv_ref[...], s_v[slot])
            r = pl.multiple_of(base + i * CH, CH)
            s_y[pl.ds(r, CH), :] = s_x[o_idx, pl.ds(r, CH), :] * (yc + s_y[pl.ds(r, CH), :] * bias)
            return cc

        loop(P, output_block, 0)

    def single_block_pair(p, carry):
        rows = [pl.multiple_of((2 * p + k) * L, CH) for k in range(2)]
        ys = [s_y[pl.ds(r, CH), :] for r in rows]
        gates = [s_x[o_idx, pl.ds(r, CH), :] for r in rows]
        a = hab_ref[0, 0, 0, 0]
        b = hab_ref[0, 0, 0, 1]
        dd = jnp.where(row == 0, hd0_ref[0, 0, 0, 0:1, :], a)
        outs = []
        for y, gate in zip(ys, gates):
            u = _dot(fwd_ref[...], y.astype(BF16))
            ure, uim = u[:CH], u[CH:]
            v = jnp.concatenate([ure * a - uim * b, ure * b + uim * dd], axis=0).astype(BF16)
            outs.append(gate * (_dot(inv_ref[...], v) + y * bias))
        for r, out in zip(rows, outs):
            s_y[pl.ds(r, CH), :] = out
        return carry

    if P == 1 and nseq % 2 == 0:
        lax.fori_loop(0, nseq // 2, single_block_pair, 0)
    else:
        lax.fori_loop(0, nseq, lambda s, carry: (one_sequence(s, 0), carry)[1], 0)

    @pl.when(o_idx == HY_ORDER - 1)
    def _():
        def finish(c, carry):
            start = pl.multiple_of(c * CH, CH)
            o_ref[pl.ds(start, CH), :] = _rms(s_y[pl.ds(start, CH), :])
            return carry

        lax.fori_loop(0, nch, finish, 0)


def _hy_kernel(hy_ref, cw_ref, cb_ref, fwd_ref, inv_ref, habc_ref, hd0c_ref, habl_ref, hd0l_ref, bias_ref,
               o_ref, s_y, s_x, s_u, s_v, *, l_ctx, l_lat, n_ctx_blk):
    i = pl.program_id(0)
    o_idx = pl.program_id(1)
    scratch = (s_y, s_x, s_u, s_v)

    @pl.when(i < n_ctx_blk)
    def _():
        _hy_variant(l_ctx, o_idx, hy_ref, cw_ref, cb_ref, fwd_ref, inv_ref, habc_ref, hd0c_ref, bias_ref, o_ref,
                    *scratch)

    @pl.when(i >= n_ctx_blk)
    def _():
        _hy_variant(l_lat, o_idx, hy_ref, cw_ref, cb_ref, fwd_ref, inv_ref, habl_ref, hd0l_ref, bias_ref, o_ref,
                    *scratch)


def _hy_call(layer, hyb, conv_w, conv_b, fwd, inv, habc, hd0c, habl, hd0l, bias, l_ctx, l_lat, n_ctx_blk):
    n = hyb.shape[0]
    ndc, ndl = habc.shape[2], habl.shape[2]
    pmax = max(l_ctx, l_lat) // CH
    row = lambda i, o: (i, 0)
    fixed = lambda i, o: (0, 0)
    lat_o = lambda i, o: jnp.where(i >= n_ctx_blk, o, 0)
    ctx_o = lambda i, o: jnp.where(i < n_ctx_blk, o, 0)
    return pl.pallas_call(
        functools.partial(_hy_kernel, l_ctx=l_ctx, l_lat=l_lat, n_ctx_blk=n_ctx_blk),
        grid=(n // SEG, HY_ORDER),
        in_specs=[pl.BlockSpec((SEG, 3 * D_B), row),
                  pl.BlockSpec((3, 3 * D_B), fixed), pl.BlockSpec((1, 3 * D_B), fixed),
                  pl.BlockSpec((2 * CH, CH), fixed), pl.BlockSpec((CH, 2 * CH), fixed),
                  pl.BlockSpec((1, 1, ndc, 2, CH, D_B), lambda i, o: (layer, ctx_o(i, o), 0, 0, 0, 0)),
                  pl.BlockSpec((1, 1, ndc, SUB, D_B), lambda i, o: (layer, ctx_o(i, o), 0, 0, 0)),
                  pl.BlockSpec((1, 1, ndl, 2, CH, D_B), lambda i, o: (layer, lat_o(i, o), 0, 0, 0, 0)),
                  pl.BlockSpec((1, 1, ndl, SUB, D_B), lambda i, o: (layer, lat_o(i, o), 0, 0, 0)),
                  pl.BlockSpec((1, 1, D_B), lambda i, o: (o, 0, 0))],
        out_specs=pl.BlockSpec((SEG, D_B), row),
        out_shape=jax.ShapeDtypeStruct((n, D_B), F32),
        scratch_shapes=[pltpu.VMEM((SEG, D_B), F32), pltpu.VMEM((HY_ORDER, SEG, D_B), F32),
                        pltpu.VMEM((max(pmax, 2), 2 * CH, D_B), F32), pltpu.VMEM((2, 2 * CH, D_B), BF16)],
        compiler_params=_cp(("parallel", "arbitrary")),
        name="hyena",
    )(hyb, conv_w, conv_b, fwd, inv, habc, hd0c, habl, hd0l, bias)


def _row_scan(x, op, fill, reverse):
    t, width = x.shape
    n_tiles = t // SUB
    sub = lax.broadcasted_iota(jnp.int32, (1, SUB, 1), 1)
    x3 = x.reshape(n_tiles, SUB, width)
    for s in (1, 2, 4):
        shift, keep = (SUB - s, sub < SUB - s) if reverse else (s, sub >= s)
        x3 = op(x3, jnp.where(keep, pltpu.roll(x3, shift, 1), fill))
    x = x3.reshape(t, width)
    out = [None] * n_tiles
    carry = None
    for i in (reversed(range(n_tiles)) if reverse else range(n_tiles)):
        tile = x[i * SUB:(i + 1) * SUB]
        out[i] = tile if carry is None else op(tile, carry)
        carry = out[i][0:1] if reverse else out[i][SUB - 1:SUB]
    return jnp.concatenate(out, axis=0)


_STK_ONE = 3 * SUB


def _mlstm_prep(d, g_ref, m_old):
    T = CH
    reverse = d == 1
    g = g_ref[...]
    if d == 1:
        g = pltpu.roll(g, LANE - 2 * H_C, 1)
    lane = lax.broadcasted_iota(jnp.int32, (1, LANE), 1)
    head = lane < H_C
    b = pltpu.roll(_row_scan(_log_sigmoid(g), jnp.add, 0.0, reverse), LANE - H_C, 1)
    r = jnp.where(head, g - b, 0.0)
    big_m = jnp.maximum(m_old, _row_scan(r, jnp.maximum, -jnp.inf, reverse))
    last = 0 if reverse else T - 1
    m_last = big_m[last:last + 1, :]
    low = lane < SUB
    p0, p1, p2 = (jnp.where(low, p.astype(F32), 0.0) for p in _split3(-big_m))
    cols = (p0 + pltpu.roll(p1, SUB, 1) + pltpu.roll(p2, 2 * SUB, 1)
            + jnp.where(jnp.logical_and(lane >= _STK_ONE, lane < _STK_ONE + SUB), 1.0, 0.0))
    rowid = lax.broadcasted_iota(jnp.int32, (SUB, 1), 0)
    r8 = r.T[0:SUB, :]
    m_last8 = sum(jnp.where(rowid == h, m_last[:, h:h + 1], 0.0) for h in range(H_C))
    ws8 = jnp.where(rowid < H_C, jnp.exp(r8 - m_last8), 0.0)
    return {"r3": [p.astype(F32) for p in _split3(r8)], "ws8": ws8, "wc": jnp.exp(m_old - m_last),
            "m_new": jnp.where(head, b[last:last + 1, :] + m_last, 0.0),
            "wi": jnp.exp(m_old - big_m), "e": jnp.exp(-(b + big_m)), "cols_b": cols.astype(BF16)}


def _mlstm_variant(carry, qf, vf, ktf, gf, qb_, vb_, ktb_, gb_, hf_ref, hb_ref, co_ref, no_ref, mo_ref,
                   s_cx, s_n, s_m):
    T = CH
    ii = lax.broadcasted_iota(jnp.int32, (T, T), 0)
    jj = lax.broadcasted_iota(jnp.int32, (T, T), 1)
    rowid = lax.broadcasted_iota(jnp.int32, (SUB, 1), 0)
    one_col = jnp.where(lax.broadcasted_iota(jnp.int32, (T, DK), 1) == 0, 1.0, 0.0).astype(BF16)
    prep = []
    for d, g_ref in ((0, gf), (1, gb_ if carry else gf)):
        m_old = s_m[d:d + 1, :] if carry else jnp.zeros((1, LANE), F32)
        prep.append(_mlstm_prep(d, g_ref, m_old))
    refs = ((qf, vf, ktf, hf_ref), (qb_, vb_, ktb_, hb_ref))
    for h in range(H_C):
        sl = slice(h * DK, (h + 1) * DK)
        for d in range(2):
            p = prep[d]
            q_ref, v_ref, kt_ref, h_ref = refs[d]
            idx = d * H_C + h
            if carry or d == 0:
                qb = (q_ref[:, sl] * (DK ** -0.5)).astype(BF16)
                kt = kt_ref[sl, :]
                ktb = kt.astype(BF16)
                v_ext = jnp.concatenate([v_ref[:, sl].astype(BF16), one_col], axis=1)
                s_raw = _dot(qb, ktb)
            tri = (jj >= ii) if d == 1 else (jj <= ii)
            sel = jnp.broadcast_to(jnp.where(rowid == h, 1.0, 0.0), (SUB, T))
            rr = sum(jnp.where(rowid == i, piece[h:h + 1, :], 0.0) for i, piece in enumerate(p["r3"]))
            rmat = jnp.concatenate([sel, sel, sel, rr, jnp.zeros((LANE - 4 * SUB, T), F32)], axis=0)
            expo = _dot(p["cols_b"], rmat.astype(BF16))
            s = s_raw * jnp.exp(jnp.where(tri, expo, -jnp.inf))
            intra = _dot(s.astype(BF16), v_ext)
            num, den = intra[:, :DK], intra[:, DK:DK + 1]
            if carry:
                cx = s_cx[idx]
                inter = _dot(qb, cx.astype(BF16))
                wi = p["wi"][:, h:h + 1]
                num, den = num + wi * inter[:, :DK], den + wi * inter[:, DK:DK + 1]
            h_ref[:, sl] = num / jnp.maximum(jnp.abs(den), p["e"][:, h:h + 1])
            upd = _dot((kt * p["ws8"][h:h + 1, :]).astype(BF16), v_ext)
            n_upd = lax.dot_general(p["ws8"].astype(BF16), ktb, _NT, preferred_element_type=F32)[h:h + 1, :]
            if carry:
                wc = p["wc"][:, h:h + 1]
                s_cx[idx] = wc * cx + upd
                s_n[idx:idx + 1, :] = wc * s_n[idx:idx + 1, :] + n_upd
            else:
                co_ref[0, co_ref.shape[1] - 1, idx] = upd[:, :DK]
                no_ref[0, idx:idx + 1, :] = n_upd
    m_rows = jnp.concatenate([prep[0]["m_new"], prep[1]["m_new"], jnp.zeros((SUB - 2, LANE), F32)], axis=0)
    if carry:
        s_m[...] = m_rows
    else:
        mo_ref[0] = m_rows


def _mlstm_kernel(*refs, n_ctx_steps, nc_lat, n_prev):
    cx0_ref, n0_ref, m0_ref = refs[8:11]
    s_cx, s_n, s_m = refs[-3:]
    n_in = 12 if n_prev else 11
    data = refs[:8] + refs[n_in:]
    co_ref = data[10]
    t = pl.program_id(0)
    is_ctx = t < n_ctx_steps

    @pl.when(is_ctx)
    def _():
        if n_prev:
            co_ref[0, 0:n_prev] = refs[11][0]
        _mlstm_variant(False, *data)

    @pl.when(jnp.logical_not(is_ctx))
    def _():
        @pl.when((t - n_ctx_steps) % nc_lat == 0)
        def _():
            s_cx[...] = cx0_ref[0]
            s_n[...] = n0_ref[0]
            s_m[...] = m0_ref[0]

        _mlstm_variant(True, *data)


def _mlstm_call(q, v, kt, gates, cx0, n0, m0, prev_c, n_ctx_steps, nc_lat):
    n = q.shape[0]
    steps = n // CH
    nst = 2 * H_C
    n_prev = 0 if prev_c is None else prev_c.shape[1]

    def bwd_blk(t):
        r = jnp.maximum(t - n_ctx_steps, 0)
        return n_ctx_steps + (r // nc_lat) * nc_lat + (nc_lat - 1 - r % nc_lat)

    out_bwd = lambda t: jnp.where(t < n_ctx_steps, t, bwd_blk(t))
    lat_b = lambda t: jnp.maximum(t - n_ctx_steps, 0) // nc_lat
    ctx_b = lambda t: jnp.minimum(t, n_ctx_steps - 1)
    rows = lambda w, blk: pl.BlockSpec((CH, w), lambda t: (blk(t), 0))
    cols = lambda h, blk: pl.BlockSpec((h, CH), lambda t: (0, blk(t)))
    ident = lambda t: t
    state = lambda layers: pl.BlockSpec((1, layers, nst, DK, DK), lambda t: (ctx_b(t), 0, 0, 0, 0))
    prev_specs, prev_args = ([state(n_prev)], [prev_c]) if n_prev else ([], [])
    return pl.pallas_call(
        functools.partial(_mlstm_kernel, n_ctx_steps=n_ctx_steps, nc_lat=nc_lat, n_prev=n_prev),
        grid=(steps,),
        in_specs=[rows(D_C, ident), rows(D_C, ident), cols(D_C, ident), rows(LANE, ident),
                  rows(D_C, bwd_blk), rows(D_C, bwd_blk), cols(D_C, bwd_blk), rows(LANE, bwd_blk),
                  pl.BlockSpec((1, nst, DK, 2 * DK), lambda t: (lat_b(t), 0, 0, 0)),
                  pl.BlockSpec((1, nst, DK), lambda t: (lat_b(t), 0, 0)),
                  pl.BlockSpec((1, SUB, LANE), lambda t: (lat_b(t), 0, 0))] + prev_specs,
        out_specs=[rows(D_C, ident), rows(D_C, out_bwd),
                   state(n_prev + 1),
                   pl.BlockSpec((1, nst, DK), lambda t: (ctx_b(t), 0, 0)),
                   pl.BlockSpec((1, SUB, LANE), lambda t: (ctx_b(t), 0, 0))],
        out_shape=[jax.ShapeDtypeStruct((n, D_C), F32), jax.ShapeDtypeStruct((n, D_C), F32),
                   jax.ShapeDtypeStruct((n_ctx_steps, n_prev + 1, nst, DK, DK), F32),
                   jax.ShapeDtypeStruct((n_ctx_steps, nst, DK), F32),
                   jax.ShapeDtypeStruct((n_ctx_steps, SUB, LANE), F32)],
        scratch_shapes=[pltpu.VMEM((nst, DK, 2 * DK), F32), pltpu.VMEM((nst, DK), F32),
                        pltpu.VMEM((SUB, LANE), F32)],
        compiler_params=_cp(("arbitrary",)),
        name="mlstm",
    )(q, v, kt, gates, q, v, kt, gates, cx0, n0, m0, *prev_args)


def _out_kernel(x_ref, oa_ref, ob_ref, hf_ref, hb_ref, og_ref, mod_ref, mg_ref, w_ref, g_ref, b_ref,
                rw_ref, rb_ref, x1_ref, he_ref, cnt_ref, it_ref, s_cnt):
    i = pl.program_id(0)

    @pl.when(i == 0)
    def _():
        s_cnt[...] = jnp.zeros(s_cnt.shape, F32)

    m = mod_ref[0]
    mg = mg_ref[...]
    hc = hf_ref[...] + hb_ref[...]
    og = og_ref[...]
    out_c = [_sigmoid(og[:, h * DK:(h + 1) * DK]) * _rms(hc[:, h * DK:(h + 1) * DK]) for h in range(H_C)]
    mix = jnp.concatenate([oa_ref[...], ob_ref[...]] + out_c, axis=1) * mg
    acc = _dot(mix.astype(BF16), w_ref[...])
    x1 = _ln_plain(ALPHA * x_ref[...] + m[2:3] * acc) * g_ref[...] + b_ref[...]
    x1_ref[...] = x1
    h2 = x1 * (1.0 + m[4:5]) + m[3:4]
    he_ref[:, :D_MODEL] = h2

    lg = _dot3(h2, rw_ref[...]) + rb_ref[...]
    col = lax.broadcasted_iota(jnp.int32, lg.shape, 1)
    colf = col.astype(F32)
    first_at = lambda hit: jnp.min(jnp.where(hit, colf, float(LANE)), -1, keepdims=True).astype(jnp.int32)
    ninf = -jnp.inf
    lgm = jnp.where(col < N_GROUPS, lg, ninf)
    mx = jnp.max(lgm, -1, keepdims=True)
    gi = first_at(lgm == mx)
    pg_top = 1.0 / jnp.sum(jnp.where(col < N_GROUPS, jnp.exp(lg - mx), 0.0), -1, keepdims=True)
    lo4 = N_GROUPS + E_PER_GROUP * gi
    lem = jnp.where(jnp.logical_and(col >= lo4, col < lo4 + E_PER_GROUP), lg, ninf)
    v1 = jnp.max(lem, -1, keepdims=True)
    i1 = first_at(lem == v1)
    lem2 = jnp.where(col == i1, ninf, lem)
    v2 = jnp.max(lem2, -1, keepdims=True)
    i2 = first_at(lem2 == v2)
    e21 = jnp.exp(v2 - v1)
    w1 = pg_top / (1.0 + e21)
    w2 = pg_top * e21 / (1.0 + e21)
    e1, e2 = i1 - N_GROUPS, i2 - N_GROUPS
    first_lo = e1 < e2
    elo, ehi = jnp.minimum(e1, e2), jnp.maximum(e1, e2)
    wlo, whi = jnp.where(first_lo, w1, w2), jnp.where(first_lo, w2, w1)
    llo, lhi = elo - E_PER_GROUP * gi, ehi - E_PER_GROUP * gi
    cls = gi * N_PAIRS + ((llo * (7 - llo)) >> 1) + lhi - llo - 1

    oh = jnp.where(col == cls, 1.0, 0.0)
    ii = lax.broadcasted_iota(jnp.int32, (TM_OUT, TM_OUT), 0)
    jj = lax.broadcasted_iota(jnp.int32, (TM_OUT, TM_OUT), 1)
    before = jnp.where(jj < ii, 1.0, 0.0).astype(BF16)
    cnt = s_cnt[0:1, :]
    rank = jnp.sum(oh * (_dot(before, oh.astype(BF16)) + cnt), -1, keepdims=True)
    cnt = cnt + jnp.sum(oh, 0, keepdims=True)
    s_cnt[...] = jnp.broadcast_to(cnt, s_cnt.shape)
    cnt_ref[...] = jnp.broadcast_to(cnt, cnt_ref.shape)

    info = jnp.zeros(lg.shape, F32)
    for c, val in ((I_CLS, cls.astype(F32)), (I_ELO, elo.astype(F32)), (I_EHI, ehi.astype(F32)),
                   (I_WLO, wlo), (I_WHI, whi), (I_RANK, rank)):
        info = jnp.where(col == c, val, info)
    he_ref[:, D_MODEL:] = info
    it_ref[...] = info.T[:SUB, :]


def _out_call(x, out_a, out_b, hcf, hcb, ogate, mod_l, mix_g, w_out, ln_g, ln_b, rt_w, rt_b, group_of_tile):
    n = x.shape[0]
    row = lambda i: (i, 0)
    fixed = lambda i: (0, 0)
    return pl.pallas_call(
        _out_kernel,
        grid=(n // TM_OUT,),
        in_specs=[pl.BlockSpec((TM_OUT, D_MODEL), row), pl.BlockSpec((TM_OUT, D_A), row),
                  pl.BlockSpec((TM_OUT, D_B), row),
                  pl.BlockSpec((TM_OUT, D_C), row), pl.BlockSpec((TM_OUT, D_C), row), pl.BlockSpec((TM_OUT, D_C), row),
                  pl.BlockSpec((1, 6, D_MODEL), lambda i: (group_of_tile(i), 0, 0)),
                  pl.BlockSpec((1, D_MODEL), fixed), pl.BlockSpec((D_MODEL, D_MODEL), fixed),
                  pl.BlockSpec((1, D_MODEL), fixed), pl.BlockSpec((1, D_MODEL), fixed),
                  pl.BlockSpec((D_MODEL, LANE), fixed), pl.BlockSpec((1, LANE), fixed)],
        out_specs=[pl.BlockSpec((TM_OUT, D_MODEL), row), pl.BlockSpec((TM_OUT, D_EXT), row),
                   pl.BlockSpec((SUB, LANE), fixed), pl.BlockSpec((SUB, TM_OUT), lambda i: (0, i))],
        out_shape=[jax.ShapeDtypeStruct((n, D_MODEL), F32), jax.ShapeDtypeStruct((n, D_EXT), F32),
                   jax.ShapeDtypeStruct((SUB, LANE), F32), jax.ShapeDtypeStruct((SUB, n), F32)],
        scratch_shapes=[pltpu.VMEM((SUB, LANE), F32)],
        compiler_params=_cp(("arbitrary",)),
        name="out_proj_router",
    )(x, out_a, out_b, hcf, hcb, ogate, mod_l, mix_g, w_out, ln_g, ln_b, rt_w, rt_b)


def _row_copy(src_ref, src_row, dst_ref, dst_row, sem):
    return pltpu.make_async_copy(src_ref.at[pl.ds(src_row, 1), :], dst_ref.at[pl.ds(dst_row, 1), :], sem)


def _sorted_row(cls_ref, rank_ref, start_ref, t):
    return start_ref[cls_ref[t]] + rank_ref[t]


def _scatter_kernel(cls_ref, rank_ref, start_ref, pad_ref, na_ref, x_ref, o_ref, ring, z_ref, sem_z, load_sems,
                    row_sems):
    n_tiles = o_ref.shape[0] // TM

    def zero_tile(row):
        return pltpu.make_async_copy(z_ref, o_ref.at[pl.ds(pl.multiple_of(row, TM), TM), :], sem_z)

    def for_zero_tiles(fn):
        def per_class(c, carry):
            row = pad_ref[c]

            @pl.when(row >= 0)
            def _():
                fn(zero_tile(row))

            return carry

        lax.fori_loop(0, N_CLASS, per_class, 0)

        def per_idle(t, carry):
            fn(zero_tile(t * TM))
            return carry

        lax.fori_loop(na_ref[0], n_tiles, per_idle, 0)

    i = pl.program_id(0)
    n_steps = x_ref.shape[0] // TM_PERM

    def load(step):
        return pltpu.make_async_copy(x_ref.at[pl.ds(pl.multiple_of(step * TM_PERM, TM_PERM), TM_PERM), :],
                                     ring.at[step % RING], load_sems.at[step % RING])

    def start_rows(step):
        slot = step % RING

        def body(g, carry):
            for u in range(DMA_UNROLL):
                r = g * DMA_UNROLL + u
                dst = _sorted_row(cls_ref, rank_ref, start_ref, step * TM_PERM + r)
                _row_copy(ring.at[slot], r, o_ref, dst, row_sems.at[slot]).start(priority=u % 2)
            return carry

        lax.fori_loop(0, TM_PERM // DMA_UNROLL, body, 0)

    def wait_rows(step):
        slot = step % RING
        pltpu.make_async_copy(ring.at[slot], o_ref.at[pl.ds(0, TM_PERM), :], row_sems.at[slot]).wait()

    @pl.when(i == 0)
    def _():
        z_ref[...] = jnp.zeros(z_ref.shape, F32)
        for_zero_tiles(lambda cp: cp.start())
        for_zero_tiles(lambda cp: cp.wait())
        load(0).start()

    @pl.when(i >= RING - 1)
    def _():
        wait_rows(i - (RING - 1))

    @pl.when(i + 1 < n_steps)
    def _():
        load(i + 1).start()

    load(i).wait()
    start_rows(i)

    @pl.when(i == n_steps - 1)
    def _():
        for back in range(min(RING - 1, n_steps) - 1, -1, -1):
            wait_rows(i - back)


def _scatter_call(plan, h_ext, n_tiles_max):
    any_spec = pl.BlockSpec(memory_space=pl.ANY)
    return pl.pallas_call(
        _scatter_kernel,
        grid_spec=pltpu.PrefetchScalarGridSpec(
            num_scalar_prefetch=5, grid=(h_ext.shape[0] // TM_PERM,),
            in_specs=[any_spec],
            out_specs=any_spec,
            scratch_shapes=[pltpu.VMEM((RING, TM_PERM, D_EXT), F32), pltpu.VMEM((TM, D_EXT), F32),
                            pltpu.SemaphoreType.DMA(()),
                            pltpu.SemaphoreType.DMA((RING,)), pltpu.SemaphoreType.DMA((RING,))]),
        out_shape=jax.ShapeDtypeStruct((n_tiles_max * TM, D_EXT), F32),
        compiler_params=_cp(("arbitrary",)),
        name="moe_scatter",
    )(plan["cls"], plan["rank"], plan["row_start"], plan["pad_rows"], plan["n_act"], h_ext)


def _moe_kernel(tg_ref, ng_ref, lo_ref, hi_ref, cg_ref, nv_ref, na_ref, x_ref, w1_hbm, w3_hbm, w2_hbm, o_ref,
                w1_ref, w3_ref, w2_ref, s1, s3, s2, wsem, *, layer):
    t = pl.program_id(0)
    active = t < na_ref[0]
    half = TM // 2

    def group_copies(g):
        rows = pl.ds(g * E_PER_GROUP, E_PER_GROUP)
        return [pltpu.make_async_copy(src.at[layer, rows], dst, wsem.at[k])
                for k, (src, dst) in enumerate(((w1_hbm, w1_ref), (w3_hbm, w3_ref), (w2_hbm, w2_ref)))]

    @pl.when(t == 0)
    def _():
        for cp in group_copies(tg_ref[0]):
            cp.start()

    @pl.when(jnp.logical_not(active))
    def _():
        o_ref[...] = jnp.zeros(o_ref.shape, F32)

    @pl.when(jnp.logical_and(active, cg_ref[t] == 1))
    def _():
        for cp in group_copies(tg_ref[t]):
            cp.wait()
        for e in range(E_PER_GROUP):
            s1[e] = w1_ref[e].astype(BF16)
            s3[e] = w3_ref[e].astype(BF16)
            s2[e] = w2_ref[e].astype(BF16)

        @pl.when(ng_ref[t] >= 0)
        def _():
            for cp in group_copies(ng_ref[t]):
                cp.start()

    def run(rows):
        xe = x_ref[0:rows, :]
        x = xe[:, :D_MODEL].astype(BF16)

        def expert(e, gate):
            a = _dot(x, s1[e])
            hm = a * _sigmoid(a) * _dot(x, s3[e]) * gate
            return _dot(hm.astype(BF16), s2[e])

        o_ref[0:rows, :] = (expert(lo_ref[t], xe[:, D_MODEL + I_WLO:D_MODEL + I_WLO + 1])
                            + expert(hi_ref[t], xe[:, D_MODEL + I_WHI:D_MODEL + I_WHI + 1]))

    @pl.when(jnp.logical_and(active, nv_ref[t] > half))
    def _():
        run(TM)

    @pl.when(jnp.logical_and(active, nv_ref[t] <= half))
    def _():
        run(half)
        o_ref[half:, :] = jnp.zeros((TM - half, D_MODEL), F32)


def _moe_call(layer, plan, x_sorted, w1, w3, w2):
    r = x_sorted.shape[0]
    act = lambda t, *scalars: (jnp.minimum(t, scalars[-1][0] - 1), 0)
    any_spec = pl.BlockSpec(memory_space=pl.ANY)
    up, down = (E_PER_GROUP, D_MODEL, D_E), (E_PER_GROUP, D_E, D_MODEL)
    return pl.pallas_call(
        functools.partial(_moe_kernel, layer=layer),
        grid_spec=pltpu.PrefetchScalarGridSpec(
            num_scalar_prefetch=7, grid=(r // TM,),
            in_specs=[pl.BlockSpec((TM, D_EXT), act), any_spec, any_spec, any_spec],
            out_specs=pl.BlockSpec((TM, D_MODEL), lambda t, *_: (t, 0)),
            scratch_shapes=[pltpu.VMEM(up, F32), pltpu.VMEM(up, F32), pltpu.VMEM(down, F32),
                            pltpu.VMEM(up, BF16), pltpu.VMEM(up, BF16), pltpu.VMEM(down, BF16),
                            pltpu.SemaphoreType.DMA((3,))]),
        out_shape=jax.ShapeDtypeStruct((r, D_MODEL), F32),
        compiler_params=_cp(("arbitrary",)),
        name="moe_experts",
    )(plan["tile_grp"], plan["next_grp"], plan["tile_lo"], plan["tile_hi"], plan["chg_grp"], plan["valid"],
      plan["n_act"], x_sorted, w1, w3, w2)


def _ln2_kernel(cls_ref, rank_ref, start_ref, x1_ref, mod_ref, g_ref, b_ref, y_ref, *rest, n_ctx_tiles):
    o_refs, (buf, sems) = rest[:-2], rest[-2:]
    i = pl.program_id(0)
    n_steps = pl.num_programs(0)

    def start_rows(step):
        slot = step % 2

        def body(g, carry):
            for u in range(DMA_UNROLL):
                r = g * DMA_UNROLL + u
                src = _sorted_row(cls_ref, rank_ref, start_ref, step * TM_PERM + r)
                _row_copy(y_ref, src, buf.at[slot], r, sems.at[slot]).start(priority=u % 2)
            return carry

        lax.fori_loop(0, TM_PERM // DMA_UNROLL, body, 0)

    @pl.when(i == 0)
    def _():
        start_rows(0)

    @pl.when(i + 1 < n_steps)
    def _():
        start_rows(i + 1)

    slot = i % 2
    pltpu.make_async_copy(y_ref.at[pl.ds(0, TM_PERM), :], buf.at[slot], sems.at[slot]).wait()
    m = mod_ref[0]
    y = _ln_plain(ALPHA * x1_ref[...] + m[5:6] * buf[slot]) * g_ref[...] + b_ref[...]
    if len(o_refs) == 1:
        o_refs[0][...] = y
    else:
        @pl.when(i < n_ctx_tiles)
        def _():
            o_refs[0][...] = y

        @pl.when(i >= n_ctx_tiles)
        def _():
            o_refs[1][...] = y


def _ln2_call(plan, x1, y_sorted, mod_l, ln_g, ln_b, group_of_tile, n_ctx_tiles, split):
    n = x1.shape[0]
    row = lambda i, *_: (i, 0)
    fixed = lambda i, *_: (0, 0)
    if split:
        n_ctx = n_ctx_tiles * TM_PERM
        out_specs = [pl.BlockSpec((TM_PERM, D_MODEL), lambda i, *_: (jnp.minimum(i, n_ctx_tiles - 1), 0)),
                     pl.BlockSpec((TM_PERM, D_MODEL), lambda i, *_: (jnp.maximum(i - n_ctx_tiles, 0), 0))]
        out_shape = [jax.ShapeDtypeStruct((n_ctx, D_MODEL), F32), jax.ShapeDtypeStruct((n - n_ctx, D_MODEL), F32)]
    else:
        out_specs = [pl.BlockSpec((TM_PERM, D_MODEL), row)]
        out_shape = [jax.ShapeDtypeStruct((n, D_MODEL), F32)]
    return pl.pallas_call(
        functools.partial(_ln2_kernel, n_ctx_tiles=n_ctx_tiles),
        grid_spec=pltpu.PrefetchScalarGridSpec(
            num_scalar_prefetch=3, grid=(n // TM_PERM,),
            in_specs=[pl.BlockSpec((TM_PERM, D_MODEL), row),
                      pl.BlockSpec((1, 6, D_MODEL), lambda i, *_: (group_of_tile(i), 0, 0)),
                      pl.BlockSpec((1, D_MODEL), fixed), pl.BlockSpec((1, D_MODEL), fixed),
                      pl.BlockSpec(memory_space=pl.ANY)],
            out_specs=out_specs,
            scratch_shapes=[pltpu.VMEM((2, TM_PERM, D_MODEL), F32), pltpu.SemaphoreType.DMA((2,))]),
        out_shape=out_shape,
        compiler_params=_cp(("arbitrary",)),
        name="moe_gather_ln2",
    )(plan["cls"], plan["rank"], plan["row_start"], x1, mod_l, ln_g, ln_b, y_sorted)


def _dft_matrices():
    n2 = 2 * CH
    f = np.arange(CH, dtype=np.float64)[:, None]
    t = np.arange(CH, dtype=np.float64)[None, :]
    ang = 2.0 * np.pi * f * t / n2
    re, im = np.cos(ang), -np.sin(ang)
    im[0, :] = np.cos(np.pi * t[0])
    fwd = np.concatenate([re, im], axis=0)
    scale = np.full((CH, 1), 2.0 / n2)
    scale[0, 0] = 1.0 / n2
    inv = np.concatenate([(re * scale).T, (im * scale).T], axis=1)
    return fwd.astype(np.float32), inv.astype(np.float32)


def _filter_features(L):
    lag = np.arange(-L, L)
    m = np.minimum(np.abs(lag), L - 1)
    t = (np.arange(L, dtype=np.float32) / np.float32(max(L - 1, 1)))[m]
    w = (np.float32(2.0 * math.pi) * np.arange(L, dtype=np.float32) / np.float32(L))[m]
    bands = np.linspace(1e-4, HY_BANDS - 1, HY_BANDS, dtype=np.float32)
    z = np.zeros((2 * L, LANE), np.float32)
    z[:, 0] = t
    z[:, 1:1 + HY_BANDS] = np.cos(w[:, None] * bands)
    z[:, 1 + HY_BANDS:HY_EMB] = -np.sin(w[:, None] * bands)
    lo, hi = math.log(HY_DECAY_TARGET) / 1.5, math.log(HY_DECAY_TARGET) / 0.3
    deltas = np.abs(np.linspace(lo, hi, D_B, dtype=np.float32))
    dec = np.exp(-t[:, None] * deltas)
    return z, dec.astype(np.float32)


def _sincos_2d(rows, cols):
    quarter = D_MODEL // 4
    omega = 1.0 / (10000.0 ** (jnp.arange(quarter, dtype=F32) / quarter))

    def emb(n):
        ang = jnp.arange(n, dtype=F32)[:, None] * omega[None]
        return jnp.concatenate([jnp.sin(ang), jnp.cos(ang)], -1)

    er, ec = emb(rows), emb(cols)
    half = D_MODEL // 2
    pos = jnp.concatenate([jnp.broadcast_to(er[:, None], (rows, cols, half)),
                           jnp.broadcast_to(ec[None], (rows, cols, half))], -1)
    return pos.reshape(rows * cols, D_MODEL)


def _pad_to(x, shape):
    return jnp.pad(x, [(0, s - d) for d, s in zip(x.shape, shape)])


def _block_diag(w):
    eye = jnp.eye(H_A, dtype=w.dtype)
    return jnp.einsum("hij,hg->higj", w, eye).reshape(D_A, D_A)


_PAIR_LO = np.array([0, 0, 0, 1, 1, 2], np.int32)
_PAIR_HI = np.array([1, 2, 3, 2, 3, 3], np.int32)


def _routing_plan(info_t, counts, n_tiles_max):
    cnt = counts[0, :N_CLASS].astype(jnp.int32)
    tiles = (cnt + TM - 1) // TM
    tile_end = jnp.cumsum(tiles)
    n_act = tile_end[-1]
    t = jnp.minimum(jnp.arange(n_tiles_max, dtype=jnp.int32), n_act - 1)
    tcls = jnp.minimum(jnp.sum((tile_end[None, :] <= t[:, None]).astype(jnp.int32), 1), N_CLASS - 1)
    grp, pair = (tcls // N_PAIRS).astype(jnp.int32), tcls % N_PAIRS
    valid = jnp.clip(cnt[tcls] - (t - (tile_end - tiles)[tcls]) * TM, 0, TM).astype(jnp.int32)
    first = jnp.ones((1,), jnp.int32)
    changed = lambda e: jnp.concatenate([first, (e[1:] != e[:-1]).astype(jnp.int32)])
    chg_grp = changed(grp)
    idx = jnp.arange(n_tiles_max, dtype=jnp.int32)
    later_change = jnp.logical_and(idx[None, :] > idx[:, None], chg_grp[None, :] == 1)
    next_pos = jnp.min(jnp.where(later_change, idx[None, :], n_tiles_max), axis=1)
    next_grp = jnp.where(next_pos < n_tiles_max, grp[jnp.minimum(next_pos, n_tiles_max - 1)], -1).astype(jnp.int32)
    return {"cls": info_t[I_CLS].astype(jnp.int32), "rank": info_t[I_RANK].astype(jnp.int32),
            "row_start": ((tile_end - tiles) * TM).astype(jnp.int32), "tile_grp": grp, "chg_grp": chg_grp, "next_grp": next_grp,
            "tile_lo": jnp.asarray(_PAIR_LO)[pair], "tile_hi": jnp.asarray(_PAIR_HI)[pair], "valid": valid,
            "n_act": n_act.reshape(1).astype(jnp.int32),
            "pad_rows": jnp.where(tiles > 0, (tile_end - 1) * TM, -1).astype(jnp.int32)}


def kernel(x_prompt, x_sample, c, state_lru, state_mlstm_C, state_mlstm_n, state_mlstm_m, c_ctx, w_ada, b_ada, w_in, b_in, conv_a_w, conv_a_b, lru_wa, lru_ba, lru_wx, lru_bx, lru_lam, conv_b_w, conv_b_b, hy_w1, hy_b1, hy_w2, hy_b2, hy_freq, hy_w3, hy_bias, mix_g, w_out, ln1_g, ln1_b, rt_wg, rt_bg, rt_we, rt_be, moe_w1, moe_w3, moe_w2, ln2_g, ln2_b):
    B, l_ctx, D = x_prompt.shape
    b_lat, l_lat, _ = x_sample.shape
    n_ctx, n_lat = B * l_ctx, b_lat * l_lat
    n = n_ctx + n_lat
    assert D == D_MODEL and w_in.shape[-1] == D_MAIN + N_GATE
    assert SEG % l_ctx == 0 and l_lat == SEG and l_ctx % CH == 0 and n_ctx % SEG == 0
    assert l_ctx == CH, "the mLSTM step schedule assumes one chunk per context sequence"
    assert 1 + b_lat <= SUB
    n_ctx_blk = n_ctx // SEG
    nc_lat = l_lat // CH

    def group_of(tile_rows):
        first_lat, per_seq = n_ctx // tile_rows, l_lat // tile_rows
        return lambda i: jnp.where(i < first_lat, 0, 1 + (i - first_lat) // per_seq)

    cond = jnp.concatenate([c_ctx[None], c, jnp.zeros((SUB - 1 - b_lat, D), F32)], 0)
    mod = _mod_call(cond, w_ada, b_ada).reshape(DEPTH, SUB, 6, D)
    pos = _sincos_2d(l_lat // GRID_W, GRID_W)
    x = _entry_call(x_prompt.reshape(n_ctx, D), x_sample.reshape(n_lat, D), pos)

    fwd_np, inv_np = _dft_matrices()
    fwd32 = jnp.asarray(fwd_np)
    fwd16, inv16 = fwd32.astype(BF16), jnp.asarray(inv_np).astype(BF16)
    fw1 = _pad_to(hy_w1, (DEPTH, LANE, LANE))
    fb1 = _pad_to(hy_b1[:, None, :], (DEPTH, 1, LANE))
    fw2 = _pad_to(hy_w2, (DEPTH, LANE, LANE))
    fb2 = _pad_to(hy_b2[:, None, :], (DEPTH, 1, LANE))
    ffr = _pad_to(hy_freq[:, None, :], (DEPTH, 1, LANE))
    fw3 = _pad_to(hy_w3, (DEPTH, LANE, HY_ORDER * 2 * D_B))
    spectra = {}
    for L in (l_ctx, l_lat):
        z_np, dec_np = _filter_features(L)
        spectra[L] = _filt_call(L, jnp.asarray(z_np), jnp.asarray(dec_np), fw1, fb1, fw2, fb2, ffr, fw3, fwd32)

    w_in_t = jnp.swapaxes(w_in, 1, 2)
    lat_slots = SEG // l_ctx
    st_lru, st_n, st_m = [], [], []
    c_all = None
    for l in range(DEPTH):
        b_main = b_in[l, None, :D_MAIN]
        k_lo, k_hi = _Z_CUTS[_K_CUT], _Z_CUTS[_K_CUT + 1]
        b_kt = jnp.broadcast_to(b_in[l, k_lo:k_hi, None], (D_C, TM_IN))
        w_gate = _pad_to(w_in[l, :, D_MAIN:], (D, LANE))
        b_gate = _pad_to(b_in[l, None, D_MAIN:], (1, LANE))
        xa, ya, hyb, q, v, og, gates, kt = _in_call(l, x, mod[l], w_in_t, b_main, b_kt, w_gate, b_gate,
                                                    group_of(TM_IN))

        lru_w = jnp.concatenate([_block_diag(lru_wa[l, 0]), _block_diag(lru_wx[l, 0]),
                                 _block_diag(lru_wa[l, 1]), _block_diag(lru_wx[l, 1])], 1).astype(BF16)
        lru_b = jnp.concatenate([lru_ba[l, 0], lru_bx[l, 0], lru_ba[l, 1], lru_bx[l, 1]])[None]
        h0_lat = _pad_to(state_lru[:, l][:, None], (b_lat, lat_slots, 2, D_A))
        h0_all = jnp.concatenate([jnp.zeros((n_ctx_blk, lat_slots, 2, D_A), F32), h0_lat], 0)
        out_a, lru_last = _lru_call(xa, ya, conv_a_w[l], conv_a_b[l, None], lru_w, lru_b, lru_lam[l], h0_all,
                                    l_ctx, l_lat, n_ctx_blk)

        habc, hd0c = spectra[l_ctx]
        habl, hd0l = spectra[l_lat]
        out_b = _hy_call(l, hyb, conv_b_w[l], conv_b_b[l, None], fwd16, inv16, habc, hd0c, habl, hd0l,
                         hy_bias[l][:, None, :], l_ctx, l_lat, n_ctx_blk)

        n0 = state_mlstm_n[:, l].reshape(b_lat, 2 * H_C, DK)
        cx0 = jnp.concatenate([state_mlstm_C[:, l].reshape(b_lat, 2 * H_C, DK, DK), n0[..., None],
                               jnp.zeros((b_lat, 2 * H_C, DK, DK - 1), F32)], -1)
        m0 = _pad_to(state_mlstm_m[:, l], (b_lat, SUB, LANE))
        hcf, hcb, c_all, n_fin, m_fin = _mlstm_call(q, v, kt, gates, cx0, n0, m0, c_all, n_ctx // CH, nc_lat)

        rt_w = _pad_to(jnp.concatenate([rt_wg[l], rt_we[l]], 1), (D, LANE))
        rt_b = _pad_to(jnp.concatenate([rt_bg[l], rt_be[l]])[None], (1, LANE))
        x1, h_ext, counts, info_t = _out_call(x, out_a, out_b, hcf, hcb, og, mod[l], mix_g[l, None],
                                              w_out[l].astype(BF16), ln1_g[l, None], ln1_b[l, None], rt_w, rt_b,
                                              group_of(TM_OUT))

        n_tiles_max = n // TM + N_CLASS
        plan = _routing_plan(info_t, counts, n_tiles_max)
        x_sorted = _scatter_call(plan, h_ext, n_tiles_max)
        y_sorted = _moe_call(l, plan, x_sorted, moe_w1, moe_w3, moe_w2)
        outs = _ln2_call(plan, x1, y_sorted, mod[l], ln2_g[l, None], ln2_b[l, None], group_of(TM_PERM),
                         n_ctx // TM_PERM, split=(l == DEPTH - 1))
        x = outs[0]

        st_lru.append(lru_last[:n_ctx_blk].reshape(B, 2, D_A))
        st_n.append(n_fin.reshape(B, 2, H_C, DK))
        st_m.append(m_fin[:, :2, :H_C])

    return (outs[0].reshape(B, l_ctx, D), outs[1].reshape(b_lat, l_lat, D),
            jnp.stack(st_lru, 1), c_all.reshape(B, DEPTH, 2, H_C, DK, DK), jnp.stack(st_n, 1), jnp.stack(st_m, 1))
```
